```python
import math
import jax, jax.numpy as jnp
from jax import lax
import numpy as np


D_MODEL = 2048
BATCH = 4
SEQ = 2048
DEPTH = 1
DEC_BATCH = 128
DEC_SEQ = 1
PAST_LEN = 16384
PAGE_SIZE = 128

D_SSM = D_MODEL // 2
SSM_GROUP = 16
N_SSM_GROUPS = D_SSM // SSM_GROUP
SSM_STATE = 64
DT_MIN = 0.001
DT_MAX = 0.1
D_CONV = D_MODEL // 2
CONV_W = 3
D_IN = D_SSM + 3 * D_CONV + 2 * D_MODEL
N_EXPERTS = 64
TOP_K = 8
N_EXPERT_GROUPS = 8
EXPERTS_PER_GROUP = N_EXPERTS // N_EXPERT_GROUPS
TOPK_GROUPS = 4
D_EXPERT = D_MODEL // 4
D_SHARED = D_EXPERT
ROUTED_SCALE = 2.5
MOE_BLOCK = 64
RMS_EPS = 1e-6

kernel_name = 'hybrid_s5_shortconv_moe_step'


def rms_norm(x, g):
    xf = x.astype(jnp.float32)
    y = xf * lax.rsqrt(jnp.mean(xf * xf, axis=-1, keepdims=True) + RMS_EPS)
    return (y * g.astype(jnp.float32)).astype(x.dtype)


def modulate(h, shift, scale):
    return h * (1 + scale) + shift


def _scan_combine(left, right):
    a_l, b_l = left
    a_r, b_r = right
    return a_r * a_l, a_r * b_l + b_r


def s5_ssm(u, s0_re, s0_im, a_re, a_im, log_dt, b_re, b_im, c_re, c_im, d_skip):
    bsz, seq, _ = u.shape
    f32 = jnp.float32
    uf = u.astype(f32)
    ug = uf.reshape(bsz, seq, N_SSM_GROUPS, SSM_GROUP).astype(jnp.complex64)
    lam = lax.complex(a_re.astype(f32), a_im.astype(f32))
    dt = jnp.exp(log_dt.astype(f32))[:, None]
    a_bar = jnp.exp(lam * dt)
    b_bar = ((a_bar - 1.0) / lam)[:, :, None] * lax.complex(b_re.astype(f32), b_im.astype(f32))
    c_mat = lax.complex(c_re.astype(f32), c_im.astype(f32))
    bu = jnp.einsum('bsgh,gph->bsgp', ug, b_bar)
    a_seq = jnp.broadcast_to(a_bar, bu.shape)
    a_cum, s = lax.associative_scan(_scan_combine, (a_seq, bu), axis=1)
    s0 = lax.complex(s0_re.astype(f32), s0_im.astype(f32))
    s = s + a_cum * s0[:, None]
    y = jnp.einsum('bsgp,ghp->bsgh', s, c_mat).real.reshape(bsz, seq, D_SSM)
    y = y + d_skip.astype(f32) * uf
    s_last = s[:, -1]
    return y.astype(u.dtype), jnp.real(s_last), jnp.imag(s_last)


def causal_short_conv(v, buf, conv_w, conv_b):
    seq = v.shape[1]
    full = jnp.concatenate([buf.astype(v.dtype), v], axis=1)
    out = conv_b
    for k in range(CONV_W):
        out = out + conv_w[k] * full[:, k:k + seq]
    return out, full[:, -(CONV_W - 1):]


def moe_ffn(h, w_router, router_bias, w_exp_gate, w_exp_up, w_exp_down, w_sh_gate, w_sh_up, w_sh_down):
    bsz, seq, d = h.shape
    t = bsz * seq
    ht = h.reshape(t, d)
    shared = (jax.nn.silu(ht @ w_sh_gate) * (ht @ w_sh_up)) @ w_sh_down
    scores = jax.nn.sigmoid((ht @ w_router).astype(jnp.float32))
    choice = scores + router_bias.astype(jnp.float32)
    grouped = choice.reshape(t, N_EXPERT_GROUPS, EXPERTS_PER_GROUP)
    group_score = lax.top_k(grouped, 2)[0].sum(-1)
    _, top_g = lax.top_k(group_score, TOPK_GROUPS)
    group_mask = jnp.any(top_g[:, :, None] == jnp.arange(N_EXPERT_GROUPS)[None, None, :], axis=1)
    expert_mask = jnp.repeat(group_mask, EXPERTS_PER_GROUP, axis=1)
    _, top_e = lax.top_k(jnp.where(expert_mask, choice, -jnp.inf), TOP_K)
    gate = jnp.take_along_axis(scores, top_e, axis=1)
    gate = gate / jnp.sum(gate, axis=-1, keepdims=True) * ROUTED_SCALE
    n_assign = t * TOP_K
    n_blocks = -(-n_assign // MOE_BLOCK) + N_EXPERTS
    flat_e = top_e.reshape(-1)
    flat_tok = jnp.repeat(jnp.arange(t, dtype=jnp.int32), TOP_K)
    flat_gate = gate.reshape(-1)
    order = jnp.argsort(flat_e)
    sorted_e = flat_e[order]
    counts = jnp.bincount(flat_e, length=N_EXPERTS)
    starts = jnp.cumsum(counts) - counts
    padded = (counts + MOE_BLOCK - 1) // MOE_BLOCK * MOE_BLOCK
    padded_end = jnp.cumsum(padded)
    dest = padded_end[sorted_e] - padded[sorted_e] + (jnp.arange(n_assign) - starts[sorted_e])
    tok_buf = jnp.zeros((n_blocks * MOE_BLOCK,), jnp.int32).at[dest].set(flat_tok[order])
    gate_buf = jnp.zeros((n_blocks * MOE_BLOCK,), jnp.float32).at[dest].set(flat_gate[order])
    block_expert = jnp.minimum(
        jnp.searchsorted(padded_end, jnp.arange(n_blocks) * MOE_BLOCK, side='right'), N_EXPERTS - 1)

    def expert_block(args):
        e, toks, g = args
        xb = ht[toks]
        yb = (jax.nn.silu(xb @ w_exp_gate[e]) * (xb @ w_exp_up[e])) @ w_exp_down[e]
        return (yb.astype(jnp.float32) * g[:, None]).astype(ht.dtype)

    ys = lax.map(expert_block, (block_expert, tok_buf.reshape(n_blocks, MOE_BLOCK),
                                gate_buf.reshape(n_blocks, MOE_BLOCK)))
    routed = jnp.zeros_like(ht).at[tok_buf].add(ys.reshape(-1, d))
    return (shared + routed).reshape(bsz, seq, d)


def hybrid_block(x, c, s_re, s_im, conv_buf, w_ada, b_ada, norm1_g, norm2_g, w_in,
                 ssm_a_re, ssm_a_im, ssm_log_dt, ssm_b_re, ssm_b_im, ssm_c_re, ssm_c_im, ssm_d,
                 w_glu, b_glu, w_lift_a, conv_w, conv_b, w_lift_b, w_out,
                 w_router, router_bias, w_exp_gate, w_exp_up, w_exp_down, w_sh_gate, w_sh_up, w_sh_down):
    ada = jax.nn.silu(c) @ w_ada + b_ada
    shift1, scale1, gate1, shift2, scale2, gate2 = jnp.split(ada[:, None, :], 6, axis=-1)
    h = modulate(rms_norm(x, norm1_g), shift1, scale1)
    proj = h @ w_in
    cuts = [D_SSM, D_SSM + D_CONV, D_SSM + 2 * D_CONV, D_SSM + 3 * D_CONV, D_SSM + 3 * D_CONV + D_MODEL]
    u_a, x_b, g_b, g_c, gl_a, gl_b = jnp.split(proj, cuts, axis=-1)
    y_a, new_re, new_im = s5_ssm(u_a, s_re, s_im, ssm_a_re, ssm_a_im, ssm_log_dt,
                                 ssm_b_re, ssm_b_im, ssm_c_re, ssm_c_im, ssm_d)
    y_a = jax.nn.gelu(y_a)
    y_a = y_a * jax.nn.sigmoid(y_a @ w_glu + b_glu)
    branch_a = y_a @ w_lift_a
    conv_out, new_conv = causal_short_conv(g_c * x_b, conv_buf, conv_w, conv_b)
    branch_b = (g_b * conv_out) @ w_lift_b
    merged = jax.nn.sigmoid(gl_a) * branch_a + jax.nn.sigmoid(gl_b) * branch_b
    x = x + gate1 * (merged @ w_out)
    h2 = modulate(rms_norm(x, norm2_g), shift2, scale2)
    x = x + gate2 * moe_ffn(h2, w_router, router_bias, w_exp_gate, w_exp_up, w_exp_down,
                            w_sh_gate, w_sh_up, w_sh_down)
    return x, new_re, new_im, new_conv


def setup_inputs(seed: int = 0) -> dict:
    key = jax.random.key(seed)
    ks = iter(jax.random.split(key, 48))
    f32 = jnp.float32
    L, G, P, H = DEPTH, N_SSM_GROUPS, SSM_STATE, SSM_GROUP

    def nrm(shape, scale):
        return jax.random.normal(next(ks), shape, f32) * scale

    def gain(shape):
        return 1.0 + nrm(shape, 0.01)

    x_prompt = nrm((BATCH, SEQ, D_MODEL), 1.0)
    x_sample = nrm((DEC_BATCH, DEC_SEQ, D_MODEL), 1.0)
    state_ssm_re = nrm((L, DEC_BATCH, G, P), 0.5)
    state_ssm_im = nrm((L, DEC_BATCH, G, P), 0.5)
    state_conv = nrm((L, DEC_BATCH, CONV_W - 1, D_CONV), 1.0)
    c_prompt = nrm((BATCH, D_MODEL), 1.0)
    c_sample = nrm((DEC_BATCH, D_MODEL), 1.0)
    w_ada = nrm((L, D_MODEL, 6 * D_MODEL), 0.5 * D_MODEL ** -0.5)
    b_ada = nrm((L, 6 * D_MODEL), 0.01)
    norm1_g = gain((L, D_MODEL))
    norm2_g = gain((L, D_MODEL))
    w_in = nrm((L, D_MODEL, D_IN), D_MODEL ** -0.5)
    ssm_a_re = -0.5 + nrm((L, G, P), 0.01)
    ssm_a_im = jnp.pi * jnp.arange(P, dtype=f32) + nrm((L, G, P), 0.01)
    ssm_log_dt = jax.random.uniform(next(ks), (L, G), f32, math.log(DT_MIN), math.log(DT_MAX))
    ssm_b_re = nrm((L, G, P, H), (2 * H) ** -0.5)
    ssm_b_im = nrm((L, G, P, H), (2 * H) ** -0.5)
    ssm_c_re = nrm((L, G, H, P), P ** -0.5)
    ssm_c_im = nrm((L, G, H, P), P ** -0.5)
    ssm_d = nrm((L, D_SSM), 1.0)
    w_glu = nrm((L, D_SSM, D_SSM), D_SSM ** -0.5)
    b_glu = nrm((L, D_SSM), 0.01)
    w_lift_a = nrm((L, D_SSM, D_MODEL), D_SSM ** -0.5)
    conv_w = nrm((L, CONV_W, D_CONV), CONV_W ** -0.5)
    conv_b = nrm((L, D_CONV), 0.01)
    w_lift_b = nrm((L, D_CONV, D_MODEL), D_CONV ** -0.5)
    w_out = nrm((L, D_MODEL, D_MODEL), D_MODEL ** -0.5)
    w_router = nrm((L, D_MODEL, N_EXPERTS), D_MODEL ** -0.5)
    router_bias = nrm((L, N_EXPERTS), 0.01)
    w_exp_gate = nrm((L, N_EXPERTS, D_MODEL, D_EXPERT), D_MODEL ** -0.5)
    w_exp_up = nrm((L, N_EXPERTS, D_MODEL, D_EXPERT), D_MODEL ** -0.5)
    w_exp_down = nrm((L, N_EXPERTS, D_EXPERT, D_MODEL), D_EXPERT ** -0.5)
    w_sh_gate = nrm((L, D_MODEL, D_SHARED), D_MODEL ** -0.5)
    w_sh_up = nrm((L, D_MODEL, D_SHARED), D_MODEL ** -0.5)
    w_sh_down = nrm((L, D_SHARED, D_MODEL), D_SHARED ** -0.5)
    final_norm_g = gain((D_MODEL,))
    return {'x_prompt': x_prompt, 'x_sample': x_sample,
            'state_ssm_re': state_ssm_re, 'state_ssm_im': state_ssm_im, 'state_conv': state_conv,
            'c_prompt': c_prompt, 'c_sample': c_sample,
            'w_ada': w_ada, 'b_ada': b_ada, 'norm1_g': norm1_g, 'norm2_g': norm2_g, 'w_in': w_in,
            'ssm_a_re': ssm_a_re, 'ssm_a_im': ssm_a_im, 'ssm_log_dt': ssm_log_dt,
            'ssm_b_re': ssm_b_re, 'ssm_b_im': ssm_b_im, 'ssm_c_re': ssm_c_re, 'ssm_c_im': ssm_c_im,
            'ssm_d': ssm_d, 'w_glu': w_glu, 'b_glu': b_glu, 'w_lift_a': w_lift_a,
            'conv_w': conv_w, 'conv_b': conv_b, 'w_lift_b': w_lift_b, 'w_out': w_out,
            'w_router': w_router, 'router_bias': router_bias,
            'w_exp_gate': w_exp_gate, 'w_exp_up': w_exp_up, 'w_exp_down': w_exp_down,
            'w_sh_gate': w_sh_gate, 'w_sh_up': w_sh_up, 'w_sh_down': w_sh_down,
            'final_norm_g': final_norm_g}


def reference(x_prompt, x_sample, state_ssm_re, state_ssm_im, state_conv, c_prompt, c_sample,
              w_ada, b_ada, norm1_g, norm2_g, w_in,
              ssm_a_re, ssm_a_im, ssm_log_dt, ssm_b_re, ssm_b_im, ssm_c_re, ssm_c_im, ssm_d,
              w_glu, b_glu, w_lift_a, conv_w, conv_b, w_lift_b, w_out,
              w_router, router_bias, w_exp_gate, w_exp_up, w_exp_down, w_sh_gate, w_sh_up, w_sh_down,
              final_norm_g):
    xp, xs = x_prompt, x_sample
    bp = x_prompt.shape[0]
    p_re, p_im, p_conv, s_re, s_im, s_conv = [], [], [], [], [], []
    for l in range(DEPTH):
        lw = (w_ada[l], b_ada[l], norm1_g[l], norm2_g[l], w_in[l],
              ssm_a_re[l], ssm_a_im[l], ssm_log_dt[l], ssm_b_re[l], ssm_b_im[l], ssm_c_re[l], ssm_c_im[l], ssm_d[l],
              w_glu[l], b_glu[l], w_lift_a[l], conv_w[l], conv_b[l], w_lift_b[l], w_out[l],
              w_router[l], router_bias[l], w_exp_gate[l], w_exp_up[l], w_exp_down[l],
              w_sh_gate[l], w_sh_up[l], w_sh_down[l])
        zero_ssm = jnp.zeros((bp, N_SSM_GROUPS, SSM_STATE), jnp.float32)
        zero_conv = jnp.zeros((bp, CONV_W - 1, D_CONV), x_prompt.dtype)
        xp, n_re, n_im, n_cv = hybrid_block(xp, c_prompt, zero_ssm, zero_ssm, zero_conv, *lw)
        p_re.append(n_re)
        p_im.append(n_im)
        p_conv.append(n_cv)
        xs, n_re, n_im, n_cv = hybrid_block(xs, c_sample, state_ssm_re[l], state_ssm_im[l], state_conv[l], *lw)
        s_re.append(n_re)
        s_im.append(n_im)
        s_conv.append(n_cv)
    y_prompt = rms_norm(xp, final_norm_g)
    y_sample = rms_norm(xs, final_norm_g)
    ssm_re_prompt = jnp.stack(p_re)
    ssm_im_prompt = jnp.stack(p_im)
    conv_prompt = jnp.stack(p_conv)
    ssm_re_sample = jnp.stack(s_re)
    ssm_im_sample = jnp.stack(s_im)
    conv_sample = jnp.stack(s_conv)
    return (y_prompt, y_sample, ssm_re_prompt, ssm_im_prompt, conv_prompt, ssm_re_sample, ssm_im_sample, conv_sample)
```

```python
import functools

import jax
import jax.numpy as jnp
from jax import lax
from jax.experimental import pallas as pl
from jax.experimental.pallas import tpu as pltpu

F32 = jnp.float32
BF16 = jnp.bfloat16
I32 = jnp.int32

D_MODEL = 2048
D_SSM = 1024
SSM_GROUP = 16
N_GROUPS = 64
SSM_STATE = 64
D_CONV = 1024
CONV_W = 3
N_EXPERTS = 64
TOP_K = 8
N_EXPERT_GROUPS = 8
EXPERTS_PER_GROUP = 8
TOPK_GROUPS = 4
D_EXPERT = 512
ROUTED_SCALE = 2.5
RMS_EPS = 1e-6

LANES = 128
SUBLANES = 8
OCTETS = N_GROUPS * SSM_GROUP // LANES
OCT_STATE = 2 * (N_GROUPS // OCTETS) * SSM_STATE
SSM_CHUNK = 8
MOE_ROWS = 256
COMB_TOK = 128
COMB_CH = 16
TILE_WORDS = D_MODEL // 2 // LANES
ROW_STRIDE = MOE_ROWS + SUBLANES
NEG_INF = float("-inf")


def _cparams(sem, vmem_mb):
    return pltpu.CompilerParams(dimension_semantics=sem, vmem_limit_bytes=vmem_mb * 1024 * 1024)


def _const_spec(shape):
    nd = len(shape)
    return pl.BlockSpec(shape, lambda *_: (0,) * nd, pipeline_mode=pl.Buffered(1))


def _mod(ref):
    v = ref[...]
    return v[0] if v.ndim == 3 else v


def _silu(x):
    return x * jax.nn.sigmoid(x)


def _pack_halves(x):
    half = x.shape[1] // 2
    xb = x.astype(BF16).astype(F32)
    bits = lax.bitcast_convert_type(xb, I32)
    lo = lax.shift_right_logical(bits[:, :half], 16)
    hi = bits[:, half:] & jnp.int32(-65536)
    return hi | lo


def _unpack_lo(w):
    return lax.bitcast_convert_type(lax.shift_left(w, 16), F32)


def _unpack_hi(w):
    return lax.bitcast_convert_type(w & jnp.int32(-65536), F32)


def _ada_kernel(c_ref, w_ref, b_ref, o_ref):
    a = _silu(c_ref[...]).astype(BF16)
    o_ref[...] = jnp.dot(a, w_ref[...].astype(BF16), preferred_element_type=F32) + b_ref[...]


def _ada(c_all, w_ada, b_ada):
    m = c_all.shape[0]
    n = w_ada.shape[1]
    tn = 1024
    return pl.pallas_call(
        _ada_kernel,
        grid=(n // tn,),
        in_specs=[pl.BlockSpec((m, D_MODEL), lambda j: (0, 0)),
                  pl.BlockSpec((D_MODEL, tn), lambda j: (0, j)),
                  pl.BlockSpec((1, tn), lambda j: (0, j))],
        out_specs=pl.BlockSpec((m, tn), lambda j: (0, j)),
        out_shape=jax.ShapeDtypeStruct((m, n), F32),
        compiler_params=_cparams(("arbitrary",), 40),
        name="ada",
    )(c_all, w_ada, b_ada.reshape(1, n))


def _mod_spec(mods, col, rows_per_batch, tm):
    if mods.ndim == 3:
        tiles_per_batch = rows_per_batch // tm
        return pl.BlockSpec((1, 1, D_MODEL), lambda m, *_: (m // tiles_per_batch, 0, col))
    return pl.BlockSpec((tm, D_MODEL), lambda m, *_: (m, col))


def _inproj_kernel(x_ref, g_ref, sh_ref, sc_ref, wu_ref, wr_ref, u_ref, r_ref, h_scr):
    @pl.when(pl.program_id(1) == 0)
    def _():
        x = x_ref[...]
        y = x * lax.rsqrt(jnp.mean(x * x, axis=-1, keepdims=True) + RMS_EPS) * g_ref[...]
        h = (y * (1.0 + _mod(sc_ref)) + _mod(sh_ref)).astype(BF16)
        h_scr[...] = h
        u_ref[...] = jnp.dot(h, wu_ref[...], preferred_element_type=F32)

    r_ref[...] = jnp.dot(h_scr[...], wr_ref[...], preferred_element_type=F32).astype(BF16)


def _in_proj(x, mods, rows_per_batch, g1, w_u, w_rest, tm):
    t = x.shape[0]
    n_rest = w_rest.shape[1]
    tn = 1024
    return pl.pallas_call(
        _inproj_kernel,
        grid=(t // tm, n_rest // tn),
        in_specs=[pl.BlockSpec((tm, D_MODEL), lambda m, n: (m, 0)),
                  pl.BlockSpec((1, D_MODEL), lambda m, n: (0, 0)),
                  _mod_spec(mods, 0, rows_per_batch, tm),
                  _mod_spec(mods, 1, rows_per_batch, tm),
                  pl.BlockSpec((D_MODEL, D_SSM), lambda m, n: (0, 0)),
                  pl.BlockSpec((D_MODEL, tn), lambda m, n: (0, n))],
        out_specs=[pl.BlockSpec((tm, D_SSM), lambda m, n: (m, 0)),
                   pl.BlockSpec((tm, tn), lambda m, n: (m, n))],
        out_shape=[jax.ShapeDtypeStruct((t, D_SSM), F32),
                   jax.ShapeDtypeStruct((t, n_rest), BF16)],
        scratch_shapes=[pltpu.VMEM((tm, D_MODEL), BF16)],
        compiler_params=_cparams(("arbitrary", "arbitrary"), 48),
        name="in_proj",
    )(x, g1, mods, mods, w_u, w_rest)


def _ssm_prep_kernel(are_ref, aim_ref, ldt_ref, btr_ref, bti_ref, cre_ref, cim_ref,
                     apow_ref, w1_ref, w2_ref, k_ref):
    lr = are_ref[...]
    li = aim_ref[...]
    dt = jnp.exp(ldt_ref[...])
    mag = jnp.exp(lr * dt)
    ar = mag * jnp.cos(li * dt)
    ai = mag * jnp.sin(li * dt)
    den = lr * lr + li * li
    xr = ar - 1.0
    cr = (xr * lr + ai * li) / den
    ci = (ai * lr - xr * li) / den
    btr = btr_ref[...]
    bti = bti_ref[...]
    bbr = cr * btr - ci * bti
    bbi = cr * bti + ci * btr
    cre = cre_ref[...]
    cim = cim_ref[...]
    dn = (((2,), (2,)), ((0,), (0,)))
    pr = jnp.ones_like(ar)
    pi = jnp.zeros_like(ar)
    for l in range(SSM_CHUNK):
        w1r = pr * bbr - pi * bbi
        w1i = pr * bbi + pi * bbr
        w1_ref[0, l] = w1r
        w1_ref[1, l] = w1i
        k_ref[l] = (lax.dot_general(cre, w1r, dn, precision=lax.Precision.HIGHEST,
                                    preferred_element_type=F32)
                    - lax.dot_general(cim, w1i, dn, precision=lax.Precision.HIGHEST,
                                      preferred_element_type=F32))
        pr, pi = pr * ar - pi * ai, pr * ai + pi * ar
        w2_ref[0, l] = cre * pr - cim * pi
        w2_ref[1, l] = cre * pi + cim * pr
        if l == 0:
            apow_ref[0] = pr
            apow_ref[1] = pi
    apow_ref[2] = pr
    apow_ref[3] = pi


def _ssm_prep(a_re, a_im, log_dt, b_re, b_im, c_re, c_im):
    g, p, h = N_GROUPS, SSM_STATE, SSM_GROUP
    lc = SSM_CHUNK
    return pl.pallas_call(
        _ssm_prep_kernel,
        out_shape=[jax.ShapeDtypeStruct((4, g, 1, p), F32),
                   jax.ShapeDtypeStruct((2, lc, g, h, p), F32),
                   jax.ShapeDtypeStruct((2, lc, g, h, p), F32),
                   jax.ShapeDtypeStruct((lc, g, h, h), F32)],
        compiler_params=pltpu.CompilerParams(vmem_limit_bytes=32 * 1024 * 1024),
        name="ssm_prep",
    )(a_re.reshape(g, 1, p), a_im.reshape(g, 1, p), log_dt.reshape(g, 1, 1),
      b_re.transpose(0, 2, 1), b_im.transpose(0, 2, 1), c_re, c_im)


def _assemble_ssm(w1, w2, k, lc):
    o, j, h, p = OCTETS, N_GROUPS // OCTETS, SSM_GROUP, SSM_STATE
    eye = jnp.eye(j, dtype=F32)
    w1c = w1[:, :lc][:, ::-1].reshape(2, lc, o, j, h, p)
    bk = jnp.einsum("csojhp,jk->osjhckp", w1c, eye).reshape(o, lc * LANES, OCT_STATE)
    sgn = jnp.array([1.0, -1.0], F32).reshape(2, 1, 1, 1, 1, 1)
    w2c = w2[:, :lc].reshape(2, lc, o, j, h, p) * sgn
    ck = jnp.einsum("ctojhp,jk->ocjptkh", w2c, eye).reshape(o, OCT_STATE, lc * LANES)
    lag = jnp.arange(lc)[None, :] - jnp.arange(lc)[:, None]
    kst = jnp.where((lag >= 0)[:, :, None, None, None], k[jnp.clip(lag, 0, lc - 1)], 0.0)
    kst = kst.reshape(lc, lc, o, j, h, h)
    tm = jnp.einsum("stojab,jk->osjbtka", kst, eye).reshape(o, lc * LANES, lc * LANES)
    return tm.astype(BF16), bk.astype(BF16), ck.astype(BF16)


def _state_layout(v):
    return v.reshape(OCTETS, 1, OCT_STATE // 2)


def _load_chunks(u_ref, lc, rb):
    parts = [u_ref[pl.ds(s, rb, stride=lc), :] if lc > 1 else u_ref[...] for s in range(lc)]
    u = parts[0] if lc == 1 else jnp.concatenate(parts, axis=-1)
    return u.astype(BF16)


def _ssm_state_kernel(u_ref, bk_ref, s_ref, *, lc, rb):
    s_ref[...] = jnp.dot(_load_chunks(u_ref, lc, rb), bk_ref[...], preferred_element_type=F32)


def _ssm_state(u, bk, lc, rb):
    t = u.shape[0]
    nb = t // (rb * lc)
    return pl.pallas_call(
        functools.partial(_ssm_state_kernel, lc=lc, rb=rb),
        grid=(OCTETS, nb),
        in_specs=[pl.BlockSpec((rb * lc, LANES), lambda o, b: (b, o)),
                  pl.BlockSpec((None, lc * LANES, OCT_STATE), lambda o, b: (o, 0, 0))],
        out_specs=pl.BlockSpec((rb, OCT_STATE), lambda o, b: (b, o)),
        out_shape=jax.ShapeDtypeStruct((t // lc, OCTETS * OCT_STATE), F32),
        compiler_params=_cparams(("arbitrary", "arbitrary"), 32),
        name="ssm_state",
    )(u, bk)


def _ssm_scan_kernel(sl_ref, ar_ref, ai_ref, s0_ref, sp_ref, sf_ref, *, nb, nc):
    half = OCT_STATE // 2
    ar = ar_ref[0]
    ai = ai_ref[0]
    init = tuple((s0_ref[b:b + 1, 0:half], s0_ref[b:b + 1, half:OCT_STATE]) for b in range(nb))

    def body(c, carry):
        new = []
        for b in range(nb):
            sr, si = carry[b]
            sp_ref[b, pl.ds(c, 1), 0:half] = sr
            sp_ref[b, pl.ds(c, 1), half:OCT_STATE] = si
            xr = sl_ref[b, pl.ds(c, 1), 0:half]
            xi = sl_ref[b, pl.ds(c, 1), half:OCT_STATE]
            new.append((ar * sr - ai * si + xr, ar * si + ai * sr + xi))
        return tuple(new)

    fin = lax.fori_loop(0, nc, body, init)
    for b in range(nb):
        sf_ref[b:b + 1, 0:half] = fin[b][0]
        sf_ref[b:b + 1, half:OCT_STATE] = fin[b][1]


def _ssm_scan(s_loc, al_re, al_im, s0):
    nb, nc, width = s_loc.shape
    return pl.pallas_call(
        functools.partial(_ssm_scan_kernel, nb=nb, nc=nc),
        grid=(OCTETS,),
        in_specs=[pl.BlockSpec((nb, nc, OCT_STATE), lambda o: (0, 0, o)),
                  pl.BlockSpec((1, 1, OCT_STATE // 2), lambda o: (o, 0, 0)),
                  pl.BlockSpec((1, 1, OCT_STATE // 2), lambda o: (o, 0, 0)),
                  pl.BlockSpec((nb, OCT_STATE), lambda o: (0, o))],
        out_specs=[pl.BlockSpec((nb, nc, OCT_STATE), lambda o: (0, 0, o)),
                   pl.BlockSpec((nb, OCT_STATE), lambda o: (0, o))],
        out_shape=[jax.ShapeDtypeStruct((nb, nc, width), F32),
                   jax.ShapeDtypeStruct((nb, width), F32)],
        compiler_params=_cparams(("arbitrary",), 32),
        name="ssm_scan",
    )(s_loc, al_re, al_im, s0)


def _ssm_out_kernel(u_ref, sp_ref, tm_ref, ck_ref, y_ref, *, lc, rb):
    y = (jnp.dot(_load_chunks(u_ref, lc, rb), tm_ref[...], preferred_element_type=F32)
         + jnp.dot(sp_ref[...].astype(BF16), ck_ref[...], preferred_element_type=F32))
    if lc == 1:
        y_ref[...] = y
    else:
        for t in range(lc):
            y_ref[pl.ds(t, rb, stride=lc), :] = y[:, t * LANES:(t + 1) * LANES]


def _ssm_out(u, s_prev, tm, ck, lc, rb):
    t = u.shape[0]
    nb = t // (rb * lc)
    return pl.pallas_call(
        functools.partial(_ssm_out_kernel, lc=lc, rb=rb),
        grid=(OCTETS, nb),
        in_specs=[pl.BlockSpec((rb * lc, LANES), lambda o, b: (b, o)),
                  pl.BlockSpec((rb, OCT_STATE), lambda o, b: (b, o)),
                  pl.BlockSpec((None, lc * LANES, lc * LANES), lambda o, b: (o, 0, 0)),
                  pl.BlockSpec((None, OCT_STATE, lc * LANES), lambda o, b: (o, 0, 0))],
        out_specs=pl.BlockSpec((rb * lc, LANES), lambda o, b: (b, o)),
        out_shape=jax.ShapeDtypeStruct((t, D_SSM), F32),
        compiler_params=_cparams(("arbitrary", "arbitrary"), 32),
        name="ssm_out",
    )(u, s_prev, tm, ck)


def _ssm_step_kernel(sl_ref, s0_ref, ar_ref, ai_ref, sn_ref):
    half = OCT_STATE // 2
    ar = ar_ref[0]
    ai = ai_ref[0]
    sr = s0_ref[:, 0:half]
    si = s0_ref[:, half:OCT_STATE]
    sn_ref[:, 0:half] = ar * sr - ai * si + sl_ref[:, 0:half]
    sn_ref[:, half:OCT_STATE] = ar * si + ai * sr + sl_ref[:, half:OCT_STATE]


def _ssm_step(s_loc, s0, a_re, a_im):
    rows, width = s_loc.shape
    return pl.pallas_call(
        _ssm_step_kernel,
        grid=(OCTETS,),
        in_specs=[pl.BlockSpec((rows, OCT_STATE), lambda o: (0, o)),
                  pl.BlockSpec((rows, OCT_STATE), lambda o: (0, o)),
                  pl.BlockSpec((1, 1, OCT_STATE // 2), lambda o: (o, 0, 0)),
                  pl.BlockSpec((1, 1, OCT_STATE // 2), lambda o: (o, 0, 0))],
        out_specs=pl.BlockSpec((rows, OCT_STATE), lambda o: (0, o)),
        out_shape=jax.ShapeDtypeStruct((rows, width), F32),
        compiler_params=_cparams(("arbitrary",), 32),
        name="ssm_step",
    )(s_loc, s0, a_re, a_im)


def _pack_state(s_re, s_im):
    b = s_re.shape[0]
    half = OCT_STATE // 2
    return jnp.stack([s_re.reshape(b, OCTETS, half), s_im.reshape(b, OCTETS, half)], axis=2).reshape(b, -1)


def _unpack_state(s):
    b = s.shape[0]
    s4 = s.reshape(b, OCTETS, 2, OCT_STATE // 2)
    return (s4[:, :, 0].reshape(b, N_GROUPS, SSM_STATE), s4[:, :, 1].reshape(b, N_GROUPS, SSM_STATE))


def _mixer_kernel(*refs, seq, tm, tiles_per_batch):
    (x_ref, ys_ref, u_ref, r_ref, g1_ref, sh2_ref, sc2_ref, d_ref, wglu_ref, bglu_ref, wla_ref,
     cw_ref, cb_ref, wlb_ref, wout_ref, n2g_ref, wrh_ref, wrl_ref) = refs[:18]
    if seq:
        x1_ref, h2_ref, h2p_ref, lg_ref, nc_ref, carry = refs[18:]
    else:
        cb0_ref, cb1_ref, x1_ref, h2_ref, h2p_ref, lg_ref, nc0_ref, nc1_ref = refs[18:]

    ya = jax.nn.gelu(ys_ref[...] + d_ref[...] * u_ref[...], approximate=True)
    z = jnp.dot(ya.astype(BF16), wglu_ref[...], preferred_element_type=F32) + bglu_ref[...]
    ya = ya * jax.nn.sigmoid(z)
    branch_a = jnp.dot(ya.astype(BF16), wla_ref[...], preferred_element_type=F32)

    xb = r_ref[:, 0:D_CONV].astype(F32)
    gb = r_ref[:, D_CONV:2 * D_CONV].astype(F32)
    gc = r_ref[:, 2 * D_CONV:3 * D_CONV].astype(F32)
    v = gc * xb
    if seq:
        first = pl.program_id(0) % tiles_per_batch == 0

        @pl.when(first)
        def _():
            carry[...] = jnp.zeros_like(carry)

        c0 = carry[0:1, :]
        c1 = carry[1:2, :]
        row = lax.broadcasted_iota(I32, (tm, 1), 0)
        v1 = jnp.where(row == 0, c1, pltpu.roll(v, 1, 0))
        v2 = jnp.where(row == 0, c0, jnp.where(row == 1, c1, pltpu.roll(v, 2, 0)))
        tail = v[tm - 2:tm, :]
        carry[...] = tail
        nc_ref[0] = tail
    else:
        v2 = cb0_ref[...]
        v1 = cb1_ref[...]
        nc0_ref[...] = v1
        nc1_ref[...] = v
    conv = cb_ref[...] + cw_ref[0:1, :] * v2 + cw_ref[1:2, :] * v1 + cw_ref[2:3, :] * v
    branch_b = jnp.dot((gb * conv).astype(BF16), wlb_ref[...], preferred_element_type=F32)

    off = 3 * D_CONV
    gla = r_ref[:, off:off + D_MODEL].astype(F32)
    glb = r_ref[:, off + D_MODEL:off + 2 * D_MODEL].astype(F32)
    merged = jax.nn.sigmoid(gla) * branch_a + jax.nn.sigmoid(glb) * branch_b
    x1 = x_ref[...] + _mod(g1_ref) * jnp.dot(merged.astype(BF16), wout_ref[...],
                                             preferred_element_type=F32)
    x1_ref[...] = x1

    y = x1 * lax.rsqrt(jnp.mean(x1 * x1, axis=-1, keepdims=True) + RMS_EPS) * n2g_ref[...]
    h2 = y * (1.0 + _mod(sc2_ref)) + _mod(sh2_ref)
    h2b = h2.astype(BF16)
    h2_ref[...] = h2b
    h2p_ref[...] = _pack_halves(h2)

    h2l = (h2 - h2b.astype(F32)).astype(BF16)
    dn = (((1,), (1,)), ((), ()))
    lg_ref[...] = (lax.dot_general(wrh_ref[...], h2b, dn, preferred_element_type=F32)
                   + lax.dot_general(wrh_ref[...], h2l, dn, preferred_element_type=F32)
                   + lax.dot_general(wrl_ref[...], h2b, dn, preferred_element_type=F32))


def _mixer_post(x, y_ssm, u, rest, mods, rows_per_batch, weights, conv_state, tm):
    t = x.shape[0]
    seq = conv_state is None
    tiles_per_batch = rows_per_batch // tm
    (ssm_d, w_glu, b_glu, w_la, conv_w, conv_b, w_lb, w_out, n2g, wr_hi, wr_lo) = weights
    row = lambda m: (m, 0)
    in_specs = [pl.BlockSpec((tm, D_MODEL), row),
                pl.BlockSpec((tm, D_SSM), row),
                pl.BlockSpec((tm, D_SSM), row),
                pl.BlockSpec((tm, rest.shape[1]), row),
                _mod_spec(mods, 2, rows_per_batch, tm),
                _mod_spec(mods, 3, rows_per_batch, tm),
                _mod_spec(mods, 4, rows_per_batch, tm),
                _const_spec((1, D_SSM)), _const_spec((D_SSM, D_SSM)), _const_spec((1, D_SSM)),
                _const_spec((D_SSM, D_MODEL)), _const_spec((CONV_W, D_CONV)), _const_spec((1, D_CONV)),
                _const_spec((D_CONV, D_MODEL)), _const_spec((D_MODEL, D_MODEL)),
                _const_spec((1, D_MODEL)), _const_spec((N_EXPERTS, D_MODEL)),
                _const_spec((N_EXPERTS, D_MODEL))]
    args = [x, y_ssm, u, rest, mods, mods, mods, ssm_d, w_glu, b_glu, w_la, conv_w, conv_b, w_lb,
            w_out, n2g, wr_hi, wr_lo]
    out_specs = [pl.BlockSpec((tm, D_MODEL), row),
                 pl.BlockSpec((tm, D_MODEL), row),
                 pl.BlockSpec((tm, D_MODEL // 2), row),
                 pl.BlockSpec((N_EXPERTS, tm), lambda m: (0, m))]
    out_shape = [jax.ShapeDtypeStruct((t, D_MODEL), F32),
                 jax.ShapeDtypeStruct((t, D_MODEL), BF16),
                 jax.ShapeDtypeStruct((t, D_MODEL // 2), I32),
                 jax.ShapeDtypeStruct((N_EXPERTS, t), F32)]
    scratch = []
    if seq:
        nb = t // rows_per_batch
        out_specs.append(pl.BlockSpec((1, CONV_W - 1, D_CONV), lambda m: (m // tiles_per_batch, 0, 0)))
        out_shape.append(jax.ShapeDtypeStruct((nb, CONV_W - 1, D_CONV), F32))
        scratch.append(pltpu.VMEM((CONV_W - 1, D_CONV), F32))
    else:
        in_specs += [pl.BlockSpec((tm, D_CONV), row), pl.BlockSpec((tm, D_CONV), row)]
        args += [conv_state[:, 0, :], conv_state[:, 1, :]]
        out_specs += [pl.BlockSpec((tm, D_CONV), row), pl.BlockSpec((tm, D_CONV), row)]
        out_shape += [jax.ShapeDtypeStruct((t, D_CONV), F32), jax.ShapeDtypeStruct((t, D_CONV), F32)]
    return pl.pallas_call(
        functools.partial(_mixer_kernel, seq=seq, tm=tm, tiles_per_batch=tiles_per_batch),
        grid=(t // tm,),
        in_specs=in_specs,
        out_specs=out_specs,
        out_shape=out_shape,
        scratch_shapes=scratch,
        compiler_params=_cparams(("arbitrary",), 56),
        name="mixer_post",
    )(*args)


def _route_kernel(lg_ref, bias_ref, te_ref, gate_ref):
    scores = [jax.nn.sigmoid(lg_ref[e]) for e in range(N_EXPERTS)]
    choice = [scores[e] + bias_ref[e] for e in range(N_EXPERTS)]
    shape = scores[0].shape

    def first_max(vals):
        m = functools.reduce(jnp.maximum, vals)
        found = jnp.zeros(shape, jnp.bool_)
        hot = []
        for v in vals:
            is_m = jnp.logical_and(v == m, jnp.logical_not(found))
            hot.append(is_m)
            found = jnp.logical_or(found, is_m)
        return m, hot

    gscore = []
    for g in range(N_EXPERT_GROUPS):
        vals = choice[g * EXPERTS_PER_GROUP:(g + 1) * EXPERTS_PER_GROUP]
        m1, hot = first_max(vals)
        rest = [jnp.where(h, NEG_INF, v) for h, v in zip(hot, vals)]
        gscore.append(m1 + functools.reduce(jnp.maximum, rest))

    gsel = [jnp.zeros(shape, jnp.bool_) for _ in range(N_EXPERT_GROUPS)]
    for _ in range(TOPK_GROUPS):
        _, hot = first_max(gscore)
        gsel = [jnp.logical_or(s, h) for s, h in zip(gsel, hot)]
        gscore = [jnp.where(h, NEG_INF, v) for h, v in zip(hot, gscore)]

    cand = [jnp.where(gsel[e // EXPERTS_PER_GROUP], choice[e], NEG_INF) for e in range(N_EXPERTS)]
    picked_e = []
    picked_s = []
    for _ in range(TOP_K):
        _, hot = first_max(cand)
        idx = jnp.zeros(shape, I32)
        sc = jnp.zeros(shape, F32)
        for e in range(N_EXPERTS):
            idx = jnp.where(hot[e], e, idx)
            sc = jnp.where(hot[e], scores[e], sc)
        cand = [jnp.where(h, NEG_INF, v) for h, v in zip(hot, cand)]
        picked_e.append(idx)
        picked_s.append(sc)
    total = functools.reduce(lambda a, b: a + b, picked_s)
    for k in range(TOP_K):
        te_ref[k] = picked_e[k]
        gate_ref[k] = picked_s[k] / total * ROUTED_SCALE


def _route(logits_t, bias):
    t = logits_t.shape[1]
    r = t // LANES
    rb = min(r, SUBLANES)
    lg = logits_t.reshape(N_EXPERTS, r, LANES)
    bias_b = jnp.broadcast_to(bias.astype(F32).reshape(N_EXPERTS, 1, 1), (N_EXPERTS, 1, LANES))
    te, gate = pl.pallas_call(
        _route_kernel,
        grid=(r // rb,),
        in_specs=[pl.BlockSpec((N_EXPERTS, rb, LANES), lambda i: (0, i, 0)),
                  pl.BlockSpec((N_EXPERTS, 1, LANES), lambda i: (0, 0, 0))],
        out_specs=[pl.BlockSpec((TOP_K, rb, LANES), lambda i: (0, i, 0)),
                   pl.BlockSpec((TOP_K, rb, LANES), lambda i: (0, i, 0))],
        out_shape=[jax.ShapeDtypeStruct((TOP_K, r, LANES), I32),
                   jax.ShapeDtypeStruct((TOP_K, r, LANES), F32)],
        compiler_params=_cparams(("arbitrary",), 32),
        name="route",
    )(lg, bias_b)
    return te.reshape(TOP_K, t), gate.reshape(TOP_K, t)


def _dispatch_kernel(nused_ref, tok_ref, h_ref, xs_ref, tile):
    @pl.when(pl.program_id(0) >= nused_ref[0])
    def _():
        xs_ref[...] = jnp.zeros_like(xs_ref)

    @pl.when(pl.program_id(0) < nused_ref[0])
    def _():
        unroll = 8

        def body(g, _):
            for i in range(unroll):
                mi = g * unroll + i
                tile[pl.ds(mi, TILE_WORDS, stride=ROW_STRIDE), :] = h_ref[tok_ref[mi]]
            return 0

        lax.fori_loop(0, MOE_ROWS // unroll, body, 0)
        half = D_MODEL // 2
        for j in range(TILE_WORDS):
            w = tile[j * ROW_STRIDE:j * ROW_STRIDE + MOE_ROWS, :]
            xs_ref[:, j * LANES:(j + 1) * LANES] = _unpack_lo(w).astype(BF16)
            xs_ref[:, half + j * LANES:half + (j + 1) * LANES] = _unpack_hi(w).astype(BF16)


def _dispatch(h2p, tok_buf, n_used, n_blocks):
    t = h2p.shape[0]
    return pl.pallas_call(
        _dispatch_kernel,
        grid_spec=pltpu.PrefetchScalarGridSpec(
            num_scalar_prefetch=1,
            grid=(n_blocks,),
            in_specs=[pl.BlockSpec((MOE_ROWS,), lambda i, nu: (jnp.minimum(i, nu[0] - 1),),
                                   memory_space=pltpu.SMEM),
                      pl.BlockSpec(memory_space=pltpu.VMEM)],
            out_specs=pl.BlockSpec((MOE_ROWS, D_MODEL), lambda i, nu: (i, 0)),
            scratch_shapes=[pltpu.VMEM((TILE_WORDS * ROW_STRIDE, LANES), I32)]),
        out_shape=jax.ShapeDtypeStruct((n_blocks * MOE_ROWS, D_MODEL), BF16),
        compiler_params=_cparams(("arbitrary",), 56),
        name="dispatch",
    )(n_used, tok_buf, h2p.reshape(t, TILE_WORDS, LANES))


def _experts_kernel(be_ref, nused_ref, xs_ref, wg_ref, wu_ref, wd_ref, ys_ref, wg_s, wu_s, wd_s, tile):
    i = pl.program_id(0)

    @pl.when(i >= nused_ref[0])
    def _():
        ys_ref[...] = jnp.zeros_like(ys_ref)

    @pl.when(i < nused_ref[0])
    def _():
        prev = be_ref[jnp.maximum(i - 1, 0)]

        @pl.when(jnp.logical_or(i == 0, be_ref[i] != prev))
        def _():
            wg_s[...] = wg_ref[...].astype(BF16)
            wu_s[...] = wu_ref[...].astype(BF16)
            wd_s[...] = wd_ref[...].astype(BF16)

        x = xs_ref[...]
        g = jnp.dot(x, wg_s[...], preferred_element_type=F32)
        u = jnp.dot(x, wu_s[...], preferred_element_type=F32)
        h = (_silu(g) * u).astype(BF16)
        y = jnp.dot(h, wd_s[...], preferred_element_type=F32)
        packed = _pack_halves(y)
        for j in range(TILE_WORDS):
            tile[j * ROW_STRIDE:j * ROW_STRIDE + MOE_ROWS, :] = packed[:, j * LANES:(j + 1) * LANES]
        unroll = 8

        def body(gi, _):
            for k in range(unroll):
                mi = gi * unroll + k
                ys_ref[mi] = tile[pl.ds(mi, TILE_WORDS, stride=ROW_STRIDE), :]
            return 0

        lax.fori_loop(0, MOE_ROWS // unroll, body, 0)


def _experts(xs, block_expert, n_used, w_gate, w_up, w_down, n_blocks):
    def xmap(i, be, nu):
        return (jnp.minimum(i, nu[0] - 1), 0)

    def wmap(i, be, nu):
        return (be[jnp.minimum(i, nu[0] - 1)], 0, 0)

    def ymap(i, be, nu):
        return (i, 0, 0)

    return pl.pallas_call(
        _experts_kernel,
        grid_spec=pltpu.PrefetchScalarGridSpec(
            num_scalar_prefetch=2,
            grid=(n_blocks,),
            in_specs=[pl.BlockSpec((MOE_ROWS, D_MODEL), xmap),
                      pl.BlockSpec((None, D_MODEL, D_EXPERT), wmap),
                      pl.BlockSpec((None, D_MODEL, D_EXPERT), wmap),
                      pl.BlockSpec((None, D_EXPERT, D_MODEL), wmap)],
            out_specs=pl.BlockSpec((MOE_ROWS, TILE_WORDS, LANES), ymap),
            scratch_shapes=[pltpu.VMEM((D_MODEL, D_EXPERT), BF16),
                            pltpu.VMEM((D_MODEL, D_EXPERT), BF16),
                            pltpu.VMEM((D_EXPERT, D_MODEL), BF16),
                            pltpu.VMEM((TILE_WORDS * ROW_STRIDE, LANES), I32)]),
        out_shape=jax.ShapeDtypeStruct((n_blocks * MOE_ROWS, TILE_WORDS, LANES), I32),
        compiler_params=_cparams(("arbitrary",), 56),
        name="experts",
    )(block_expert, n_used, xs, w_gate, w_up, w_down)


def _combine_kernel(src_ref, nch_ref, off_ref, ntot_ref, lidx_ref, gate_ref, ys_hbm, out_ref,
                    stage, tile, sem, *, stage_rows):
    i = pl.program_id(0)
    n_tiles = pl.num_programs(0)

    def chunk_copy(src, slot, dst):
        return pltpu.make_async_copy(ys_hbm.at[pl.ds(src, COMB_CH)],
                                     stage.at[pl.ds(slot * stage_rows + dst, COMB_CH)],
                                     sem.at[slot])

    def issue(tile_idx, slot):
        def e_body(e, _):
            src = src_ref[tile_idx, e]
            dst = off_ref[tile_idx, e]

            def c_body(c, _):
                chunk_copy(src + c * COMB_CH, slot, dst + c * COMB_CH).start()
                return 0

            lax.fori_loop(0, nch_ref[tile_idx, e], c_body, 0)
            return 0

        lax.fori_loop(0, N_EXPERTS, e_body, 0)

    @pl.when(i == 0)
    def _():
        issue(0, 0)

    @pl.when(i + 1 < n_tiles)
    def _():
        issue(i + 1, (i + 1) % 2)

    slot = i % 2

    def w_body(c, _):
        chunk_copy(0, slot, 0).wait()
        return 0

    lax.fori_loop(0, ntot_ref[i], w_body, 0)

    base = slot * stage_rows
    stride = COMB_TOK + SUBLANES
    half_rows = TILE_WORDS * stride

    def t_body(tk, _):
        acc_lo = jnp.zeros((TILE_WORDS, LANES), F32)
        acc_hi = jnp.zeros((TILE_WORDS, LANES), F32)
        for k in range(TOP_K):
            w = stage[base + lidx_ref[tk * TOP_K + k]]
            g = gate_ref[tk * TOP_K + k]
            acc_lo = acc_lo + g * _unpack_lo(w)
            acc_hi = acc_hi + g * _unpack_hi(w)
        tile[pl.ds(tk, TILE_WORDS, stride=stride), :] = acc_lo
        tile[pl.ds(half_rows + tk, TILE_WORDS, stride=stride), :] = acc_hi
        return 0

    lax.fori_loop(0, COMB_TOK, t_body, 0)
    half = D_MODEL // 2
    for j in range(TILE_WORDS):
        out_ref[:, j * LANES:(j + 1) * LANES] = tile[j * stride:j * stride + COMB_TOK, :]
        out_ref[:, half + j * LANES:half + (j + 1) * LANES] = (
            tile[half_rows + j * stride:half_rows + j * stride + COMB_TOK, :])


def _combine(ys, src_start, n_chunks, slot_off, n_total, lidx, gate, t, stage_rows):
    n_tiles = t // COMB_TOK
    stride = COMB_TOK + SUBLANES
    return pl.pallas_call(
        functools.partial(_combine_kernel, stage_rows=stage_rows),
        grid_spec=pltpu.PrefetchScalarGridSpec(
            num_scalar_prefetch=4,
            grid=(n_tiles,),
            in_specs=[pl.BlockSpec((COMB_TOK * TOP_K,), lambda i, *_: (i,), memory_space=pltpu.SMEM),
                      pl.BlockSpec((COMB_TOK * TOP_K,), lambda i, *_: (i,), memory_space=pltpu.SMEM),
                      pl.BlockSpec(memory_space=pl.ANY)],
            out_specs=pl.BlockSpec((COMB_TOK, D_MODEL), lambda i, *_: (i, 0)),
            scratch_shapes=[pltpu.VMEM((2 * stage_rows, TILE_WORDS, LANES), I32),
                            pltpu.VMEM((2 * TILE_WORDS * stride, LANES), F32),
                            pltpu.SemaphoreType.DMA((2,))]),
        out_shape=jax.ShapeDtypeStruct((t, D_MODEL), F32),
        compiler_params=_cparams(("arbitrary",), 48),
        name="combine",
    )(src_start, n_chunks, slot_off, n_total, lidx, gate, ys)


def _final_kernel(h_ref, x1_ref, rt_ref, g2_ref, wsg_ref, wsu_ref, wsd_ref, gf_ref, y_ref):
    h = h_ref[...]
    a = _silu(jnp.dot(h, wsg_ref[...], preferred_element_type=F32))
    b = jnp.dot(h, wsu_ref[...], preferred_element_type=F32)
    shared = jnp.dot((a * b).astype(BF16), wsd_ref[...], preferred_element_type=F32)
    xo = x1_ref[...] + _mod(g2_ref) * (shared + rt_ref[...])
    y_ref[...] = xo * lax.rsqrt(jnp.mean(xo * xo, axis=-1, keepdims=True) + RMS_EPS) * gf_ref[...]


def _final(h2, x1, routed, mods, rows_per_batch, w_sg, w_su, w_sd, gf, tm):
    t = x1.shape[0]
    row = lambda m: (m, 0)
    return pl.pallas_call(
        _final_kernel,
        grid=(t // tm,),
        in_specs=[pl.BlockSpec((tm, D_MODEL), row), pl.BlockSpec((tm, D_MODEL), row),
                  pl.BlockSpec((tm, D_MODEL), row), _mod_spec(mods, 5, rows_per_batch, tm),
                  _const_spec((D_MODEL, D_EXPERT)), _const_spec((D_MODEL, D_EXPERT)),
                  _const_spec((D_EXPERT, D_MODEL)), _const_spec((1, D_MODEL))],
        out_specs=pl.BlockSpec((tm, D_MODEL), row),
        out_shape=jax.ShapeDtypeStruct((t, D_MODEL), F32),
        compiler_params=_cparams(("arbitrary",), 48),
        name="final",
    )(h2, x1, routed, mods, w_sg, w_su, w_sd, gf)


def _moe_plan(top_e, t):
    ar = jnp.arange(N_EXPERTS, dtype=I32)
    sel = top_e[:, :, None] == ar
    onehot = jnp.any(sel, axis=1).astype(I32)
    csum = jnp.cumsum(onehot, axis=0)
    excl = csum - onehot
    counts = csum[-1]
    padded = (counts + MOE_ROWS - 1) // MOE_ROWS * MOE_ROWS
    pend = jnp.cumsum(padded)
    pstart = pend - padded
    pick = lambda per_expert: jnp.sum(jnp.where(sel, per_expert, 0), axis=-1)
    rank = pick(excl[:, None, :])
    dest = pick(pstart) + rank
    n_blocks = -(-(t * TOP_K + N_EXPERTS * (MOE_ROWS - 1)) // MOE_ROWS)
    block_expert = jnp.minimum(
        jnp.searchsorted(pend, jnp.arange(n_blocks, dtype=I32) * MOE_ROWS, side="right"),
        N_EXPERTS - 1).astype(I32)
    n_used = (pend[-1] // MOE_ROWS).astype(I32).reshape(1)
    tok = jnp.repeat(jnp.arange(t, dtype=I32), TOP_K)
    tok_buf = jnp.zeros((n_blocks * MOE_ROWS,), I32).at[dest.reshape(-1)].set(tok, unique_indices=True)
    n_tiles = t // COMB_TOK
    lo = excl[::COMB_TOK]
    hi = jnp.concatenate([lo[1:], counts[None]], axis=0)
    n_chunks = (hi - lo + COMB_CH - 1) // COMB_CH
    slot_end = jnp.cumsum(n_chunks * COMB_CH, axis=1)
    slot_off = slot_end - n_chunks * COMB_CH
    src_start = pstart[None, :] + lo
    n_total = jnp.sum(n_chunks, axis=1).astype(I32)
    local = jnp.repeat(slot_off - lo, COMB_TOK, axis=0)
    lidx = pick(local[:, None, :]) + rank
    stage_rows = COMB_TOK * TOP_K + N_EXPERTS * (COMB_CH - 1)
    stage_rows = -(-stage_rows // COMB_CH) * COMB_CH
    return dict(dest=dest, block_expert=block_expert, n_used=n_used, tok_buf=tok_buf,
                n_blocks=n_blocks, src_start=src_start.astype(I32), n_chunks=n_chunks.astype(I32),
                slot_off=slot_off.astype(I32), n_total=n_total, lidx=lidx.astype(I32),
                stage_rows=stage_rows)


def kernel(x_prompt, x_sample, state_ssm_re, state_ssm_im, state_conv, c_prompt, c_sample, w_ada, b_ada, norm1_g, norm2_g, w_in, ssm_a_re, ssm_a_im, ssm_log_dt, ssm_b_re, ssm_b_im, ssm_c_re, ssm_c_im, ssm_d, w_glu, b_glu, w_lift_a, conv_w, conv_b, w_lift_b, w_out, w_router, router_bias, w_exp_gate, w_exp_up, w_exp_down, w_sh_gate, w_sh_up, w_sh_down, final_norm_g):
    depth = w_ada.shape[0]
    assert depth == 1
    bp, seq, _ = x_prompt.shape
    bs = x_sample.shape[0]
    tp = bp * seq
    l = 0

    w_in_b = w_in[l].astype(BF16)
    w_u, w_rest = w_in_b[:, :D_SSM], w_in_b[:, D_SSM:]
    wr_t = w_router[l].T
    wr_hi = wr_t.astype(BF16)
    wr_lo = (wr_t - wr_hi.astype(F32)).astype(BF16)
    mix_w = (ssm_d[l].reshape(1, -1), w_glu[l].astype(BF16), b_glu[l].reshape(1, -1),
             w_lift_a[l].astype(BF16), conv_w[l], conv_b[l].reshape(1, -1), w_lift_b[l].astype(BF16),
             w_out[l].astype(BF16), norm2_g[l].reshape(1, -1), wr_hi, wr_lo)
    g1 = norm1_g[l].reshape(1, -1)

    ada = _ada(jnp.concatenate([c_prompt, c_sample], axis=0), w_ada[l], b_ada[l])
    mods_p = ada[:bp].reshape(bp, 1, -1)
    mods_s = ada[bp:]

    apow, w1, w2, kk = _ssm_prep(ssm_a_re[l], ssm_a_im[l], ssm_log_dt[l], ssm_b_re[l], ssm_b_im[l],
                                 ssm_c_re[l], ssm_c_im[l])
    tm_p, bk_p, ck_p = _assemble_ssm(w1, w2, kk, SSM_CHUNK)
    tm_s, bk_s, ck_s = _assemble_ssm(w1, w2, kk, 1)
    a1_re, a1_im, al_re, al_im = (_state_layout(apow[i]) for i in range(4))

    xp = x_prompt.reshape(tp, D_MODEL)
    u_p, rest_p = _in_proj(xp, mods_p, seq, g1, w_u, w_rest, tm=512)
    n_chunks = seq // SSM_CHUNK
    s_loc = _ssm_state(u_p, bk_p, SSM_CHUNK, n_chunks)
    s_prev, s_fin_p = _ssm_scan(s_loc.reshape(bp, n_chunks, -1), al_re, al_im,
                                jnp.zeros((bp, OCTETS * OCT_STATE), F32))
    y_p = _ssm_out(u_p, s_prev.reshape(bp * n_chunks, -1), tm_p, ck_p, SSM_CHUNK, n_chunks)
    x1_p, h2_p, h2p_p, lg_p, conv_p = _mixer_post(xp, y_p, u_p, rest_p, mods_p, seq, mix_w, None, tm=256)

    xs = x_sample.reshape(bs, D_MODEL)
    u_s, rest_s = _in_proj(xs, mods_s, 1, g1, w_u, w_rest, tm=bs)
    s0_s = _pack_state(state_ssm_re[l], state_ssm_im[l])
    s_new_s = _ssm_step(_ssm_state(u_s, bk_s, 1, bs), s0_s, a1_re, a1_im)
    y_s = _ssm_out(u_s, s0_s, tm_s, ck_s, 1, bs)
    x1_s, h2_s, h2p_s, lg_s, nc0, nc1 = _mixer_post(xs, y_s, u_s, rest_s, mods_s, 1, mix_w,
                                                    state_conv[l], tm=bs)

    t = tp + bs
    te_p, gate_p = _route(lg_p, router_bias[l])
    te_s, gate_s = _route(lg_s, router_bias[l])
    top_e = jnp.concatenate([te_p, te_s], axis=1).T
    gate = jnp.concatenate([gate_p, gate_s], axis=1).T
    plan = _moe_plan(top_e, t)
    h2p = jnp.concatenate([h2p_p, h2p_s], axis=0)
    xs_sorted = _dispatch(h2p, plan["tok_buf"], plan["n_used"], plan["n_blocks"])
    ys = _experts(xs_sorted, plan["block_expert"], plan["n_used"], w_exp_gate[l], w_exp_up[l],
                  w_exp_down[l], plan["n_blocks"])
    routed = _combine(ys, plan["src_start"], plan["n_chunks"], plan["slot_off"], plan["n_total"],
                      plan["lidx"].reshape(-1), gate.reshape(-1), t, plan["stage_rows"])

    w_sg, w_su, w_sd = w_sh_gate[l].astype(BF16), w_sh_up[l].astype(BF16), w_sh_down[l].astype(BF16)
    gf = final_norm_g.reshape(1, -1)
    y_prompt = _final(h2_p, x1_p, routed[:tp], mods_p, seq, w_sg, w_su, w_sd, gf, tm=512)
    y_sample = _final(h2_s, x1_s, routed[tp:], mods_s, 1, w_sg, w_su, w_sd, gf, tm=bs)

    re_p, im_p = _unpack_state(s_fin_p)
    re_s, im_s = _unpack_state(s_new_s)
    return (y_prompt.reshape(bp, seq, D_MODEL), y_sample.reshape(bs, 1, D_MODEL),
            re_p[None], im_p[None], conv_p[None],
            re_s[None], im_s[None], jnp.stack([nc0, nc1], axis=1)[None])
```

```python
import functools

import jax
import jax.numpy as jnp
from jax import lax
from jax.experimental import pallas as pl
from jax.experimental.pallas import tpu as pltpu

F32 = jnp.float32
BF16 = jnp.bfloat16
I32 = jnp.int32

D_MODEL = 2048
D_SSM = 1024
SSM_GROUP = 16
N_GROUPS = 64
SSM_STATE = 64
D_CONV = 1024
CONV_W = 3
N_EXPERTS = 64
TOP_K = 8
N_EXPERT_GROUPS = 8
EXPERTS_PER_GROUP = 8
TOPK_GROUPS = 4
D_EXPERT = 512
ROUTED_SCALE = 2.5
RMS_EPS = 1e-6

LANES = 128
SUBLANES = 8
OCTETS = N_GROUPS * SSM_GROUP // LANES
OCT_STATE = 2 * (N_GROUPS // OCTETS) * SSM_STATE
SSM_CHUNK = 8
MOE_ROWS = 256
COMB_TOK = 128
COMB_CH = 16
ROW_TILES = D_MODEL // LANES
ROW_STRIDE = MOE_ROWS + SUBLANES
DISP_ROWS = 2 * MOE_ROWS
DISP_STRIDE = DISP_ROWS + SUBLANES
COMB_MAX_CHUNKS = COMB_TOK * TOP_K // COMB_CH + N_EXPERTS
TOKEN_BITS = 14
NEG_INF = float("-inf")


def _cparams(sem, vmem_mb):
    return pltpu.CompilerParams(dimension_semantics=sem, vmem_limit_bytes=vmem_mb * 1024 * 1024)


def _const_spec(shape):
    nd = len(shape)
    return pl.BlockSpec(shape, lambda *_: (0,) * nd, pipeline_mode=pl.Buffered(1))


def _mod(ref):
    v = ref[...]
    return v[0] if v.ndim == 3 else v


def _silu(x):
    return x * jax.nn.sigmoid(x)


def _ada_kernel(c_ref, w_ref, b_ref, o_ref):
    a = _silu(c_ref[...]).astype(BF16)
    o_ref[...] = jnp.dot(a, w_ref[...].astype(BF16), preferred_element_type=F32) + b_ref[...]


def _ada(c_all, w_ada, b_ada):
    m = c_all.shape[0]
    n = w_ada.shape[1]
    tn = 1024
    return pl.pallas_call(
        _ada_kernel,
        grid=(n // tn,),
        in_specs=[pl.BlockSpec((m, D_MODEL), lambda j: (0, 0)),
                  pl.BlockSpec((D_MODEL, tn), lambda j: (0, j)),
                  pl.BlockSpec((1, tn), lambda j: (0, j))],
        out_specs=pl.BlockSpec((m, tn), lambda j: (0, j)),
        out_shape=jax.ShapeDtypeStruct((m, n), F32),
        compiler_params=_cparams(("arbitrary",), 40),
        name="ada",
    )(c_all, w_ada, b_ada.reshape(1, n))


def _mod_spec(mods, col, rows_per_batch, tm):
    if mods.ndim == 3:
        tiles_per_batch = rows_per_batch // tm
        return pl.BlockSpec((1, 1, D_MODEL), lambda m, *_: (m // tiles_per_batch, 0, col))
    return pl.BlockSpec((tm, D_MODEL), lambda m, *_: (m, col))


def _inproj_kernel(x_ref, g_ref, sh_ref, sc_ref, wu_ref, wr_ref, u_ref, r_ref, h_scr):
    @pl.when(pl.program_id(1) == 0)
    def _():
        x = x_ref[...]
        y = x * lax.rsqrt(jnp.mean(x * x, axis=-1, keepdims=True) + RMS_EPS) * g_ref[...]
        h = (y * (1.0 + _mod(sc_ref)) + _mod(sh_ref)).astype(BF16)
        h_scr[...] = h
        u_ref[...] = jnp.dot(h, wu_ref[...], preferred_element_type=F32)

    r_ref[...] = jnp.dot(h_scr[...], wr_ref[...], preferred_element_type=F32).astype(BF16)


def _in_proj(x, mods, rows_per_batch, g1, w_in, tm):
    t = x.shape[0]
    tn = D_SSM
    n_rest = w_in.shape[1] - tn
    return pl.pallas_call(
        _inproj_kernel,
        grid=(t // tm, n_rest // tn),
        in_specs=[pl.BlockSpec((tm, D_MODEL), lambda m, n: (m, 0)),
                  pl.BlockSpec((1, D_MODEL), lambda m, n: (0, 0)),
                  _mod_spec(mods, 0, rows_per_batch, tm),
                  _mod_spec(mods, 1, rows_per_batch, tm),
                  pl.BlockSpec((D_MODEL, D_SSM), lambda m, n: (0, 0)),
                  pl.BlockSpec((D_MODEL, tn), lambda m, n: (0, n + 1))],
        out_specs=[pl.BlockSpec((tm, D_SSM), lambda m, n: (m, 0)),
                   pl.BlockSpec((tm, tn), lambda m, n: (m, n))],
        out_shape=[jax.ShapeDtypeStruct((t, D_SSM), F32),
                   jax.ShapeDtypeStruct((t, n_rest), BF16)],
        scratch_shapes=[pltpu.VMEM((tm, D_MODEL), BF16)],
        compiler_params=_cparams(("arbitrary", "arbitrary"), 48),
        name="in_proj",
    )(x, g1, mods, mods, w_in, w_in)


def _ssm_prep_kernel(are_ref, aim_ref, ldt_ref, btr_ref, bti_ref, cre_ref, cim_ref,
                     apow_ref, tm_ref, bk_ref, ck_ref):
    lc = SSM_CHUNK
    ng = N_GROUPS // OCTETS
    half = OCT_STATE // 2
    lr = are_ref[...]
    li = aim_ref[...]
    dt = jnp.exp(ldt_ref[...])
    mag = jnp.exp(lr * dt)
    ar = mag * jnp.cos(li * dt)
    ai = mag * jnp.sin(li * dt)
    den = lr * lr + li * li
    xr = ar - 1.0
    cr = (xr * lr + ai * li) / den
    ci = (ai * lr - xr * li) / den
    btr = btr_ref[...]
    bti = bti_ref[...]
    bbr = cr * btr - ci * bti
    bbi = cr * bti + ci * btr
    cre = cre_ref[...]
    cim = cim_ref[...]
    cre_w = jnp.concatenate([cre] * ng, axis=1)
    cim_w = jnp.concatenate([cim] * ng, axis=1)

    def iota(shape, dim):
        return lax.broadcasted_iota(I32, shape, dim)

    sel_b = ((iota((SSM_STATE, half), 1) & (SSM_STATE - 1)) == iota((SSM_STATE, half), 0)).astype(BF16)
    mask_b = (iota((LANES, half), 0) >> 4) == (iota((LANES, half), 1) >> 6)
    sel_c = ((iota((half, SSM_STATE), 0) & (SSM_STATE - 1)) == iota((half, SSM_STATE), 1)).astype(BF16)
    mask_c = (iota((half, LANES), 0) >> 6) == (iota((half, LANES), 1) >> 4)
    mask_t = (iota((LANES, LANES), 0) >> 4) == (iota((LANES, LANES), 1) >> 4)
    dn_batch = (((2,), (2,)), ((0,), (0,)))
    dn_nt = (((1,), (1,)), ((), ()))

    pr = jnp.ones_like(ar)
    pi = jnp.zeros_like(ar)
    t_blocks = []
    for l in range(lc):
        w1r = pr * bbr - pi * bbi
        w1i = pr * bbi + pi * bbr
        s = lc - 1 - l
        for c, w in ((0, w1r), (1, w1i)):
            rep = jnp.dot(w.reshape(LANES, SSM_STATE).astype(BF16), sel_b, preferred_element_type=F32)
            bk_ref[s * LANES:(s + 1) * LANES, c * half:(c + 1) * half] = (
                jnp.where(mask_b, rep, 0.0).astype(BF16))
        kt = (lax.dot_general(w1r, cre_w, dn_batch, precision=lax.Precision.HIGHEST,
                              preferred_element_type=F32)
              - lax.dot_general(w1i, cim_w, dn_batch, precision=lax.Precision.HIGHEST,
                                preferred_element_type=F32))
        t_blocks.append(jnp.where(mask_t, kt.reshape(LANES, LANES), 0.0).astype(BF16))
        pr, pi = pr * ar - pi * ai, pr * ai + pi * ar
        w2r = cre * pr - cim * pi
        w2i = cre * pi + cim * pr
        for c, w in ((0, w2r), (1, -w2i)):
            rep = lax.dot_general(sel_c, w.reshape(LANES, SSM_STATE).astype(BF16), dn_nt,
                                  preferred_element_type=F32)
            ck_ref[c * half:(c + 1) * half, l * LANES:(l + 1) * LANES] = (
                jnp.where(mask_c, rep, 0.0).astype(BF16))
        if l == 0:
            apow_ref[0] = pr
            apow_ref[1] = pi
    apow_ref[2] = pr
    apow_ref[3] = pi
    zero = jnp.zeros((LANES, LANES), BF16)
    for s in range(lc):
        for t in range(lc):
            tm_ref[s * LANES:(s + 1) * LANES, t * LANES:(t + 1) * LANES] = t_blocks[t - s] if t >= s else zero


def _ssm_prep(a_re, a_im, log_dt, b_re, b_im, c_re, c_im):
    g, p, h = N_GROUPS, SSM_STATE, SSM_GROUP
    ng = g // OCTETS
    width = SSM_CHUNK * LANES
    vec = pl.BlockSpec((ng, 1, p), lambda o: (o, 0, 0))
    mat = pl.BlockSpec((ng, h, p), lambda o: (o, 0, 0))
    big = lambda rows, cols: pl.BlockSpec((None, rows, cols), lambda o: (o, 0, 0))
    return pl.pallas_call(
        _ssm_prep_kernel,
        grid=(OCTETS,),
        in_specs=[vec, vec, pl.BlockSpec((ng, 1, 1), lambda o: (o, 0, 0)), mat, mat, mat, mat],
        out_specs=[pl.BlockSpec((4, ng, 1, p), lambda o: (0, o, 0, 0)),
                   big(width, width), big(width, OCT_STATE), big(OCT_STATE, width)],
        out_shape=[jax.ShapeDtypeStruct((4, g, 1, p), F32),
                   jax.ShapeDtypeStruct((OCTETS, width, width), BF16),
                   jax.ShapeDtypeStruct((OCTETS, width, OCT_STATE), BF16),
                   jax.ShapeDtypeStruct((OCTETS, OCT_STATE, width), BF16)],
        compiler_params=_cparams(("arbitrary",), 32),
        name="ssm_prep",
    )(a_re.reshape(g, 1, p), a_im.reshape(g, 1, p), log_dt.reshape(g, 1, 1),
      b_re.transpose(0, 2, 1), b_im.transpose(0, 2, 1), c_re, c_im)


def _state_layout(v):
    return v.reshape(OCTETS, 1, OCT_STATE // 2)


def _load_chunks(u_ref, lc, rb):
    parts = [u_ref[pl.ds(s, rb, stride=lc), :] if lc > 1 else u_ref[...] for s in range(lc)]
    u = parts[0] if lc == 1 else jnp.concatenate(parts, axis=-1)
    return u.astype(BF16)


def _ssm_state_kernel(u_ref, bk_ref, s_ref, *, lc, rb):
    s_ref[...] = jnp.dot(_load_chunks(u_ref, lc, rb), bk_ref[...], preferred_element_type=F32)


def _ssm_state(u, bk, lc, rb):
    t = u.shape[0]
    nb = t // (rb * lc)
    lag0_block = SSM_CHUNK // lc - 1
    return pl.pallas_call(
        functools.partial(_ssm_state_kernel, lc=lc, rb=rb),
        grid=(OCTETS, nb),
        in_specs=[pl.BlockSpec((rb * lc, LANES), lambda o, b: (b, o)),
                  pl.BlockSpec((None, lc * LANES, OCT_STATE), lambda o, b: (o, lag0_block, 0))],
        out_specs=pl.BlockSpec((rb, OCT_STATE), lambda o, b: (b, o)),
        out_shape=jax.ShapeDtypeStruct((t // lc, OCTETS * OCT_STATE), F32),
        compiler_params=_cparams(("arbitrary", "arbitrary"), 32),
        name="ssm_state",
    )(u, bk)


def _ssm_scan_kernel(sl_ref, ar_ref, ai_ref, s0_ref, sp_ref, sf_ref, *, nb, nc):
    half = OCT_STATE // 2
    ar = ar_ref[0]
    ai = ai_ref[0]
    init = tuple((s0_ref[b:b + 1, 0:half], s0_ref[b:b + 1, half:OCT_STATE]) for b in range(nb))

    def body(c, carry):
        new = []
        for b in range(nb):
            sr, si = carry[b]
            sp_ref[b, pl.ds(c, 1), 0:half] = sr
            sp_ref[b, pl.ds(c, 1), half:OCT_STATE] = si
            xr = sl_ref[b, pl.ds(c, 1), 0:half]
            xi = sl_ref[b, pl.ds(c, 1), half:OCT_STATE]
            new.append((ar * sr - ai * si + xr, ar * si + ai * sr + xi))
        return tuple(new)

    fin = lax.fori_loop(0, nc, body, init)
    for b in range(nb):
        sf_ref[b:b + 1, 0:half] = fin[b][0]
        sf_ref[b:b + 1, half:OCT_STATE] = fin[b][1]


def _ssm_scan(s_loc, al_re, al_im, s0):
    nb, nc, width = s_loc.shape
    return pl.pallas_call(
        functools.partial(_ssm_scan_kernel, nb=nb, nc=nc),
        grid=(OCTETS,),
        in_specs=[pl.BlockSpec((nb, nc, OCT_STATE), lambda o: (0, 0, o)),
                  pl.BlockSpec((1, 1, OCT_STATE // 2), lambda o: (o, 0, 0)),
                  pl.BlockSpec((1, 1, OCT_STATE // 2), lambda o: (o, 0, 0)),
                  pl.BlockSpec((nb, OCT_STATE), lambda o: (0, o))],
        out_specs=[pl.BlockSpec((nb, nc, OCT_STATE), lambda o: (0, 0, o)),
                   pl.BlockSpec((nb, OCT_STATE), lambda o: (0, o))],
        out_shape=[jax.ShapeDtypeStruct((nb, nc, width), F32),
                   jax.ShapeDtypeStruct((nb, width), F32)],
        compiler_params=_cparams(("arbitrary",), 32),
        name="ssm_scan",
    )(s_loc, al_re, al_im, s0)


def _ssm_out_kernel(u_ref, sp_ref, tm_ref, ck_ref, y_ref, *, lc, rb):
    y = (jnp.dot(_load_chunks(u_ref, lc, rb), tm_ref[...], preferred_element_type=F32)
         + jnp.dot(sp_ref[...].astype(BF16), ck_ref[...], preferred_element_type=F32))
    if lc == 1:
        y_ref[...] = y
    else:
        for t in range(lc):
            y_ref[pl.ds(t, rb, stride=lc), :] = y[:, t * LANES:(t + 1) * LANES]


def _ssm_out(u, s_prev, tm, ck, lc, rb):
    t = u.shape[0]
    nb = t // (rb * lc)
    return pl.pallas_call(
        functools.partial(_ssm_out_kernel, lc=lc, rb=rb),
        grid=(OCTETS, nb),
        in_specs=[pl.BlockSpec((rb * lc, LANES), lambda o, b: (b, o)),
                  pl.BlockSpec((rb, OCT_STATE), lambda o, b: (b, o)),
                  pl.BlockSpec((None, lc * LANES, lc * LANES), lambda o, b: (o, 0, 0)),
                  pl.BlockSpec((None, OCT_STATE, lc * LANES), lambda o, b: (o, 0, 0))],
        out_specs=pl.BlockSpec((rb * lc, LANES), lambda o, b: (b, o)),
        out_shape=jax.ShapeDtypeStruct((t, D_SSM), F32),
        compiler_params=_cparams(("arbitrary", "arbitrary"), 32),
        name="ssm_out",
    )(u, s_prev, tm, ck)


def _ssm_step_kernel(sl_ref, s0_ref, ar_ref, ai_ref, sn_ref):
    half = OCT_STATE // 2
    ar = ar_ref[0]
    ai = ai_ref[0]
    sr = s0_ref[:, 0:half]
    si = s0_ref[:, half:OCT_STATE]
    sn_ref[:, 0:half] = ar * sr - ai * si + sl_ref[:, 0:half]
    sn_ref[:, half:OCT_STATE] = ar * si + ai * sr + sl_ref[:, half:OCT_STATE]


def _ssm_step(s_loc, s0, a_re, a_im):
    rows, width = s_loc.shape
    return pl.pallas_call(
        _ssm_step_kernel,
        grid=(OCTETS,),
        in_specs=[pl.BlockSpec((rows, OCT_STATE), lambda o: (0, o)),
                  pl.BlockSpec((rows, OCT_STATE), lambda o: (0, o)),
                  pl.BlockSpec((1, 1, OCT_STATE // 2), lambda o: (o, 0, 0)),
                  pl.BlockSpec((1, 1, OCT_STATE // 2), lambda o: (o, 0, 0))],
        out_specs=pl.BlockSpec((rows, OCT_STATE), lambda o: (0, o)),
        out_shape=jax.ShapeDtypeStruct((rows, width), F32),
        compiler_params=_cparams(("arbitrary",), 32),
        name="ssm_step",
    )(s_loc, s0, a_re, a_im)


def _pack_state(s_re, s_im):
    b = s_re.shape[0]
    half = OCT_STATE // 2
    return jnp.stack([s_re.reshape(b, OCTETS, half), s_im.reshape(b, OCTETS, half)], axis=2).reshape(b, -1)


def _unpack_state(s):
    b = s.shape[0]
    s4 = s.reshape(b, OCTETS, 2, OCT_STATE // 2)
    return (s4[:, :, 0].reshape(b, N_GROUPS, SSM_STATE), s4[:, :, 1].reshape(b, N_GROUPS, SSM_STATE))


def _mixer_kernel(*refs, seq, tm, tiles_per_batch):
    (x_ref, ys_ref, u_ref, r_ref, g1_ref, sh2_ref, sc2_ref, d_ref, wglu_ref, bglu_ref, wla_ref,
     cw_ref, cb_ref, wlb_ref, wout_ref, n2g_ref, wrh_ref, wrl_ref) = refs[:18]
    if seq:
        x1_ref, h2_ref, lg_ref, nc_ref, carry = refs[18:]
    else:
        cb0_ref, cb1_ref, x1_ref, h2_ref, lg_ref, nc0_ref, nc1_ref = refs[18:]

    ya = jax.nn.gelu(ys_ref[...] + d_ref[...] * u_ref[...], approximate=True)
    z = jnp.dot(ya.astype(BF16), wglu_ref[...], preferred_element_type=F32) + bglu_ref[...]
    ya = ya * jax.nn.sigmoid(z)
    branch_a = jnp.dot(ya.astype(BF16), wla_ref[...], preferred_element_type=F32)

    xb = r_ref[:, 0:D_CONV].astype(F32)
    gb = r_ref[:, D_CONV:2 * D_CONV].astype(F32)
    gc = r_ref[:, 2 * D_CONV:3 * D_CONV].astype(F32)
    v = gc * xb
    if seq:
        first = pl.program_id(0) % tiles_per_batch == 0

        @pl.when(first)
        def _():
            carry[...] = jnp.zeros_like(carry)

        c0 = carry[0:1, :]
        c1 = carry[1:2, :]
        row = lax.broadcasted_iota(I32, (tm, 1), 0)
        v1 = jnp.where(row == 0, c1, pltpu.roll(v, 1, 0))
        v2 = jnp.where(row == 0, c0, jnp.where(row == 1, c1, pltpu.roll(v, 2, 0)))
        tail = v[tm - 2:tm, :]
        carry[...] = tail
        nc_ref[0] = tail
    else:
        v2 = cb0_ref[...]
        v1 = cb1_ref[...]
        nc0_ref[...] = v1
        nc1_ref[...] = v
    conv = cb_ref[...] + cw_ref[0:1, :] * v2 + cw_ref[1:2, :] * v1 + cw_ref[2:3, :] * v
    branch_b = jnp.dot((gb * conv).astype(BF16), wlb_ref[...], preferred_element_type=F32)

    off = 3 * D_CONV
    gla = r_ref[:, off:off + D_MODEL].astype(F32)
    glb = r_ref[:, off + D_MODEL:off + 2 * D_MODEL].astype(F32)
    merged = jax.nn.sigmoid(gla) * branch_a + jax.nn.sigmoid(glb) * branch_b
    x1 = x_ref[...] + _mod(g1_ref) * jnp.dot(merged.astype(BF16), wout_ref[...],
                                             preferred_element_type=F32)
    x1_ref[...] = x1

    y = x1 * lax.rsqrt(jnp.mean(x1 * x1, axis=-1, keepdims=True) + RMS_EPS) * n2g_ref[...]
    h2 = y * (1.0 + _mod(sc2_ref)) + _mod(sh2_ref)
    h2b = h2.astype(BF16)
    h2_ref[...] = h2b

    h2l = (h2 - h2b.astype(F32)).astype(BF16)
    dn = (((1,), (1,)), ((), ()))
    lg_ref[...] = (lax.dot_general(wrh_ref[...], h2b, dn, preferred_element_type=F32)
                   + lax.dot_general(wrh_ref[...], h2l, dn, preferred_element_type=F32)
                   + lax.dot_general(wrl_ref[...], h2b, dn, preferred_element_type=F32))


def _mixer_post(x, y_ssm, u, rest, mods, rows_per_batch, weights, conv_state, tm):
    t = x.shape[0]
    seq = conv_state is None
    tiles_per_batch = rows_per_batch // tm
    (ssm_d, w_glu, b_glu, w_la, conv_w, conv_b, w_lb, w_out, n2g, wr_hi, wr_lo) = weights
    row = lambda m: (m, 0)
    in_specs = [pl.BlockSpec((tm, D_MODEL), row),
                pl.BlockSpec((tm, D_SSM), row),
                pl.BlockSpec((tm, D_SSM), row),
                pl.BlockSpec((tm, rest.shape[1]), row),
                _mod_spec(mods, 2, rows_per_batch, tm),
                _mod_spec(mods, 3, rows_per_batch, tm),
                _mod_spec(mods, 4, rows_per_batch, tm),
                _const_spec((1, D_SSM)), _const_spec((D_SSM, D_SSM)), _const_spec((1, D_SSM)),
                _const_spec((D_SSM, D_MODEL)), _const_spec((CONV_W, D_CONV)), _const_spec((1, D_CONV)),
                _const_spec((D_CONV, D_MODEL)), _const_spec((D_MODEL, D_MODEL)),
                _const_spec((1, D_MODEL)), _const_spec((N_EXPERTS, D_MODEL)),
                _const_spec((N_EXPERTS, D_MODEL))]
    args = [x, y_ssm, u, rest, mods, mods, mods, ssm_d, w_glu, b_glu, w_la, conv_w, conv_b, w_lb,
            w_out, n2g, wr_hi, wr_lo]
    out_specs = [pl.BlockSpec((tm, D_MODEL), row),
                 pl.BlockSpec((tm, D_MODEL), row),
                 pl.BlockSpec((N_EXPERTS, tm), lambda m: (0, m))]
    out_shape = [jax.ShapeDtypeStruct((t, D_MODEL), F32),
                 jax.ShapeDtypeStruct((t, D_MODEL), BF16),
                 jax.ShapeDtypeStruct((N_EXPERTS, t), F32)]
    scratch = []
    if seq:
        nb = t // rows_per_batch
        out_specs.append(pl.BlockSpec((1, CONV_W - 1, D_CONV), lambda m: (m // tiles_per_batch, 0, 0)))
        out_shape.append(jax.ShapeDtypeStruct((nb, CONV_W - 1, D_CONV), F32))
        scratch.append(pltpu.VMEM((CONV_W - 1, D_CONV), F32))
    else:
        in_specs += [pl.BlockSpec((tm, D_CONV), row), pl.BlockSpec((tm, D_CONV), row)]
        args += [conv_state[:, 0, :], conv_state[:, 1, :]]
        out_specs += [pl.BlockSpec((tm, D_CONV), row), pl.BlockSpec((tm, D_CONV), row)]
        out_shape += [jax.ShapeDtypeStruct((t, D_CONV), F32), jax.ShapeDtypeStruct((t, D_CONV), F32)]
    return pl.pallas_call(
        functools.partial(_mixer_kernel, seq=seq, tm=tm, tiles_per_batch=tiles_per_batch),
        grid=(t // tm,),
        in_specs=in_specs,
        out_specs=out_specs,
        out_shape=out_shape,
        scratch_shapes=scratch,
        compiler_params=_cparams(("arbitrary",), 56),
        name="mixer_post",
    )(*args)


def _route_kernel(lg_ref, bias_ref, te_ref, gate_ref):
    scores = [jax.nn.sigmoid(lg_ref[e]) for e in range(N_EXPERTS)]
    choice = [scores[e] + bias_ref[e] for e in range(N_EXPERTS)]
    shape = scores[0].shape

    def first_max(vals):
        m = functools.reduce(jnp.maximum, vals)
        found = jnp.zeros(shape, jnp.bool_)
        hot = []
        for v in vals:
            is_m = jnp.logical_and(v == m, jnp.logical_not(found))
            hot.append(is_m)
            found = jnp.logical_or(found, is_m)
        return m, hot

    gscore = []
    for g in range(N_EXPERT_GROUPS):
        vals = choice[g * EXPERTS_PER_GROUP:(g + 1) * EXPERTS_PER_GROUP]
        m1, hot = first_max(vals)
        rest = [jnp.where(h, NEG_INF, v) for h, v in zip(hot, vals)]
        gscore.append(m1 + functools.reduce(jnp.maximum, rest))

    gsel = [jnp.zeros(shape, jnp.bool_) for _ in range(N_EXPERT_GROUPS)]
    for _ in range(TOPK_GROUPS):
        _, hot = first_max(gscore)
        gsel = [jnp.logical_or(s, h) for s, h in zip(gsel, hot)]
        gscore = [jnp.where(h, NEG_INF, v) for h, v in zip(hot, gscore)]

    cand = [jnp.where(gsel[e // EXPERTS_PER_GROUP], choice[e], NEG_INF) for e in range(N_EXPERTS)]
    picked_e = []
    picked_s = []
    for _ in range(TOP_K):
        _, hot = first_max(cand)
        idx = jnp.zeros(shape, I32)
        sc = jnp.zeros(shape, F32)
        for e in range(N_EXPERTS):
            idx = jnp.where(hot[e], e, idx)
            sc = jnp.where(hot[e], scores[e], sc)
        cand = [jnp.where(h, NEG_INF, v) for h, v in zip(hot, cand)]
        picked_e.append(idx)
        picked_s.append(sc)
    total = functools.reduce(lambda a, b: a + b, picked_s)
    for k in range(TOP_K):
        te_ref[k] = picked_e[k]
        gate_ref[k] = picked_s[k] / total * ROUTED_SCALE


def _route(logits_t, bias):
    t = logits_t.shape[1]
    r = t // LANES
    rb = min(r, SUBLANES)
    lg = logits_t.reshape(N_EXPERTS, r, LANES)
    bias_b = jnp.broadcast_to(bias.astype(F32).reshape(N_EXPERTS, 1, 1), (N_EXPERTS, 1, LANES))
    te, gate = pl.pallas_call(
        _route_kernel,
        grid=(r // rb,),
        in_specs=[pl.BlockSpec((N_EXPERTS, rb, LANES), lambda i: (0, i, 0)),
                  pl.BlockSpec((N_EXPERTS, 1, LANES), lambda i: (0, 0, 0))],
        out_specs=[pl.BlockSpec((TOP_K, rb, LANES), lambda i: (0, i, 0)),
                   pl.BlockSpec((TOP_K, rb, LANES), lambda i: (0, i, 0))],
        out_shape=[jax.ShapeDtypeStruct((TOP_K, r, LANES), I32),
                   jax.ShapeDtypeStruct((TOP_K, r, LANES), F32)],
        compiler_params=_cparams(("arbitrary",), 32),
        name="route",
    )(lg, bias_b)
    return te.reshape(TOP_K, t), gate.reshape(TOP_K, t)


def _dispatch_kernel(nused_ref, tok_ref, h_ref, xs_ref, tile):
    used_rows = nused_ref[0] * MOE_ROWS
    live = pl.program_id(0) * DISP_ROWS < used_rows

    @pl.when(jnp.logical_not(live))
    def _():
        xs_ref[...] = jnp.zeros_like(xs_ref)

    @pl.when(live)
    def _():
        unroll = 8

        def body(g, _):
            for i in range(unroll):
                mi = g * unroll + i
                tile[pl.ds(mi, ROW_TILES, stride=DISP_STRIDE), :] = h_ref[tok_ref[mi]].astype(F32)
            return 0

        lax.fori_loop(0, DISP_ROWS // unroll, body, 0)
        for j in range(ROW_TILES):
            xs_ref[:, j * LANES:(j + 1) * LANES] = (
                tile[j * DISP_STRIDE:j * DISP_STRIDE + DISP_ROWS, :].astype(BF16))


def _dispatch(h_tiles, tok_buf, n_used, n_blocks):
    n_rows = n_blocks * MOE_ROWS
    assert n_rows % DISP_ROWS == 0

    def last_live(i, nu):
        return jnp.minimum(i, (nu[0] * MOE_ROWS - 1) // DISP_ROWS)

    return pl.pallas_call(
        _dispatch_kernel,
        grid_spec=pltpu.PrefetchScalarGridSpec(
            num_scalar_prefetch=1,
            grid=(n_rows // DISP_ROWS,),
            in_specs=[pl.BlockSpec((DISP_ROWS,), lambda i, nu: (last_live(i, nu),),
                                   memory_space=pltpu.SMEM),
                      pl.BlockSpec(memory_space=pltpu.VMEM)],
            out_specs=pl.BlockSpec((DISP_ROWS, D_MODEL), lambda i, nu: (i, 0)),
            scratch_shapes=[pltpu.VMEM((ROW_TILES * DISP_STRIDE, LANES), F32)]),
        out_shape=jax.ShapeDtypeStruct((n_rows, D_MODEL), BF16),
        compiler_params=_cparams(("arbitrary",), 56),
        name="dispatch",
    )(n_used, tok_buf, h_tiles)


def _experts_kernel(be_ref, nused_ref, first_ref, next_ref, xs_ref, wg_hbm, wu_hbm, wd_hbm, ys_ref,
                    wg_f, wu_f, wd_f, wg_s, wu_s, wd_s, tile, sem):
    i = pl.program_id(0)

    def weight_copies(e):
        return (pltpu.make_async_copy(wg_hbm.at[e], wg_f, sem.at[0]),
                pltpu.make_async_copy(wu_hbm.at[e], wu_f, sem.at[1]),
                pltpu.make_async_copy(wd_hbm.at[e], wd_f, sem.at[2]))

    @pl.when(i >= nused_ref[0])
    def _():
        ys_ref[...] = jnp.zeros_like(ys_ref)

    @pl.when(i < nused_ref[0])
    def _():
        @pl.when(i == 0)
        def _():
            for c in weight_copies(be_ref[0]):
                c.start()

        @pl.when(first_ref[i] == 1)
        def _():
            for c in weight_copies(be_ref[i]):
                c.wait()
            wg_s[...] = wg_f[...].astype(BF16)
            wu_s[...] = wu_f[...].astype(BF16)
            wd_s[...] = wd_f[...].astype(BF16)

            @pl.when(next_ref[i] >= 0)
            def _():
                for c in weight_copies(next_ref[i]):
                    c.start()

        x = xs_ref[...]
        g = jnp.dot(x, wg_s[...], preferred_element_type=F32)
        u = jnp.dot(x, wu_s[...], preferred_element_type=F32)
        h = (_silu(g) * u).astype(BF16)
        y = jnp.dot(h, wd_s[...], preferred_element_type=F32)
        for j in range(ROW_TILES):
            tile[j * ROW_STRIDE:j * ROW_STRIDE + MOE_ROWS, :] = y[:, j * LANES:(j + 1) * LANES]
        unroll = 8

        def body(gi, _):
            for k in range(unroll):
                mi = gi * unroll + k
                ys_ref[mi] = tile[pl.ds(mi, ROW_TILES, stride=ROW_STRIDE), :].astype(BF16)
            return 0

        lax.fori_loop(0, MOE_ROWS // unroll, body, 0)


def _experts(xs, block_expert, n_used, first, next_expert, w_gate, w_up, w_down, n_blocks):
    def xmap(i, be, nu, fi, nx):
        return (jnp.minimum(i, nu[0] - 1), 0)

    any_spec = pl.BlockSpec(memory_space=pl.ANY)
    return pl.pallas_call(
        _experts_kernel,
        grid_spec=pltpu.PrefetchScalarGridSpec(
            num_scalar_prefetch=4,
            grid=(n_blocks,),
            in_specs=[pl.BlockSpec((MOE_ROWS, D_MODEL), xmap), any_spec, any_spec, any_spec],
            out_specs=pl.BlockSpec((MOE_ROWS, ROW_TILES, LANES), lambda i, *_: (i, 0, 0)),
            scratch_shapes=[pltpu.VMEM((D_MODEL, D_EXPERT), F32),
                            pltpu.VMEM((D_MODEL, D_EXPERT), F32),
                            pltpu.VMEM((D_EXPERT, D_MODEL), F32),
                            pltpu.VMEM((D_MODEL, D_EXPERT), BF16),
                            pltpu.VMEM((D_MODEL, D_EXPERT), BF16),
                            pltpu.VMEM((D_EXPERT, D_MODEL), BF16),
                            pltpu.VMEM((ROW_TILES * ROW_STRIDE, LANES), F32),
                            pltpu.SemaphoreType.DMA((3,))]),
        out_shape=jax.ShapeDtypeStruct((n_blocks * MOE_ROWS, ROW_TILES, LANES), BF16),
        compiler_params=_cparams(("arbitrary",), 48),
        name="experts",
    )(block_expert, n_used, first, next_expert, xs, w_gate, w_up, w_down)


def _combine_kernel(src_ref, ntot_ref, lidx_ref, gate_ref, ys_hbm, out_ref,
                    stage, tile, sem, *, stage_rows):
    i = pl.program_id(0)
    n_tiles = pl.num_programs(0)

    def chunk_copy(src, slot, dst):
        return pltpu.make_async_copy(ys_hbm.at[pl.ds(src, COMB_CH)],
                                     stage.at[pl.ds(slot * stage_rows + dst, COMB_CH)],
                                     sem.at[slot])

    def issue(tile_idx, slot):
        def c_body(q, _):
            chunk_copy(src_ref[tile_idx, q], slot, q * COMB_CH).start()
            return 0

        lax.fori_loop(0, ntot_ref[tile_idx], c_body, 0)

    @pl.when(i == 0)
    def _():
        issue(0, 0)

    @pl.when(i + 1 < n_tiles)
    def _():
        issue(i + 1, (i + 1) % 2)

    slot = i % 2

    def w_body(c, _):
        chunk_copy(0, slot, 0).wait()
        return 0

    lax.fori_loop(0, ntot_ref[i], w_body, 0)

    base = slot * stage_rows
    stride = COMB_TOK + SUBLANES

    def t_body(tk, _):
        acc = jnp.zeros((ROW_TILES, LANES), F32)
        for k in range(TOP_K):
            w = stage[base + lidx_ref[tk * TOP_K + k]].astype(F32)
            acc = acc + gate_ref[tk * TOP_K + k] * w
        tile[pl.ds(tk, ROW_TILES, stride=stride), :] = acc
        return 0

    lax.fori_loop(0, COMB_TOK, t_body, 0)
    for j in range(ROW_TILES):
        out_ref[:, j * LANES:(j + 1) * LANES] = tile[j * stride:j * stride + COMB_TOK, :]


def _combine(ys, src_flat, n_total, lidx, gate, t, stage_rows):
    n_tiles = t // COMB_TOK
    stride = COMB_TOK + SUBLANES
    return pl.pallas_call(
        functools.partial(_combine_kernel, stage_rows=stage_rows),
        grid_spec=pltpu.PrefetchScalarGridSpec(
            num_scalar_prefetch=2,
            grid=(n_tiles,),
            in_specs=[pl.BlockSpec((COMB_TOK * TOP_K,), lambda i, *_: (i,), memory_space=pltpu.SMEM),
                      pl.BlockSpec((COMB_TOK * TOP_K,), lambda i, *_: (i,), memory_space=pltpu.SMEM),
                      pl.BlockSpec(memory_space=pl.ANY)],
            out_specs=pl.BlockSpec((COMB_TOK, D_MODEL), lambda i, *_: (i, 0)),
            scratch_shapes=[pltpu.VMEM((2 * stage_rows, ROW_TILES, LANES), BF16),
                            pltpu.VMEM((ROW_TILES * stride, LANES), F32),
                            pltpu.SemaphoreType.DMA((2,))]),
        out_shape=jax.ShapeDtypeStruct((t, D_MODEL), F32),
        compiler_params=_cparams(("arbitrary",), 48),
        name="combine",
    )(src_flat, n_total, lidx, gate, ys)


def _final_kernel(h_ref, x1_ref, rt_ref, g2_ref, wsg_ref, wsu_ref, wsd_ref, gf_ref, y_ref):
    h = h_ref[...]
    a = _silu(jnp.dot(h, wsg_ref[...], preferred_element_type=F32))
    b = jnp.dot(h, wsu_ref[...], preferred_element_type=F32)
    shared = jnp.dot((a * b).astype(BF16), wsd_ref[...], preferred_element_type=F32)
    xo = x1_ref[...] + _mod(g2_ref) * (shared + rt_ref[...])
    y_ref[...] = xo * lax.rsqrt(jnp.mean(xo * xo, axis=-1, keepdims=True) + RMS_EPS) * gf_ref[...]


def _final(h2, x1, routed, routed_row0, mods, rows_per_batch, w_sg, w_su, w_sd, gf, tm):
    t = x1.shape[0]
    row = lambda m: (m, 0)
    assert routed_row0 % tm == 0
    block0 = routed_row0 // tm
    return pl.pallas_call(
        _final_kernel,
        grid=(t // tm,),
        in_specs=[pl.BlockSpec((tm, D_MODEL), row), pl.BlockSpec((tm, D_MODEL), row),
                  pl.BlockSpec((tm, D_MODEL), lambda m: (m + block0, 0)),
                  _mod_spec(mods, 5, rows_per_batch, tm),
                  _const_spec((D_MODEL, D_EXPERT)), _const_spec((D_MODEL, D_EXPERT)),
                  _const_spec((D_EXPERT, D_MODEL)), _const_spec((1, D_MODEL))],
        out_specs=pl.BlockSpec((tm, D_MODEL), row),
        out_shape=jax.ShapeDtypeStruct((t, D_MODEL), F32),
        compiler_params=_cparams(("arbitrary",), 48),
        name="final",
    )(h2, x1, routed, mods, w_sg, w_su, w_sd, gf)


def _moe_plan(top_e, t):
    ar = jnp.arange(N_EXPERTS, dtype=I32)
    sel = top_e[:, :, None] == ar
    onehot = jnp.any(sel, axis=1).astype(I32)
    csum = jnp.cumsum(onehot, axis=0)
    excl = csum - onehot
    counts = csum[-1]
    padded = (counts + MOE_ROWS - 1) // MOE_ROWS * MOE_ROWS
    pend = jnp.cumsum(padded)
    pstart = pend - padded
    pick = lambda per_expert: jnp.sum(jnp.where(sel, per_expert, 0), axis=-1)
    rank = pick(excl[:, None, :])
    n_blocks = -(-(t * TOP_K + N_EXPERTS * (MOE_ROWS - 1)) // MOE_ROWS)
    block_row = jnp.arange(n_blocks, dtype=I32) * MOE_ROWS
    block_expert = jnp.minimum(jnp.sum((pend[None, :] <= block_row[:, None]).astype(I32), axis=1),
                               N_EXPERTS - 1)
    n_used = (pend[-1] // MOE_ROWS).astype(I32).reshape(1)
    first = jnp.concatenate([jnp.ones((1,), I32), (block_expert[1:] != block_expert[:-1]).astype(I32)])
    later = jnp.logical_and(counts[None, :] > 0, ar[None, :] > ar[:, None])
    next_nonempty = jnp.min(jnp.where(later, ar[None, :], N_EXPERTS), axis=1)
    next_nonempty = jnp.where(next_nonempty < N_EXPERTS, next_nonempty, -1)
    next_expert = jnp.sum(jnp.where(block_expert[:, None] == ar[None, :], next_nonempty[None, :], 0), axis=1)
    assert t + MOE_ROWS <= 1 << TOKEN_BITS
    real = ((top_e << TOKEN_BITS) | jnp.arange(t, dtype=I32)[:, None]).reshape(-1)
    r = jnp.arange(MOE_ROWS - 1, dtype=I32)
    int_max = jnp.iinfo(jnp.int32).max
    pad_keys = jnp.where(r[None, :] < (padded - counts)[:, None],
                         (ar[:, None] << TOKEN_BITS) | (t + r[None, :]), int_max).reshape(-1)
    n_rows = n_blocks * MOE_ROWS
    fill = jnp.full((n_rows - real.shape[0] - pad_keys.shape[0],), int_max, I32)
    tokv = jnp.sort(jnp.concatenate([real, pad_keys, fill])) & ((1 << TOKEN_BITS) - 1)
    tok_buf = jnp.where(tokv < t, tokv, 0)
    lo = excl[::COMB_TOK]
    hi = jnp.concatenate([lo[1:], counts[None]], axis=0)
    n_chunks = (hi - lo + COMB_CH - 1) // COMB_CH
    cend = jnp.cumsum(n_chunks, axis=1)
    cstart = cend - n_chunks
    n_total = cend[:, -1].astype(I32)
    src_start = pstart[None, :] + lo
    q = jnp.arange(COMB_MAX_CHUNKS, dtype=I32)
    eq = jnp.minimum(jnp.sum((cend[:, None, :] <= q[None, :, None]).astype(I32), axis=-1), N_EXPERTS - 1)
    sel_q = eq[:, :, None] == ar
    pick_q = lambda v: jnp.sum(jnp.where(sel_q, v[:, None, :], 0), axis=-1)
    src_flat = pick_q(src_start) + (q[None, :] - pick_q(cstart)) * COMB_CH
    src_flat = jnp.where(q[None, :] < n_total[:, None], src_flat, 0)
    local = jnp.repeat(cstart * COMB_CH - lo, COMB_TOK, axis=0)
    lidx = pick(local[:, None, :]) + rank
    stage_rows = COMB_MAX_CHUNKS * COMB_CH
    return dict(block_expert=block_expert.astype(I32), n_used=n_used, tok_buf=tok_buf.astype(I32),
                first=first, next_expert=next_expert.astype(I32),
                n_blocks=n_blocks, src_flat=src_flat.astype(I32), n_total=n_total,
                lidx=lidx.astype(I32), stage_rows=stage_rows)


def kernel(x_prompt, x_sample, state_ssm_re, state_ssm_im, state_conv, c_prompt, c_sample, w_ada, b_ada, norm1_g, norm2_g, w_in, ssm_a_re, ssm_a_im, ssm_log_dt, ssm_b_re, ssm_b_im, ssm_c_re, ssm_c_im, ssm_d, w_glu, b_glu, w_lift_a, conv_w, conv_b, w_lift_b, w_out, w_router, router_bias, w_exp_gate, w_exp_up, w_exp_down, w_sh_gate, w_sh_up, w_sh_down, final_norm_g):
    depth = w_ada.shape[0]
    assert depth == 1
    bp, seq, _ = x_prompt.shape
    bs = x_sample.shape[0]
    tp = bp * seq
    l = 0

    w_in_b = w_in[l].astype(BF16)
    wr_t = w_router[l].T
    wr_hi = wr_t.astype(BF16)
    wr_lo = (wr_t - wr_hi.astype(F32)).astype(BF16)
    mix_w = (ssm_d[l].reshape(1, -1), w_glu[l].astype(BF16), b_glu[l].reshape(1, -1),
             w_lift_a[l].astype(BF16), conv_w[l], conv_b[l].reshape(1, -1), w_lift_b[l].astype(BF16),
             w_out[l].astype(BF16), norm2_g[l].reshape(1, -1), wr_hi, wr_lo)
    g1 = norm1_g[l].reshape(1, -1)

    ada = _ada(jnp.concatenate([c_prompt, c_sample], axis=0), w_ada[l], b_ada[l])
    mods_p = ada[:bp].reshape(bp, 1, -1)
    mods_s = ada[bp:]

    apow, ssm_tm, ssm_bk, ssm_ck = _ssm_prep(ssm_a_re[l], ssm_a_im[l], ssm_log_dt[l], ssm_b_re[l],
                                             ssm_b_im[l], ssm_c_re[l], ssm_c_im[l])
    a1_re, a1_im, al_re, al_im = (_state_layout(apow[i]) for i in range(4))

    xp = x_prompt.reshape(tp, D_MODEL)
    u_p, rest_p = _in_proj(xp, mods_p, seq, g1, w_in_b, tm=512)
    n_chunks = seq // SSM_CHUNK
    s_loc = _ssm_state(u_p, ssm_bk, SSM_CHUNK, n_chunks)
    s_prev, s_fin_p = _ssm_scan(s_loc.reshape(bp, n_chunks, -1), al_re, al_im,
                                jnp.zeros((bp, OCTETS * OCT_STATE), F32))
    y_p = _ssm_out(u_p, s_prev.reshape(bp * n_chunks, -1), ssm_tm, ssm_ck, SSM_CHUNK, n_chunks)
    x1_p, h2_p, lg_p, conv_p = _mixer_post(xp, y_p, u_p, rest_p, mods_p, seq, mix_w, None, tm=256)

    xs = x_sample.reshape(bs, D_MODEL)
    u_s, rest_s = _in_proj(xs, mods_s, 1, g1, w_in_b, tm=bs)
    s0_s = _pack_state(state_ssm_re[l], state_ssm_im[l])
    s_new_s = _ssm_step(_ssm_state(u_s, ssm_bk, 1, bs), s0_s, a1_re, a1_im)
    y_s = _ssm_out(u_s, s0_s, ssm_tm, ssm_ck, 1, bs)
    x1_s, h2_s, lg_s, nc0, nc1 = _mixer_post(xs, y_s, u_s, rest_s, mods_s, 1, mix_w,
                                                    state_conv[l], tm=bs)

    t = tp + bs
    te_p, gate_p = _route(lg_p, router_bias[l])
    te_s, gate_s = _route(lg_s, router_bias[l])
    top_e = jnp.concatenate([te_p, te_s], axis=1).T
    gate = jnp.concatenate([gate_p, gate_s], axis=1).T
    plan = _moe_plan(top_e, t)
    h_tiles = jnp.concatenate([h2_p, h2_s], axis=0).reshape(t, ROW_TILES, LANES)
    xs_sorted = _dispatch(h_tiles, plan["tok_buf"], plan["n_used"], plan["n_blocks"])
    ys = _experts(xs_sorted, plan["block_expert"], plan["n_used"], plan["first"], plan["next_expert"],
                  w_exp_gate[l], w_exp_up[l], w_exp_down[l], plan["n_blocks"])
    routed = _combine(ys, plan["src_flat"], plan["n_total"], plan["lidx"].reshape(-1),
                      gate.reshape(-1), t, plan["stage_rows"])

    w_sg, w_su, w_sd = w_sh_gate[l].astype(BF16), w_sh_up[l].astype(BF16), w_sh_down[l].astype(BF16)
    gf = final_norm_g.reshape(1, -1)
    y_prompt = _final(h2_p, x1_p, routed, 0, mods_p, seq, w_sg, w_su, w_sd, gf, tm=512)
    y_sample = _final(h2_s, x1_s, routed, tp, mods_s, 1, w_sg, w_su, w_sd, gf, tm=bs)

    re_p, im_p = _unpack_state(s_fin_p)
    re_s, im_s = _unpack_state(s_new_s)
    return (y_prompt.reshape(bp, seq, D_MODEL), y_sample.reshape(bs, 1, D_MODEL),
            re_p[None], im_p[None], conv_p[None],
            re_s[None], im_s[None], jnp.stack([nc0, nc1], axis=1)[None])
```

```python
import functools

import jax
import jax.numpy as jnp
from jax import lax
from jax.experimental import pallas as pl
from jax.experimental.pallas import tpu as pltpu

F32 = jnp.float32
BF16 = jnp.bfloat16
I32 = jnp.int32

D_MODEL = 2048
D_SSM = 1024
SSM_GROUP = 16
N_GROUPS = 64
SSM_STATE = 64
D_CONV = 1024
CONV_W = 3
N_EXPERTS = 64
TOP_K = 8
N_EXPERT_GROUPS = 8
EXPERTS_PER_GROUP = 8
TOPK_GROUPS = 4
D_EXPERT = 512
ROUTED_SCALE = 2.5
RMS_EPS = 1e-6

LANES = 128
SUBLANES = 8
OCTETS = N_GROUPS * SSM_GROUP // LANES
OCT_STATE = 2 * (N_GROUPS // OCTETS) * SSM_STATE
SSM_CHUNK = 8
MOE_ROWS = 256
COMB_TOK = 128
COMB_CH = 16
ROW_TILES = D_MODEL // LANES
ROW_STRIDE = MOE_ROWS + SUBLANES
DISP_ROWS = 2 * MOE_ROWS
DISP_STRIDE = DISP_ROWS + SUBLANES
COMB_MAX_CHUNKS = COMB_TOK * TOP_K // COMB_CH + N_EXPERTS
TOKEN_BITS = 14
NEG_INF = float("-inf")


def _cparams(sem, vmem_mb):
    return pltpu.CompilerParams(dimension_semantics=sem, vmem_limit_bytes=vmem_mb * 1024 * 1024)


def _const_spec(shape):
    nd = len(shape)
    return pl.BlockSpec(shape, lambda *_: (0,) * nd, pipeline_mode=pl.Buffered(1))


def _mod(ref):
    v = ref[...]
    return v[0] if v.ndim == 3 else v


def _silu(x):
    return x * jax.nn.sigmoid(x)


def _ada_kernel(c_ref, w_ref, b_ref, o_ref):
    a = _silu(c_ref[...]).astype(BF16)
    o_ref[...] = jnp.dot(a, w_ref[...].astype(BF16), preferred_element_type=F32) + b_ref[...]


def _ada(c_all, w_ada, b_ada):
    m = c_all.shape[0]
    n = w_ada.shape[1]
    tn = 1024
    return pl.pallas_call(
        _ada_kernel,
        grid=(n // tn,),
        in_specs=[pl.BlockSpec((m, D_MODEL), lambda j: (0, 0)),
                  pl.BlockSpec((D_MODEL, tn), lambda j: (0, j)),
                  pl.BlockSpec((1, tn), lambda j: (0, j))],
        out_specs=pl.BlockSpec((m, tn), lambda j: (0, j)),
        out_shape=jax.ShapeDtypeStruct((m, n), F32),
        compiler_params=_cparams(("arbitrary",), 40),
        name="ada",
    )(c_all, w_ada, b_ada.reshape(1, n))


def _mod_spec(mods, col, rows_per_batch, tm):
    if mods.ndim == 3:
        tiles_per_batch = rows_per_batch // tm
        return pl.BlockSpec((1, 1, D_MODEL), lambda m, *_: (m // tiles_per_batch, 0, col))
    return pl.BlockSpec((tm, D_MODEL), lambda m, *_: (m, col))


def _inproj_kernel(x_ref, g_ref, sh_ref, sc_ref, wu_ref, wr_ref, u_ref, r_ref, h_scr):
    @pl.when(pl.program_id(1) == 0)
    def _():
        x = x_ref[...]
        y = x * lax.rsqrt(jnp.mean(x * x, axis=-1, keepdims=True) + RMS_EPS) * g_ref[...]
        h = (y * (1.0 + _mod(sc_ref)) + _mod(sh_ref)).astype(BF16)
        h_scr[...] = h
        u_ref[...] = jnp.dot(h, wu_ref[...], preferred_element_type=F32)

    r_ref[...] = jnp.dot(h_scr[...], wr_ref[...], preferred_element_type=F32).astype(BF16)


def _in_proj(x, mods, rows_per_batch, g1, w_in, tm):
    t = x.shape[0]
    tn = D_SSM
    n_rest = w_in.shape[1] - tn
    return pl.pallas_call(
        _inproj_kernel,
        grid=(t // tm, n_rest // tn),
        in_specs=[pl.BlockSpec((tm, D_MODEL), lambda m, n: (m, 0)),
                  pl.BlockSpec((1, D_MODEL), lambda m, n: (0, 0)),
                  _mod_spec(mods, 0, rows_per_batch, tm),
                  _mod_spec(mods, 1, rows_per_batch, tm),
                  pl.BlockSpec((D_MODEL, D_SSM), lambda m, n: (0, 0)),
                  pl.BlockSpec((D_MODEL, tn), lambda m, n: (0, n + 1))],
        out_specs=[pl.BlockSpec((tm, D_SSM), lambda m, n: (m, 0)),
                   pl.BlockSpec((tm, tn), lambda m, n: (m, n))],
        out_shape=[jax.ShapeDtypeStruct((t, D_SSM), F32),
                   jax.ShapeDtypeStruct((t, n_rest), BF16)],
        scratch_shapes=[pltpu.VMEM((tm, D_MODEL), BF16)],
        compiler_params=_cparams(("arbitrary", "arbitrary"), 48),
        name="in_proj",
    )(x, g1, mods, mods, w_in, w_in)


def _ssm_prep_kernel(are_ref, aim_ref, ldt_ref, btr_ref, bti_ref, cre_ref, cim_ref,
                     apow_ref, tm_ref, bk_ref, ck_ref):
    lc = SSM_CHUNK
    ng = N_GROUPS // OCTETS
    half = OCT_STATE // 2
    lr = are_ref[...]
    li = aim_ref[...]
    dt = jnp.exp(ldt_ref[...])
    mag = jnp.exp(lr * dt)
    ar = mag * jnp.cos(li * dt)
    ai = mag * jnp.sin(li * dt)
    den = lr * lr + li * li
    xr = ar - 1.0
    cr = (xr * lr + ai * li) / den
    ci = (ai * lr - xr * li) / den
    btr = btr_ref[...]
    bti = bti_ref[...]
    bbr = cr * btr - ci * bti
    bbi = cr * bti + ci * btr
    cre = cre_ref[...]
    cim = cim_ref[...]
    cre_w = jnp.concatenate([cre] * ng, axis=1)
    cim_w = jnp.concatenate([cim] * ng, axis=1)

    def iota(shape, dim):
        return lax.broadcasted_iota(I32, shape, dim)

    width = lc * LANES
    sel_b = ((iota((SSM_STATE, half), 1) & (SSM_STATE - 1)) == iota((SSM_STATE, half), 0)).astype(BF16)
    mask_b = ((iota((width, half), 0) & (LANES - 1)) >> 4) == (iota((width, half), 1) >> 6)
    sel_c = ((iota((half, SSM_STATE), 0) & (SSM_STATE - 1)) == iota((half, SSM_STATE), 1)).astype(BF16)
    mask_c = (iota((half, width), 0) >> 6) == ((iota((half, width), 1) & (LANES - 1)) >> 4)
    mask_t = (iota((LANES, LANES), 0) >> 4) == (iota((LANES, LANES), 1) >> 4)
    dn_batch = (((2,), (2,)), ((0,), (0,)))
    dn_nt = (((1,), (1,)), ((), ()))

    pr = jnp.ones_like(ar)
    pi = jnp.zeros_like(ar)
    w1r, w1i, w2r, w2i = [], [], [], []
    for l in range(lc):
        w1r.append(pr * bbr - pi * bbi)
        w1i.append(pr * bbi + pi * bbr)
        pr, pi = pr * ar - pi * ai, pr * ai + pi * ar
        w2r.append(cre * pr - cim * pi)
        w2i.append(-(cre * pi + cim * pr))
        if l == 0:
            apow_ref[0] = pr
            apow_ref[1] = pi
    apow_ref[2] = pr
    apow_ref[3] = pi

    def rows(ws, order):
        return jnp.concatenate([ws[l] for l in order], axis=0).reshape(width, SSM_STATE).astype(BF16)

    rev = range(lc - 1, -1, -1)
    for c, ws in ((0, w1r), (1, w1i)):
        rep = jnp.dot(rows(ws, rev), sel_b, preferred_element_type=F32)
        bk_ref[:, c * half:(c + 1) * half] = jnp.where(mask_b, rep, 0.0).astype(BF16)
    for c, ws in ((0, w2r), (1, w2i)):
        rep = lax.dot_general(sel_c, rows(ws, range(lc)), dn_nt, preferred_element_type=F32)
        ck_ref[c * half:(c + 1) * half, :] = jnp.where(mask_c, rep, 0.0).astype(BF16)
    w1r_all = jnp.concatenate(w1r, axis=1)
    w1i_all = jnp.concatenate(w1i, axis=1)
    kt = (lax.dot_general(w1r_all, cre_w, dn_batch, precision=lax.Precision.HIGHEST,
                          preferred_element_type=F32)
          - lax.dot_general(w1i_all, cim_w, dn_batch, precision=lax.Precision.HIGHEST,
                            preferred_element_type=F32))
    hh = SSM_GROUP
    t_blocks = [jnp.where(mask_t, kt[:, l * hh:(l + 1) * hh, :].reshape(LANES, LANES), 0.0).astype(BF16)
                for l in range(lc)]
    zero = jnp.zeros((LANES, LANES), BF16)
    for s in range(lc):
        for t in range(lc):
            tm_ref[s * LANES:(s + 1) * LANES, t * LANES:(t + 1) * LANES] = t_blocks[t - s] if t >= s else zero


def _ssm_prep(a_re, a_im, log_dt, b_re, b_im, c_re, c_im):
    g, p, h = N_GROUPS, SSM_STATE, SSM_GROUP
    ng = g // OCTETS
    width = SSM_CHUNK * LANES
    vec = pl.BlockSpec((ng, 1, p), lambda o: (o, 0, 0))
    mat = pl.BlockSpec((ng, h, p), lambda o: (o, 0, 0))
    big = lambda rows, cols: pl.BlockSpec((None, rows, cols), lambda o: (o, 0, 0))
    return pl.pallas_call(
        _ssm_prep_kernel,
        grid=(OCTETS,),
        in_specs=[vec, vec, pl.BlockSpec((ng, 1, 1), lambda o: (o, 0, 0)), mat, mat, mat, mat],
        out_specs=[pl.BlockSpec((4, ng, 1, p), lambda o: (0, o, 0, 0)),
                   big(width, width), big(width, OCT_STATE), big(OCT_STATE, width)],
        out_shape=[jax.ShapeDtypeStruct((4, g, 1, p), F32),
                   jax.ShapeDtypeStruct((OCTETS, width, width), BF16),
                   jax.ShapeDtypeStruct((OCTETS, width, OCT_STATE), BF16),
                   jax.ShapeDtypeStruct((OCTETS, OCT_STATE, width), BF16)],
        compiler_params=_cparams(("arbitrary",), 32),
        name="ssm_prep",
    )(a_re.reshape(g, 1, p), a_im.reshape(g, 1, p), log_dt.reshape(g, 1, 1),
      b_re.transpose(0, 2, 1), b_im.transpose(0, 2, 1), c_re, c_im)


def _state_layout(v):
    return v.reshape(OCTETS, 1, OCT_STATE // 2)


def _load_chunks(u_ref, lc, rb):
    parts = [u_ref[pl.ds(s, rb, stride=lc), :] if lc > 1 else u_ref[...] for s in range(lc)]
    u = parts[0] if lc == 1 else jnp.concatenate(parts, axis=-1)
    return u.astype(BF16)


def _ssm_state_kernel(u_ref, bk_ref, s_ref, *, lc, rb):
    s_ref[...] = jnp.dot(_load_chunks(u_ref, lc, rb), bk_ref[...], preferred_element_type=F32)


def _ssm_state(u, bk, lc, rb):
    t = u.shape[0]
    nb = t // (rb * lc)
    lag0_block = SSM_CHUNK // lc - 1
    return pl.pallas_call(
        functools.partial(_ssm_state_kernel, lc=lc, rb=rb),
        grid=(OCTETS, nb),
        in_specs=[pl.BlockSpec((rb * lc, LANES), lambda o, b: (b, o)),
                  pl.BlockSpec((None, lc * LANES, OCT_STATE), lambda o, b: (o, lag0_block, 0))],
        out_specs=pl.BlockSpec((rb, OCT_STATE), lambda o, b: (b, o)),
        out_shape=jax.ShapeDtypeStruct((t // lc, OCTETS * OCT_STATE), F32),
        compiler_params=_cparams(("arbitrary", "arbitrary"), 32),
        name="ssm_state",
    )(u, bk)


def _ssm_scan_kernel(sl_ref, ar_ref, ai_ref, s0_ref, sp_ref, sf_ref, *, nb, nc):
    half = OCT_STATE // 2
    ar = ar_ref[0]
    ai = ai_ref[0]
    init = tuple((s0_ref[b:b + 1, 0:half], s0_ref[b:b + 1, half:OCT_STATE]) for b in range(nb))

    def body(c, carry):
        new = []
        for b in range(nb):
            sr, si = carry[b]
            sp_ref[b, pl.ds(c, 1), 0:half] = sr
            sp_ref[b, pl.ds(c, 1), half:OCT_STATE] = si
            xr = sl_ref[b, pl.ds(c, 1), 0:half]
            xi = sl_ref[b, pl.ds(c, 1), half:OCT_STATE]
            new.append((ar * sr - ai * si + xr, ar * si + ai * sr + xi))
        return tuple(new)

    fin = lax.fori_loop(0, nc, body, init)
    for b in range(nb):
        sf_ref[b:b + 1, 0:half] = fin[b][0]
        sf_ref[b:b + 1, half:OCT_STATE] = fin[b][1]


def _ssm_scan(s_loc, al_re, al_im, s0):
    nb, nc, width = s_loc.shape
    return pl.pallas_call(
        functools.partial(_ssm_scan_kernel, nb=nb, nc=nc),
        grid=(OCTETS,),
        in_specs=[pl.BlockSpec((nb, nc, OCT_STATE), lambda o: (0, 0, o)),
                  pl.BlockSpec((1, 1, OCT_STATE // 2), lambda o: (o, 0, 0)),
                  pl.BlockSpec((1, 1, OCT_STATE // 2), lambda o: (o, 0, 0)),
                  pl.BlockSpec((nb, OCT_STATE), lambda o: (0, o))],
        out_specs=[pl.BlockSpec((nb, nc, OCT_STATE), lambda o: (0, 0, o)),
                   pl.BlockSpec((nb, OCT_STATE), lambda o: (0, o))],
        out_shape=[jax.ShapeDtypeStruct((nb, nc, width), F32),
                   jax.ShapeDtypeStruct((nb, width), F32)],
        compiler_params=_cparams(("arbitrary",), 32),
        name="ssm_scan",
    )(s_loc, al_re, al_im, s0)


def _ssm_out_kernel(u_ref, sp_ref, tm_ref, ck_ref, y_ref, *, lc, rb):
    y = (jnp.dot(_load_chunks(u_ref, lc, rb), tm_ref[...], preferred_element_type=F32)
         + jnp.dot(sp_ref[...].astype(BF16), ck_ref[...], preferred_element_type=F32))
    if lc == 1:
        y_ref[...] = y
    else:
        for t in range(lc):
            y_ref[pl.ds(t, rb, stride=lc), :] = y[:, t * LANES:(t + 1) * LANES]


def _ssm_out(u, s_prev, tm, ck, lc, rb):
    t = u.shape[0]
    nb = t // (rb * lc)
    return pl.pallas_call(
        functools.partial(_ssm_out_kernel, lc=lc, rb=rb),
        grid=(OCTETS, nb),
        in_specs=[pl.BlockSpec((rb * lc, LANES), lambda o, b: (b, o)),
                  pl.BlockSpec((rb, OCT_STATE), lambda o, b: (b, o)),
                  pl.BlockSpec((None, lc * LANES, lc * LANES), lambda o, b: (o, 0, 0)),
                  pl.BlockSpec((None, OCT_STATE, lc * LANES), lambda o, b: (o, 0, 0))],
        out_specs=pl.BlockSpec((rb * lc, LANES), lambda o, b: (b, o)),
        out_shape=jax.ShapeDtypeStruct((t, D_SSM), F32),
        compiler_params=_cparams(("arbitrary", "arbitrary"), 32),
        name="ssm_out",
    )(u, s_prev, tm, ck)


def _ssm_step_kernel(sl_ref, s0_ref, ar_ref, ai_ref, sn_ref):
    half = OCT_STATE // 2
    ar = ar_ref[0]
    ai = ai_ref[0]
    sr = s0_ref[:, 0:half]
    si = s0_ref[:, half:OCT_STATE]
    sn_ref[:, 0:half] = ar * sr - ai * si + sl_ref[:, 0:half]
    sn_ref[:, half:OCT_STATE] = ar * si + ai * sr + sl_ref[:, half:OCT_STATE]


def _ssm_step(s_loc, s0, a_re, a_im):
    rows, width = s_loc.shape
    return pl.pallas_call(
        _ssm_step_kernel,
        grid=(OCTETS,),
        in_specs=[pl.BlockSpec((rows, OCT_STATE), lambda o: (0, o)),
                  pl.BlockSpec((rows, OCT_STATE), lambda o: (0, o)),
                  pl.BlockSpec((1, 1, OCT_STATE // 2), lambda o: (o, 0, 0)),
                  pl.BlockSpec((1, 1, OCT_STATE // 2), lambda o: (o, 0, 0))],
        out_specs=pl.BlockSpec((rows, OCT_STATE), lambda o: (0, o)),
        out_shape=jax.ShapeDtypeStruct((rows, width), F32),
        compiler_params=_cparams(("arbitrary",), 32),
        name="ssm_step",
    )(s_loc, s0, a_re, a_im)


def _pack_state(s_re, s_im):
    b = s_re.shape[0]
    half = OCT_STATE // 2
    return jnp.stack([s_re.reshape(b, OCTETS, half), s_im.reshape(b, OCTETS, half)], axis=2).reshape(b, -1)


def _unpack_state(s):
    b = s.shape[0]
    s4 = s.reshape(b, OCTETS, 2, OCT_STATE // 2)
    return (s4[:, :, 0].reshape(b, N_GROUPS, SSM_STATE), s4[:, :, 1].reshape(b, N_GROUPS, SSM_STATE))


def _mixer_kernel(*refs, seq, tm, tiles_per_batch):
    (x_ref, ys_ref, u_ref, r_ref, g1_ref, sh2_ref, sc2_ref, d_ref, wglu_ref, bglu_ref, wla_ref,
     cw_ref, cb_ref, wlb_ref, wout_ref, n2g_ref, wrh_ref, wrl_ref) = refs[:18]
    if seq:
        x1_ref, h2_ref, lg_ref, nc_ref, carry = refs[18:]
    else:
        cb0_ref, cb1_ref, x1_ref, h2_ref, lg_ref, nc0_ref, nc1_ref = refs[18:]

    ya = jax.nn.gelu(ys_ref[...] + d_ref[...] * u_ref[...], approximate=True)
    z = jnp.dot(ya.astype(BF16), wglu_ref[...], preferred_element_type=F32) + bglu_ref[...]
    ya = ya * jax.nn.sigmoid(z)
    branch_a = jnp.dot(ya.astype(BF16), wla_ref[...], preferred_element_type=F32)

    xb = r_ref[:, 0:D_CONV].astype(F32)
    gb = r_ref[:, D_CONV:2 * D_CONV].astype(F32)
    gc = r_ref[:, 2 * D_CONV:3 * D_CONV].astype(F32)
    v = gc * xb
    if seq:
        first = pl.program_id(0) % tiles_per_batch == 0

        @pl.when(first)
        def _():
            carry[...] = jnp.zeros_like(carry)

        c0 = carry[0:1, :]
        c1 = carry[1:2, :]
        row = lax.broadcasted_iota(I32, (tm, 1), 0)
        v1 = jnp.where(row == 0, c1, pltpu.roll(v, 1, 0))
        v2 = jnp.where(row == 0, c0, jnp.where(row == 1, c1, pltpu.roll(v, 2, 0)))
        tail = v[tm - 2:tm, :]
        carry[...] = tail
        nc_ref[0] = tail
    else:
        v2 = cb0_ref[...]
        v1 = cb1_ref[...]
        nc0_ref[...] = v1
        nc1_ref[...] = v
    conv = cb_ref[...] + cw_ref[0:1, :] * v2 + cw_ref[1:2, :] * v1 + cw_ref[2:3, :] * v
    branch_b = jnp.dot((gb * conv).astype(BF16), wlb_ref[...], preferred_element_type=F32)

    off = 3 * D_CONV
    gla = r_ref[:, off:off + D_MODEL].astype(F32)
    glb = r_ref[:, off + D_MODEL:off + 2 * D_MODEL].astype(F32)
    merged = jax.nn.sigmoid(gla) * branch_a + jax.nn.sigmoid(glb) * branch_b
    x1 = x_ref[...] + _mod(g1_ref) * jnp.dot(merged.astype(BF16), wout_ref[...],
                                             preferred_element_type=F32)
    x1_ref[...] = x1

    y = x1 * lax.rsqrt(jnp.mean(x1 * x1, axis=-1, keepdims=True) + RMS_EPS) * n2g_ref[...]
    h2 = y * (1.0 + _mod(sc2_ref)) + _mod(sh2_ref)
    h2b = h2.astype(BF16)
    h2_ref[...] = h2b

    h2l = (h2 - h2b.astype(F32)).astype(BF16)
    dn = (((1,), (1,)), ((), ()))
    lg_ref[...] = (lax.dot_general(wrh_ref[...], h2b, dn, preferred_element_type=F32)
                   + lax.dot_general(wrh_ref[...], h2l, dn, preferred_element_type=F32)
                   + lax.dot_general(wrl_ref[...], h2b, dn, preferred_element_type=F32))


def _mixer_post(x, y_ssm, u, rest, mods, rows_per_batch, weights, conv_state, tm):
    t = x.shape[0]
    seq = conv_state is None
    tiles_per_batch = rows_per_batch // tm
    (ssm_d, w_glu, b_glu, w_la, conv_w, conv_b, w_lb, w_out, n2g, wr_hi, wr_lo) = weights
    row = lambda m: (m, 0)
    in_specs = [pl.BlockSpec((tm, D_MODEL), row),
                pl.BlockSpec((tm, D_SSM), row),
                pl.BlockSpec((tm, D_SSM), row),
                pl.BlockSpec((tm, rest.shape[1]), row),
                _mod_spec(mods, 2, rows_per_batch, tm),
                _mod_spec(mods, 3, rows_per_batch, tm),
                _mod_spec(mods, 4, rows_per_batch, tm),
                _const_spec((1, D_SSM)), _const_spec((D_SSM, D_SSM)), _const_spec((1, D_SSM)),
                _const_spec((D_SSM, D_MODEL)), _const_spec((CONV_W, D_CONV)), _const_spec((1, D_CONV)),
                _const_spec((D_CONV, D_MODEL)), _const_spec((D_MODEL, D_MODEL)),
                _const_spec((1, D_MODEL)), _const_spec((N_EXPERTS, D_MODEL)),
                _const_spec((N_EXPERTS, D_MODEL))]
    args = [x, y_ssm, u, rest, mods, mods, mods, ssm_d, w_glu, b_glu, w_la, conv_w, conv_b, w_lb,
            w_out, n2g, wr_hi, wr_lo]
    out_specs = [pl.BlockSpec((tm, D_MODEL), row),
                 pl.BlockSpec((tm, D_MODEL), row),
                 pl.BlockSpec((N_EXPERTS, tm), lambda m: (0, m))]
    out_shape = [jax.ShapeDtypeStruct((t, D_MODEL), F32),
                 jax.ShapeDtypeStruct((t, D_MODEL), BF16),
                 jax.ShapeDtypeStruct((N_EXPERTS, t), F32)]
    scratch = []
    if seq:
        nb = t // rows_per_batch
        out_specs.append(pl.BlockSpec((1, CONV_W - 1, D_CONV), lambda m: (m // tiles_per_batch, 0, 0)))
        out_shape.append(jax.ShapeDtypeStruct((nb, CONV_W - 1, D_CONV), F32))
        scratch.append(pltpu.VMEM((CONV_W - 1, D_CONV), F32))
    else:
        in_specs += [pl.BlockSpec((tm, D_CONV), row), pl.BlockSpec((tm, D_CONV), row)]
        args += [conv_state[:, 0, :], conv_state[:, 1, :]]
        out_specs += [pl.BlockSpec((tm, D_CONV), row), pl.BlockSpec((tm, D_CONV), row)]
        out_shape += [jax.ShapeDtypeStruct((t, D_CONV), F32), jax.ShapeDtypeStruct((t, D_CONV), F32)]
    return pl.pallas_call(
        functools.partial(_mixer_kernel, seq=seq, tm=tm, tiles_per_batch=tiles_per_batch),
        grid=(t // tm,),
        in_specs=in_specs,
        out_specs=out_specs,
        out_shape=out_shape,
        scratch_shapes=scratch,
        compiler_params=_cparams(("arbitrary",), 56),
        name="mixer_post",
    )(*args)


def _route_kernel(lg_ref, bias_ref, te_ref, gate_ref):
    scores = [jax.nn.sigmoid(lg_ref[e]) for e in range(N_EXPERTS)]
    choice = [scores[e] + bias_ref[e] for e in range(N_EXPERTS)]
    shape = scores[0].shape

    def first_max(vals):
        m = functools.reduce(jnp.maximum, vals)
        found = jnp.zeros(shape, jnp.bool_)
        hot = []
        for v in vals:
            is_m = jnp.logical_and(v == m, jnp.logical_not(found))
            hot.append(is_m)
            found = jnp.logical_or(found, is_m)
        return m, hot

    gscore = []
    for g in range(N_EXPERT_GROUPS):
        vals = choice[g * EXPERTS_PER_GROUP:(g + 1) * EXPERTS_PER_GROUP]
        m1, hot = first_max(vals)
        rest = [jnp.where(h, NEG_INF, v) for h, v in zip(hot, vals)]
        gscore.append(m1 + functools.reduce(jnp.maximum, rest))

    gsel = [jnp.zeros(shape, jnp.bool_) for _ in range(N_EXPERT_GROUPS)]
    for _ in range(TOPK_GROUPS):
        _, hot = first_max(gscore)
        gsel = [jnp.logical_or(s, h) for s, h in zip(gsel, hot)]
        gscore = [jnp.where(h, NEG_INF, v) for h, v in zip(hot, gscore)]

    cand = [jnp.where(gsel[e // EXPERTS_PER_GROUP], choice[e], NEG_INF) for e in range(N_EXPERTS)]
    picked_e = []
    picked_s = []
    for _ in range(TOP_K):
        _, hot = first_max(cand)
        idx = jnp.zeros(shape, I32)
        sc = jnp.zeros(shape, F32)
        for e in range(N_EXPERTS):
            idx = jnp.where(hot[e], e, idx)
            sc = jnp.where(hot[e], scores[e], sc)
        cand = [jnp.where(h, NEG_INF, v) for h, v in zip(hot, cand)]
        picked_e.append(idx)
        picked_s.append(sc)
    total = functools.reduce(lambda a, b: a + b, picked_s)
    for k in range(TOP_K):
        te_ref[k] = picked_e[k]
        gate_ref[k] = picked_s[k] / total * ROUTED_SCALE


def _route(logits_t, bias):
    t = logits_t.shape[1]
    r = t // LANES
    rb = min(r, SUBLANES)
    lg = logits_t.reshape(N_EXPERTS, r, LANES)
    bias_b = jnp.broadcast_to(bias.astype(F32).reshape(N_EXPERTS, 1, 1), (N_EXPERTS, 1, LANES))
    te, gate = pl.pallas_call(
        _route_kernel,
        grid=(r // rb,),
        in_specs=[pl.BlockSpec((N_EXPERTS, rb, LANES), lambda i: (0, i, 0)),
                  pl.BlockSpec((N_EXPERTS, 1, LANES), lambda i: (0, 0, 0))],
        out_specs=[pl.BlockSpec((TOP_K, rb, LANES), lambda i: (0, i, 0)),
                   pl.BlockSpec((TOP_K, rb, LANES), lambda i: (0, i, 0))],
        out_shape=[jax.ShapeDtypeStruct((TOP_K, r, LANES), I32),
                   jax.ShapeDtypeStruct((TOP_K, r, LANES), F32)],
        compiler_params=_cparams(("arbitrary",), 32),
        name="route",
    )(lg, bias_b)
    return te.reshape(TOP_K, t), gate.reshape(TOP_K, t)


def _dispatch_kernel(nused_ref, tok_ref, h_ref, xs_ref, tile):
    used_rows = nused_ref[0] * MOE_ROWS
    live = pl.program_id(0) * DISP_ROWS < used_rows

    @pl.when(jnp.logical_not(live))
    def _():
        xs_ref[...] = jnp.zeros_like(xs_ref)

    @pl.when(live)
    def _():
        for mi in range(DISP_ROWS):
            tile[pl.ds(mi, ROW_TILES, stride=DISP_STRIDE), :] = h_ref[tok_ref[mi]].astype(F32)
        for j in range(ROW_TILES):
            xs_ref[:, j * LANES:(j + 1) * LANES] = (
                tile[j * DISP_STRIDE:j * DISP_STRIDE + DISP_ROWS, :].astype(BF16))


def _dispatch(h_tiles, tok_buf, n_used, n_blocks):
    n_rows = n_blocks * MOE_ROWS
    assert n_rows % DISP_ROWS == 0

    def last_live(i, nu):
        return jnp.minimum(i, (nu[0] * MOE_ROWS - 1) // DISP_ROWS)

    return pl.pallas_call(
        _dispatch_kernel,
        grid_spec=pltpu.PrefetchScalarGridSpec(
            num_scalar_prefetch=1,
            grid=(n_rows // DISP_ROWS,),
            in_specs=[pl.BlockSpec((DISP_ROWS,), lambda i, nu: (last_live(i, nu),),
                                   memory_space=pltpu.SMEM),
                      pl.BlockSpec(memory_space=pltpu.VMEM)],
            out_specs=pl.BlockSpec((DISP_ROWS, D_MODEL), lambda i, nu: (i, 0)),
            scratch_shapes=[pltpu.VMEM((ROW_TILES * DISP_STRIDE, LANES), F32)]),
        out_shape=jax.ShapeDtypeStruct((n_rows, D_MODEL), BF16),
        compiler_params=_cparams(("arbitrary",), 56),
        name="dispatch",
    )(n_used, tok_buf, h_tiles)


def _experts_kernel(be_ref, nused_ref, first_ref, next_ref, xs_ref, wg_hbm, wu_hbm, wd_hbm, ys_ref,
                    wg_f, wu_f, wd_f, wg_s, wu_s, wd_s, tile, sem):
    i = pl.program_id(0)

    def weight_copies(e):
        return (pltpu.make_async_copy(wg_hbm.at[e], wg_f, sem.at[0]),
                pltpu.make_async_copy(wu_hbm.at[e], wu_f, sem.at[1]),
                pltpu.make_async_copy(wd_hbm.at[e], wd_f, sem.at[2]))

    n_used = nused_ref[0]

    @pl.when(i < n_used)
    def _():
        @pl.when(i == 0)
        def _():
            for c in weight_copies(be_ref[0]):
                c.start()

        @pl.when(first_ref[i] == 1)
        def _():
            for c in weight_copies(be_ref[i]):
                c.wait()
            wg_s[...] = wg_f[...].astype(BF16)
            wu_s[...] = wu_f[...].astype(BF16)
            wd_s[...] = wd_f[...].astype(BF16)

            @pl.when(next_ref[i] >= 0)
            def _():
                for c in weight_copies(next_ref[i]):
                    c.start()

    def ffn_to_tile():
        x = xs_ref[...]
        g = jnp.dot(x, wg_s[...], preferred_element_type=F32)
        u = jnp.dot(x, wu_s[...], preferred_element_type=F32)
        h = (_silu(g) * u).astype(BF16)
        y = jnp.dot(h, wd_s[...], preferred_element_type=F32)
        for j in range(ROW_TILES):
            tile[j * ROW_STRIDE:j * ROW_STRIDE + MOE_ROWS, :] = y[:, j * LANES:(j + 1) * LANES]

    def tile_to_rows():
        for mi in range(MOE_ROWS):
            ys_ref[mi] = tile[pl.ds(mi, ROW_TILES, stride=ROW_STRIDE), :].astype(BF16)

    @pl.when(jnp.logical_and(i >= 1, i < n_used))
    def _():
        tile_to_rows()
        ffn_to_tile()

    @pl.when(i == 0)
    def _():
        ffn_to_tile()

    @pl.when(i == n_used)
    def _():
        tile_to_rows()

    @pl.when(i > n_used)
    def _():
        ys_ref[...] = jnp.zeros_like(ys_ref)


def _experts(xs, block_expert, n_used, first, next_expert, w_gate, w_up, w_down, n_blocks):
    def xmap(i, be, nu, fi, nx):
        return (jnp.minimum(i, nu[0] - 1), 0)

    any_spec = pl.BlockSpec(memory_space=pl.ANY)
    return pl.pallas_call(
        _experts_kernel,
        grid_spec=pltpu.PrefetchScalarGridSpec(
            num_scalar_prefetch=4,
            grid=(n_blocks + 1,),
            in_specs=[pl.BlockSpec((MOE_ROWS, D_MODEL), xmap), any_spec, any_spec, any_spec],
            out_specs=pl.BlockSpec((MOE_ROWS, ROW_TILES, LANES),
                                   lambda i, *_: (jnp.maximum(i - 1, 0), 0, 0)),
            scratch_shapes=[pltpu.VMEM((D_MODEL, D_EXPERT), F32),
                            pltpu.VMEM((D_MODEL, D_EXPERT), F32),
                            pltpu.VMEM((D_EXPERT, D_MODEL), F32),
                            pltpu.VMEM((D_MODEL, D_EXPERT), BF16),
                            pltpu.VMEM((D_MODEL, D_EXPERT), BF16),
                            pltpu.VMEM((D_EXPERT, D_MODEL), BF16),
                            pltpu.VMEM((ROW_TILES * ROW_STRIDE, LANES), F32),
                            pltpu.SemaphoreType.DMA((3,))]),
        out_shape=jax.ShapeDtypeStruct((n_blocks * MOE_ROWS, ROW_TILES, LANES), BF16),
        compiler_params=_cparams(("arbitrary",), 48),
        name="experts",
    )(block_expert, n_used, first, next_expert, xs, w_gate, w_up, w_down)


def _combine_kernel(src_ref, ntot_ref, lidx_ref, gate_ref, ys_hbm, out_ref,
                    stage, tile, sem, *, stage_rows):
    i = pl.program_id(0)
    n_tiles = pl.num_programs(0)

    def chunk_copy(src, slot, dst):
        return pltpu.make_async_copy(ys_hbm.at[pl.ds(src, COMB_CH)],
                                     stage.at[pl.ds(slot * stage_rows + dst, COMB_CH)],
                                     sem.at[slot])

    def issue(tile_idx, slot):
        def c_body(q, _):
            chunk_copy(src_ref[tile_idx, q], slot, q * COMB_CH).start()
            return 0

        lax.fori_loop(0, ntot_ref[tile_idx], c_body, 0)

    @pl.when(i == 0)
    def _():
        issue(0, 0)

    @pl.when(i + 1 < n_tiles)
    def _():
        issue(i + 1, (i + 1) % 2)

    slot = i % 2

    def w_body(c, _):
        chunk_copy(0, slot, 0).wait()
        return 0

    lax.fori_loop(0, ntot_ref[i], w_body, 0)

    stride = COMB_TOK + SUBLANES

    for tk in range(COMB_TOK):
        acc = None
        for k in range(TOP_K):
            w = stage[lidx_ref[tk * TOP_K + k]].astype(F32)
            term = gate_ref[tk * TOP_K + k] * w
            acc = term if acc is None else acc + term
        tile[pl.ds(tk, ROW_TILES, stride=stride), :] = acc
    for j in range(ROW_TILES):
        out_ref[:, j * LANES:(j + 1) * LANES] = tile[j * stride:j * stride + COMB_TOK, :]


def _combine(ys, src_flat, n_total, lidx, gate, t, stage_rows):
    n_tiles = t // COMB_TOK
    stride = COMB_TOK + SUBLANES
    return pl.pallas_call(
        functools.partial(_combine_kernel, stage_rows=stage_rows),
        grid_spec=pltpu.PrefetchScalarGridSpec(
            num_scalar_prefetch=2,
            grid=(n_tiles,),
            in_specs=[pl.BlockSpec((COMB_TOK * TOP_K,), lambda i, *_: (i,), memory_space=pltpu.SMEM),
                      pl.BlockSpec((COMB_TOK * TOP_K,), lambda i, *_: (i,), memory_space=pltpu.SMEM),
                      pl.BlockSpec(memory_space=pl.ANY)],
            out_specs=pl.BlockSpec((COMB_TOK, D_MODEL), lambda i, *_: (i, 0)),
            scratch_shapes=[pltpu.VMEM((2 * stage_rows, ROW_TILES, LANES), BF16),
                            pltpu.VMEM((ROW_TILES * stride, LANES), F32),
                            pltpu.SemaphoreType.DMA((2,))]),
        out_shape=jax.ShapeDtypeStruct((t, D_MODEL), F32),
        compiler_params=_cparams(("arbitrary",), 48),
        name="combine",
    )(src_flat, n_total, lidx, gate, ys)


def _final_kernel(h_ref, x1_ref, rt_ref, g2_ref, wsg_ref, wsu_ref, wsd_ref, gf_ref, y_ref):
    h = h_ref[...]
    a = _silu(jnp.dot(h, wsg_ref[...], preferred_element_type=F32))
    b = jnp.dot(h, wsu_ref[...], preferred_element_type=F32)
    shared = jnp.dot((a * b).astype(BF16), wsd_ref[...], preferred_element_type=F32)
    xo = x1_ref[...] + _mod(g2_ref) * (shared + rt_ref[...])
    y_ref[...] = xo * lax.rsqrt(jnp.mean(xo * xo, axis=-1, keepdims=True) + RMS_EPS) * gf_ref[...]


def _final(h2, x1, routed, routed_row0, mods, rows_per_batch, w_sg, w_su, w_sd, gf, tm):
    t = x1.shape[0]
    row = lambda m: (m, 0)
    assert routed_row0 % tm == 0
    block0 = routed_row0 // tm
    return pl.pallas_call(
        _final_kernel,
        grid=(t // tm,),
        in_specs=[pl.BlockSpec((tm, D_MODEL), row), pl.BlockSpec((tm, D_MODEL), row),
                  pl.BlockSpec((tm, D_MODEL), lambda m: (m + block0, 0)),
                  _mod_spec(mods, 5, rows_per_batch, tm),
                  _const_spec((D_MODEL, D_EXPERT)), _const_spec((D_MODEL, D_EXPERT)),
                  _const_spec((D_EXPERT, D_MODEL)), _const_spec((1, D_MODEL))],
        out_specs=pl.BlockSpec((tm, D_MODEL), row),
        out_shape=jax.ShapeDtypeStruct((t, D_MODEL), F32),
        compiler_params=_cparams(("arbitrary",), 48),
        name="final",
    )(h2, x1, routed, mods, w_sg, w_su, w_sd, gf)


def _moe_plan(top_e, t):
    ar = jnp.arange(N_EXPERTS, dtype=I32)
    sel = top_e[:, :, None] == ar
    onehot = jnp.any(sel, axis=1).astype(I32)
    csum = jnp.cumsum(onehot, axis=0)
    excl = csum - onehot
    counts = csum[-1]
    padded = (counts + MOE_ROWS - 1) // MOE_ROWS * MOE_ROWS
    pend = jnp.cumsum(padded)
    pstart = pend - padded
    pick = lambda per_expert: jnp.sum(jnp.where(sel, per_expert, 0), axis=-1)
    rank = pick(excl[:, None, :])
    n_blocks = -(-(t * TOP_K + N_EXPERTS * (MOE_ROWS - 1)) // MOE_ROWS)
    block_row = jnp.arange(n_blocks, dtype=I32) * MOE_ROWS
    block_expert = jnp.minimum(jnp.sum((pend[None, :] <= block_row[:, None]).astype(I32), axis=1),
                               N_EXPERTS - 1)
    n_used = (pend[-1] // MOE_ROWS).astype(I32).reshape(1)
    first = jnp.concatenate([jnp.ones((1,), I32), (block_expert[1:] != block_expert[:-1]).astype(I32)])
    later = jnp.logical_and(counts[None, :] > 0, ar[None, :] > ar[:, None])
    next_nonempty = jnp.min(jnp.where(later, ar[None, :], N_EXPERTS), axis=1)
    next_nonempty = jnp.where(next_nonempty < N_EXPERTS, next_nonempty, -1)
    next_expert = jnp.sum(jnp.where(block_expert[:, None] == ar[None, :], next_nonempty[None, :], 0), axis=1)
    assert t + MOE_ROWS <= 1 << TOKEN_BITS
    real = ((top_e << TOKEN_BITS) | jnp.arange(t, dtype=I32)[:, None]).reshape(-1)
    r = jnp.arange(MOE_ROWS - 1, dtype=I32)
    int_max = jnp.iinfo(jnp.int32).max
    pad_keys = jnp.where(r[None, :] < (padded - counts)[:, None],
                         (ar[:, None] << TOKEN_BITS) | (t + r[None, :]), int_max).reshape(-1)
    n_rows = n_blocks * MOE_ROWS
    fill = jnp.full((n_rows - real.shape[0] - pad_keys.shape[0],), int_max, I32)
    tokv = jnp.sort(jnp.concatenate([real, pad_keys, fill])) & ((1 << TOKEN_BITS) - 1)
    tok_buf = jnp.where(tokv < t, tokv, 0)
    lo = excl[::COMB_TOK]
    hi = jnp.concatenate([lo[1:], counts[None]], axis=0)
    n_chunks = (hi - lo + COMB_CH - 1) // COMB_CH
    cend = jnp.cumsum(n_chunks, axis=1)
    cstart = cend - n_chunks
    n_total = cend[:, -1].astype(I32)
    src_start = pstart[None, :] + lo
    q = jnp.arange(COMB_MAX_CHUNKS, dtype=I32)
    eq = jnp.minimum(jnp.sum((cend[:, None, :] <= q[None, :, None]).astype(I32), axis=-1), N_EXPERTS - 1)
    sel_q = eq[:, :, None] == ar
    pick_q = lambda v: jnp.sum(jnp.where(sel_q, v[:, None, :], 0), axis=-1)
    src_flat = pick_q(src_start) + (q[None, :] - pick_q(cstart)) * COMB_CH
    src_flat = jnp.where(q[None, :] < n_total[:, None], src_flat, 0)
    stage_rows = COMB_MAX_CHUNKS * COMB_CH
    slot_base = (jnp.arange(t // COMB_TOK, dtype=I32) % 2) * stage_rows
    local = jnp.repeat(cstart * COMB_CH - lo + slot_base[:, None], COMB_TOK, axis=0)
    lidx = pick(local[:, None, :]) + rank
    return dict(block_expert=block_expert.astype(I32), n_used=n_used, tok_buf=tok_buf.astype(I32),
                first=first, next_expert=next_expert.astype(I32),
                n_blocks=n_blocks, src_flat=src_flat.astype(I32), n_total=n_total,
                lidx=lidx.astype(I32), stage_rows=stage_rows)


def kernel(x_prompt, x_sample, state_ssm_re, state_ssm_im, state_conv, c_prompt, c_sample, w_ada, b_ada, norm1_g, norm2_g, w_in, ssm_a_re, ssm_a_im, ssm_log_dt, ssm_b_re, ssm_b_im, ssm_c_re, ssm_c_im, ssm_d, w_glu, b_glu, w_lift_a, conv_w, conv_b, w_lift_b, w_out, w_router, router_bias, w_exp_gate, w_exp_up, w_exp_down, w_sh_gate, w_sh_up, w_sh_down, final_norm_g):
    depth = w_ada.shape[0]
    assert depth == 1
    bp, seq, _ = x_prompt.shape
    bs = x_sample.shape[0]
    tp = bp * seq
    l = 0

    w_in_b = w_in[l].astype(BF16)
    wr_t = w_router[l].T
    wr_hi = wr_t.astype(BF16)
    wr_lo = (wr_t - wr_hi.astype(F32)).astype(BF16)
    mix_w = (ssm_d[l].reshape(1, -1), w_glu[l].astype(BF16), b_glu[l].reshape(1, -1),
             w_lift_a[l].astype(BF16), conv_w[l], conv_b[l].reshape(1, -1), w_lift_b[l].astype(BF16),
             w_out[l].astype(BF16), norm2_g[l].reshape(1, -1), wr_hi, wr_lo)
    g1 = norm1_g[l].reshape(1, -1)

    ada = _ada(jnp.concatenate([c_prompt, c_sample], axis=0), w_ada[l], b_ada[l])
    mods_p = ada[:bp].reshape(bp, 1, -1)
    mods_s = ada[bp:]

    apow, ssm_tm, ssm_bk, ssm_ck = _ssm_prep(ssm_a_re[l], ssm_a_im[l], ssm_log_dt[l], ssm_b_re[l],
                                             ssm_b_im[l], ssm_c_re[l], ssm_c_im[l])
    a1_re, a1_im, al_re, al_im = (_state_layout(apow[i]) for i in range(4))

    xp = x_prompt.reshape(tp, D_MODEL)
    u_p, rest_p = _in_proj(xp, mods_p, seq, g1, w_in_b, tm=512)
    n_chunks = seq // SSM_CHUNK
    s_loc = _ssm_state(u_p, ssm_bk, SSM_CHUNK, n_chunks)
    s_prev, s_fin_p = _ssm_scan(s_loc.reshape(bp, n_chunks, -1), al_re, al_im,
                                jnp.zeros((bp, OCTETS * OCT_STATE), F32))
    y_p = _ssm_out(u_p, s_prev.reshape(bp * n_chunks, -1), ssm_tm, ssm_ck, SSM_CHUNK, n_chunks)
    x1_p, h2_p, lg_p, conv_p = _mixer_post(xp, y_p, u_p, rest_p, mods_p, seq, mix_w, None, tm=256)

    xs = x_sample.reshape(bs, D_MODEL)
    u_s, rest_s = _in_proj(xs, mods_s, 1, g1, w_in_b, tm=bs)
    s0_s = _pack_state(state_ssm_re[l], state_ssm_im[l])
    s_new_s = _ssm_step(_ssm_state(u_s, ssm_bk, 1, bs), s0_s, a1_re, a1_im)
    y_s = _ssm_out(u_s, s0_s, ssm_tm, ssm_ck, 1, bs)
    x1_s, h2_s, lg_s, nc0, nc1 = _mixer_post(xs, y_s, u_s, rest_s, mods_s, 1, mix_w,
                                                    state_conv[l], tm=bs)

    t = tp + bs
    te_p, gate_p = _route(lg_p, router_bias[l])
    te_s, gate_s = _route(lg_s, router_bias[l])
    top_e = jnp.concatenate([te_p, te_s], axis=1).T
    gate = jnp.concatenate([gate_p, gate_s], axis=1).T
    plan = _moe_plan(top_e, t)
    h_tiles = jnp.concatenate([h2_p, h2_s], axis=0).reshape(t, ROW_TILES, LANES)
    xs_sorted = _dispatch(h_tiles, plan["tok_buf"], plan["n_used"], plan["n_blocks"])
    ys = _experts(xs_sorted, plan["block_expert"], plan["n_used"], plan["first"], plan["next_expert"],
                  w_exp_gate[l], w_exp_up[l], w_exp_down[l], plan["n_blocks"])
    routed = _combine(ys, plan["src_flat"], plan["n_total"], plan["lidx"].reshape(-1),
                      gate.reshape(-1), t, plan["stage_rows"])

    w_sg, w_su, w_sd = w_sh_gate[l].astype(BF16), w_sh_up[l].astype(BF16), w_sh_down[l].astype(BF16)
    gf = final_norm_g.reshape(1, -1)
    y_prompt = _final(h2_p, x1_p, routed, 0, mods_p, seq, w_sg, w_su, w_sd, gf, tm=512)
    y_sample = _final(h2_s, x1_s, routed, tp, mods_s, 1, w_sg, w_su, w_sd, gf, tm=bs)

    re_p, im_p = _unpack_state(s_fin_p)
    re_s, im_s = _unpack_state(s_new_s)
    return (y_prompt.reshape(bp, seq, D_MODEL), y_sample.reshape(bs, 1, D_MODEL),
            re_p[None], im_p[None], conv_p[None],
            re_s[None], im_s[None], jnp.stack([nc0, nc1], axis=1)[None])
```

```python
import functools

import jax
import jax.numpy as jnp
from jax import lax
from jax.experimental import pallas as pl
from jax.experimental.pallas import tpu as pltpu

F32 = jnp.float32
BF16 = jnp.bfloat16
I32 = jnp.int32

D_MODEL = 2048
D_SSM = 1024
SSM_GROUP = 16
N_GROUPS = 64
SSM_STATE = 64
D_CONV = 1024
CONV_W = 3
N_EXPERTS = 64
TOP_K = 8
N_EXPERT_GROUPS = 8
EXPERTS_PER_GROUP = 8
TOPK_GROUPS = 4
D_EXPERT = 512
ROUTED_SCALE = 2.5
RMS_EPS = 1e-6

LANES = 128
SUBLANES = 8
OCTETS = N_GROUPS * SSM_GROUP // LANES
OCT_STATE = 2 * (N_GROUPS // OCTETS) * SSM_STATE
SSM_CHUNK = 8
MOE_ROWS = 256
COMB_TOK = 128
COMB_CH = 16
ROW_TILES = D_MODEL // LANES
ROW_STRIDE = MOE_ROWS + SUBLANES
DISP_ROWS = 2 * MOE_ROWS
DISP_STRIDE = DISP_ROWS + SUBLANES
COMB_MAX_CHUNKS = COMB_TOK * TOP_K // COMB_CH + N_EXPERTS
TOKEN_BITS = 14
MIX_ROWS = 128
NEG_INF = float("-inf")


def _cparams(sem, vmem_mb):
    return pltpu.CompilerParams(dimension_semantics=sem, vmem_limit_bytes=vmem_mb * 1024 * 1024)


def _const_spec(shape):
    nd = len(shape)
    return pl.BlockSpec(shape, lambda *_: (0,) * nd, pipeline_mode=pl.Buffered(1))


def _mod(ref):
    v = ref[...]
    return v[0] if v.ndim == 3 else v


def _silu(x):
    return x * jax.nn.sigmoid(x)


def _ada_kernel(c_ref, w_ref, b_ref, o_ref):
    a = _silu(c_ref[...]).astype(BF16)
    o_ref[...] = jnp.dot(a, w_ref[...].astype(BF16), preferred_element_type=F32) + b_ref[...]


def _ada(c_all, w_ada, b_ada):
    m = c_all.shape[0]
    n = w_ada.shape[1]
    tn = 1024
    return pl.pallas_call(
        _ada_kernel,
        grid=(n // tn,),
        in_specs=[pl.BlockSpec((m, D_MODEL), lambda j: (0, 0)),
                  pl.BlockSpec((D_MODEL, tn), lambda j: (0, j)),
                  pl.BlockSpec((1, tn), lambda j: (0, j))],
        out_specs=pl.BlockSpec((m, tn), lambda j: (0, j)),
        out_shape=jax.ShapeDtypeStruct((m, n), F32),
        compiler_params=_cparams(("arbitrary",), 40),
        name="ada",
    )(c_all, w_ada, b_ada.reshape(1, n))


def _mod_spec(mods, col, rows_per_batch, tm):
    if mods.ndim == 3:
        tiles_per_batch = rows_per_batch // tm
        return pl.BlockSpec((1, 1, D_MODEL), lambda m, *_: (m // tiles_per_batch, 0, col))
    return pl.BlockSpec((tm, D_MODEL), lambda m, *_: (m, col))


def _inproj_kernel(x_ref, g_ref, sh_ref, sc_ref, w_ref, u_ref, r_ref):
    x = x_ref[...]
    y = x * lax.rsqrt(jnp.mean(x * x, axis=-1, keepdims=True) + RMS_EPS) * g_ref[...]
    h = (y * (1.0 + _mod(sc_ref)) + _mod(sh_ref)).astype(BF16)
    tn = D_SSM
    u_ref[...] = jnp.dot(h, w_ref[:, 0:tn], preferred_element_type=F32)
    for n in range(1, w_ref.shape[1] // tn):
        r_ref[:, (n - 1) * tn:n * tn] = jnp.dot(h, w_ref[:, n * tn:(n + 1) * tn],
                                                preferred_element_type=F32).astype(BF16)


def _in_proj(x, mods, rows_per_batch, g1, w_in, tm):
    t = x.shape[0]
    d_in = w_in.shape[1]
    n_rest = d_in - D_SSM
    return pl.pallas_call(
        _inproj_kernel,
        grid=(t // tm,),
        in_specs=[pl.BlockSpec((tm, D_MODEL), lambda m: (m, 0)),
                  _const_spec((1, D_MODEL)),
                  _mod_spec(mods, 0, rows_per_batch, tm),
                  _mod_spec(mods, 1, rows_per_batch, tm),
                  _const_spec((D_MODEL, d_in))],
        out_specs=[pl.BlockSpec((tm, D_SSM), lambda m: (m, 0)),
                   pl.BlockSpec((tm, n_rest), lambda m: (m, 0))],
        out_shape=[jax.ShapeDtypeStruct((t, D_SSM), F32),
                   jax.ShapeDtypeStruct((t, n_rest), BF16)],
        compiler_params=_cparams(("arbitrary",), 56),
        name="in_proj",
    )(x, g1, mods, mods, w_in)


def _ssm_prep_kernel(are_ref, aim_ref, ldt_ref, btr_ref, bti_ref, cre_ref, cim_ref,
                     apow_ref, tm_ref, bk_ref, ck_ref):
    lc = SSM_CHUNK
    ng = N_GROUPS // OCTETS
    half = OCT_STATE // 2
    lr = are_ref[...]
    li = aim_ref[...]
    dt = jnp.exp(ldt_ref[...])
    mag = jnp.exp(lr * dt)
    ar = mag * jnp.cos(li * dt)
    ai = mag * jnp.sin(li * dt)
    den = lr * lr + li * li
    xr = ar - 1.0
    cr = (xr * lr + ai * li) / den
    ci = (ai * lr - xr * li) / den
    btr = btr_ref[...]
    bti = bti_ref[...]
    bbr = cr * btr - ci * bti
    bbi = cr * bti + ci * btr
    cre = cre_ref[...]
    cim = cim_ref[...]
    cre_w = jnp.concatenate([cre] * ng, axis=1)
    cim_w = jnp.concatenate([cim] * ng, axis=1)

    def iota(shape, dim):
        return lax.broadcasted_iota(I32, shape, dim)

    width = lc * LANES
    sel_b = ((iota((SSM_STATE, half), 1) & (SSM_STATE - 1)) == iota((SSM_STATE, half), 0)).astype(BF16)
    mask_b = ((iota((width, half), 0) & (LANES - 1)) >> 4) == (iota((width, half), 1) >> 6)
    sel_c = ((iota((half, SSM_STATE), 0) & (SSM_STATE - 1)) == iota((half, SSM_STATE), 1)).astype(BF16)
    mask_c = (iota((half, width), 0) >> 6) == ((iota((half, width), 1) & (LANES - 1)) >> 4)
    mask_t = (iota((LANES, LANES), 0) >> 4) == (iota((LANES, LANES), 1) >> 4)
    dn_batch = (((2,), (2,)), ((0,), (0,)))
    dn_nt = (((1,), (1,)), ((), ()))

    pr = jnp.ones_like(ar)
    pi = jnp.zeros_like(ar)
    w1r, w1i, w2r, w2i = [], [], [], []
    for l in range(lc):
        w1r.append(pr * bbr - pi * bbi)
        w1i.append(pr * bbi + pi * bbr)
        pr, pi = pr * ar - pi * ai, pr * ai + pi * ar
        w2r.append(cre * pr - cim * pi)
        w2i.append(-(cre * pi + cim * pr))
        if l == 0:
            apow_ref[0] = pr
            apow_ref[1] = pi
    apow_ref[2] = pr
    apow_ref[3] = pi

    def rows(ws, order):
        return jnp.concatenate([ws[l] for l in order], axis=0).reshape(width, SSM_STATE).astype(BF16)

    rev = range(lc - 1, -1, -1)
    for c, ws in ((0, w1r), (1, w1i)):
        rep = jnp.dot(rows(ws, rev), sel_b, preferred_element_type=F32)
        bk_ref[:, c * half:(c + 1) * half] = jnp.where(mask_b, rep, 0.0).astype(BF16)
    for c, ws in ((0, w2r), (1, w2i)):
        rep = lax.dot_general(sel_c, rows(ws, range(lc)), dn_nt, preferred_element_type=F32)
        ck_ref[c * half:(c + 1) * half, :] = jnp.where(mask_c, rep, 0.0).astype(BF16)
    w1r_all = jnp.concatenate(w1r, axis=1)
    w1i_all = jnp.concatenate(w1i, axis=1)
    kt = (lax.dot_general(w1r_all, cre_w, dn_batch, precision=lax.Precision.HIGHEST,
                          preferred_element_type=F32)
          - lax.dot_general(w1i_all, cim_w, dn_batch, precision=lax.Precision.HIGHEST,
                            preferred_element_type=F32))
    hh = SSM_GROUP
    t_blocks = [jnp.where(mask_t, kt[:, l * hh:(l + 1) * hh, :].reshape(LANES, LANES), 0.0).astype(BF16)
                for l in range(lc)]
    zero = jnp.zeros((LANES, LANES), BF16)
    for s in range(lc):
        for t in range(lc):
            tm_ref[s * LANES:(s + 1) * LANES, t * LANES:(t + 1) * LANES] = t_blocks[t - s] if t >= s else zero


def _ssm_prep(a_re, a_im, log_dt, b_re, b_im, c_re, c_im):
    g, p, h = N_GROUPS, SSM_STATE, SSM_GROUP
    ng = g // OCTETS
    width = SSM_CHUNK * LANES
    vec = pl.BlockSpec((ng, 1, p), lambda o: (o, 0, 0))
    mat = pl.BlockSpec((ng, h, p), lambda o: (o, 0, 0))
    big = lambda rows, cols: pl.BlockSpec((None, rows, cols), lambda o: (o, 0, 0))
    return pl.pallas_call(
        _ssm_prep_kernel,
        grid=(OCTETS,),
        in_specs=[vec, vec, pl.BlockSpec((ng, 1, 1), lambda o: (o, 0, 0)), mat, mat, mat, mat],
        out_specs=[pl.BlockSpec((4, ng, 1, p), lambda o: (0, o, 0, 0)),
                   big(width, width), big(width, OCT_STATE), big(OCT_STATE, width)],
        out_shape=[jax.ShapeDtypeStruct((4, g, 1, p), F32),
                   jax.ShapeDtypeStruct((OCTETS, width, width), BF16),
                   jax.ShapeDtypeStruct((OCTETS, width, OCT_STATE), BF16),
                   jax.ShapeDtypeStruct((OCTETS, OCT_STATE, width), BF16)],
        compiler_params=_cparams(("arbitrary",), 32),
        name="ssm_prep",
    )(a_re.reshape(g, 1, p), a_im.reshape(g, 1, p), log_dt.reshape(g, 1, 1),
      b_re.transpose(0, 2, 1), b_im.transpose(0, 2, 1), c_re, c_im)


def _state_layout(v):
    return v.reshape(OCTETS, 1, OCT_STATE // 2)


def _load_chunks(u_ref, lc, rb):
    parts = [u_ref[pl.ds(s, rb, stride=lc), :] if lc > 1 else u_ref[...] for s in range(lc)]
    u = parts[0] if lc == 1 else jnp.concatenate(parts, axis=-1)
    return u.astype(BF16)


def _ssm_state_kernel(u_ref, bk_ref, s_ref, *, lc, rb):
    s_ref[...] = jnp.dot(_load_chunks(u_ref, lc, rb), bk_ref[...], preferred_element_type=F32)


def _ssm_state(u, bk, lc, rb):
    t = u.shape[0]
    nb = t // (rb * lc)
    lag0_block = SSM_CHUNK // lc - 1
    return pl.pallas_call(
        functools.partial(_ssm_state_kernel, lc=lc, rb=rb),
        grid=(OCTETS, nb),
        in_specs=[pl.BlockSpec((rb * lc, LANES), lambda o, b: (b, o)),
                  pl.BlockSpec((None, lc * LANES, OCT_STATE), lambda o, b: (o, lag0_block, 0))],
        out_specs=pl.BlockSpec((rb, OCT_STATE), lambda o, b: (b, o)),
        out_shape=jax.ShapeDtypeStruct((t // lc, OCTETS * OCT_STATE), F32),
        compiler_params=_cparams(("arbitrary", "arbitrary"), 32),
        name="ssm_state",
    )(u, bk)


def _ssm_scan_kernel(sl_ref, ar_ref, ai_ref, s0_ref, sp_ref, sf_ref, *, nb, nc):
    half = OCT_STATE // 2
    ar = ar_ref[0]
    ai = ai_ref[0]
    init = tuple((s0_ref[b:b + 1, 0:half], s0_ref[b:b + 1, half:OCT_STATE]) for b in range(nb))

    def body(c, carry):
        new = []
        for b in range(nb):
            sr, si = carry[b]
            sp_ref[b, pl.ds(c, 1), 0:half] = sr
            sp_ref[b, pl.ds(c, 1), half:OCT_STATE] = si
            xr = sl_ref[b, pl.ds(c, 1), 0:half]
            xi = sl_ref[b, pl.ds(c, 1), half:OCT_STATE]
            new.append((ar * sr - ai * si + xr, ar * si + ai * sr + xi))
        return tuple(new)

    fin = lax.fori_loop(0, nc, body, init)
    for b in range(nb):
        sf_ref[b:b + 1, 0:half] = fin[b][0]
        sf_ref[b:b + 1, half:OCT_STATE] = fin[b][1]


def _ssm_scan(s_loc, al_re, al_im, s0):
    nb, nc, width = s_loc.shape
    return pl.pallas_call(
        functools.partial(_ssm_scan_kernel, nb=nb, nc=nc),
        grid=(OCTETS,),
        in_specs=[pl.BlockSpec((nb, nc, OCT_STATE), lambda o: (0, 0, o)),
                  pl.BlockSpec((1, 1, OCT_STATE // 2), lambda o: (o, 0, 0)),
                  pl.BlockSpec((1, 1, OCT_STATE // 2), lambda o: (o, 0, 0)),
                  pl.BlockSpec((nb, OCT_STATE), lambda o: (0, o))],
        out_specs=[pl.BlockSpec((nb, nc, OCT_STATE), lambda o: (0, 0, o)),
                   pl.BlockSpec((nb, OCT_STATE), lambda o: (0, o))],
        out_shape=[jax.ShapeDtypeStruct((nb, nc, width), F32),
                   jax.ShapeDtypeStruct((nb, width), F32)],
        compiler_params=_cparams(("arbitrary",), 32),
        name="ssm_scan",
    )(s_loc, al_re, al_im, s0)


def _ssm_out_kernel(u_ref, sp_ref, tm_ref, ck_ref, y_ref, *, lc, rb):
    y = (jnp.dot(_load_chunks(u_ref, lc, rb), tm_ref[...], preferred_element_type=F32)
         + jnp.dot(sp_ref[...].astype(BF16), ck_ref[...], preferred_element_type=F32))
    if lc == 1:
        y_ref[...] = y
    else:
        for t in range(lc):
            y_ref[pl.ds(t, rb, stride=lc), :] = y[:, t * LANES:(t + 1) * LANES]


def _ssm_out(u, s_prev, tm, ck, lc, rb):
    t = u.shape[0]
    nb = t // (rb * lc)
    return pl.pallas_call(
        functools.partial(_ssm_out_kernel, lc=lc, rb=rb),
        grid=(OCTETS, nb),
        in_specs=[pl.BlockSpec((rb * lc, LANES), lambda o, b: (b, o)),
                  pl.BlockSpec((rb, OCT_STATE), lambda o, b: (b, o)),
                  pl.BlockSpec((None, lc * LANES, lc * LANES), lambda o, b: (o, 0, 0)),
                  pl.BlockSpec((None, OCT_STATE, lc * LANES), lambda o, b: (o, 0, 0))],
        out_specs=pl.BlockSpec((rb * lc, LANES), lambda o, b: (b, o)),
        out_shape=jax.ShapeDtypeStruct((t, D_SSM), F32),
        compiler_params=_cparams(("arbitrary", "arbitrary"), 32),
        name="ssm_out",
    )(u, s_prev, tm, ck)


def _ssm_step_kernel(sl_ref, s0_ref, ar_ref, ai_ref, sn_ref):
    half = OCT_STATE // 2
    ar = ar_ref[0]
    ai = ai_ref[0]
    sr = s0_ref[:, 0:half]
    si = s0_ref[:, half:OCT_STATE]
    sn_ref[:, 0:half] = ar * sr - ai * si + sl_ref[:, 0:half]
    sn_ref[:, half:OCT_STATE] = ar * si + ai * sr + sl_ref[:, half:OCT_STATE]


def _ssm_step(s_loc, s0, a_re, a_im):
    rows, width = s_loc.shape
    return pl.pallas_call(
        _ssm_step_kernel,
        grid=(OCTETS,),
        in_specs=[pl.BlockSpec((rows, OCT_STATE), lambda o: (0, o)),
                  pl.BlockSpec((rows, OCT_STATE), lambda o: (0, o)),
                  pl.BlockSpec((1, 1, OCT_STATE // 2), lambda o: (o, 0, 0)),
                  pl.BlockSpec((1, 1, OCT_STATE // 2), lambda o: (o, 0, 0))],
        out_specs=pl.BlockSpec((rows, OCT_STATE), lambda o: (0, o)),
        out_shape=jax.ShapeDtypeStruct((rows, width), F32),
        compiler_params=_cparams(("arbitrary",), 32),
        name="ssm_step",
    )(s_loc, s0, a_re, a_im)


def _pack_state(s_re, s_im):
    b = s_re.shape[0]
    half = OCT_STATE // 2
    return jnp.stack([s_re.reshape(b, OCTETS, half), s_im.reshape(b, OCTETS, half)], axis=2).reshape(b, -1)


def _unpack_state(s):
    b = s.shape[0]
    s4 = s.reshape(b, OCTETS, 2, OCT_STATE // 2)
    return (s4[:, :, 0].reshape(b, N_GROUPS, SSM_STATE), s4[:, :, 1].reshape(b, N_GROUPS, SSM_STATE))


def _mixer_kernel(*refs, seq, tm, tiles_per_batch):
    (x_ref, ys_ref, u_ref, r_ref, g1_ref, sh2_ref, sc2_ref, d_ref, wglu_ref, bglu_ref, wla_ref,
     cw_ref, cb_ref, wlb_ref, wout_ref, n2g_ref, wrh_ref, wrl_ref) = refs[:18]
    if seq:
        x1_ref, h2_ref, lg_ref, nc_ref, carry = refs[18:]
    else:
        cb0_ref, cb1_ref, x1_ref, h2_ref, lg_ref, nc0_ref, nc1_ref = refs[18:]

    xb = r_ref[:, 0:D_CONV].astype(F32)
    gb = r_ref[:, D_CONV:2 * D_CONV].astype(F32)
    gc = r_ref[:, 2 * D_CONV:3 * D_CONV].astype(F32)
    v = gc * xb
    if seq:
        first = pl.program_id(0) % tiles_per_batch == 0

        @pl.when(first)
        def _():
            carry[...] = jnp.zeros_like(carry)

        c0 = carry[0:1, :]
        c1 = carry[1:2, :]
        row = lax.broadcasted_iota(I32, (tm, 1), 0)
        v1 = jnp.where(row == 0, c1, pltpu.roll(v, 1, 0))
        v2 = jnp.where(row == 0, c0, jnp.where(row == 1, c1, pltpu.roll(v, 2, 0)))
        tail = v[tm - 2:tm, :]
        carry[...] = tail
        nc_ref[0] = tail
    else:
        v2 = cb0_ref[...]
        v1 = cb1_ref[...]
        nc0_ref[...] = v1
        nc1_ref[...] = v
    conv = cb_ref[...] + cw_ref[0:1, :] * v2 + cw_ref[1:2, :] * v1 + cw_ref[2:3, :] * v
    gated_conv = (gb * conv).astype(BF16)

    def mod_rows(ref, rs):
        return ref[0] if len(ref.shape) == 3 else ref[rs, :]

    n_split = max(tm // MIX_ROWS, 1)
    rows = tm // n_split
    off = 3 * D_CONV
    dn = (((1,), (1,)), ((), ()))
    for part in range(n_split):
        rs = slice(part * rows, (part + 1) * rows)
        ya = jax.nn.gelu(ys_ref[rs, :] + d_ref[...] * u_ref[rs, :], approximate=True)
        z = jnp.dot(ya.astype(BF16), wglu_ref[...], preferred_element_type=F32) + bglu_ref[...]
        ya = ya * jax.nn.sigmoid(z)
        branch_a = jnp.dot(ya.astype(BF16), wla_ref[...], preferred_element_type=F32)
        branch_b = jnp.dot(gated_conv[rs], wlb_ref[...], preferred_element_type=F32)
        gla = r_ref[rs, off:off + D_MODEL].astype(F32)
        glb = r_ref[rs, off + D_MODEL:off + 2 * D_MODEL].astype(F32)
        merged = jax.nn.sigmoid(gla) * branch_a + jax.nn.sigmoid(glb) * branch_b
        x1 = x_ref[rs, :] + mod_rows(g1_ref, rs) * jnp.dot(merged.astype(BF16), wout_ref[...],
                                                           preferred_element_type=F32)
        x1_ref[rs, :] = x1

        y = x1 * lax.rsqrt(jnp.mean(x1 * x1, axis=-1, keepdims=True) + RMS_EPS) * n2g_ref[...]
        h2 = y * (1.0 + mod_rows(sc2_ref, rs)) + mod_rows(sh2_ref, rs)
        h2b = h2.astype(BF16)
        h2_ref[rs, :] = h2b

        h2l = (h2 - h2b.astype(F32)).astype(BF16)
        lg_ref[:, rs] = (lax.dot_general(wrh_ref[...], h2b, dn, preferred_element_type=F32)
                         + lax.dot_general(wrh_ref[...], h2l, dn, preferred_element_type=F32)
                         + lax.dot_general(wrl_ref[...], h2b, dn, preferred_element_type=F32))


def _mixer_post(x, y_ssm, u, rest, mods, rows_per_batch, weights, conv_state, tm):
    t = x.shape[0]
    seq = conv_state is None
    tiles_per_batch = rows_per_batch // tm
    (ssm_d, w_glu, b_glu, w_la, conv_w, conv_b, w_lb, w_out, n2g, wr_hi, wr_lo) = weights
    row = lambda m: (m, 0)
    in_specs = [pl.BlockSpec((tm, D_MODEL), row),
                pl.BlockSpec((tm, D_SSM), row),
                pl.BlockSpec((tm, D_SSM), row),
                pl.BlockSpec((tm, rest.shape[1]), row),
                _mod_spec(mods, 2, rows_per_batch, tm),
                _mod_spec(mods, 3, rows_per_batch, tm),
                _mod_spec(mods, 4, rows_per_batch, tm),
                _const_spec((1, D_SSM)), _const_spec((D_SSM, D_SSM)), _const_spec((1, D_SSM)),
                _const_spec((D_SSM, D_MODEL)), _const_spec((CONV_W, D_CONV)), _const_spec((1, D_CONV)),
                _const_spec((D_CONV, D_MODEL)), _const_spec((D_MODEL, D_MODEL)),
                _const_spec((1, D_MODEL)), _const_spec((N_EXPERTS, D_MODEL)),
                _const_spec((N_EXPERTS, D_MODEL))]
    args = [x, y_ssm, u, rest, mods, mods, mods, ssm_d, w_glu, b_glu, w_la, conv_w, conv_b, w_lb,
            w_out, n2g, wr_hi, wr_lo]
    out_specs = [pl.BlockSpec((tm, D_MODEL), row),
                 pl.BlockSpec((tm, D_MODEL), row),
                 pl.BlockSpec((N_EXPERTS, tm), lambda m: (0, m))]
    out_shape = [jax.ShapeDtypeStruct((t, D_MODEL), F32),
                 jax.ShapeDtypeStruct((t, D_MODEL), BF16),
                 jax.ShapeDtypeStruct((N_EXPERTS, t), F32)]
    scratch = []
    if seq:
        nb = t // rows_per_batch
        out_specs.append(pl.BlockSpec((1, CONV_W - 1, D_CONV), lambda m: (m // tiles_per_batch, 0, 0)))
        out_shape.append(jax.ShapeDtypeStruct((nb, CONV_W - 1, D_CONV), F32))
        scratch.append(pltpu.VMEM((CONV_W - 1, D_CONV), F32))
    else:
        in_specs += [pl.BlockSpec((tm, D_CONV), row), pl.BlockSpec((tm, D_CONV), row)]
        args += [conv_state[:, 0, :], conv_state[:, 1, :]]
        out_specs += [pl.BlockSpec((tm, D_CONV), row), pl.BlockSpec((tm, D_CONV), row)]
        out_shape += [jax.ShapeDtypeStruct((t, D_CONV), F32), jax.ShapeDtypeStruct((t, D_CONV), F32)]
    return pl.pallas_call(
        functools.partial(_mixer_kernel, seq=seq, tm=tm, tiles_per_batch=tiles_per_batch),
        grid=(t // tm,),
        in_specs=in_specs,
        out_specs=out_specs,
        out_shape=out_shape,
        scratch_shapes=scratch,
        compiler_params=_cparams(("arbitrary",), 56),
        name="mixer_post",
    )(*args)


def _route_kernel(lg_ref, bias_ref, te_ref, gate_ref):
    scores = [jax.nn.sigmoid(lg_ref[e]) for e in range(N_EXPERTS)]
    choice = [scores[e] + bias_ref[e] for e in range(N_EXPERTS)]
    shape = scores[0].shape

    def first_max(vals):
        m = functools.reduce(jnp.maximum, vals)
        found = jnp.zeros(shape, jnp.bool_)
        hot = []
        for v in vals:
            is_m = jnp.logical_and(v == m, jnp.logical_not(found))
            hot.append(is_m)
            found = jnp.logical_or(found, is_m)
        return m, hot

    gscore = []
    for g in range(N_EXPERT_GROUPS):
        vals = choice[g * EXPERTS_PER_GROUP:(g + 1) * EXPERTS_PER_GROUP]
        m1, hot = first_max(vals)
        rest = [jnp.where(h, NEG_INF, v) for h, v in zip(hot, vals)]
        gscore.append(m1 + functools.reduce(jnp.maximum, rest))

    gsel = [jnp.zeros(shape, jnp.bool_) for _ in range(N_EXPERT_GROUPS)]
    for _ in range(TOPK_GROUPS):
        _, hot = first_max(gscore)
        gsel = [jnp.logical_or(s, h) for s, h in zip(gsel, hot)]
        gscore = [jnp.where(h, NEG_INF, v) for h, v in zip(hot, gscore)]

    cand = [jnp.where(gsel[e // EXPERTS_PER_GROUP], choice[e], NEG_INF) for e in range(N_EXPERTS)]
    picked_e = []
    picked_s = []
    for _ in range(TOP_K):
        _, hot = first_max(cand)
        idx = jnp.zeros(shape, I32)
        sc = jnp.zeros(shape, F32)
        for e in range(N_EXPERTS):
            idx = jnp.where(hot[e], e, idx)
            sc = jnp.where(hot[e], scores[e], sc)
        cand = [jnp.where(h, NEG_INF, v) for h, v in zip(hot, cand)]
        picked_e.append(idx)
        picked_s.append(sc)
    total = functools.reduce(lambda a, b: a + b, picked_s)
    for k in range(TOP_K):
        te_ref[k] = picked_e[k]
        gate_ref[k] = picked_s[k] / total * ROUTED_SCALE


def _route(logits_t, bias):
    t = logits_t.shape[1]
    r = t // LANES
    rb = min(r, SUBLANES)
    lg = logits_t.reshape(N_EXPERTS, r, LANES)
    bias_b = jnp.broadcast_to(bias.astype(F32).reshape(N_EXPERTS, 1, 1), (N_EXPERTS, 1, LANES))
    te, gate = pl.pallas_call(
        _route_kernel,
        grid=(r // rb,),
        in_specs=[pl.BlockSpec((N_EXPERTS, rb, LANES), lambda i: (0, i, 0)),
                  pl.BlockSpec((N_EXPERTS, 1, LANES), lambda i: (0, 0, 0))],
        out_specs=[pl.BlockSpec((TOP_K, rb, LANES), lambda i: (0, i, 0)),
                   pl.BlockSpec((TOP_K, rb, LANES), lambda i: (0, i, 0))],
        out_shape=[jax.ShapeDtypeStruct((TOP_K, r, LANES), I32),
                   jax.ShapeDtypeStruct((TOP_K, r, LANES), F32)],
        compiler_params=_cparams(("arbitrary",), 32),
        name="route",
    )(lg, bias_b)
    return te.reshape(TOP_K, t), gate.reshape(TOP_K, t)


def _dispatch_kernel(nused_ref, tok_ref, h_ref, xs_ref, tile):
    used_rows = nused_ref[0] * MOE_ROWS
    live = pl.program_id(0) * DISP_ROWS < used_rows

    @pl.when(jnp.logical_not(live))
    def _():
        xs_ref[...] = jnp.zeros_like(xs_ref)

    @pl.when(live)
    def _():
        for mi in range(DISP_ROWS):
            tile[pl.ds(mi, ROW_TILES, stride=DISP_STRIDE), :] = h_ref[tok_ref[mi]].astype(F32)
        for j in range(ROW_TILES):
            xs_ref[:, j * LANES:(j + 1) * LANES] = (
                tile[j * DISP_STRIDE:j * DISP_STRIDE + DISP_ROWS, :].astype(BF16))


def _dispatch(h_tiles, tok_buf, n_used, n_blocks):
    n_rows = n_blocks * MOE_ROWS
    assert n_rows % DISP_ROWS == 0

    def last_live(i, nu):
        return jnp.minimum(i, (nu[0] * MOE_ROWS - 1) // DISP_ROWS)

    return pl.pallas_call(
        _dispatch_kernel,
        grid_spec=pltpu.PrefetchScalarGridSpec(
            num_scalar_prefetch=1,
            grid=(n_rows // DISP_ROWS,),
            in_specs=[pl.BlockSpec((DISP_ROWS,), lambda i, nu: (last_live(i, nu),),
                                   memory_space=pltpu.SMEM),
                      pl.BlockSpec(memory_space=pltpu.VMEM)],
            out_specs=pl.BlockSpec((DISP_ROWS, D_MODEL), lambda i, nu: (i, 0)),
            scratch_shapes=[pltpu.VMEM((ROW_TILES * DISP_STRIDE, LANES), F32)]),
        out_shape=jax.ShapeDtypeStruct((n_rows, D_MODEL), BF16),
        compiler_params=_cparams(("arbitrary",), 56),
        name="dispatch",
    )(n_used, tok_buf, h_tiles)


def _experts_kernel(be_ref, nused_ref, first_ref, next_ref, xs_ref, wg_hbm, wu_hbm, wd_hbm, ys_ref,
                    wg_f, wu_f, wd_f, wg_s, wu_s, wd_s, tile, sem):
    i = pl.program_id(0)

    def weight_copies(e):
        return (pltpu.make_async_copy(wg_hbm.at[e], wg_f, sem.at[0]),
                pltpu.make_async_copy(wu_hbm.at[e], wu_f, sem.at[1]),
                pltpu.make_async_copy(wd_hbm.at[e], wd_f, sem.at[2]))

    n_used = nused_ref[0]

    @pl.when(i < n_used)
    def _():
        @pl.when(i == 0)
        def _():
            for c in weight_copies(be_ref[0]):
                c.start()

        @pl.when(first_ref[i] == 1)
        def _():
            for c in weight_copies(be_ref[i]):
                c.wait()
            wg_s[...] = wg_f[...].astype(BF16)
            wu_s[...] = wu_f[...].astype(BF16)
            wd_s[...] = wd_f[...].astype(BF16)

            @pl.when(next_ref[i] >= 0)
            def _():
                for c in weight_copies(next_ref[i]):
                    c.start()

    def ffn_to_tile():
        x = xs_ref[...]
        g = jnp.dot(x, wg_s[...], preferred_element_type=F32)
        u = jnp.dot(x, wu_s[...], preferred_element_type=F32)
        h = (_silu(g) * u).astype(BF16)
        y = jnp.dot(h, wd_s[...], preferred_element_type=F32)
        for j in range(ROW_TILES):
            tile[j * ROW_STRIDE:j * ROW_STRIDE + MOE_ROWS, :] = y[:, j * LANES:(j + 1) * LANES]

    def tile_to_rows():
        for mi in range(MOE_ROWS):
            ys_ref[mi] = tile[pl.ds(mi, ROW_TILES, stride=ROW_STRIDE), :].astype(BF16)

    @pl.when(jnp.logical_and(i >= 1, i < n_used))
    def _():
        tile_to_rows()
        ffn_to_tile()

    @pl.when(i == 0)
    def _():
        ffn_to_tile()

    @pl.when(i == n_used)
    def _():
        tile_to_rows()

    @pl.when(i > n_used)
    def _():
        ys_ref[...] = jnp.zeros_like(ys_ref)


def _experts(xs, block_expert, n_used, first, next_expert, w_gate, w_up, w_down, n_blocks):
    def xmap(i, be, nu, fi, nx):
        return (jnp.minimum(i, nu[0] - 1), 0)

    any_spec = pl.BlockSpec(memory_space=pl.ANY)
    return pl.pallas_call(
        _experts_kernel,
        grid_spec=pltpu.PrefetchScalarGridSpec(
            num_scalar_prefetch=4,
            grid=(n_blocks + 1,),
            in_specs=[pl.BlockSpec((MOE_ROWS, D_MODEL), xmap), any_spec, any_spec, any_spec],
            out_specs=pl.BlockSpec((MOE_ROWS, ROW_TILES, LANES),
                                   lambda i, *_: (jnp.maximum(i - 1, 0), 0, 0)),
            scratch_shapes=[pltpu.VMEM((D_MODEL, D_EXPERT), F32),
                            pltpu.VMEM((D_MODEL, D_EXPERT), F32),
                            pltpu.VMEM((D_EXPERT, D_MODEL), F32),
                            pltpu.VMEM((D_MODEL, D_EXPERT), BF16),
                            pltpu.VMEM((D_MODEL, D_EXPERT), BF16),
                            pltpu.VMEM((D_EXPERT, D_MODEL), BF16),
                            pltpu.VMEM((ROW_TILES * ROW_STRIDE, LANES), F32),
                            pltpu.SemaphoreType.DMA((3,))]),
        out_shape=jax.ShapeDtypeStruct((n_blocks * MOE_ROWS, ROW_TILES, LANES), BF16),
        compiler_params=_cparams(("arbitrary",), 48),
        name="experts",
    )(block_expert, n_used, first, next_expert, xs, w_gate, w_up, w_down)


def _combine_kernel(src_ref, ntot_ref, lidx_ref, gate_ref, ys_hbm, out_ref,
                    stage, tile, sem, *, stage_rows):
    i = pl.program_id(0)
    n_tiles = pl.num_programs(0)

    def chunk_copy(src, slot, dst):
        return pltpu.make_async_copy(ys_hbm.at[pl.ds(src, COMB_CH)],
                                     stage.at[pl.ds(slot * stage_rows + dst, COMB_CH)],
                                     sem.at[slot])

    def issue(tile_idx, slot):
        def c_body(q, _):
            chunk_copy(src_ref[tile_idx, q], slot, q * COMB_CH).start()
            return 0

        lax.fori_loop(0, ntot_ref[tile_idx], c_body, 0)

    @pl.when(i == 0)
    def _():
        issue(0, 0)

    @pl.when(i + 1 < n_tiles)
    def _():
        issue(i + 1, (i + 1) % 2)

    slot = i % 2

    def w_body(c, _):
        chunk_copy(0, slot, 0).wait()
        return 0

    lax.fori_loop(0, ntot_ref[i], w_body, 0)

    stride = COMB_TOK + SUBLANES

    for tk in range(COMB_TOK):
        acc = None
        for k in range(TOP_K):
            w = stage[lidx_ref[tk * TOP_K + k]].astype(F32)
            term = gate_ref[tk * TOP_K + k] * w
            acc = term if acc is None else acc + term
        tile[pl.ds(tk, ROW_TILES, stride=stride), :] = acc
    for j in range(ROW_TILES):
        out_ref[:, j * LANES:(j + 1) * LANES] = tile[j * stride:j * stride + COMB_TOK, :]


def _combine(ys, src_flat, n_total, lidx, gate, t, stage_rows):
    n_tiles = t // COMB_TOK
    stride = COMB_TOK + SUBLANES
    return pl.pallas_call(
        functools.partial(_combine_kernel, stage_rows=stage_rows),
        grid_spec=pltpu.PrefetchScalarGridSpec(
            num_scalar_prefetch=2,
            grid=(n_tiles,),
            in_specs=[pl.BlockSpec((COMB_TOK * TOP_K,), lambda i, *_: (i,), memory_space=pltpu.SMEM),
                      pl.BlockSpec((COMB_TOK * TOP_K,), lambda i, *_: (i,), memory_space=pltpu.SMEM),
                      pl.BlockSpec(memory_space=pl.ANY)],
            out_specs=pl.BlockSpec((COMB_TOK, D_MODEL), lambda i, *_: (i, 0)),
            scratch_shapes=[pltpu.VMEM((2 * stage_rows, ROW_TILES, LANES), BF16),
                            pltpu.VMEM((ROW_TILES * stride, LANES), F32),
                            pltpu.SemaphoreType.DMA((2,))]),
        out_shape=jax.ShapeDtypeStruct((t, D_MODEL), F32),
        compiler_params=_cparams(("arbitrary",), 48),
        name="combine",
    )(src_flat, n_total, lidx, gate, ys)


def _final_kernel(h_ref, x1_ref, rt_ref, g2_ref, wsg_ref, wsu_ref, wsd_ref, gf_ref, y_ref):
    h = h_ref[...]
    a = _silu(jnp.dot(h, wsg_ref[...], preferred_element_type=F32))
    b = jnp.dot(h, wsu_ref[...], preferred_element_type=F32)
    shared = jnp.dot((a * b).astype(BF16), wsd_ref[...], preferred_element_type=F32)
    xo = x1_ref[...] + _mod(g2_ref) * (shared + rt_ref[...])
    y_ref[...] = xo * lax.rsqrt(jnp.mean(xo * xo, axis=-1, keepdims=True) + RMS_EPS) * gf_ref[...]


def _final(h2, x1, routed, routed_row0, mods, rows_per_batch, w_sg, w_su, w_sd, gf, tm):
    t = x1.shape[0]
    row = lambda m: (m, 0)
    assert routed_row0 % tm == 0
    block0 = routed_row0 // tm
    return pl.pallas_call(
        _final_kernel,
        grid=(t // tm,),
        in_specs=[pl.BlockSpec((tm, D_MODEL), row), pl.BlockSpec((tm, D_MODEL), row),
                  pl.BlockSpec((tm, D_MODEL), lambda m: (m + block0, 0)),
                  _mod_spec(mods, 5, rows_per_batch, tm),
                  _const_spec((D_MODEL, D_EXPERT)), _const_spec((D_MODEL, D_EXPERT)),
                  _const_spec((D_EXPERT, D_MODEL)), _const_spec((1, D_MODEL))],
        out_specs=pl.BlockSpec((tm, D_MODEL), row),
        out_shape=jax.ShapeDtypeStruct((t, D_MODEL), F32),
        compiler_params=_cparams(("arbitrary",), 48),
        name="final",
    )(h2, x1, routed, mods, w_sg, w_su, w_sd, gf)


def _moe_plan(top_e, t):
    ar = jnp.arange(N_EXPERTS, dtype=I32)
    sel = top_e[:, :, None] == ar
    onehot = jnp.any(sel, axis=1).astype(I32)
    csum = jnp.cumsum(onehot, axis=0)
    excl = csum - onehot
    counts = csum[-1]
    padded = (counts + MOE_ROWS - 1) // MOE_ROWS * MOE_ROWS
    pend = jnp.cumsum(padded)
    pstart = pend - padded
    pick = lambda per_expert: jnp.sum(jnp.where(sel, per_expert, 0), axis=-1)
    rank = pick(excl[:, None, :])
    n_blocks = -(-(t * TOP_K + N_EXPERTS * (MOE_ROWS - 1)) // MOE_ROWS)
    block_row = jnp.arange(n_blocks, dtype=I32) * MOE_ROWS
    block_expert = jnp.minimum(jnp.sum((pend[None, :] <= block_row[:, None]).astype(I32), axis=1),
                               N_EXPERTS - 1)
    n_used = (pend[-1] // MOE_ROWS).astype(I32).reshape(1)
    first = jnp.concatenate([jnp.ones((1,), I32), (block_expert[1:] != block_expert[:-1]).astype(I32)])
    later = jnp.logical_and(counts[None, :] > 0, ar[None, :] > ar[:, None])
    next_nonempty = jnp.min(jnp.where(later, ar[None, :], N_EXPERTS), axis=1)
    next_nonempty = jnp.where(next_nonempty < N_EXPERTS, next_nonempty, -1)
    next_expert = jnp.sum(jnp.where(block_expert[:, None] == ar[None, :], next_nonempty[None, :], 0), axis=1)
    assert t + MOE_ROWS <= 1 << TOKEN_BITS
    real = ((top_e << TOKEN_BITS) | jnp.arange(t, dtype=I32)[:, None]).reshape(-1)
    r = jnp.arange(MOE_ROWS - 1, dtype=I32)
    int_max = jnp.iinfo(jnp.int32).max
    pad_keys = jnp.where(r[None, :] < (padded - counts)[:, None],
                         (ar[:, None] << TOKEN_BITS) | (t + r[None, :]), int_max).reshape(-1)
    n_rows = n_blocks * MOE_ROWS
    fill = jnp.full((n_rows - real.shape[0] - pad_keys.shape[0],), int_max, I32)
    tokv = jnp.sort(jnp.concatenate([real, pad_keys, fill])) & ((1 << TOKEN_BITS) - 1)
    tok_buf = jnp.where(tokv < t, tokv, 0)
    lo = excl[::COMB_TOK]
    hi = jnp.concatenate([lo[1:], counts[None]], axis=0)
    n_chunks = (hi - lo + COMB_CH - 1) // COMB_CH
    cend = jnp.cumsum(n_chunks, axis=1)
    cstart = cend - n_chunks
    n_total = cend[:, -1].astype(I32)
    src_start = pstart[None, :] + lo
    q = jnp.arange(COMB_MAX_CHUNKS, dtype=I32)
    eq = jnp.minimum(jnp.sum((cend[:, None, :] <= q[None, :, None]).astype(I32), axis=-1), N_EXPERTS - 1)
    sel_q = eq[:, :, None] == ar
    pick_q = lambda v: jnp.sum(jnp.where(sel_q, v[:, None, :], 0), axis=-1)
    src_flat = pick_q(src_start) + (q[None, :] - pick_q(cstart)) * COMB_CH
    src_flat = jnp.where(q[None, :] < n_total[:, None], src_flat, 0)
    stage_rows = COMB_MAX_CHUNKS * COMB_CH
    slot_base = (jnp.arange(t // COMB_TOK, dtype=I32) % 2) * stage_rows
    local = jnp.repeat(cstart * COMB_CH - lo + slot_base[:, None], COMB_TOK, axis=0)
    lidx = pick(local[:, None, :]) + rank
    return dict(block_expert=block_expert.astype(I32), n_used=n_used, tok_buf=tok_buf.astype(I32),
                first=first, next_expert=next_expert.astype(I32),
                n_blocks=n_blocks, src_flat=src_flat.astype(I32), n_total=n_total,
                lidx=lidx.astype(I32), stage_rows=stage_rows)


def kernel(x_prompt, x_sample, state_ssm_re, state_ssm_im, state_conv, c_prompt, c_sample, w_ada, b_ada, norm1_g, norm2_g, w_in, ssm_a_re, ssm_a_im, ssm_log_dt, ssm_b_re, ssm_b_im, ssm_c_re, ssm_c_im, ssm_d, w_glu, b_glu, w_lift_a, conv_w, conv_b, w_lift_b, w_out, w_router, router_bias, w_exp_gate, w_exp_up, w_exp_down, w_sh_gate, w_sh_up, w_sh_down, final_norm_g):
    depth = w_ada.shape[0]
    assert depth == 1
    bp, seq, _ = x_prompt.shape
    bs = x_sample.shape[0]
    tp = bp * seq
    l = 0

    w_in_b = w_in[l].astype(BF16)
    wr_t = w_router[l].T
    wr_hi = wr_t.astype(BF16)
    wr_lo = (wr_t - wr_hi.astype(F32)).astype(BF16)
    mix_w = (ssm_d[l].reshape(1, -1), w_glu[l].astype(BF16), b_glu[l].reshape(1, -1),
             w_lift_a[l].astype(BF16), conv_w[l], conv_b[l].reshape(1, -1), w_lift_b[l].astype(BF16),
             w_out[l].astype(BF16), norm2_g[l].reshape(1, -1), wr_hi, wr_lo)
    g1 = norm1_g[l].reshape(1, -1)

    ada = _ada(jnp.concatenate([c_prompt, c_sample], axis=0), w_ada[l], b_ada[l])
    mods_p = ada[:bp].reshape(bp, 1, -1)
    mods_s = ada[bp:]

    apow, ssm_tm, ssm_bk, ssm_ck = _ssm_prep(ssm_a_re[l], ssm_a_im[l], ssm_log_dt[l], ssm_b_re[l],
                                             ssm_b_im[l], ssm_c_re[l], ssm_c_im[l])
    a1_re, a1_im, al_re, al_im = (_state_layout(apow[i]) for i in range(4))

    xp = x_prompt.reshape(tp, D_MODEL)
    u_p, rest_p = _in_proj(xp, mods_p, seq, g1, w_in_b, tm=256)
    n_chunks = seq // SSM_CHUNK
    s_loc = _ssm_state(u_p, ssm_bk, SSM_CHUNK, n_chunks)
    s_prev, s_fin_p = _ssm_scan(s_loc.reshape(bp, n_chunks, -1), al_re, al_im,
                                jnp.zeros((bp, OCTETS * OCT_STATE), F32))
    y_p = _ssm_out(u_p, s_prev.reshape(bp * n_chunks, -1), ssm_tm, ssm_ck, SSM_CHUNK, n_chunks)
    x1_p, h2_p, lg_p, conv_p = _mixer_post(xp, y_p, u_p, rest_p, mods_p, seq, mix_w, None, tm=256)

    xs = x_sample.reshape(bs, D_MODEL)
    u_s, rest_s = _in_proj(xs, mods_s, 1, g1, w_in_b, tm=bs)
    s0_s = _pack_state(state_ssm_re[l], state_ssm_im[l])
    s_new_s = _ssm_step(_ssm_state(u_s, ssm_bk, 1, bs), s0_s, a1_re, a1_im)
    y_s = _ssm_out(u_s, s0_s, ssm_tm, ssm_ck, 1, bs)
    x1_s, h2_s, lg_s, nc0, nc1 = _mixer_post(xs, y_s, u_s, rest_s, mods_s, 1, mix_w,
                                                    state_conv[l], tm=bs)

    t = tp + bs
    te_p, gate_p = _route(lg_p, router_bias[l])
    te_s, gate_s = _route(lg_s, router_bias[l])
    top_e = jnp.concatenate([te_p, te_s], axis=1).T
    gate = jnp.concatenate([gate_p, gate_s], axis=1).T
    plan = _moe_plan(top_e, t)
    h_tiles = jnp.concatenate([h2_p, h2_s], axis=0).reshape(t, ROW_TILES, LANES)
    xs_sorted = _dispatch(h_tiles, plan["tok_buf"], plan["n_used"], plan["n_blocks"])
    ys = _experts(xs_sorted, plan["block_expert"], plan["n_used"], plan["first"], plan["next_expert"],
                  w_exp_gate[l], w_exp_up[l], w_exp_down[l], plan["n_blocks"])
    routed = _combine(ys, plan["src_flat"], plan["n_total"], plan["lidx"].reshape(-1),
                      gate.reshape(-1), t, plan["stage_rows"])

    w_sg, w_su, w_sd = w_sh_gate[l].astype(BF16), w_sh_up[l].astype(BF16), w_sh_down[l].astype(BF16)
    gf = final_norm_g.reshape(1, -1)
    y_prompt = _final(h2_p, x1_p, routed, 0, mods_p, seq, w_sg, w_su, w_sd, gf, tm=512)
    y_sample = _final(h2_s, x1_s, routed, tp, mods_s, 1, w_sg, w_su, w_sd, gf, tm=bs)

    re_p, im_p = _unpack_state(s_fin_p)
    re_s, im_s = _unpack_state(s_new_s)
    return (y_prompt.reshape(bp, seq, D_MODEL), y_sample.reshape(bs, 1, D_MODEL),
            re_p[None], im_p[None], conv_p[None],
            re_s[None], im_s[None], jnp.stack([nc0, nc1], axis=1)[None])
```

```python
import functools

import jax
import jax.numpy as jnp
from jax import lax
from jax.experimental import pallas as pl
from jax.experimental.pallas import tpu as pltpu

F32 = jnp.float32
BF16 = jnp.bfloat16
I32 = jnp.int32

D_MODEL = 2048
D_SSM = 1024
SSM_GROUP = 16
N_GROUPS = 64
SSM_STATE = 64
D_CONV = 1024
CONV_W = 3
N_EXPERTS = 64
TOP_K = 8
N_EXPERT_GROUPS = 8
EXPERTS_PER_GROUP = 8
TOPK_GROUPS = 4
D_EXPERT = 512
ROUTED_SCALE = 2.5
RMS_EPS = 1e-6

LANES = 128
SUBLANES = 8
OCTETS = N_GROUPS * SSM_GROUP // LANES
OCT_STATE = 2 * (N_GROUPS // OCTETS) * SSM_STATE
SSM_CHUNK = 8
MOE_ROWS = 256
COMB_TOK = 128
COMB_CH = 16
ROW_TILES = D_MODEL // LANES
ROW_STRIDE = MOE_ROWS + SUBLANES
DISP_ROWS = 2 * MOE_ROWS
DISP_STRIDE = DISP_ROWS + SUBLANES
COMB_MAX_CHUNKS = COMB_TOK * TOP_K // COMB_CH + N_EXPERTS
TOKEN_BITS = 14
MIX_ROWS = 128
NEG_INF = float("-inf")


def _cparams(sem, vmem_mb):
    return pltpu.CompilerParams(dimension_semantics=sem, vmem_limit_bytes=vmem_mb * 1024 * 1024)


def _const_spec(shape):
    nd = len(shape)
    return pl.BlockSpec(shape, lambda *_: (0,) * nd, pipeline_mode=pl.Buffered(1))


def _mod(ref):
    v = ref[...]
    return v[0] if v.ndim == 3 else v


def _silu(x):
    return x * jax.nn.sigmoid(x)


def _ada_kernel(c_ref, w_ref, b_ref, o_ref):
    a = _silu(c_ref[...]).astype(BF16)
    o_ref[...] = jnp.dot(a, w_ref[...].astype(BF16), preferred_element_type=F32) + b_ref[...]


def _ada(c_all, w_ada, b_ada):
    m = c_all.shape[0]
    n = w_ada.shape[1]
    tn = 1024
    return pl.pallas_call(
        _ada_kernel,
        grid=(n // tn,),
        in_specs=[pl.BlockSpec((m, D_MODEL), lambda j: (0, 0)),
                  pl.BlockSpec((D_MODEL, tn), lambda j: (0, j)),
                  pl.BlockSpec((1, tn), lambda j: (0, j))],
        out_specs=pl.BlockSpec((m, tn), lambda j: (0, j)),
        out_shape=jax.ShapeDtypeStruct((m, n), F32),
        compiler_params=_cparams(("arbitrary",), 40),
        name="ada",
    )(c_all, w_ada, b_ada.reshape(1, n))


def _mod_spec(mods, col, rows_per_batch, tm):
    if mods.ndim == 3:
        tiles_per_batch = rows_per_batch // tm
        return pl.BlockSpec((1, 1, D_MODEL), lambda m, *_: (m // tiles_per_batch, 0, col))
    return pl.BlockSpec((tm, D_MODEL), lambda m, *_: (m, col))


def _inproj_kernel(x_ref, g_ref, sh_ref, sc_ref, w_ref, u_ref, r_ref):
    x = x_ref[...]
    y = x * lax.rsqrt(jnp.mean(x * x, axis=-1, keepdims=True) + RMS_EPS) * g_ref[...]
    h = (y * (1.0 + _mod(sc_ref)) + _mod(sh_ref)).astype(BF16)
    tn = D_SSM
    u_ref[...] = jnp.dot(h, w_ref[:, 0:tn], preferred_element_type=F32)
    for n in range(1, w_ref.shape[1] // tn):
        r_ref[:, (n - 1) * tn:n * tn] = jnp.dot(h, w_ref[:, n * tn:(n + 1) * tn],
                                                preferred_element_type=F32).astype(BF16)


def _in_proj(x, mods, rows_per_batch, g1, w_in, tm):
    t = x.shape[0]
    d_in = w_in.shape[1]
    n_rest = d_in - D_SSM
    return pl.pallas_call(
        _inproj_kernel,
        grid=(t // tm,),
        in_specs=[pl.BlockSpec((tm, D_MODEL), lambda m: (m, 0)),
                  _const_spec((1, D_MODEL)),
                  _mod_spec(mods, 0, rows_per_batch, tm),
                  _mod_spec(mods, 1, rows_per_batch, tm),
                  _const_spec((D_MODEL, d_in))],
        out_specs=[pl.BlockSpec((tm, D_SSM), lambda m: (m, 0)),
                   pl.BlockSpec((tm, n_rest), lambda m: (m, 0))],
        out_shape=[jax.ShapeDtypeStruct((t, D_SSM), F32),
                   jax.ShapeDtypeStruct((t, n_rest), BF16)],
        compiler_params=_cparams(("arbitrary",), 56),
        name="in_proj",
    )(x, g1, mods, mods, w_in)


def _ssm_prep_kernel(are_ref, aim_ref, ldt_ref, btr_ref, bti_ref, cre_ref, cim_ref,
                     apow_ref, tm_ref, bk_ref, ck_ref):
    lc = SSM_CHUNK
    ng = N_GROUPS // OCTETS
    half = OCT_STATE // 2
    lr = are_ref[...]
    li = aim_ref[...]
    dt = jnp.exp(ldt_ref[...])
    mag = jnp.exp(lr * dt)
    ar = mag * jnp.cos(li * dt)
    ai = mag * jnp.sin(li * dt)
    den = lr * lr + li * li
    xr = ar - 1.0
    cr = (xr * lr + ai * li) / den
    ci = (ai * lr - xr * li) / den
    btr = btr_ref[...]
    bti = bti_ref[...]
    bbr = cr * btr - ci * bti
    bbi = cr * bti + ci * btr
    cre = cre_ref[...]
    cim = cim_ref[...]
    cre_w = jnp.concatenate([cre] * ng, axis=1)
    cim_w = jnp.concatenate([cim] * ng, axis=1)

    def iota(shape, dim):
        return lax.broadcasted_iota(I32, shape, dim)

    width = lc * LANES
    sel_b = ((iota((SSM_STATE, half), 1) & (SSM_STATE - 1)) == iota((SSM_STATE, half), 0)).astype(BF16)
    mask_b = ((iota((width, half), 0) & (LANES - 1)) >> 4) == (iota((width, half), 1) >> 6)
    sel_c = ((iota((half, SSM_STATE), 0) & (SSM_STATE - 1)) == iota((half, SSM_STATE), 1)).astype(BF16)
    mask_c = (iota((half, width), 0) >> 6) == ((iota((half, width), 1) & (LANES - 1)) >> 4)
    mask_t = (iota((LANES, LANES), 0) >> 4) == (iota((LANES, LANES), 1) >> 4)
    dn_batch = (((2,), (2,)), ((0,), (0,)))
    dn_nt = (((1,), (1,)), ((), ()))

    pr = jnp.ones_like(ar)
    pi = jnp.zeros_like(ar)
    w1r, w1i, w2r, w2i = [], [], [], []
    for l in range(lc):
        w1r.append(pr * bbr - pi * bbi)
        w1i.append(pr * bbi + pi * bbr)
        pr, pi = pr * ar - pi * ai, pr * ai + pi * ar
        w2r.append(cre * pr - cim * pi)
        w2i.append(-(cre * pi + cim * pr))
        if l == 0:
            apow_ref[0] = pr
            apow_ref[1] = pi
    apow_ref[2] = pr
    apow_ref[3] = pi

    def rows(ws, order):
        return jnp.concatenate([ws[l] for l in order], axis=0).reshape(width, SSM_STATE).astype(BF16)

    rev = range(lc - 1, -1, -1)
    for c, ws in ((0, w1r), (1, w1i)):
        rep = jnp.dot(rows(ws, rev), sel_b, preferred_element_type=F32)
        bk_ref[:, c * half:(c + 1) * half] = jnp.where(mask_b, rep, 0.0).astype(BF16)
    for c, ws in ((0, w2r), (1, w2i)):
        rep = lax.dot_general(sel_c, rows(ws, range(lc)), dn_nt, preferred_element_type=F32)
        ck_ref[c * half:(c + 1) * half, :] = jnp.where(mask_c, rep, 0.0).astype(BF16)
    w1r_all = jnp.concatenate(w1r, axis=1)
    w1i_all = jnp.concatenate(w1i, axis=1)
    kt = (lax.dot_general(w1r_all, cre_w, dn_batch, precision=lax.Precision.HIGHEST,
                          preferred_element_type=F32)
          - lax.dot_general(w1i_all, cim_w, dn_batch, precision=lax.Precision.HIGHEST,
                            preferred_element_type=F32))
    hh = SSM_GROUP
    t_blocks = [jnp.where(mask_t, kt[:, l * hh:(l + 1) * hh, :].reshape(LANES, LANES), 0.0).astype(BF16)
                for l in range(lc)]
    zero = jnp.zeros((LANES, LANES), BF16)
    for s in range(lc):
        for t in range(lc):
            tm_ref[s * LANES:(s + 1) * LANES, t * LANES:(t + 1) * LANES] = t_blocks[t - s] if t >= s else zero


def _ssm_prep(a_re, a_im, log_dt, b_re, b_im, c_re, c_im):
    g, p, h = N_GROUPS, SSM_STATE, SSM_GROUP
    ng = g // OCTETS
    width = SSM_CHUNK * LANES
    vec = pl.BlockSpec((ng, 1, p), lambda o: (o, 0, 0))
    mat = pl.BlockSpec((ng, h, p), lambda o: (o, 0, 0))
    big = lambda rows, cols: pl.BlockSpec((None, rows, cols), lambda o: (o, 0, 0))
    return pl.pallas_call(
        _ssm_prep_kernel,
        grid=(OCTETS,),
        in_specs=[vec, vec, pl.BlockSpec((ng, 1, 1), lambda o: (o, 0, 0)), mat, mat, mat, mat],
        out_specs=[pl.BlockSpec((4, ng, 1, p), lambda o: (0, o, 0, 0)),
                   big(width, width), big(width, OCT_STATE), big(OCT_STATE, width)],
        out_shape=[jax.ShapeDtypeStruct((4, g, 1, p), F32),
                   jax.ShapeDtypeStruct((OCTETS, width, width), BF16),
                   jax.ShapeDtypeStruct((OCTETS, width, OCT_STATE), BF16),
                   jax.ShapeDtypeStruct((OCTETS, OCT_STATE, width), BF16)],
        compiler_params=_cparams(("arbitrary",), 32),
        name="ssm_prep",
    )(a_re.reshape(g, 1, p), a_im.reshape(g, 1, p), log_dt.reshape(g, 1, 1),
      b_re.transpose(0, 2, 1), b_im.transpose(0, 2, 1), c_re, c_im)


def _state_layout(v):
    return v.reshape(OCTETS, 1, OCT_STATE // 2)


def _load_chunks(u_ref, lc, rb):
    parts = [u_ref[pl.ds(s, rb, stride=lc), :] if lc > 1 else u_ref[...] for s in range(lc)]
    u = parts[0] if lc == 1 else jnp.concatenate(parts, axis=-1)
    return u.astype(BF16)


def _ssm_state_kernel(u_ref, bk_ref, s_ref, *, lc, rb):
    s_ref[...] = jnp.dot(_load_chunks(u_ref, lc, rb), bk_ref[...], preferred_element_type=F32)


def _ssm_state(u, bk, lc, rb):
    t = u.shape[0]
    nb = t // (rb * lc)
    lag0_block = SSM_CHUNK // lc - 1
    return pl.pallas_call(
        functools.partial(_ssm_state_kernel, lc=lc, rb=rb),
        grid=(OCTETS, nb),
        in_specs=[pl.BlockSpec((rb * lc, LANES), lambda o, b: (b, o)),
                  pl.BlockSpec((None, lc * LANES, OCT_STATE), lambda o, b: (o, lag0_block, 0))],
        out_specs=pl.BlockSpec((rb, OCT_STATE), lambda o, b: (b, o)),
        out_shape=jax.ShapeDtypeStruct((t // lc, OCTETS * OCT_STATE), F32),
        compiler_params=_cparams(("arbitrary", "arbitrary"), 32),
        name="ssm_state",
    )(u, bk)


def _ssm_scan_kernel(sl_ref, ar_ref, ai_ref, s0_ref, sp_ref, sf_ref, *, nb, nc):
    half = OCT_STATE // 2
    ar = ar_ref[0]
    ai = ai_ref[0]
    init = tuple((s0_ref[b:b + 1, 0:half], s0_ref[b:b + 1, half:OCT_STATE]) for b in range(nb))

    def body(c, carry):
        new = []
        for b in range(nb):
            sr, si = carry[b]
            sp_ref[b, pl.ds(c, 1), 0:half] = sr
            sp_ref[b, pl.ds(c, 1), half:OCT_STATE] = si
            xr = sl_ref[b, pl.ds(c, 1), 0:half]
            xi = sl_ref[b, pl.ds(c, 1), half:OCT_STATE]
            new.append((ar * sr - ai * si + xr, ar * si + ai * sr + xi))
        return tuple(new)

    fin = lax.fori_loop(0, nc, body, init)
    for b in range(nb):
        sf_ref[b:b + 1, 0:half] = fin[b][0]
        sf_ref[b:b + 1, half:OCT_STATE] = fin[b][1]


def _ssm_scan(s_loc, al_re, al_im, s0):
    nb, nc, width = s_loc.shape
    return pl.pallas_call(
        functools.partial(_ssm_scan_kernel, nb=nb, nc=nc),
        grid=(OCTETS,),
        in_specs=[pl.BlockSpec((nb, nc, OCT_STATE), lambda o: (0, 0, o)),
                  pl.BlockSpec((1, 1, OCT_STATE // 2), lambda o: (o, 0, 0)),
                  pl.BlockSpec((1, 1, OCT_STATE // 2), lambda o: (o, 0, 0)),
                  pl.BlockSpec((nb, OCT_STATE), lambda o: (0, o))],
        out_specs=[pl.BlockSpec((nb, nc, OCT_STATE), lambda o: (0, 0, o)),
                   pl.BlockSpec((nb, OCT_STATE), lambda o: (0, o))],
        out_shape=[jax.ShapeDtypeStruct((nb, nc, width), F32),
                   jax.ShapeDtypeStruct((nb, width), F32)],
        compiler_params=_cparams(("arbitrary",), 32),
        name="ssm_scan",
    )(s_loc, al_re, al_im, s0)


def _ssm_out_kernel(u_ref, sp_ref, tm_ref, ck_ref, y_ref, *, lc, rb):
    y = (jnp.dot(_load_chunks(u_ref, lc, rb), tm_ref[...], preferred_element_type=F32)
         + jnp.dot(sp_ref[...].astype(BF16), ck_ref[...], preferred_element_type=F32))
    if lc == 1:
        y_ref[...] = y
    else:
        for t in range(lc):
            y_ref[pl.ds(t, rb, stride=lc), :] = y[:, t * LANES:(t + 1) * LANES]


def _ssm_out(u, s_prev, tm, ck, lc, rb):
    t = u.shape[0]
    nb = t // (rb * lc)
    return pl.pallas_call(
        functools.partial(_ssm_out_kernel, lc=lc, rb=rb),
        grid=(OCTETS, nb),
        in_specs=[pl.BlockSpec((rb * lc, LANES), lambda o, b: (b, o)),
                  pl.BlockSpec((rb, OCT_STATE), lambda o, b: (b, o)),
                  pl.BlockSpec((None, lc * LANES, lc * LANES), lambda o, b: (o, 0, 0)),
                  pl.BlockSpec((None, OCT_STATE, lc * LANES), lambda o, b: (o, 0, 0))],
        out_specs=pl.BlockSpec((rb * lc, LANES), lambda o, b: (b, o)),
        out_shape=jax.ShapeDtypeStruct((t, D_SSM), F32),
        compiler_params=_cparams(("arbitrary", "arbitrary"), 32),
        name="ssm_out",
    )(u, s_prev, tm, ck)


def _ssm_step_kernel(sl_ref, s0_ref, ar_ref, ai_ref, sn_ref):
    half = OCT_STATE // 2
    ar = ar_ref[0]
    ai = ai_ref[0]
    sr = s0_ref[:, 0:half]
    si = s0_ref[:, half:OCT_STATE]
    sn_ref[:, 0:half] = ar * sr - ai * si + sl_ref[:, 0:half]
    sn_ref[:, half:OCT_STATE] = ar * si + ai * sr + sl_ref[:, half:OCT_STATE]


def _ssm_step(s_loc, s0, a_re, a_im):
    rows, width = s_loc.shape
    return pl.pallas_call(
        _ssm_step_kernel,
        grid=(OCTETS,),
        in_specs=[pl.BlockSpec((rows, OCT_STATE), lambda o: (0, o)),
                  pl.BlockSpec((rows, OCT_STATE), lambda o: (0, o)),
                  pl.BlockSpec((1, 1, OCT_STATE // 2), lambda o: (o, 0, 0)),
                  pl.BlockSpec((1, 1, OCT_STATE // 2), lambda o: (o, 0, 0))],
        out_specs=pl.BlockSpec((rows, OCT_STATE), lambda o: (0, o)),
        out_shape=jax.ShapeDtypeStruct((rows, width), F32),
        compiler_params=_cparams(("arbitrary",), 32),
        name="ssm_step",
    )(s_loc, s0, a_re, a_im)


def _pack_state(s_re, s_im):
    b = s_re.shape[0]
    half = OCT_STATE // 2
    return jnp.stack([s_re.reshape(b, OCTETS, half), s_im.reshape(b, OCTETS, half)], axis=2).reshape(b, -1)


def _unpack_state(s):
    b = s.shape[0]
    s4 = s.reshape(b, OCTETS, 2, OCT_STATE // 2)
    return (s4[:, :, 0].reshape(b, N_GROUPS, SSM_STATE), s4[:, :, 1].reshape(b, N_GROUPS, SSM_STATE))


def _mixer_kernel(*refs, seq, tm, tiles_per_batch):
    (x_ref, ys_ref, u_ref, r_ref, g1_ref, sh2_ref, sc2_ref, d_ref, wglu_ref, bglu_ref, wla_ref,
     cw_ref, cb_ref, wlb_ref, wout_ref, n2g_ref, wrh_ref, wrl_ref) = refs[:18]
    if seq:
        x1_ref, h2_ref, lg_ref, nc_ref, carry = refs[18:]
    else:
        cb0_ref, cb1_ref, x1_ref, h2_ref, lg_ref, nc0_ref, nc1_ref = refs[18:]

    xb = r_ref[:, 0:D_CONV].astype(F32)
    gb = r_ref[:, D_CONV:2 * D_CONV].astype(F32)
    gc = r_ref[:, 2 * D_CONV:3 * D_CONV].astype(F32)
    v = gc * xb
    if seq:
        first = pl.program_id(0) % tiles_per_batch == 0

        @pl.when(first)
        def _():
            carry[...] = jnp.zeros_like(carry)

        c0 = carry[0:1, :]
        c1 = carry[1:2, :]
        row = lax.broadcasted_iota(I32, (tm, 1), 0)
        v1 = jnp.where(row == 0, c1, pltpu.roll(v, 1, 0))
        v2 = jnp.where(row == 0, c0, jnp.where(row == 1, c1, pltpu.roll(v, 2, 0)))
        tail = v[tm - 2:tm, :]
        carry[...] = tail
        nc_ref[0] = tail
    else:
        v2 = cb0_ref[...]
        v1 = cb1_ref[...]
        nc0_ref[...] = v1
        nc1_ref[...] = v
    conv = cb_ref[...] + cw_ref[0:1, :] * v2 + cw_ref[1:2, :] * v1 + cw_ref[2:3, :] * v
    gated_conv = (gb * conv).astype(BF16)

    def mod_rows(ref, rs):
        return ref[0] if len(ref.shape) == 3 else ref[rs, :]

    n_split = max(tm // MIX_ROWS, 1)
    rows = tm // n_split
    off = 3 * D_CONV
    dn = (((1,), (1,)), ((), ()))
    for part in range(n_split):
        rs = slice(part * rows, (part + 1) * rows)
        ya = jax.nn.gelu(ys_ref[rs, :] + d_ref[...] * u_ref[rs, :], approximate=True)
        z = jnp.dot(ya.astype(BF16), wglu_ref[...], preferred_element_type=F32) + bglu_ref[...]
        ya = ya * jax.nn.sigmoid(z)
        branch_a = jnp.dot(ya.astype(BF16), wla_ref[...], preferred_element_type=F32)
        branch_b = jnp.dot(gated_conv[rs], wlb_ref[...], preferred_element_type=F32)
        gla = r_ref[rs, off:off + D_MODEL].astype(F32)
        glb = r_ref[rs, off + D_MODEL:off + 2 * D_MODEL].astype(F32)
        merged = jax.nn.sigmoid(gla) * branch_a + jax.nn.sigmoid(glb) * branch_b
        x1 = x_ref[rs, :] + mod_rows(g1_ref, rs) * jnp.dot(merged.astype(BF16), wout_ref[...],
                                                           preferred_element_type=F32)
        x1_ref[rs, :] = x1

        y = x1 * lax.rsqrt(jnp.mean(x1 * x1, axis=-1, keepdims=True) + RMS_EPS) * n2g_ref[...]
        h2 = y * (1.0 + mod_rows(sc2_ref, rs)) + mod_rows(sh2_ref, rs)
        h2b = h2.astype(BF16)
        h2_ref[rs, :] = h2b

        h2l = (h2 - h2b.astype(F32)).astype(BF16)
        lg_ref[:, rs] = (lax.dot_general(wrh_ref[...], h2b, dn, preferred_element_type=F32)
                         + lax.dot_general(wrh_ref[...], h2l, dn, preferred_element_type=F32)
                         + lax.dot_general(wrl_ref[...], h2b, dn, preferred_element_type=F32))


def _mixer_post(x, y_ssm, u, rest, mods, rows_per_batch, weights, conv_state, tm):
    t = x.shape[0]
    seq = conv_state is None
    tiles_per_batch = rows_per_batch // tm
    (ssm_d, w_glu, b_glu, w_la, conv_w, conv_b, w_lb, w_out, n2g, wr_hi, wr_lo) = weights
    row = lambda m: (m, 0)
    in_specs = [pl.BlockSpec((tm, D_MODEL), row),
                pl.BlockSpec((tm, D_SSM), row),
                pl.BlockSpec((tm, D_SSM), row),
                pl.BlockSpec((tm, rest.shape[1]), row),
                _mod_spec(mods, 2, rows_per_batch, tm),
                _mod_spec(mods, 3, rows_per_batch, tm),
                _mod_spec(mods, 4, rows_per_batch, tm),
                _const_spec((1, D_SSM)), _const_spec((D_SSM, D_SSM)), _const_spec((1, D_SSM)),
                _const_spec((D_SSM, D_MODEL)), _const_spec((CONV_W, D_CONV)), _const_spec((1, D_CONV)),
                _const_spec((D_CONV, D_MODEL)), _const_spec((D_MODEL, D_MODEL)),
                _const_spec((1, D_MODEL)), _const_spec((N_EXPERTS, D_MODEL)),
                _const_spec((N_EXPERTS, D_MODEL))]
    args = [x, y_ssm, u, rest, mods, mods, mods, ssm_d, w_glu, b_glu, w_la, conv_w, conv_b, w_lb,
            w_out, n2g, wr_hi, wr_lo]
    out_specs = [pl.BlockSpec((tm, D_MODEL), row),
                 pl.BlockSpec((tm, D_MODEL), row),
                 pl.BlockSpec((N_EXPERTS, tm), lambda m: (0, m))]
    out_shape = [jax.ShapeDtypeStruct((t, D_MODEL), F32),
                 jax.ShapeDtypeStruct((t, D_MODEL), BF16),
                 jax.ShapeDtypeStruct((N_EXPERTS, t), F32)]
    scratch = []
    if seq:
        nb = t // rows_per_batch
        out_specs.append(pl.BlockSpec((1, CONV_W - 1, D_CONV), lambda m: (m // tiles_per_batch, 0, 0)))
        out_shape.append(jax.ShapeDtypeStruct((nb, CONV_W - 1, D_CONV), F32))
        scratch.append(pltpu.VMEM((CONV_W - 1, D_CONV), F32))
    else:
        in_specs += [pl.BlockSpec((tm, D_CONV), row), pl.BlockSpec((tm, D_CONV), row)]
        args += [conv_state[:, 0, :], conv_state[:, 1, :]]
        out_specs += [pl.BlockSpec((tm, D_CONV), row), pl.BlockSpec((tm, D_CONV), row)]
        out_shape += [jax.ShapeDtypeStruct((t, D_CONV), F32), jax.ShapeDtypeStruct((t, D_CONV), F32)]
    return pl.pallas_call(
        functools.partial(_mixer_kernel, seq=seq, tm=tm, tiles_per_batch=tiles_per_batch),
        grid=(t // tm,),
        in_specs=in_specs,
        out_specs=out_specs,
        out_shape=out_shape,
        scratch_shapes=scratch,
        compiler_params=_cparams(("arbitrary",), 56),
        name="mixer_post",
    )(*args)


def _route_kernel(lg_ref, bias_ref, te_ref, gate_ref):
    scores = [jax.nn.sigmoid(lg_ref[e]) for e in range(N_EXPERTS)]
    choice = [scores[e] + bias_ref[e] for e in range(N_EXPERTS)]
    shape = scores[0].shape

    def first_max(vals):
        m = functools.reduce(jnp.maximum, vals)
        found = jnp.zeros(shape, jnp.bool_)
        hot = []
        for v in vals:
            is_m = jnp.logical_and(v == m, jnp.logical_not(found))
            hot.append(is_m)
            found = jnp.logical_or(found, is_m)
        return m, hot

    gscore = []
    for g in range(N_EXPERT_GROUPS):
        vals = choice[g * EXPERTS_PER_GROUP:(g + 1) * EXPERTS_PER_GROUP]
        m1, hot = first_max(vals)
        rest = [jnp.where(h, NEG_INF, v) for h, v in zip(hot, vals)]
        gscore.append(m1 + functools.reduce(jnp.maximum, rest))

    gsel = [jnp.zeros(shape, jnp.bool_) for _ in range(N_EXPERT_GROUPS)]
    for _ in range(TOPK_GROUPS):
        _, hot = first_max(gscore)
        gsel = [jnp.logical_or(s, h) for s, h in zip(gsel, hot)]
        gscore = [jnp.where(h, NEG_INF, v) for h, v in zip(hot, gscore)]

    cand = [jnp.where(gsel[e // EXPERTS_PER_GROUP], choice[e], NEG_INF) for e in range(N_EXPERTS)]
    picked_e = []
    picked_s = []
    for _ in range(TOP_K):
        _, hot = first_max(cand)
        idx = jnp.zeros(shape, I32)
        sc = jnp.zeros(shape, F32)
        for e in range(N_EXPERTS):
            idx = jnp.where(hot[e], e, idx)
            sc = jnp.where(hot[e], scores[e], sc)
        cand = [jnp.where(h, NEG_INF, v) for h, v in zip(hot, cand)]
        picked_e.append(idx)
        picked_s.append(sc)
    total = functools.reduce(lambda a, b: a + b, picked_s)
    for k in range(TOP_K):
        te_ref[k] = picked_e[k]
        gate_ref[k] = picked_s[k] / total * ROUTED_SCALE


def _route(logits_t, bias):
    t = logits_t.shape[1]
    r = t // LANES
    rb = min(r, SUBLANES)
    lg = logits_t.reshape(N_EXPERTS, r, LANES)
    bias_b = jnp.broadcast_to(bias.astype(F32).reshape(N_EXPERTS, 1, 1), (N_EXPERTS, 1, LANES))
    te, gate = pl.pallas_call(
        _route_kernel,
        grid=(r // rb,),
        in_specs=[pl.BlockSpec((N_EXPERTS, rb, LANES), lambda i: (0, i, 0)),
                  pl.BlockSpec((N_EXPERTS, 1, LANES), lambda i: (0, 0, 0))],
        out_specs=[pl.BlockSpec((TOP_K, rb, LANES), lambda i: (0, i, 0)),
                   pl.BlockSpec((TOP_K, rb, LANES), lambda i: (0, i, 0))],
        out_shape=[jax.ShapeDtypeStruct((TOP_K, r, LANES), I32),
                   jax.ShapeDtypeStruct((TOP_K, r, LANES), F32)],
        compiler_params=_cparams(("arbitrary",), 32),
        name="route",
    )(lg, bias_b)
    return te.reshape(TOP_K, t), gate.reshape(TOP_K, t)


def _dispatch_kernel(nused_ref, tok_ref, h_ref, xs_ref, tile):
    used_rows = nused_ref[0] * MOE_ROWS
    live = pl.program_id(0) * DISP_ROWS < used_rows

    @pl.when(jnp.logical_not(live))
    def _():
        xs_ref[...] = jnp.zeros_like(xs_ref)

    @pl.when(live)
    def _():
        for mi in range(DISP_ROWS):
            tile[pl.ds(mi, ROW_TILES, stride=DISP_STRIDE), :] = h_ref[tok_ref[mi]].astype(F32)
        for j in range(ROW_TILES):
            xs_ref[:, j * LANES:(j + 1) * LANES] = (
                tile[j * DISP_STRIDE:j * DISP_STRIDE + DISP_ROWS, :].astype(BF16))


def _dispatch(h_tiles, tok_buf, n_used, n_blocks):
    n_rows = n_blocks * MOE_ROWS
    assert n_rows % DISP_ROWS == 0

    def last_live(i, nu):
        return jnp.minimum(i, (nu[0] * MOE_ROWS - 1) // DISP_ROWS)

    return pl.pallas_call(
        _dispatch_kernel,
        grid_spec=pltpu.PrefetchScalarGridSpec(
            num_scalar_prefetch=1,
            grid=(n_rows // DISP_ROWS,),
            in_specs=[pl.BlockSpec((DISP_ROWS,), lambda i, nu: (last_live(i, nu),),
                                   memory_space=pltpu.SMEM),
                      pl.BlockSpec(memory_space=pltpu.VMEM)],
            out_specs=pl.BlockSpec((DISP_ROWS, D_MODEL), lambda i, nu: (i, 0)),
            scratch_shapes=[pltpu.VMEM((ROW_TILES * DISP_STRIDE, LANES), F32)]),
        out_shape=jax.ShapeDtypeStruct((n_rows, D_MODEL), BF16),
        compiler_params=_cparams(("arbitrary",), 56),
        name="dispatch",
    )(n_used, tok_buf, h_tiles)


def _experts_kernel(be_ref, nused_ref, first_ref, next_ref, xs_ref, wg_hbm, wu_hbm, wd_hbm, ys_ref,
                    wg_f, wu_f, wd_f, wg_s, wu_s, wd_s, tile, sem):
    i = pl.program_id(0)

    def weight_copies(e):
        return (pltpu.make_async_copy(wg_hbm.at[e], wg_f, sem.at[0]),
                pltpu.make_async_copy(wu_hbm.at[e], wu_f, sem.at[1]),
                pltpu.make_async_copy(wd_hbm.at[e], wd_f, sem.at[2]))

    n_used = nused_ref[0]

    @pl.when(i < n_used)
    def _():
        @pl.when(i == 0)
        def _():
            for c in weight_copies(be_ref[0]):
                c.start()

        @pl.when(first_ref[i] == 1)
        def _():
            for c in weight_copies(be_ref[i]):
                c.wait()
            wg_s[...] = wg_f[...].astype(BF16)
            wu_s[...] = wu_f[...].astype(BF16)
            wd_s[...] = wd_f[...].astype(BF16)

            @pl.when(next_ref[i] >= 0)
            def _():
                for c in weight_copies(next_ref[i]):
                    c.start()

    def ffn_to_tile():
        x = xs_ref[...]
        g = jnp.dot(x, wg_s[...], preferred_element_type=F32)
        u = jnp.dot(x, wu_s[...], preferred_element_type=F32)
        h = (_silu(g) * u).astype(BF16)
        y = jnp.dot(h, wd_s[...], preferred_element_type=F32)
        for j in range(ROW_TILES):
            tile[j * ROW_STRIDE:j * ROW_STRIDE + MOE_ROWS, :] = y[:, j * LANES:(j + 1) * LANES]

    def tile_to_rows():
        for mi in range(MOE_ROWS):
            ys_ref[mi] = tile[pl.ds(mi, ROW_TILES, stride=ROW_STRIDE), :].astype(BF16)

    @pl.when(jnp.logical_and(i >= 1, i < n_used))
    def _():
        tile_to_rows()
        ffn_to_tile()

    @pl.when(i == 0)
    def _():
        ffn_to_tile()

    @pl.when(i == n_used)
    def _():
        tile_to_rows()

    @pl.when(i > n_used)
    def _():
        ys_ref[...] = jnp.zeros_like(ys_ref)


def _experts(xs, block_expert, n_used, first, next_expert, w_gate, w_up, w_down, n_blocks):
    def xmap(i, be, nu, fi, nx):
        return (jnp.minimum(i, nu[0] - 1), 0)

    any_spec = pl.BlockSpec(memory_space=pl.ANY)
    return pl.pallas_call(
        _experts_kernel,
        grid_spec=pltpu.PrefetchScalarGridSpec(
            num_scalar_prefetch=4,
            grid=(n_blocks + 1,),
            in_specs=[pl.BlockSpec((MOE_ROWS, D_MODEL), xmap), any_spec, any_spec, any_spec],
            out_specs=pl.BlockSpec((MOE_ROWS, ROW_TILES, LANES),
                                   lambda i, *_: (jnp.maximum(i - 1, 0), 0, 0)),
            scratch_shapes=[pltpu.VMEM((D_MODEL, D_EXPERT), F32),
                            pltpu.VMEM((D_MODEL, D_EXPERT), F32),
                            pltpu.VMEM((D_EXPERT, D_MODEL), F32),
                            pltpu.VMEM((D_MODEL, D_EXPERT), BF16),
                            pltpu.VMEM((D_MODEL, D_EXPERT), BF16),
                            pltpu.VMEM((D_EXPERT, D_MODEL), BF16),
                            pltpu.VMEM((ROW_TILES * ROW_STRIDE, LANES), F32),
                            pltpu.SemaphoreType.DMA((3,))]),
        out_shape=jax.ShapeDtypeStruct((n_blocks * MOE_ROWS, ROW_TILES, LANES), BF16),
        compiler_params=_cparams(("arbitrary",), 48),
        name="experts",
    )(block_expert, n_used, first, next_expert, xs, w_gate, w_up, w_down)


def _combine_final_kernel(src_ref, ntot_ref, lidx_ref, gate_ref, ys_hbm,
                          x1p_ref, x1s_ref, h2p_ref, h2s_ref, g2p_ref, g2s_ref,
                          wsg_ref, wsu_ref, wsd_ref, gf_ref, yp_ref, ysm_ref,
                          stage, tile, routed, sem, *, stage_rows, prompt_tiles):
    i = pl.program_id(0)
    n_tiles = pl.num_programs(0)

    def chunk_copy(src, slot, dst):
        return pltpu.make_async_copy(ys_hbm.at[pl.ds(src, COMB_CH)],
                                     stage.at[pl.ds(slot * stage_rows + dst, COMB_CH)],
                                     sem.at[slot])

    def issue(tile_idx, slot):
        def c_body(q, _):
            chunk_copy(src_ref[tile_idx, q], slot, q * COMB_CH).start()
            return 0

        lax.fori_loop(0, ntot_ref[tile_idx], c_body, 0)

    @pl.when(i == 0)
    def _():
        issue(0, 0)

    @pl.when(i + 1 < n_tiles)
    def _():
        issue(i + 1, (i + 1) % 2)

    slot = i % 2

    def w_body(c, _):
        chunk_copy(0, slot, 0).wait()
        return 0

    lax.fori_loop(0, ntot_ref[i], w_body, 0)

    stride = COMB_TOK + SUBLANES

    for tk in range(COMB_TOK):
        acc = None
        for k in range(TOP_K):
            w = stage[lidx_ref[tk * TOP_K + k]].astype(F32)
            term = gate_ref[tk * TOP_K + k] * w
            acc = term if acc is None else acc + term
        tile[pl.ds(tk, ROW_TILES, stride=stride), :] = acc
    for j in range(ROW_TILES):
        routed[:, j * LANES:(j + 1) * LANES] = tile[j * stride:j * stride + COMB_TOK, :]

    is_prompt = i < prompt_tiles
    h = jnp.where(is_prompt, h2p_ref[...], h2s_ref[...])
    x1 = jnp.where(is_prompt, x1p_ref[...], x1s_ref[...])
    g2 = jnp.where(is_prompt, g2p_ref[0], g2s_ref[...])
    a = _silu(jnp.dot(h, wsg_ref[...], preferred_element_type=F32))
    b = jnp.dot(h, wsu_ref[...], preferred_element_type=F32)
    shared = jnp.dot((a * b).astype(BF16), wsd_ref[...], preferred_element_type=F32)
    xo = x1 + g2 * (shared + routed[...])
    y = xo * lax.rsqrt(jnp.mean(xo * xo, axis=-1, keepdims=True) + RMS_EPS) * gf_ref[...]

    @pl.when(is_prompt)
    def _():
        yp_ref[...] = y

    @pl.when(jnp.logical_not(is_prompt))
    def _():
        ysm_ref[...] = y


def _combine_final(ys, src_flat, n_total, lidx, gate, x1_p, x1_s, h2_p, h2_s, mods_p, mods_s,
                   rows_per_batch, w_sg, w_su, w_sd, gf, stage_rows):
    tp, ts = x1_p.shape[0], x1_s.shape[0]
    assert ts == COMB_TOK and tp % COMB_TOK == 0 and rows_per_batch % COMB_TOK == 0
    prompt_tiles = tp // COMB_TOK
    tiles_per_batch = rows_per_batch // COMB_TOK
    stride = COMB_TOK + SUBLANES
    last = prompt_tiles - 1
    prow = lambda i, *_: (jnp.minimum(i, last), 0)
    srow = lambda i, *_: (0, 0)
    return pl.pallas_call(
        functools.partial(_combine_final_kernel, stage_rows=stage_rows, prompt_tiles=prompt_tiles),
        grid_spec=pltpu.PrefetchScalarGridSpec(
            num_scalar_prefetch=2,
            grid=(prompt_tiles + 1,),
            in_specs=[pl.BlockSpec((COMB_TOK * TOP_K,), lambda i, *_: (i,), memory_space=pltpu.SMEM),
                      pl.BlockSpec((COMB_TOK * TOP_K,), lambda i, *_: (i,), memory_space=pltpu.SMEM),
                      pl.BlockSpec(memory_space=pl.ANY),
                      pl.BlockSpec((COMB_TOK, D_MODEL), prow), pl.BlockSpec((COMB_TOK, D_MODEL), srow),
                      pl.BlockSpec((COMB_TOK, D_MODEL), prow), pl.BlockSpec((COMB_TOK, D_MODEL), srow),
                      pl.BlockSpec((1, 1, D_MODEL),
                                   lambda i, *_: (jnp.minimum(i, last) // tiles_per_batch, 0, 5)),
                      pl.BlockSpec((COMB_TOK, D_MODEL), lambda i, *_: (0, 5)),
                      _const_spec((D_MODEL, D_EXPERT)), _const_spec((D_MODEL, D_EXPERT)),
                      _const_spec((D_EXPERT, D_MODEL)), _const_spec((1, D_MODEL))],
            out_specs=[pl.BlockSpec((COMB_TOK, D_MODEL), prow),
                       pl.BlockSpec((COMB_TOK, D_MODEL), srow)],
            scratch_shapes=[pltpu.VMEM((2 * stage_rows, ROW_TILES, LANES), BF16),
                            pltpu.VMEM((ROW_TILES * stride, LANES), F32),
                            pltpu.VMEM((COMB_TOK, D_MODEL), F32),
                            pltpu.SemaphoreType.DMA((2,))]),
        out_shape=[jax.ShapeDtypeStruct((tp, D_MODEL), F32),
                   jax.ShapeDtypeStruct((ts, D_MODEL), F32)],
        compiler_params=_cparams(("arbitrary",), 48),
        name="combine_final",
    )(src_flat, n_total, lidx, gate, ys, x1_p, x1_s, h2_p, h2_s, mods_p, mods_s, w_sg, w_su, w_sd, gf)


def _moe_plan(top_e, t):
    ar = jnp.arange(N_EXPERTS, dtype=I32)
    sel = top_e[:, :, None] == ar
    onehot = jnp.any(sel, axis=1).astype(I32)
    csum = jnp.cumsum(onehot, axis=0)
    excl = csum - onehot
    counts = csum[-1]
    padded = (counts + MOE_ROWS - 1) // MOE_ROWS * MOE_ROWS
    pend = jnp.cumsum(padded)
    pstart = pend - padded
    pick = lambda per_expert: jnp.sum(jnp.where(sel, per_expert, 0), axis=-1)
    rank = pick(excl[:, None, :])
    n_blocks = -(-(t * TOP_K + N_EXPERTS * (MOE_ROWS - 1)) // MOE_ROWS)
    block_row = jnp.arange(n_blocks, dtype=I32) * MOE_ROWS
    block_expert = jnp.minimum(jnp.sum((pend[None, :] <= block_row[:, None]).astype(I32), axis=1),
                               N_EXPERTS - 1)
    n_used = (pend[-1] // MOE_ROWS).astype(I32).reshape(1)
    first = jnp.concatenate([jnp.ones((1,), I32), (block_expert[1:] != block_expert[:-1]).astype(I32)])
    later = jnp.logical_and(counts[None, :] > 0, ar[None, :] > ar[:, None])
    next_nonempty = jnp.min(jnp.where(later, ar[None, :], N_EXPERTS), axis=1)
    next_nonempty = jnp.where(next_nonempty < N_EXPERTS, next_nonempty, -1)
    next_expert = jnp.sum(jnp.where(block_expert[:, None] == ar[None, :], next_nonempty[None, :], 0), axis=1)
    assert t + MOE_ROWS <= 1 << TOKEN_BITS
    real = ((top_e << TOKEN_BITS) | jnp.arange(t, dtype=I32)[:, None]).reshape(-1)
    r = jnp.arange(MOE_ROWS - 1, dtype=I32)
    int_max = jnp.iinfo(jnp.int32).max
    pad_keys = jnp.where(r[None, :] < (padded - counts)[:, None],
                         (ar[:, None] << TOKEN_BITS) | (t + r[None, :]), int_max).reshape(-1)
    n_rows = n_blocks * MOE_ROWS
    fill = jnp.full((n_rows - real.shape[0] - pad_keys.shape[0],), int_max, I32)
    tokv = jnp.sort(jnp.concatenate([real, pad_keys, fill])) & ((1 << TOKEN_BITS) - 1)
    tok_buf = jnp.where(tokv < t, tokv, 0)
    lo = excl[::COMB_TOK]
    hi = jnp.concatenate([lo[1:], counts[None]], axis=0)
    n_chunks = (hi - lo + COMB_CH - 1) // COMB_CH
    cend = jnp.cumsum(n_chunks, axis=1)
    cstart = cend - n_chunks
    n_total = cend[:, -1].astype(I32)
    src_start = pstart[None, :] + lo
    q = jnp.arange(COMB_MAX_CHUNKS, dtype=I32)
    eq = jnp.minimum(jnp.sum((cend[:, None, :] <= q[None, :, None]).astype(I32), axis=-1), N_EXPERTS - 1)
    sel_q = eq[:, :, None] == ar
    pick_q = lambda v: jnp.sum(jnp.where(sel_q, v[:, None, :], 0), axis=-1)
    src_flat = pick_q(src_start) + (q[None, :] - pick_q(cstart)) * COMB_CH
    src_flat = jnp.where(q[None, :] < n_total[:, None], src_flat, 0)
    stage_rows = COMB_MAX_CHUNKS * COMB_CH
    slot_base = (jnp.arange(t // COMB_TOK, dtype=I32) % 2) * stage_rows
    local = jnp.repeat(cstart * COMB_CH - lo + slot_base[:, None], COMB_TOK, axis=0)
    lidx = pick(local[:, None, :]) + rank
    return dict(block_expert=block_expert.astype(I32), n_used=n_used, tok_buf=tok_buf.astype(I32),
                first=first, next_expert=next_expert.astype(I32),
                n_blocks=n_blocks, src_flat=src_flat.astype(I32), n_total=n_total,
                lidx=lidx.astype(I32), stage_rows=stage_rows)


def kernel(x_prompt, x_sample, state_ssm_re, state_ssm_im, state_conv, c_prompt, c_sample, w_ada, b_ada, norm1_g, norm2_g, w_in, ssm_a_re, ssm_a_im, ssm_log_dt, ssm_b_re, ssm_b_im, ssm_c_re, ssm_c_im, ssm_d, w_glu, b_glu, w_lift_a, conv_w, conv_b, w_lift_b, w_out, w_router, router_bias, w_exp_gate, w_exp_up, w_exp_down, w_sh_gate, w_sh_up, w_sh_down, final_norm_g):
    depth = w_ada.shape[0]
    assert depth == 1
    bp, seq, _ = x_prompt.shape
    bs = x_sample.shape[0]
    tp = bp * seq
    l = 0

    w_in_b = w_in[l].astype(BF16)
    wr_t = w_router[l].T
    wr_hi = wr_t.astype(BF16)
    wr_lo = (wr_t - wr_hi.astype(F32)).astype(BF16)
    mix_w = (ssm_d[l].reshape(1, -1), w_glu[l].astype(BF16), b_glu[l].reshape(1, -1),
             w_lift_a[l].astype(BF16), conv_w[l], conv_b[l].reshape(1, -1), w_lift_b[l].astype(BF16),
             w_out[l].astype(BF16), norm2_g[l].reshape(1, -1), wr_hi, wr_lo)
    g1 = norm1_g[l].reshape(1, -1)

    ada = _ada(jnp.concatenate([c_prompt, c_sample], axis=0), w_ada[l], b_ada[l])
    mods_p = ada[:bp].reshape(bp, 1, -1)
    mods_s = ada[bp:]

    apow, ssm_tm, ssm_bk, ssm_ck = _ssm_prep(ssm_a_re[l], ssm_a_im[l], ssm_log_dt[l], ssm_b_re[l],
                                             ssm_b_im[l], ssm_c_re[l], ssm_c_im[l])
    a1_re, a1_im, al_re, al_im = (_state_layout(apow[i]) for i in range(4))

    xp = x_prompt.reshape(tp, D_MODEL)
    u_p, rest_p = _in_proj(xp, mods_p, seq, g1, w_in_b, tm=256)
    n_chunks = seq // SSM_CHUNK
    s_loc = _ssm_state(u_p, ssm_bk, SSM_CHUNK, n_chunks)
    s_prev, s_fin_p = _ssm_scan(s_loc.reshape(bp, n_chunks, -1), al_re, al_im,
                                jnp.zeros((bp, OCTETS * OCT_STATE), F32))
    y_p = _ssm_out(u_p, s_prev.reshape(bp * n_chunks, -1), ssm_tm, ssm_ck, SSM_CHUNK, n_chunks)
    x1_p, h2_p, lg_p, conv_p = _mixer_post(xp, y_p, u_p, rest_p, mods_p, seq, mix_w, None, tm=256)

    xs = x_sample.reshape(bs, D_MODEL)
    u_s, rest_s = _in_proj(xs, mods_s, 1, g1, w_in_b, tm=bs)
    s0_s = _pack_state(state_ssm_re[l], state_ssm_im[l])
    s_new_s = _ssm_step(_ssm_state(u_s, ssm_bk, 1, bs), s0_s, a1_re, a1_im)
    y_s = _ssm_out(u_s, s0_s, ssm_tm, ssm_ck, 1, bs)
    x1_s, h2_s, lg_s, nc0, nc1 = _mixer_post(xs, y_s, u_s, rest_s, mods_s, 1, mix_w,
                                                    state_conv[l], tm=bs)

    t = tp + bs
    te_p, gate_p = _route(lg_p, router_bias[l])
    te_s, gate_s = _route(lg_s, router_bias[l])
    top_e = jnp.concatenate([te_p, te_s], axis=1).T
    gate = jnp.concatenate([gate_p, gate_s], axis=1).T
    plan = _moe_plan(top_e, t)
    h_tiles = jnp.concatenate([h2_p, h2_s], axis=0).reshape(t, ROW_TILES, LANES)
    xs_sorted = _dispatch(h_tiles, plan["tok_buf"], plan["n_used"], plan["n_blocks"])
    ys = _experts(xs_sorted, plan["block_expert"], plan["n_used"], plan["first"], plan["next_expert"],
                  w_exp_gate[l], w_exp_up[l], w_exp_down[l], plan["n_blocks"])
    w_sg, w_su, w_sd = w_sh_gate[l].astype(BF16), w_sh_up[l].astype(BF16), w_sh_down[l].astype(BF16)
    gf = final_norm_g.reshape(1, -1)
    y_prompt, y_sample = _combine_final(ys, plan["src_flat"], plan["n_total"], plan["lidx"].reshape(-1),
                                        gate.reshape(-1), x1_p, x1_s, h2_p, h2_s, mods_p, mods_s, seq,
                                        w_sg, w_su, w_sd, gf, plan["stage_rows"])

    re_p, im_p = _unpack_state(s_fin_p)
    re_s, im_s = _unpack_state(s_new_s)
    return (y_prompt.reshape(bp, seq, D_MODEL), y_sample.reshape(bs, 1, D_MODEL),
            re_p[None], im_p[None], conv_p[None],
            re_s[None], im_s[None], jnp.stack([nc0, nc1], axis=1)[None])
```

```python
import functools

import jax
import jax.numpy as jnp
from jax import lax
from jax.experimental import pallas as pl
from jax.experimental.pallas import tpu as pltpu

F32 = jnp.float32
BF16 = jnp.bfloat16
I32 = jnp.int32

D_MODEL = 2048
D_SSM = 1024
SSM_GROUP = 16
N_GROUPS = 64
SSM_STATE = 64
D_CONV = 1024
CONV_W = 3
N_EXPERTS = 64
TOP_K = 8
N_EXPERT_GROUPS = 8
EXPERTS_PER_GROUP = 8
TOPK_GROUPS = 4
D_EXPERT = 512
ROUTED_SCALE = 2.5
RMS_EPS = 1e-6

LANES = 128
SUBLANES = 8
OCTETS = N_GROUPS * SSM_GROUP // LANES
OCT_STATE = 2 * (N_GROUPS // OCTETS) * SSM_STATE
SSM_CHUNK = 8
MOE_ROWS = 256
COMB_TOK = 256
COMB_CH = 16
ROW_TILES = D_MODEL // LANES
ROW_STRIDE = MOE_ROWS + SUBLANES
DISP_ROWS = 2 * MOE_ROWS
DISP_STRIDE = DISP_ROWS + SUBLANES
COMB_MAX_CHUNKS = COMB_TOK * TOP_K // COMB_CH + N_EXPERTS
TOKEN_BITS = 14
MIX_ROWS = 128
NEG_INF = float("-inf")


def _cparams(sem, vmem_mb):
    return pltpu.CompilerParams(dimension_semantics=sem, vmem_limit_bytes=vmem_mb * 1024 * 1024)


def _const_spec(shape):
    nd = len(shape)
    return pl.BlockSpec(shape, lambda *_: (0,) * nd, pipeline_mode=pl.Buffered(1))


def _mod(ref):
    v = ref[...]
    return v[0] if v.ndim == 3 else v


def _silu(x):
    return x * jax.nn.sigmoid(x)


def _ada_kernel(c_ref, w_ref, b_ref, o_ref):
    a = _silu(c_ref[...]).astype(BF16)
    o_ref[...] = jnp.dot(a, w_ref[...].astype(BF16), preferred_element_type=F32) + b_ref[...]


def _ada(c_all, w_ada, b_ada):
    m = c_all.shape[0]
    n = w_ada.shape[1]
    tn = 1024
    return pl.pallas_call(
        _ada_kernel,
        grid=(n // tn,),
        in_specs=[pl.BlockSpec((m, D_MODEL), lambda j: (0, 0)),
                  pl.BlockSpec((D_MODEL, tn), lambda j: (0, j)),
                  pl.BlockSpec((1, tn), lambda j: (0, j))],
        out_specs=pl.BlockSpec((m, tn), lambda j: (0, j)),
        out_shape=jax.ShapeDtypeStruct((m, n), F32),
        compiler_params=_cparams(("arbitrary",), 40),
        name="ada",
    )(c_all, w_ada, b_ada.reshape(1, n))


def _mod_spec(mods, col, rows_per_batch, tm):
    if mods.ndim == 3:
        tiles_per_batch = rows_per_batch // tm
        return pl.BlockSpec((1, 1, D_MODEL), lambda m, *_: (m // tiles_per_batch, 0, col))
    return pl.BlockSpec((tm, D_MODEL), lambda m, *_: (m, col))


def _inproj_kernel(x_ref, g_ref, sh_ref, sc_ref, w_ref, u_ref, r_ref):
    x = x_ref[...]
    y = x * lax.rsqrt(jnp.mean(x * x, axis=-1, keepdims=True) + RMS_EPS) * g_ref[...]
    h = (y * (1.0 + _mod(sc_ref)) + _mod(sh_ref)).astype(BF16)
    tn = D_SSM
    u_ref[...] = jnp.dot(h, w_ref[:, 0:tn], preferred_element_type=F32)
    for n in range(1, w_ref.shape[1] // tn):
        r_ref[:, (n - 1) * tn:n * tn] = jnp.dot(h, w_ref[:, n * tn:(n + 1) * tn],
                                                preferred_element_type=F32).astype(BF16)


def _in_proj(x, mods, rows_per_batch, g1, w_in, tm):
    t = x.shape[0]
    d_in = w_in.shape[1]
    n_rest = d_in - D_SSM
    return pl.pallas_call(
        _inproj_kernel,
        grid=(t // tm,),
        in_specs=[pl.BlockSpec((tm, D_MODEL), lambda m: (m, 0)),
                  _const_spec((1, D_MODEL)),
                  _mod_spec(mods, 0, rows_per_batch, tm),
                  _mod_spec(mods, 1, rows_per_batch, tm),
                  _const_spec((D_MODEL, d_in))],
        out_specs=[pl.BlockSpec((tm, D_SSM), lambda m: (m, 0)),
                   pl.BlockSpec((tm, n_rest), lambda m: (m, 0))],
        out_shape=[jax.ShapeDtypeStruct((t, D_SSM), F32),
                   jax.ShapeDtypeStruct((t, n_rest), BF16)],
        compiler_params=_cparams(("arbitrary",), 56),
        name="in_proj",
    )(x, g1, mods, mods, w_in)


def _ssm_prep_kernel(are_ref, aim_ref, ldt_ref, btr_ref, bti_ref, cre_ref, cim_ref,
                     apow_ref, tm_ref, bk_ref, ck_ref):
    lc = SSM_CHUNK
    ng = N_GROUPS // OCTETS
    half = OCT_STATE // 2
    lr = are_ref[...]
    li = aim_ref[...]
    dt = jnp.exp(ldt_ref[...])
    mag = jnp.exp(lr * dt)
    ar = mag * jnp.cos(li * dt)
    ai = mag * jnp.sin(li * dt)
    den = lr * lr + li * li
    xr = ar - 1.0
    cr = (xr * lr + ai * li) / den
    ci = (ai * lr - xr * li) / den
    btr = btr_ref[...]
    bti = bti_ref[...]
    bbr = cr * btr - ci * bti
    bbi = cr * bti + ci * btr
    cre = cre_ref[...]
    cim = cim_ref[...]
    cre_w = jnp.concatenate([cre] * ng, axis=1)
    cim_w = jnp.concatenate([cim] * ng, axis=1)

    def iota(shape, dim):
        return lax.broadcasted_iota(I32, shape, dim)

    width = lc * LANES
    sel_b = ((iota((SSM_STATE, half), 1) & (SSM_STATE - 1)) == iota((SSM_STATE, half), 0)).astype(BF16)
    mask_b = ((iota((width, half), 0) & (LANES - 1)) >> 4) == (iota((width, half), 1) >> 6)
    sel_c = ((iota((half, SSM_STATE), 0) & (SSM_STATE - 1)) == iota((half, SSM_STATE), 1)).astype(BF16)
    mask_c = (iota((half, width), 0) >> 6) == ((iota((half, width), 1) & (LANES - 1)) >> 4)
    mask_t = (iota((LANES, LANES), 0) >> 4) == (iota((LANES, LANES), 1) >> 4)
    dn_batch = (((2,), (2,)), ((0,), (0,)))
    dn_nt = (((1,), (1,)), ((), ()))

    pr = jnp.ones_like(ar)
    pi = jnp.zeros_like(ar)
    w1r, w1i, w2r, w2i = [], [], [], []
    for l in range(lc):
        w1r.append(pr * bbr - pi * bbi)
        w1i.append(pr * bbi + pi * bbr)
        pr, pi = pr * ar - pi * ai, pr * ai + pi * ar
        w2r.append(cre * pr - cim * pi)
        w2i.append(-(cre * pi + cim * pr))
        if l == 0:
            apow_ref[0] = pr
            apow_ref[1] = pi
    apow_ref[2] = pr
    apow_ref[3] = pi

    def rows(ws, order):
        return jnp.concatenate([ws[l] for l in order], axis=0).reshape(width, SSM_STATE).astype(BF16)

    rev = range(lc - 1, -1, -1)
    for c, ws in ((0, w1r), (1, w1i)):
        rep = jnp.dot(rows(ws, rev), sel_b, preferred_element_type=F32)
        bk_ref[:, c * half:(c + 1) * half] = jnp.where(mask_b, rep, 0.0).astype(BF16)
    for c, ws in ((0, w2r), (1, w2i)):
        rep = lax.dot_general(sel_c, rows(ws, range(lc)), dn_nt, preferred_element_type=F32)
        ck_ref[c * half:(c + 1) * half, :] = jnp.where(mask_c, rep, 0.0).astype(BF16)
    w1r_all = jnp.concatenate(w1r, axis=1)
    w1i_all = jnp.concatenate(w1i, axis=1)
    kt = (lax.dot_general(w1r_all, cre_w, dn_batch, precision=lax.Precision.HIGHEST,
                          preferred_element_type=F32)
          - lax.dot_general(w1i_all, cim_w, dn_batch, precision=lax.Precision.HIGHEST,
                            preferred_element_type=F32))
    hh = SSM_GROUP
    t_blocks = [jnp.where(mask_t, kt[:, l * hh:(l + 1) * hh, :].reshape(LANES, LANES), 0.0).astype(BF16)
                for l in range(lc)]
    zero = jnp.zeros((LANES, LANES), BF16)
    for s in range(lc):
        for t in range(lc):
            tm_ref[s * LANES:(s + 1) * LANES, t * LANES:(t + 1) * LANES] = t_blocks[t - s] if t >= s else zero


def _ssm_prep(a_re, a_im, log_dt, b_re, b_im, c_re, c_im):
    g, p, h = N_GROUPS, SSM_STATE, SSM_GROUP
    ng = g // OCTETS
    width = SSM_CHUNK * LANES
    vec = pl.BlockSpec((ng, 1, p), lambda o: (o, 0, 0))
    mat = pl.BlockSpec((ng, h, p), lambda o: (o, 0, 0))
    big = lambda rows, cols: pl.BlockSpec((None, rows, cols), lambda o: (o, 0, 0))
    return pl.pallas_call(
        _ssm_prep_kernel,
        grid=(OCTETS,),
        in_specs=[vec, vec, pl.BlockSpec((ng, 1, 1), lambda o: (o, 0, 0)), mat, mat, mat, mat],
        out_specs=[pl.BlockSpec((4, ng, 1, p), lambda o: (0, o, 0, 0)),
                   big(width, width), big(width, OCT_STATE), big(OCT_STATE, width)],
        out_shape=[jax.ShapeDtypeStruct((4, g, 1, p), F32),
                   jax.ShapeDtypeStruct((OCTETS, width, width), BF16),
                   jax.ShapeDtypeStruct((OCTETS, width, OCT_STATE), BF16),
                   jax.ShapeDtypeStruct((OCTETS, OCT_STATE, width), BF16)],
        compiler_params=_cparams(("arbitrary",), 32),
        name="ssm_prep",
    )(a_re.reshape(g, 1, p), a_im.reshape(g, 1, p), log_dt.reshape(g, 1, 1),
      b_re.transpose(0, 2, 1), b_im.transpose(0, 2, 1), c_re, c_im)


def _state_layout(v):
    return v.reshape(OCTETS, 1, OCT_STATE // 2)


def _load_chunks(u_ref, lc, rb):
    parts = [u_ref[pl.ds(s, rb, stride=lc), :] if lc > 1 else u_ref[...] for s in range(lc)]
    u = parts[0] if lc == 1 else jnp.concatenate(parts, axis=-1)
    return u.astype(BF16)


def _ssm_state_kernel(u_ref, bk_ref, s_ref, *, lc, rb):
    s_ref[...] = jnp.dot(_load_chunks(u_ref, lc, rb), bk_ref[...], preferred_element_type=F32)


def _ssm_state(u, bk, lc, rb):
    t = u.shape[0]
    nb = t // (rb * lc)
    lag0_block = SSM_CHUNK // lc - 1
    return pl.pallas_call(
        functools.partial(_ssm_state_kernel, lc=lc, rb=rb),
        grid=(OCTETS, nb),
        in_specs=[pl.BlockSpec((rb * lc, LANES), lambda o, b: (b, o)),
                  pl.BlockSpec((None, lc * LANES, OCT_STATE), lambda o, b: (o, lag0_block, 0))],
        out_specs=pl.BlockSpec((rb, OCT_STATE), lambda o, b: (b, o)),
        out_shape=jax.ShapeDtypeStruct((t // lc, OCTETS * OCT_STATE), F32),
        compiler_params=_cparams(("arbitrary", "arbitrary"), 32),
        name="ssm_state",
    )(u, bk)


def _ssm_scan_kernel(sl_ref, ar_ref, ai_ref, s0_ref, sp_ref, sf_ref, *, nb, nc):
    half = OCT_STATE // 2
    ar = ar_ref[0]
    ai = ai_ref[0]
    init = tuple((s0_ref[b:b + 1, 0:half], s0_ref[b:b + 1, half:OCT_STATE]) for b in range(nb))

    def body(c, carry):
        new = []
        for b in range(nb):
            sr, si = carry[b]
            sp_ref[b, pl.ds(c, 1), 0:half] = sr
            sp_ref[b, pl.ds(c, 1), half:OCT_STATE] = si
            xr = sl_ref[b, pl.ds(c, 1), 0:half]
            xi = sl_ref[b, pl.ds(c, 1), half:OCT_STATE]
            new.append((ar * sr - ai * si + xr, ar * si + ai * sr + xi))
        return tuple(new)

    fin = lax.fori_loop(0, nc, body, init)
    for b in range(nb):
        sf_ref[b:b + 1, 0:half] = fin[b][0]
        sf_ref[b:b + 1, half:OCT_STATE] = fin[b][1]


def _ssm_scan(s_loc, al_re, al_im, s0):
    nb, nc, width = s_loc.shape
    return pl.pallas_call(
        functools.partial(_ssm_scan_kernel, nb=nb, nc=nc),
        grid=(OCTETS,),
        in_specs=[pl.BlockSpec((nb, nc, OCT_STATE), lambda o: (0, 0, o)),
                  pl.BlockSpec((1, 1, OCT_STATE // 2), lambda o: (o, 0, 0)),
                  pl.BlockSpec((1, 1, OCT_STATE // 2), lambda o: (o, 0, 0)),
                  pl.BlockSpec((nb, OCT_STATE), lambda o: (0, o))],
        out_specs=[pl.BlockSpec((nb, nc, OCT_STATE), lambda o: (0, 0, o)),
                   pl.BlockSpec((nb, OCT_STATE), lambda o: (0, o))],
        out_shape=[jax.ShapeDtypeStruct((nb, nc, width), F32),
                   jax.ShapeDtypeStruct((nb, width), F32)],
        compiler_params=_cparams(("arbitrary",), 32),
        name="ssm_scan",
    )(s_loc, al_re, al_im, s0)


def _ssm_out_kernel(u_ref, sp_ref, tm_ref, ck_ref, y_ref, *, lc, rb):
    y = (jnp.dot(_load_chunks(u_ref, lc, rb), tm_ref[...], preferred_element_type=F32)
         + jnp.dot(sp_ref[...].astype(BF16), ck_ref[...], preferred_element_type=F32))
    if lc == 1:
        y_ref[...] = y
    else:
        for t in range(lc):
            y_ref[pl.ds(t, rb, stride=lc), :] = y[:, t * LANES:(t + 1) * LANES]


def _ssm_out(u, s_prev, tm, ck, lc, rb):
    t = u.shape[0]
    nb = t // (rb * lc)
    return pl.pallas_call(
        functools.partial(_ssm_out_kernel, lc=lc, rb=rb),
        grid=(OCTETS, nb),
        in_specs=[pl.BlockSpec((rb * lc, LANES), lambda o, b: (b, o)),
                  pl.BlockSpec((rb, OCT_STATE), lambda o, b: (b, o)),
                  pl.BlockSpec((None, lc * LANES, lc * LANES), lambda o, b: (o, 0, 0)),
                  pl.BlockSpec((None, OCT_STATE, lc * LANES), lambda o, b: (o, 0, 0))],
        out_specs=pl.BlockSpec((rb * lc, LANES), lambda o, b: (b, o)),
        out_shape=jax.ShapeDtypeStruct((t, D_SSM), F32),
        compiler_params=_cparams(("arbitrary", "arbitrary"), 32),
        name="ssm_out",
    )(u, s_prev, tm, ck)


def _ssm_step_kernel(sl_ref, s0_ref, ar_ref, ai_ref, sn_ref):
    half = OCT_STATE // 2
    ar = ar_ref[0]
    ai = ai_ref[0]
    sr = s0_ref[:, 0:half]
    si = s0_ref[:, half:OCT_STATE]
    sn_ref[:, 0:half] = ar * sr - ai * si + sl_ref[:, 0:half]
    sn_ref[:, half:OCT_STATE] = ar * si + ai * sr + sl_ref[:, half:OCT_STATE]


def _ssm_step(s_loc, s0, a_re, a_im):
    rows, width = s_loc.shape
    return pl.pallas_call(
        _ssm_step_kernel,
        grid=(OCTETS,),
        in_specs=[pl.BlockSpec((rows, OCT_STATE), lambda o: (0, o)),
                  pl.BlockSpec((rows, OCT_STATE), lambda o: (0, o)),
                  pl.BlockSpec((1, 1, OCT_STATE // 2), lambda o: (o, 0, 0)),
                  pl.BlockSpec((1, 1, OCT_STATE // 2), lambda o: (o, 0, 0))],
        out_specs=pl.BlockSpec((rows, OCT_STATE), lambda o: (0, o)),
        out_shape=jax.ShapeDtypeStruct((rows, width), F32),
        compiler_params=_cparams(("arbitrary",), 32),
        name="ssm_step",
    )(s_loc, s0, a_re, a_im)


def _pack_state(s_re, s_im):
    b = s_re.shape[0]
    half = OCT_STATE // 2
    return jnp.stack([s_re.reshape(b, OCTETS, half), s_im.reshape(b, OCTETS, half)], axis=2).reshape(b, -1)


def _unpack_state(s):
    b = s.shape[0]
    s4 = s.reshape(b, OCTETS, 2, OCT_STATE // 2)
    return (s4[:, :, 0].reshape(b, N_GROUPS, SSM_STATE), s4[:, :, 1].reshape(b, N_GROUPS, SSM_STATE))


def _mixer_kernel(*refs, seq, tm, tiles_per_batch):
    (x_ref, ys_ref, u_ref, r_ref, g1_ref, sh2_ref, sc2_ref, d_ref, wglu_ref, bglu_ref, wla_ref,
     cw_ref, cb_ref, wlb_ref, wout_ref, n2g_ref, wrh_ref, wrl_ref) = refs[:18]
    if seq:
        x1_ref, h2_ref, lg_ref, nc_ref, carry = refs[18:]
    else:
        cb0_ref, cb1_ref, x1_ref, h2_ref, lg_ref, nc0_ref, nc1_ref = refs[18:]

    xb = r_ref[:, 0:D_CONV].astype(F32)
    gb = r_ref[:, D_CONV:2 * D_CONV].astype(F32)
    gc = r_ref[:, 2 * D_CONV:3 * D_CONV].astype(F32)
    v = gc * xb
    if seq:
        first = pl.program_id(0) % tiles_per_batch == 0

        @pl.when(first)
        def _():
            carry[...] = jnp.zeros_like(carry)

        c0 = carry[0:1, :]
        c1 = carry[1:2, :]
        row = lax.broadcasted_iota(I32, (tm, 1), 0)
        v1 = jnp.where(row == 0, c1, pltpu.roll(v, 1, 0))
        v2 = jnp.where(row == 0, c0, jnp.where(row == 1, c1, pltpu.roll(v, 2, 0)))
        tail = v[tm - 2:tm, :]
        carry[...] = tail
        nc_ref[0] = tail
    else:
        v2 = cb0_ref[...]
        v1 = cb1_ref[...]
        nc0_ref[...] = v1
        nc1_ref[...] = v
    conv = cb_ref[...] + cw_ref[0:1, :] * v2 + cw_ref[1:2, :] * v1 + cw_ref[2:3, :] * v
    gated_conv = (gb * conv).astype(BF16)

    def mod_rows(ref, rs):
        return ref[0] if len(ref.shape) == 3 else ref[rs, :]

    n_split = max(tm // MIX_ROWS, 1)
    rows = tm // n_split
    off = 3 * D_CONV
    dn = (((1,), (1,)), ((), ()))
    for part in range(n_split):
        rs = slice(part * rows, (part + 1) * rows)
        ya = jax.nn.gelu(ys_ref[rs, :] + d_ref[...] * u_ref[rs, :], approximate=True)
        z = jnp.dot(ya.astype(BF16), wglu_ref[...], preferred_element_type=F32) + bglu_ref[...]
        ya = ya * jax.nn.sigmoid(z)
        branch_a = jnp.dot(ya.astype(BF16), wla_ref[...], preferred_element_type=F32)
        branch_b = jnp.dot(gated_conv[rs], wlb_ref[...], preferred_element_type=F32)
        gla = r_ref[rs, off:off + D_MODEL].astype(F32)
        glb = r_ref[rs, off + D_MODEL:off + 2 * D_MODEL].astype(F32)
        merged = jax.nn.sigmoid(gla) * branch_a + jax.nn.sigmoid(glb) * branch_b
        x1 = x_ref[rs, :] + mod_rows(g1_ref, rs) * jnp.dot(merged.astype(BF16), wout_ref[...],
                                                           preferred_element_type=F32)
        x1_ref[rs, :] = x1

        y = x1 * lax.rsqrt(jnp.mean(x1 * x1, axis=-1, keepdims=True) + RMS_EPS) * n2g_ref[...]
        h2 = y * (1.0 + mod_rows(sc2_ref, rs)) + mod_rows(sh2_ref, rs)
        h2b = h2.astype(BF16)
        h2_ref[rs, :] = h2b

        h2l = (h2 - h2b.astype(F32)).astype(BF16)
        lg_ref[:, rs] = (lax.dot_general(wrh_ref[...], h2b, dn, preferred_element_type=F32)
                         + lax.dot_general(wrh_ref[...], h2l, dn, preferred_element_type=F32)
                         + lax.dot_general(wrl_ref[...], h2b, dn, preferred_element_type=F32))


def _mixer_post(x, y_ssm, u, rest, mods, rows_per_batch, weights, conv_state, tm):
    t = x.shape[0]
    seq = conv_state is None
    tiles_per_batch = rows_per_batch // tm
    (ssm_d, w_glu, b_glu, w_la, conv_w, conv_b, w_lb, w_out, n2g, wr_hi, wr_lo) = weights
    row = lambda m: (m, 0)
    in_specs = [pl.BlockSpec((tm, D_MODEL), row),
                pl.BlockSpec((tm, D_SSM), row),
                pl.BlockSpec((tm, D_SSM), row),
                pl.BlockSpec((tm, rest.shape[1]), row),
                _mod_spec(mods, 2, rows_per_batch, tm),
                _mod_spec(mods, 3, rows_per_batch, tm),
                _mod_spec(mods, 4, rows_per_batch, tm),
                _const_spec((1, D_SSM)), _const_spec((D_SSM, D_SSM)), _const_spec((1, D_SSM)),
                _const_spec((D_SSM, D_MODEL)), _const_spec((CONV_W, D_CONV)), _const_spec((1, D_CONV)),
                _const_spec((D_CONV, D_MODEL)), _const_spec((D_MODEL, D_MODEL)),
                _const_spec((1, D_MODEL)), _const_spec((N_EXPERTS, D_MODEL)),
                _const_spec((N_EXPERTS, D_MODEL))]
    args = [x, y_ssm, u, rest, mods, mods, mods, ssm_d, w_glu, b_glu, w_la, conv_w, conv_b, w_lb,
            w_out, n2g, wr_hi, wr_lo]
    out_specs = [pl.BlockSpec((tm, D_MODEL), row),
                 pl.BlockSpec((tm, D_MODEL), row),
                 pl.BlockSpec((N_EXPERTS, tm), lambda m: (0, m))]
    out_shape = [jax.ShapeDtypeStruct((t, D_MODEL), F32),
                 jax.ShapeDtypeStruct((t, D_MODEL), BF16),
                 jax.ShapeDtypeStruct((N_EXPERTS, t), F32)]
    scratch = []
    if seq:
        nb = t // rows_per_batch
        out_specs.append(pl.BlockSpec((1, CONV_W - 1, D_CONV), lambda m: (m // tiles_per_batch, 0, 0)))
        out_shape.append(jax.ShapeDtypeStruct((nb, CONV_W - 1, D_CONV), F32))
        scratch.append(pltpu.VMEM((CONV_W - 1, D_CONV), F32))
    else:
        in_specs += [pl.BlockSpec((tm, D_CONV), row), pl.BlockSpec((tm, D_CONV), row)]
        args += [conv_state[:, 0, :], conv_state[:, 1, :]]
        out_specs += [pl.BlockSpec((tm, D_CONV), row), pl.BlockSpec((tm, D_CONV), row)]
        out_shape += [jax.ShapeDtypeStruct((t, D_CONV), F32), jax.ShapeDtypeStruct((t, D_CONV), F32)]
    return pl.pallas_call(
        functools.partial(_mixer_kernel, seq=seq, tm=tm, tiles_per_batch=tiles_per_batch),
        grid=(t // tm,),
        in_specs=in_specs,
        out_specs=out_specs,
        out_shape=out_shape,
        scratch_shapes=scratch,
        compiler_params=_cparams(("arbitrary",), 56),
        name="mixer_post",
    )(*args)


def _route_kernel(lg_ref, bias_ref, te_ref, gate_ref):
    scores = [jax.nn.sigmoid(lg_ref[e]) for e in range(N_EXPERTS)]
    choice = [scores[e] + bias_ref[e] for e in range(N_EXPERTS)]
    shape = scores[0].shape

    def first_max(vals):
        m = functools.reduce(jnp.maximum, vals)
        found = jnp.zeros(shape, jnp.bool_)
        hot = []
        for v in vals:
            is_m = jnp.logical_and(v == m, jnp.logical_not(found))
            hot.append(is_m)
            found = jnp.logical_or(found, is_m)
        return m, hot

    gscore = []
    for g in range(N_EXPERT_GROUPS):
        vals = choice[g * EXPERTS_PER_GROUP:(g + 1) * EXPERTS_PER_GROUP]
        m1, hot = first_max(vals)
        rest = [jnp.where(h, NEG_INF, v) for h, v in zip(hot, vals)]
        gscore.append(m1 + functools.reduce(jnp.maximum, rest))

    gsel = [jnp.zeros(shape, jnp.bool_) for _ in range(N_EXPERT_GROUPS)]
    for _ in range(TOPK_GROUPS):
        _, hot = first_max(gscore)
        gsel = [jnp.logical_or(s, h) for s, h in zip(gsel, hot)]
        gscore = [jnp.where(h, NEG_INF, v) for h, v in zip(hot, gscore)]

    cand = [jnp.where(gsel[e // EXPERTS_PER_GROUP], choice[e], NEG_INF) for e in range(N_EXPERTS)]
    picked_e = []
    picked_s = []
    for _ in range(TOP_K):
        _, hot = first_max(cand)
        idx = jnp.zeros(shape, I32)
        sc = jnp.zeros(shape, F32)
        for e in range(N_EXPERTS):
            idx = jnp.where(hot[e], e, idx)
            sc = jnp.where(hot[e], scores[e], sc)
        cand = [jnp.where(h, NEG_INF, v) for h, v in zip(hot, cand)]
        picked_e.append(idx)
        picked_s.append(sc)
    total = functools.reduce(lambda a, b: a + b, picked_s)
    for k in range(TOP_K):
        te_ref[k] = picked_e[k]
        gate_ref[k] = picked_s[k] / total * ROUTED_SCALE


def _route(logits_t, bias):
    t = logits_t.shape[1]
    r = t // LANES
    rb = min(r, SUBLANES)
    lg = logits_t.reshape(N_EXPERTS, r, LANES)
    bias_b = jnp.broadcast_to(bias.astype(F32).reshape(N_EXPERTS, 1, 1), (N_EXPERTS, 1, LANES))
    te, gate = pl.pallas_call(
        _route_kernel,
        grid=(r // rb,),
        in_specs=[pl.BlockSpec((N_EXPERTS, rb, LANES), lambda i: (0, i, 0)),
                  pl.BlockSpec((N_EXPERTS, 1, LANES), lambda i: (0, 0, 0))],
        out_specs=[pl.BlockSpec((TOP_K, rb, LANES), lambda i: (0, i, 0)),
                   pl.BlockSpec((TOP_K, rb, LANES), lambda i: (0, i, 0))],
        out_shape=[jax.ShapeDtypeStruct((TOP_K, r, LANES), I32),
                   jax.ShapeDtypeStruct((TOP_K, r, LANES), F32)],
        compiler_params=_cparams(("arbitrary",), 32),
        name="route",
    )(lg, bias_b)
    return te.reshape(TOP_K, t), gate.reshape(TOP_K, t)


def _dispatch_kernel(nused_ref, tok_ref, h_ref, xs_ref, tile_a, tile_b):
    i = pl.program_id(0)
    n_live = (nused_ref[0] * MOE_ROWS + (DISP_ROWS - 1)) // DISP_ROWS

    def gather(tile):
        for mi in range(DISP_ROWS):
            tile[pl.ds(mi, ROW_TILES, stride=DISP_STRIDE), :] = h_ref[tok_ref[mi]].astype(F32)

    def emit(tile):
        for j in range(ROW_TILES):
            xs_ref[:, j * LANES:(j + 1) * LANES] = (
                tile[j * DISP_STRIDE:j * DISP_STRIDE + DISP_ROWS, :].astype(BF16))

    even = i % 2 == 0
    both = jnp.logical_and(i >= 1, i < n_live)

    @pl.when(jnp.logical_and(both, even))
    def _():
        emit(tile_b)
        gather(tile_a)

    @pl.when(jnp.logical_and(both, jnp.logical_not(even)))
    def _():
        emit(tile_a)
        gather(tile_b)

    @pl.when(i == 0)
    def _():
        gather(tile_a)

    @pl.when(jnp.logical_and(i == n_live, even))
    def _():
        emit(tile_b)

    @pl.when(jnp.logical_and(i == n_live, jnp.logical_not(even)))
    def _():
        emit(tile_a)

    @pl.when(i > n_live)
    def _():
        xs_ref[...] = jnp.zeros_like(xs_ref)


def _dispatch(h_tiles, tok_buf, n_used, n_blocks):
    n_rows = n_blocks * MOE_ROWS
    assert n_rows % DISP_ROWS == 0

    def last_live(i, nu):
        return jnp.minimum(i, (nu[0] * MOE_ROWS - 1) // DISP_ROWS)

    scratch = pltpu.VMEM((ROW_TILES * DISP_STRIDE, LANES), F32)
    return pl.pallas_call(
        _dispatch_kernel,
        grid_spec=pltpu.PrefetchScalarGridSpec(
            num_scalar_prefetch=1,
            grid=(n_rows // DISP_ROWS + 1,),
            in_specs=[pl.BlockSpec((DISP_ROWS,), lambda i, nu: (last_live(i, nu),),
                                   memory_space=pltpu.SMEM),
                      pl.BlockSpec(memory_space=pltpu.VMEM)],
            out_specs=pl.BlockSpec((DISP_ROWS, D_MODEL), lambda i, nu: (jnp.maximum(i - 1, 0), 0)),
            scratch_shapes=[scratch, scratch]),
        out_shape=jax.ShapeDtypeStruct((n_rows, D_MODEL), BF16),
        compiler_params=_cparams(("arbitrary",), 56),
        name="dispatch",
    )(n_used, tok_buf, h_tiles)


def _experts_kernel(be_ref, nused_ref, first_ref, next_ref, xs_ref, wg_hbm, wu_hbm, wd_hbm, ys_ref,
                    wg_f, wu_f, wd_f, wg_s, wu_s, wd_s, tile, sem):
    i = pl.program_id(0)

    def weight_copies(e):
        return (pltpu.make_async_copy(wg_hbm.at[e], wg_f, sem.at[0]),
                pltpu.make_async_copy(wu_hbm.at[e], wu_f, sem.at[1]),
                pltpu.make_async_copy(wd_hbm.at[e], wd_f, sem.at[2]))

    n_used = nused_ref[0]

    @pl.when(i < n_used)
    def _():
        @pl.when(i == 0)
        def _():
            for c in weight_copies(be_ref[0]):
                c.start()

        @pl.when(first_ref[i] == 1)
        def _():
            for c in weight_copies(be_ref[i]):
                c.wait()
            wg_s[...] = wg_f[...].astype(BF16)
            wu_s[...] = wu_f[...].astype(BF16)
            wd_s[...] = wd_f[...].astype(BF16)

            @pl.when(next_ref[i] >= 0)
            def _():
                for c in weight_copies(next_ref[i]):
                    c.start()

    def ffn_to_tile():
        x = xs_ref[...]
        g = jnp.dot(x, wg_s[...], preferred_element_type=F32)
        u = jnp.dot(x, wu_s[...], preferred_element_type=F32)
        h = (_silu(g) * u).astype(BF16)
        y = jnp.dot(h, wd_s[...], preferred_element_type=F32)
        for j in range(ROW_TILES):
            tile[j * ROW_STRIDE:j * ROW_STRIDE + MOE_ROWS, :] = y[:, j * LANES:(j + 1) * LANES]

    def tile_to_rows():
        for mi in range(MOE_ROWS):
            ys_ref[mi] = tile[pl.ds(mi, ROW_TILES, stride=ROW_STRIDE), :].astype(BF16)

    @pl.when(jnp.logical_and(i >= 1, i < n_used))
    def _():
        tile_to_rows()
        ffn_to_tile()

    @pl.when(i == 0)
    def _():
        ffn_to_tile()

    @pl.when(i == n_used)
    def _():
        tile_to_rows()

    @pl.when(i > n_used)
    def _():
        ys_ref[...] = jnp.zeros_like(ys_ref)


def _experts(xs, block_expert, n_used, first, next_expert, w_gate, w_up, w_down, n_blocks):
    def xmap(i, be, nu, fi, nx):
        return (jnp.minimum(i, nu[0] - 1), 0)

    any_spec = pl.BlockSpec(memory_space=pl.ANY)
    return pl.pallas_call(
        _experts_kernel,
        grid_spec=pltpu.PrefetchScalarGridSpec(
            num_scalar_prefetch=4,
            grid=(n_blocks + 1,),
            in_specs=[pl.BlockSpec((MOE_ROWS, D_MODEL), xmap), any_spec, any_spec, any_spec],
            out_specs=pl.BlockSpec((MOE_ROWS, ROW_TILES, LANES),
                                   lambda i, *_: (jnp.maximum(i - 1, 0), 0, 0)),
            scratch_shapes=[pltpu.VMEM((D_MODEL, D_EXPERT), F32),
                            pltpu.VMEM((D_MODEL, D_EXPERT), F32),
                            pltpu.VMEM((D_EXPERT, D_MODEL), F32),
                            pltpu.VMEM((D_MODEL, D_EXPERT), BF16),
                            pltpu.VMEM((D_MODEL, D_EXPERT), BF16),
                            pltpu.VMEM((D_EXPERT, D_MODEL), BF16),
                            pltpu.VMEM((ROW_TILES * ROW_STRIDE, LANES), F32),
                            pltpu.SemaphoreType.DMA((3,))]),
        out_shape=jax.ShapeDtypeStruct((n_blocks * MOE_ROWS, ROW_TILES, LANES), BF16),
        compiler_params=_cparams(("arbitrary",), 48),
        name="experts",
    )(block_expert, n_used, first, next_expert, xs, w_gate, w_up, w_down)


def _combine_kernel(src_ref, ntot_ref, lidx_ref, gate_ref, ys_hbm, out_ref,
                    stage, tile, sem, *, stage_rows):
    i = pl.program_id(0)
    n_tiles = pl.num_programs(0)

    def chunk_copy(src, slot, dst):
        return pltpu.make_async_copy(ys_hbm.at[pl.ds(src, COMB_CH)],
                                     stage.at[pl.ds(slot * stage_rows + dst, COMB_CH)],
                                     sem.at[slot])

    def issue(tile_idx, slot):
        def c_body(q, _):
            chunk_copy(src_ref[tile_idx, q], slot, q * COMB_CH).start()
            return 0

        lax.fori_loop(0, ntot_ref[tile_idx], c_body, 0)

    @pl.when(i == 0)
    def _():
        issue(0, 0)

    @pl.when(i + 1 < n_tiles)
    def _():
        issue(i + 1, (i + 1) % 2)

    slot = i % 2

    def w_body(c, _):
        chunk_copy(0, slot, 0).wait()
        return 0

    lax.fori_loop(0, ntot_ref[i], w_body, 0)

    stride = COMB_TOK + SUBLANES

    for tk in range(COMB_TOK):
        acc = None
        for k in range(TOP_K):
            w = stage[lidx_ref[tk * TOP_K + k]].astype(F32)
            term = gate_ref[tk * TOP_K + k] * w
            acc = term if acc is None else acc + term
        tile[pl.ds(tk, ROW_TILES, stride=stride), :] = acc
    for j in range(ROW_TILES):
        out_ref[:, j * LANES:(j + 1) * LANES] = tile[j * stride:j * stride + COMB_TOK, :]


def _combine(ys, src_flat, n_total, lidx, gate, t, stage_rows):
    n_tiles = t // COMB_TOK
    stride = COMB_TOK + SUBLANES
    return pl.pallas_call(
        functools.partial(_combine_kernel, stage_rows=stage_rows),
        grid_spec=pltpu.PrefetchScalarGridSpec(
            num_scalar_prefetch=2,
            grid=(n_tiles,),
            in_specs=[pl.BlockSpec((COMB_TOK * TOP_K,), lambda i, *_: (i,), memory_space=pltpu.SMEM),
                      pl.BlockSpec((COMB_TOK * TOP_K,), lambda i, *_: (i,), memory_space=pltpu.SMEM),
                      pl.BlockSpec(memory_space=pl.ANY)],
            out_specs=pl.BlockSpec((COMB_TOK, D_MODEL), lambda i, *_: (i, 0)),
            scratch_shapes=[pltpu.VMEM((2 * stage_rows, ROW_TILES, LANES), BF16),
                            pltpu.VMEM((ROW_TILES * stride, LANES), F32),
                            pltpu.SemaphoreType.DMA((2,))]),
        out_shape=jax.ShapeDtypeStruct((t, D_MODEL), F32),
        compiler_params=_cparams(("arbitrary",), 48),
        name="combine",
    )(src_flat, n_total, lidx, gate, ys)


def _final_kernel(h_ref, x1_ref, rt_ref, g2_ref, wsg_ref, wsu_ref, wsd_ref, gf_ref, y_ref):
    h = h_ref[...]
    a = _silu(jnp.dot(h, wsg_ref[...], preferred_element_type=F32))
    b = jnp.dot(h, wsu_ref[...], preferred_element_type=F32)
    shared = jnp.dot((a * b).astype(BF16), wsd_ref[...], preferred_element_type=F32)
    xo = x1_ref[...] + _mod(g2_ref) * (shared + rt_ref[...])
    y_ref[...] = xo * lax.rsqrt(jnp.mean(xo * xo, axis=-1, keepdims=True) + RMS_EPS) * gf_ref[...]


def _final(h2, x1, routed, routed_row0, mods, rows_per_batch, w_sg, w_su, w_sd, gf, tm):
    t = x1.shape[0]
    row = lambda m: (m, 0)
    assert routed_row0 % tm == 0
    block0 = routed_row0 // tm
    return pl.pallas_call(
        _final_kernel,
        grid=(t // tm,),
        in_specs=[pl.BlockSpec((tm, D_MODEL), row), pl.BlockSpec((tm, D_MODEL), row),
                  pl.BlockSpec((tm, D_MODEL), lambda m: (m + block0, 0)),
                  _mod_spec(mods, 5, rows_per_batch, tm),
                  _const_spec((D_MODEL, D_EXPERT)), _const_spec((D_MODEL, D_EXPERT)),
                  _const_spec((D_EXPERT, D_MODEL)), _const_spec((1, D_MODEL))],
        out_specs=pl.BlockSpec((tm, D_MODEL), row),
        out_shape=jax.ShapeDtypeStruct((t, D_MODEL), F32),
        compiler_params=_cparams(("arbitrary",), 48),
        name="final",
    )(h2, x1, routed, mods, w_sg, w_su, w_sd, gf)


def _moe_plan(top_e, t):
    ar = jnp.arange(N_EXPERTS, dtype=I32)
    sel = top_e[:, :, None] == ar
    onehot = jnp.any(sel, axis=1).astype(I32)
    csum = jnp.cumsum(onehot, axis=0)
    excl = csum - onehot
    counts = csum[-1]
    padded = (counts + MOE_ROWS - 1) // MOE_ROWS * MOE_ROWS
    pend = jnp.cumsum(padded)
    pstart = pend - padded
    pick = lambda per_expert: jnp.sum(jnp.where(sel, per_expert, 0), axis=-1)
    rank = pick(excl[:, None, :])
    n_blocks = -(-(t * TOP_K + N_EXPERTS * (MOE_ROWS - 1)) // MOE_ROWS)
    block_row = jnp.arange(n_blocks, dtype=I32) * MOE_ROWS
    block_expert = jnp.minimum(jnp.sum((pend[None, :] <= block_row[:, None]).astype(I32), axis=1),
                               N_EXPERTS - 1)
    n_used = (pend[-1] // MOE_ROWS).astype(I32).reshape(1)
    first = jnp.concatenate([jnp.ones((1,), I32), (block_expert[1:] != block_expert[:-1]).astype(I32)])
    later = jnp.logical_and(counts[None, :] > 0, ar[None, :] > ar[:, None])
    next_nonempty = jnp.min(jnp.where(later, ar[None, :], N_EXPERTS), axis=1)
    next_nonempty = jnp.where(next_nonempty < N_EXPERTS, next_nonempty, -1)
    next_expert = jnp.sum(jnp.where(block_expert[:, None] == ar[None, :], next_nonempty[None, :], 0), axis=1)
    assert t + MOE_ROWS <= 1 << TOKEN_BITS
    real = ((top_e << TOKEN_BITS) | jnp.arange(t, dtype=I32)[:, None]).reshape(-1)
    r = jnp.arange(MOE_ROWS - 1, dtype=I32)
    int_max = jnp.iinfo(jnp.int32).max
    pad_keys = jnp.where(r[None, :] < (padded - counts)[:, None],
                         (ar[:, None] << TOKEN_BITS) | (t + r[None, :]), int_max).reshape(-1)
    n_rows = n_blocks * MOE_ROWS
    fill = jnp.full((n_rows - real.shape[0] - pad_keys.shape[0],), int_max, I32)
    tokv = jnp.sort(jnp.concatenate([real, pad_keys, fill])) & ((1 << TOKEN_BITS) - 1)
    tok_buf = jnp.where(tokv < t, tokv, 0)
    lo = excl[::COMB_TOK]
    hi = jnp.concatenate([lo[1:], counts[None]], axis=0)
    n_chunks = (hi - lo + COMB_CH - 1) // COMB_CH
    cend = jnp.cumsum(n_chunks, axis=1)
    cstart = cend - n_chunks
    n_total = cend[:, -1].astype(I32)
    src_start = pstart[None, :] + lo
    q = jnp.arange(COMB_MAX_CHUNKS, dtype=I32)
    eq = jnp.minimum(jnp.sum((cend[:, None, :] <= q[None, :, None]).astype(I32), axis=-1), N_EXPERTS - 1)
    sel_q = eq[:, :, None] == ar
    pick_q = lambda v: jnp.sum(jnp.where(sel_q, v[:, None, :], 0), axis=-1)
    src_flat = pick_q(src_start) + (q[None, :] - pick_q(cstart)) * COMB_CH
    src_flat = jnp.where(q[None, :] < n_total[:, None], src_flat, 0)
    stage_rows = COMB_MAX_CHUNKS * COMB_CH
    n_tiles = lo.shape[0]
    t_pad = n_tiles * COMB_TOK
    slot_base = (jnp.arange(n_tiles, dtype=I32) % 2) * stage_rows
    local = jnp.repeat(cstart * COMB_CH - lo + slot_base[:, None], COMB_TOK, axis=0)[:t]
    lidx = pick(local[:, None, :]) + rank
    lidx = jnp.concatenate([lidx, jnp.broadcast_to(slot_base[-1], (t_pad - t, TOP_K))], axis=0)
    return dict(block_expert=block_expert.astype(I32), n_used=n_used, tok_buf=tok_buf.astype(I32),
                first=first, next_expert=next_expert.astype(I32),
                n_blocks=n_blocks, src_flat=src_flat.astype(I32), n_total=n_total,
                lidx=lidx.astype(I32), stage_rows=stage_rows, t_pad=t_pad)


def kernel(x_prompt, x_sample, state_ssm_re, state_ssm_im, state_conv, c_prompt, c_sample, w_ada, b_ada, norm1_g, norm2_g, w_in, ssm_a_re, ssm_a_im, ssm_log_dt, ssm_b_re, ssm_b_im, ssm_c_re, ssm_c_im, ssm_d, w_glu, b_glu, w_lift_a, conv_w, conv_b, w_lift_b, w_out, w_router, router_bias, w_exp_gate, w_exp_up, w_exp_down, w_sh_gate, w_sh_up, w_sh_down, final_norm_g):
    depth = w_ada.shape[0]
    assert depth == 1
    bp, seq, _ = x_prompt.shape
    bs = x_sample.shape[0]
    tp = bp * seq
    l = 0

    w_in_b = w_in[l].astype(BF16)
    wr_t = w_router[l].T
    wr_hi = wr_t.astype(BF16)
    wr_lo = (wr_t - wr_hi.astype(F32)).astype(BF16)
    mix_w = (ssm_d[l].reshape(1, -1), w_glu[l].astype(BF16), b_glu[l].reshape(1, -1),
             w_lift_a[l].astype(BF16), conv_w[l], conv_b[l].reshape(1, -1), w_lift_b[l].astype(BF16),
             w_out[l].astype(BF16), norm2_g[l].reshape(1, -1), wr_hi, wr_lo)
    g1 = norm1_g[l].reshape(1, -1)

    ada = _ada(jnp.concatenate([c_prompt, c_sample], axis=0), w_ada[l], b_ada[l])
    mods_p = ada[:bp].reshape(bp, 1, -1)
    mods_s = ada[bp:]

    apow, ssm_tm, ssm_bk, ssm_ck = _ssm_prep(ssm_a_re[l], ssm_a_im[l], ssm_log_dt[l], ssm_b_re[l],
                                             ssm_b_im[l], ssm_c_re[l], ssm_c_im[l])
    a1_re, a1_im, al_re, al_im = (_state_layout(apow[i]) for i in range(4))

    xp = x_prompt.reshape(tp, D_MODEL)
    u_p, rest_p = _in_proj(xp, mods_p, seq, g1, w_in_b, tm=256)
    n_chunks = seq // SSM_CHUNK
    s_loc = _ssm_state(u_p, ssm_bk, SSM_CHUNK, n_chunks)
    s_prev, s_fin_p = _ssm_scan(s_loc.reshape(bp, n_chunks, -1), al_re, al_im,
                                jnp.zeros((bp, OCTETS * OCT_STATE), F32))
    y_p = _ssm_out(u_p, s_prev.reshape(bp * n_chunks, -1), ssm_tm, ssm_ck, SSM_CHUNK, n_chunks)
    x1_p, h2_p, lg_p, conv_p = _mixer_post(xp, y_p, u_p, rest_p, mods_p, seq, mix_w, None, tm=256)

    xs = x_sample.reshape(bs, D_MODEL)
    u_s, rest_s = _in_proj(xs, mods_s, 1, g1, w_in_b, tm=bs)
    s0_s = _pack_state(state_ssm_re[l], state_ssm_im[l])
    s_new_s = _ssm_step(_ssm_state(u_s, ssm_bk, 1, bs), s0_s, a1_re, a1_im)
    y_s = _ssm_out(u_s, s0_s, ssm_tm, ssm_ck, 1, bs)
    x1_s, h2_s, lg_s, nc0, nc1 = _mixer_post(xs, y_s, u_s, rest_s, mods_s, 1, mix_w,
                                                    state_conv[l], tm=bs)

    t = tp + bs
    te_p, gate_p = _route(lg_p, router_bias[l])
    te_s, gate_s = _route(lg_s, router_bias[l])
    top_e = jnp.concatenate([te_p, te_s], axis=1).T
    gate = jnp.concatenate([gate_p, gate_s], axis=1).T
    plan = _moe_plan(top_e, t)
    h_tiles = jnp.concatenate([h2_p, h2_s], axis=0).reshape(t, ROW_TILES, LANES)
    xs_sorted = _dispatch(h_tiles, plan["tok_buf"], plan["n_used"], plan["n_blocks"])
    ys = _experts(xs_sorted, plan["block_expert"], plan["n_used"], plan["first"], plan["next_expert"],
                  w_exp_gate[l], w_exp_up[l], w_exp_down[l], plan["n_blocks"])
    gate_pad = jnp.concatenate([gate, jnp.zeros((plan["t_pad"] - t, TOP_K), F32)], axis=0)
    routed = _combine(ys, plan["src_flat"], plan["n_total"], plan["lidx"].reshape(-1),
                      gate_pad.reshape(-1), plan["t_pad"], plan["stage_rows"])

    w_sg, w_su, w_sd = w_sh_gate[l].astype(BF16), w_sh_up[l].astype(BF16), w_sh_down[l].astype(BF16)
    gf = final_norm_g.reshape(1, -1)
    y_prompt = _final(h2_p, x1_p, routed, 0, mods_p, seq, w_sg, w_su, w_sd, gf, tm=512)
    y_sample = _final(h2_s, x1_s, routed, tp, mods_s, 1, w_sg, w_su, w_sd, gf, tm=bs)

    re_p, im_p = _unpack_state(s_fin_p)
    re_s, im_s = _unpack_state(s_new_s)
    return (y_prompt.reshape(bp, seq, D_MODEL), y_sample.reshape(bs, 1, D_MODEL),
            re_p[None], im_p[None], conv_p[None],
            re_s[None], im_s[None], jnp.stack([nc0, nc1], axis=1)[None])
```

```python
import functools

import jax
import jax.numpy as jnp
from jax import lax
from jax.experimental import pallas as pl
from jax.experimental.pallas import tpu as pltpu

F32 = jnp.float32
BF16 = jnp.bfloat16
I32 = jnp.int32

D_MODEL = 2048
D_SSM = 1024
SSM_GROUP = 16
N_GROUPS = 64
SSM_STATE = 64
D_CONV = 1024
CONV_W = 3
N_EXPERTS = 64
TOP_K = 8
N_EXPERT_GROUPS = 8
EXPERTS_PER_GROUP = 8
TOPK_GROUPS = 4
D_EXPERT = 512
ROUTED_SCALE = 2.5
RMS_EPS = 1e-6

LANES = 128
SUBLANES = 8
OCTETS = N_GROUPS * SSM_GROUP // LANES
OCT_STATE = 2 * (N_GROUPS // OCTETS) * SSM_STATE
SSM_CHUNK = 8
MOE_ROWS = 256
COMB_TOK = 128
COMB_CH = 16
ROW_TILES = D_MODEL // LANES
ROW_STRIDE = MOE_ROWS + SUBLANES
DISP_ROWS = 2 * MOE_ROWS
DISP_STRIDE = DISP_ROWS + SUBLANES
COMB_MAX_CHUNKS = COMB_TOK * TOP_K // COMB_CH + N_EXPERTS
TOKEN_BITS = 14
MIX_ROWS = 128
NEG_INF = float("-inf")


def _cparams(sem, vmem_mb):
    return pltpu.CompilerParams(dimension_semantics=sem, vmem_limit_bytes=vmem_mb * 1024 * 1024)


def _const_spec(shape):
    nd = len(shape)
    return pl.BlockSpec(shape, lambda *_: (0,) * nd, pipeline_mode=pl.Buffered(1))


def _mod(ref):
    v = ref[...]
    return v[0] if v.ndim == 3 else v


def _silu(x):
    return x * jax.nn.sigmoid(x)


def _ada_kernel(c_ref, w_ref, b_ref, o_ref):
    a = _silu(c_ref[...]).astype(BF16)
    o_ref[...] = jnp.dot(a, w_ref[...].astype(BF16), preferred_element_type=F32) + b_ref[...]


def _ada(c_all, w_ada, b_ada):
    m = c_all.shape[0]
    n = w_ada.shape[1]
    tn = 1024
    return pl.pallas_call(
        _ada_kernel,
        grid=(n // tn,),
        in_specs=[pl.BlockSpec((m, D_MODEL), lambda j: (0, 0)),
                  pl.BlockSpec((D_MODEL, tn), lambda j: (0, j)),
                  pl.BlockSpec((1, tn), lambda j: (0, j))],
        out_specs=pl.BlockSpec((m, tn), lambda j: (0, j)),
        out_shape=jax.ShapeDtypeStruct((m, n), F32),
        compiler_params=_cparams(("arbitrary",), 40),
        name="ada",
    )(c_all, w_ada, b_ada.reshape(1, n))


def _mod_spec(mods, col, rows_per_batch, tm):
    if mods.ndim == 3:
        tiles_per_batch = rows_per_batch // tm
        return pl.BlockSpec((1, 1, D_MODEL), lambda m, *_: (m // tiles_per_batch, 0, col))
    return pl.BlockSpec((tm, D_MODEL), lambda m, *_: (m, col))


def _inproj_kernel(x_ref, g_ref, sh_ref, sc_ref, w_ref, u_ref, r_ref):
    x = x_ref[...]
    y = x * lax.rsqrt(jnp.mean(x * x, axis=-1, keepdims=True) + RMS_EPS) * g_ref[...]
    h = (y * (1.0 + _mod(sc_ref)) + _mod(sh_ref)).astype(BF16)
    tn = D_SSM
    u_ref[...] = jnp.dot(h, w_ref[:, 0:tn], preferred_element_type=F32)
    for n in range(1, w_ref.shape[1] // tn):
        r_ref[:, (n - 1) * tn:n * tn] = jnp.dot(h, w_ref[:, n * tn:(n + 1) * tn],
                                                preferred_element_type=F32).astype(BF16)


def _in_proj(x, mods, rows_per_batch, g1, w_in, tm):
    t = x.shape[0]
    d_in = w_in.shape[1]
    n_rest = d_in - D_SSM
    return pl.pallas_call(
        _inproj_kernel,
        grid=(t // tm,),
        in_specs=[pl.BlockSpec((tm, D_MODEL), lambda m: (m, 0)),
                  _const_spec((1, D_MODEL)),
                  _mod_spec(mods, 0, rows_per_batch, tm),
                  _mod_spec(mods, 1, rows_per_batch, tm),
                  _const_spec((D_MODEL, d_in))],
        out_specs=[pl.BlockSpec((tm, D_SSM), lambda m: (m, 0)),
                   pl.BlockSpec((tm, n_rest), lambda m: (m, 0))],
        out_shape=[jax.ShapeDtypeStruct((t, D_SSM), F32),
                   jax.ShapeDtypeStruct((t, n_rest), BF16)],
        compiler_params=_cparams(("arbitrary",), 56),
        name="in_proj",
    )(x, g1, mods, mods, w_in)


def _ssm_prep_kernel(are_ref, aim_ref, ldt_ref, btr_ref, bti_ref, cre_ref, cim_ref,
                     apow_ref, tm_ref, bk_ref, ck_ref):
    lc = SSM_CHUNK
    ng = N_GROUPS // OCTETS
    half = OCT_STATE // 2
    lr = are_ref[...]
    li = aim_ref[...]
    dt = jnp.exp(ldt_ref[...])
    mag = jnp.exp(lr * dt)
    ar = mag * jnp.cos(li * dt)
    ai = mag * jnp.sin(li * dt)
    den = lr * lr + li * li
    xr = ar - 1.0
    cr = (xr * lr + ai * li) / den
    ci = (ai * lr - xr * li) / den
    btr = btr_ref[...]
    bti = bti_ref[...]
    bbr = cr * btr - ci * bti
    bbi = cr * bti + ci * btr
    cre = cre_ref[...]
    cim = cim_ref[...]
    cre_w = jnp.concatenate([cre] * ng, axis=1)
    cim_w = jnp.concatenate([cim] * ng, axis=1)

    def iota(shape, dim):
        return lax.broadcasted_iota(I32, shape, dim)

    width = lc * LANES
    sel_b = ((iota((SSM_STATE, half), 1) & (SSM_STATE - 1)) == iota((SSM_STATE, half), 0)).astype(BF16)
    mask_b = ((iota((width, half), 0) & (LANES - 1)) >> 4) == (iota((width, half), 1) >> 6)
    sel_c = ((iota((half, SSM_STATE), 0) & (SSM_STATE - 1)) == iota((half, SSM_STATE), 1)).astype(BF16)
    mask_c = (iota((half, width), 0) >> 6) == ((iota((half, width), 1) & (LANES - 1)) >> 4)
    mask_t = (iota((LANES, LANES), 0) >> 4) == (iota((LANES, LANES), 1) >> 4)
    dn_batch = (((2,), (2,)), ((0,), (0,)))
    dn_nt = (((1,), (1,)), ((), ()))

    pr = jnp.ones_like(ar)
    pi = jnp.zeros_like(ar)
    w1r, w1i, w2r, w2i = [], [], [], []
    for l in range(lc):
        w1r.append(pr * bbr - pi * bbi)
        w1i.append(pr * bbi + pi * bbr)
        pr, pi = pr * ar - pi * ai, pr * ai + pi * ar
        w2r.append(cre * pr - cim * pi)
        w2i.append(-(cre * pi + cim * pr))
        if l == 0:
            apow_ref[0] = pr
            apow_ref[1] = pi
    apow_ref[2] = pr
    apow_ref[3] = pi

    def rows(ws, order):
        return jnp.concatenate([ws[l] for l in order], axis=0).reshape(width, SSM_STATE).astype(BF16)

    rev = range(lc - 1, -1, -1)
    for c, ws in ((0, w1r), (1, w1i)):
        rep = jnp.dot(rows(ws, rev), sel_b, preferred_element_type=F32)
        bk_ref[:, c * half:(c + 1) * half] = jnp.where(mask_b, rep, 0.0).astype(BF16)
    for c, ws in ((0, w2r), (1, w2i)):
        rep = lax.dot_general(sel_c, rows(ws, range(lc)), dn_nt, preferred_element_type=F32)
        ck_ref[c * half:(c + 1) * half, :] = jnp.where(mask_c, rep, 0.0).astype(BF16)
    w1r_all = jnp.concatenate(w1r, axis=1)
    w1i_all = jnp.concatenate(w1i, axis=1)
    kt = (lax.dot_general(w1r_all, cre_w, dn_batch, precision=lax.Precision.HIGHEST,
                          preferred_element_type=F32)
          - lax.dot_general(w1i_all, cim_w, dn_batch, precision=lax.Precision.HIGHEST,
                            preferred_element_type=F32))
    hh = SSM_GROUP
    t_blocks = [jnp.where(mask_t, kt[:, l * hh:(l + 1) * hh, :].reshape(LANES, LANES), 0.0).astype(BF16)
                for l in range(lc)]
    zero = jnp.zeros((LANES, LANES), BF16)
    for s in range(lc):
        for t in range(lc):
            tm_ref[s * LANES:(s + 1) * LANES, t * LANES:(t + 1) * LANES] = t_blocks[t - s] if t >= s else zero


def _ssm_prep(a_re, a_im, log_dt, b_re, b_im, c_re, c_im):
    g, p, h = N_GROUPS, SSM_STATE, SSM_GROUP
    ng = g // OCTETS
    width = SSM_CHUNK * LANES
    vec = pl.BlockSpec((ng, 1, p), lambda o: (o, 0, 0))
    mat = pl.BlockSpec((ng, h, p), lambda o: (o, 0, 0))
    big = lambda rows, cols: pl.BlockSpec((None, rows, cols), lambda o: (o, 0, 0))
    return pl.pallas_call(
        _ssm_prep_kernel,
        grid=(OCTETS,),
        in_specs=[vec, vec, pl.BlockSpec((ng, 1, 1), lambda o: (o, 0, 0)), mat, mat, mat, mat],
        out_specs=[pl.BlockSpec((4, ng, 1, p), lambda o: (0, o, 0, 0)),
                   big(width, width), big(width, OCT_STATE), big(OCT_STATE, width)],
        out_shape=[jax.ShapeDtypeStruct((4, g, 1, p), F32),
                   jax.ShapeDtypeStruct((OCTETS, width, width), BF16),
                   jax.ShapeDtypeStruct((OCTETS, width, OCT_STATE), BF16),
                   jax.ShapeDtypeStruct((OCTETS, OCT_STATE, width), BF16)],
        compiler_params=_cparams(("arbitrary",), 32),
        name="ssm_prep",
    )(a_re.reshape(g, 1, p), a_im.reshape(g, 1, p), log_dt.reshape(g, 1, 1),
      b_re.transpose(0, 2, 1), b_im.transpose(0, 2, 1), c_re, c_im)


def _state_layout(v):
    return v.reshape(OCTETS, 1, OCT_STATE // 2)


def _load_chunks(u_ref, lc, rb):
    parts = [u_ref[pl.ds(s, rb, stride=lc), :] if lc > 1 else u_ref[...] for s in range(lc)]
    u = parts[0] if lc == 1 else jnp.concatenate(parts, axis=-1)
    return u.astype(BF16)


def _ssm_state_kernel(u_ref, bk_ref, s_ref, *, lc, rb):
    s_ref[...] = jnp.dot(_load_chunks(u_ref, lc, rb), bk_ref[...], preferred_element_type=F32)


def _ssm_state(u, bk, lc, rb):
    t = u.shape[0]
    nb = t // (rb * lc)
    lag0_block = SSM_CHUNK // lc - 1
    return pl.pallas_call(
        functools.partial(_ssm_state_kernel, lc=lc, rb=rb),
        grid=(OCTETS, nb),
        in_specs=[pl.BlockSpec((rb * lc, LANES), lambda o, b: (b, o)),
                  pl.BlockSpec((None, lc * LANES, OCT_STATE), lambda o, b: (o, lag0_block, 0))],
        out_specs=pl.BlockSpec((rb, OCT_STATE), lambda o, b: (b, o)),
        out_shape=jax.ShapeDtypeStruct((t // lc, OCTETS * OCT_STATE), F32),
        compiler_params=_cparams(("arbitrary", "arbitrary"), 32),
        name="ssm_state",
    )(u, bk)


def _ssm_scan_kernel(sl_ref, ar_ref, ai_ref, s0_ref, sp_ref, sf_ref, *, nb, nc):
    half = OCT_STATE // 2
    ar = ar_ref[0]
    ai = ai_ref[0]
    init = tuple((s0_ref[b:b + 1, 0:half], s0_ref[b:b + 1, half:OCT_STATE]) for b in range(nb))

    def body(c, carry):
        new = []
        for b in range(nb):
            sr, si = carry[b]
            sp_ref[b, pl.ds(c, 1), 0:half] = sr
            sp_ref[b, pl.ds(c, 1), half:OCT_STATE] = si
            xr = sl_ref[b, pl.ds(c, 1), 0:half]
            xi = sl_ref[b, pl.ds(c, 1), half:OCT_STATE]
            new.append((ar * sr - ai * si + xr, ar * si + ai * sr + xi))
        return tuple(new)

    fin = lax.fori_loop(0, nc, body, init)
    for b in range(nb):
        sf_ref[b:b + 1, 0:half] = fin[b][0]
        sf_ref[b:b + 1, half:OCT_STATE] = fin[b][1]


def _ssm_scan(s_loc, al_re, al_im, s0):
    nb, nc, width = s_loc.shape
    return pl.pallas_call(
        functools.partial(_ssm_scan_kernel, nb=nb, nc=nc),
        grid=(OCTETS,),
        in_specs=[pl.BlockSpec((nb, nc, OCT_STATE), lambda o: (0, 0, o)),
                  pl.BlockSpec((1, 1, OCT_STATE // 2), lambda o: (o, 0, 0)),
                  pl.BlockSpec((1, 1, OCT_STATE // 2), lambda o: (o, 0, 0)),
                  pl.BlockSpec((nb, OCT_STATE), lambda o: (0, o))],
        out_specs=[pl.BlockSpec((nb, nc, OCT_STATE), lambda o: (0, 0, o)),
                   pl.BlockSpec((nb, OCT_STATE), lambda o: (0, o))],
        out_shape=[jax.ShapeDtypeStruct((nb, nc, width), F32),
                   jax.ShapeDtypeStruct((nb, width), F32)],
        compiler_params=_cparams(("arbitrary",), 32),
        name="ssm_scan",
    )(s_loc, al_re, al_im, s0)


def _ssm_out_kernel(u_ref, sp_ref, tm_ref, ck_ref, y_ref, *, lc, rb):
    y = (jnp.dot(_load_chunks(u_ref, lc, rb), tm_ref[...], preferred_element_type=F32)
         + jnp.dot(sp_ref[...].astype(BF16), ck_ref[...], preferred_element_type=F32))
    if lc == 1:
        y_ref[...] = y
    else:
        for t in range(lc):
            y_ref[pl.ds(t, rb, stride=lc), :] = y[:, t * LANES:(t + 1) * LANES]


def _ssm_out(u, s_prev, tm, ck, lc, rb):
    t = u.shape[0]
    nb = t // (rb * lc)
    return pl.pallas_call(
        functools.partial(_ssm_out_kernel, lc=lc, rb=rb),
        grid=(OCTETS, nb),
        in_specs=[pl.BlockSpec((rb * lc, LANES), lambda o, b: (b, o)),
                  pl.BlockSpec((rb, OCT_STATE), lambda o, b: (b, o)),
                  pl.BlockSpec((None, lc * LANES, lc * LANES), lambda o, b: (o, 0, 0)),
                  pl.BlockSpec((None, OCT_STATE, lc * LANES), lambda o, b: (o, 0, 0))],
        out_specs=pl.BlockSpec((rb * lc, LANES), lambda o, b: (b, o)),
        out_shape=jax.ShapeDtypeStruct((t, D_SSM), F32),
        compiler_params=_cparams(("arbitrary", "arbitrary"), 32),
        name="ssm_out",
    )(u, s_prev, tm, ck)


def _ssm_step_kernel(sl_ref, s0_ref, ar_ref, ai_ref, sn_ref):
    half = OCT_STATE // 2
    ar = ar_ref[0]
    ai = ai_ref[0]
    sr = s0_ref[:, 0:half]
    si = s0_ref[:, half:OCT_STATE]
    sn_ref[:, 0:half] = ar * sr - ai * si + sl_ref[:, 0:half]
    sn_ref[:, half:OCT_STATE] = ar * si + ai * sr + sl_ref[:, half:OCT_STATE]


def _ssm_step(s_loc, s0, a_re, a_im):
    rows, width = s_loc.shape
    return pl.pallas_call(
        _ssm_step_kernel,
        grid=(OCTETS,),
        in_specs=[pl.BlockSpec((rows, OCT_STATE), lambda o: (0, o)),
                  pl.BlockSpec((rows, OCT_STATE), lambda o: (0, o)),
                  pl.BlockSpec((1, 1, OCT_STATE // 2), lambda o: (o, 0, 0)),
                  pl.BlockSpec((1, 1, OCT_STATE // 2), lambda o: (o, 0, 0))],
        out_specs=pl.BlockSpec((rows, OCT_STATE), lambda o: (0, o)),
        out_shape=jax.ShapeDtypeStruct((rows, width), F32),
        compiler_params=_cparams(("arbitrary",), 32),
        name="ssm_step",
    )(s_loc, s0, a_re, a_im)


def _pack_state(s_re, s_im):
    b = s_re.shape[0]
    half = OCT_STATE // 2
    return jnp.stack([s_re.reshape(b, OCTETS, half), s_im.reshape(b, OCTETS, half)], axis=2).reshape(b, -1)


def _unpack_state(s):
    b = s.shape[0]
    s4 = s.reshape(b, OCTETS, 2, OCT_STATE // 2)
    return (s4[:, :, 0].reshape(b, N_GROUPS, SSM_STATE), s4[:, :, 1].reshape(b, N_GROUPS, SSM_STATE))


def _mixer_kernel(*refs, seq, tm, tiles_per_batch):
    (x_ref, ys_ref, u_ref, r_ref, g1_ref, sh2_ref, sc2_ref, d_ref, wglu_ref, bglu_ref, wla_ref,
     cw_ref, cb_ref, wlb_ref, wout_ref, n2g_ref, wrh_ref, wrl_ref) = refs[:18]
    if seq:
        x1_ref, h2_ref, lg_ref, nc_ref, carry = refs[18:]
    else:
        cb0_ref, cb1_ref, x1_ref, h2_ref, lg_ref, nc0_ref, nc1_ref = refs[18:]

    xb = r_ref[:, 0:D_CONV].astype(F32)
    gb = r_ref[:, D_CONV:2 * D_CONV].astype(F32)
    gc = r_ref[:, 2 * D_CONV:3 * D_CONV].astype(F32)
    v = gc * xb
    if seq:
        first = pl.program_id(0) % tiles_per_batch == 0

        @pl.when(first)
        def _():
            carry[...] = jnp.zeros_like(carry)

        c0 = carry[0:1, :]
        c1 = carry[1:2, :]
        row = lax.broadcasted_iota(I32, (tm, 1), 0)
        v1 = jnp.where(row == 0, c1, pltpu.roll(v, 1, 0))
        v2 = jnp.where(row == 0, c0, jnp.where(row == 1, c1, pltpu.roll(v, 2, 0)))
        tail = v[tm - 2:tm, :]
        carry[...] = tail
        nc_ref[0] = tail
    else:
        v2 = cb0_ref[...]
        v1 = cb1_ref[...]
        nc0_ref[...] = v1
        nc1_ref[...] = v
    conv = cb_ref[...] + cw_ref[0:1, :] * v2 + cw_ref[1:2, :] * v1 + cw_ref[2:3, :] * v
    gated_conv = (gb * conv).astype(BF16)

    def mod_rows(ref, rs):
        return ref[0] if len(ref.shape) == 3 else ref[rs, :]

    n_split = max(tm // MIX_ROWS, 1)
    rows = tm // n_split
    off = 3 * D_CONV
    dn = (((1,), (1,)), ((), ()))
    for part in range(n_split):
        rs = slice(part * rows, (part + 1) * rows)
        ya = jax.nn.gelu(ys_ref[rs, :] + d_ref[...] * u_ref[rs, :], approximate=True)
        z = jnp.dot(ya.astype(BF16), wglu_ref[...], preferred_element_type=F32) + bglu_ref[...]
        ya = ya * jax.nn.sigmoid(z)
        branch_a = jnp.dot(ya.astype(BF16), wla_ref[...], preferred_element_type=F32)
        branch_b = jnp.dot(gated_conv[rs], wlb_ref[...], preferred_element_type=F32)
        gla = r_ref[rs, off:off + D_MODEL].astype(F32)
        glb = r_ref[rs, off + D_MODEL:off + 2 * D_MODEL].astype(F32)
        merged = jax.nn.sigmoid(gla) * branch_a + jax.nn.sigmoid(glb) * branch_b
        x1 = x_ref[rs, :] + mod_rows(g1_ref, rs) * jnp.dot(merged.astype(BF16), wout_ref[...],
                                                           preferred_element_type=F32)
        x1_ref[rs, :] = x1

        y = x1 * lax.rsqrt(jnp.mean(x1 * x1, axis=-1, keepdims=True) + RMS_EPS) * n2g_ref[...]
        h2 = y * (1.0 + mod_rows(sc2_ref, rs)) + mod_rows(sh2_ref, rs)
        h2b = h2.astype(BF16)
        h2_ref[rs, :] = h2b

        h2l = (h2 - h2b.astype(F32)).astype(BF16)
        lg_ref[:, rs] = (lax.dot_general(wrh_ref[...], h2b, dn, preferred_element_type=F32)
                         + lax.dot_general(wrh_ref[...], h2l, dn, preferred_element_type=F32)
                         + lax.dot_general(wrl_ref[...], h2b, dn, preferred_element_type=F32))


def _mixer_post(x, y_ssm, u, rest, mods, rows_per_batch, weights, conv_state, tm):
    t = x.shape[0]
    seq = conv_state is None
    tiles_per_batch = rows_per_batch // tm
    (ssm_d, w_glu, b_glu, w_la, conv_w, conv_b, w_lb, w_out, n2g, wr_hi, wr_lo) = weights
    row = lambda m: (m, 0)
    in_specs = [pl.BlockSpec((tm, D_MODEL), row),
                pl.BlockSpec((tm, D_SSM), row),
                pl.BlockSpec((tm, D_SSM), row),
                pl.BlockSpec((tm, rest.shape[1]), row),
                _mod_spec(mods, 2, rows_per_batch, tm),
                _mod_spec(mods, 3, rows_per_batch, tm),
                _mod_spec(mods, 4, rows_per_batch, tm),
                _const_spec((1, D_SSM)), _const_spec((D_SSM, D_SSM)), _const_spec((1, D_SSM)),
                _const_spec((D_SSM, D_MODEL)), _const_spec((CONV_W, D_CONV)), _const_spec((1, D_CONV)),
                _const_spec((D_CONV, D_MODEL)), _const_spec((D_MODEL, D_MODEL)),
                _const_spec((1, D_MODEL)), _const_spec((N_EXPERTS, D_MODEL)),
                _const_spec((N_EXPERTS, D_MODEL))]
    args = [x, y_ssm, u, rest, mods, mods, mods, ssm_d, w_glu, b_glu, w_la, conv_w, conv_b, w_lb,
            w_out, n2g, wr_hi, wr_lo]
    out_specs = [pl.BlockSpec((tm, D_MODEL), row),
                 pl.BlockSpec((tm, D_MODEL), row),
                 pl.BlockSpec((N_EXPERTS, tm), lambda m: (0, m))]
    out_shape = [jax.ShapeDtypeStruct((t, D_MODEL), F32),
                 jax.ShapeDtypeStruct((t, D_MODEL), BF16),
                 jax.ShapeDtypeStruct((N_EXPERTS, t), F32)]
    scratch = []
    if seq:
        nb = t // rows_per_batch
        out_specs.append(pl.BlockSpec((1, CONV_W - 1, D_CONV), lambda m: (m // tiles_per_batch, 0, 0)))
        out_shape.append(jax.ShapeDtypeStruct((nb, CONV_W - 1, D_CONV), F32))
        scratch.append(pltpu.VMEM((CONV_W - 1, D_CONV), F32))
    else:
        in_specs += [pl.BlockSpec((tm, D_CONV), row), pl.BlockSpec((tm, D_CONV), row)]
        args += [conv_state[:, 0, :], conv_state[:, 1, :]]
        out_specs += [pl.BlockSpec((tm, D_CONV), row), pl.BlockSpec((tm, D_CONV), row)]
        out_shape += [jax.ShapeDtypeStruct((t, D_CONV), F32), jax.ShapeDtypeStruct((t, D_CONV), F32)]
    return pl.pallas_call(
        functools.partial(_mixer_kernel, seq=seq, tm=tm, tiles_per_batch=tiles_per_batch),
        grid=(t // tm,),
        in_specs=in_specs,
        out_specs=out_specs,
        out_shape=out_shape,
        scratch_shapes=scratch,
        compiler_params=_cparams(("arbitrary",), 56),
        name="mixer_post",
    )(*args)


def _route_kernel(lg_ref, bias_ref, te_ref, gate_ref):
    scores = [jax.nn.sigmoid(lg_ref[e]) for e in range(N_EXPERTS)]
    choice = [scores[e] + bias_ref[e] for e in range(N_EXPERTS)]
    shape = scores[0].shape

    def first_max(vals):
        m = functools.reduce(jnp.maximum, vals)
        found = jnp.zeros(shape, jnp.bool_)
        hot = []
        for v in vals:
            is_m = jnp.logical_and(v == m, jnp.logical_not(found))
            hot.append(is_m)
            found = jnp.logical_or(found, is_m)
        return m, hot

    gscore = []
    for g in range(N_EXPERT_GROUPS):
        vals = choice[g * EXPERTS_PER_GROUP:(g + 1) * EXPERTS_PER_GROUP]
        m1, hot = first_max(vals)
        rest = [jnp.where(h, NEG_INF, v) for h, v in zip(hot, vals)]
        gscore.append(m1 + functools.reduce(jnp.maximum, rest))

    gsel = [jnp.zeros(shape, jnp.bool_) for _ in range(N_EXPERT_GROUPS)]
    for _ in range(TOPK_GROUPS):
        _, hot = first_max(gscore)
        gsel = [jnp.logical_or(s, h) for s, h in zip(gsel, hot)]
        gscore = [jnp.where(h, NEG_INF, v) for h, v in zip(hot, gscore)]

    cand = [jnp.where(gsel[e // EXPERTS_PER_GROUP], choice[e], NEG_INF) for e in range(N_EXPERTS)]
    picked_e = []
    picked_s = []
    for _ in range(TOP_K):
        _, hot = first_max(cand)
        idx = jnp.zeros(shape, I32)
        sc = jnp.zeros(shape, F32)
        for e in range(N_EXPERTS):
            idx = jnp.where(hot[e], e, idx)
            sc = jnp.where(hot[e], scores[e], sc)
        cand = [jnp.where(h, NEG_INF, v) for h, v in zip(hot, cand)]
        picked_e.append(idx)
        picked_s.append(sc)
    total = functools.reduce(lambda a, b: a + b, picked_s)
    for k in range(TOP_K):
        te_ref[k] = picked_e[k]
        gate_ref[k] = picked_s[k] / total * ROUTED_SCALE


def _route(logits_t, bias):
    t = logits_t.shape[1]
    r = t // LANES
    rb = min(r, SUBLANES)
    lg = logits_t.reshape(N_EXPERTS, r, LANES)
    bias_b = jnp.broadcast_to(bias.astype(F32).reshape(N_EXPERTS, 1, 1), (N_EXPERTS, 1, LANES))
    te, gate = pl.pallas_call(
        _route_kernel,
        grid=(r // rb,),
        in_specs=[pl.BlockSpec((N_EXPERTS, rb, LANES), lambda i: (0, i, 0)),
                  pl.BlockSpec((N_EXPERTS, 1, LANES), lambda i: (0, 0, 0))],
        out_specs=[pl.BlockSpec((TOP_K, rb, LANES), lambda i: (0, i, 0)),
                   pl.BlockSpec((TOP_K, rb, LANES), lambda i: (0, i, 0))],
        out_shape=[jax.ShapeDtypeStruct((TOP_K, r, LANES), I32),
                   jax.ShapeDtypeStruct((TOP_K, r, LANES), F32)],
        compiler_params=_cparams(("arbitrary",), 32),
        name="route",
    )(lg, bias_b)
    return te.reshape(TOP_K, t), gate.reshape(TOP_K, t)


def _dispatch_kernel(nused_ref, tok_ref, h_ref, xs_ref, tile):
    used_rows = nused_ref[0] * MOE_ROWS
    live = pl.program_id(0) * DISP_ROWS < used_rows

    @pl.when(jnp.logical_not(live))
    def _():
        xs_ref[...] = jnp.zeros_like(xs_ref)

    @pl.when(live)
    def _():
        for mi in range(DISP_ROWS):
            tile[pl.ds(mi, ROW_TILES, stride=DISP_STRIDE), :] = h_ref[tok_ref[mi]].astype(F32)
        for j in range(ROW_TILES):
            xs_ref[:, j * LANES:(j + 1) * LANES] = (
                tile[j * DISP_STRIDE:j * DISP_STRIDE + DISP_ROWS, :].astype(BF16))


def _dispatch(h_tiles, tok_buf, n_used, n_blocks):
    n_rows = n_blocks * MOE_ROWS
    assert n_rows % DISP_ROWS == 0

    def last_live(i, nu):
        return jnp.minimum(i, (nu[0] * MOE_ROWS - 1) // DISP_ROWS)

    return pl.pallas_call(
        _dispatch_kernel,
        grid_spec=pltpu.PrefetchScalarGridSpec(
            num_scalar_prefetch=1,
            grid=(n_rows // DISP_ROWS,),
            in_specs=[pl.BlockSpec((DISP_ROWS,), lambda i, nu: (last_live(i, nu),),
                                   memory_space=pltpu.SMEM),
                      pl.BlockSpec(memory_space=pltpu.VMEM)],
            out_specs=pl.BlockSpec((DISP_ROWS, D_MODEL), lambda i, nu: (i, 0)),
            scratch_shapes=[pltpu.VMEM((ROW_TILES * DISP_STRIDE, LANES), F32)]),
        out_shape=jax.ShapeDtypeStruct((n_rows, D_MODEL), BF16),
        compiler_params=_cparams(("arbitrary",), 56),
        name="dispatch",
    )(n_used, tok_buf, h_tiles)


def _experts_kernel(be_ref, nused_ref, first_ref, next_ref, xs_ref, wg_hbm, wu_hbm, wd_hbm, ys_ref,
                    wg_f, wu_f, wd_f, wg_s, wu_s, wd_s, tile_a, tile_b, sem):
    i = pl.program_id(0)

    def weight_copies(e):
        return (pltpu.make_async_copy(wg_hbm.at[e], wg_f, sem.at[0]),
                pltpu.make_async_copy(wu_hbm.at[e], wu_f, sem.at[1]),
                pltpu.make_async_copy(wd_hbm.at[e], wd_f, sem.at[2]))

    n_used = nused_ref[0]

    @pl.when(i < n_used)
    def _():
        @pl.when(i == 0)
        def _():
            for c in weight_copies(be_ref[0]):
                c.start()

        @pl.when(first_ref[i] == 1)
        def _():
            for c in weight_copies(be_ref[i]):
                c.wait()
            wg_s[...] = wg_f[...].astype(BF16)
            wu_s[...] = wu_f[...].astype(BF16)
            wd_s[...] = wd_f[...].astype(BF16)

            @pl.when(next_ref[i] >= 0)
            def _():
                for c in weight_copies(next_ref[i]):
                    c.start()

    def ffn_to_tile(tile):
        x = xs_ref[...]
        g = jnp.dot(x, wg_s[...], preferred_element_type=F32)
        u = jnp.dot(x, wu_s[...], preferred_element_type=F32)
        h = (_silu(g) * u).astype(BF16)
        y = jnp.dot(h, wd_s[...], preferred_element_type=F32)
        for j in range(ROW_TILES):
            tile[j * ROW_STRIDE:j * ROW_STRIDE + MOE_ROWS, :] = y[:, j * LANES:(j + 1) * LANES]

    def tile_to_rows(tile):
        for mi in range(MOE_ROWS):
            ys_ref[mi] = tile[pl.ds(mi, ROW_TILES, stride=ROW_STRIDE), :].astype(BF16)

    even = i % 2 == 0
    both = jnp.logical_and(i >= 1, i < n_used)

    @pl.when(jnp.logical_and(both, even))
    def _():
        ffn_to_tile(tile_a)
        tile_to_rows(tile_b)

    @pl.when(jnp.logical_and(both, jnp.logical_not(even)))
    def _():
        ffn_to_tile(tile_b)
        tile_to_rows(tile_a)

    @pl.when(i == 0)
    def _():
        ffn_to_tile(tile_a)

    @pl.when(jnp.logical_and(i == n_used, even))
    def _():
        tile_to_rows(tile_b)

    @pl.when(jnp.logical_and(i == n_used, jnp.logical_not(even)))
    def _():
        tile_to_rows(tile_a)

    @pl.when(i > n_used)
    def _():
        ys_ref[...] = jnp.zeros_like(ys_ref)


def _experts(xs, block_expert, n_used, first, next_expert, w_gate, w_up, w_down, n_blocks):
    def xmap(i, be, nu, fi, nx):
        return (jnp.minimum(i, nu[0] - 1), 0)

    any_spec = pl.BlockSpec(memory_space=pl.ANY)
    return pl.pallas_call(
        _experts_kernel,
        grid_spec=pltpu.PrefetchScalarGridSpec(
            num_scalar_prefetch=4,
            grid=(n_blocks + 1,),
            in_specs=[pl.BlockSpec((MOE_ROWS, D_MODEL), xmap), any_spec, any_spec, any_spec],
            out_specs=pl.BlockSpec((MOE_ROWS, ROW_TILES, LANES),
                                   lambda i, *_: (jnp.maximum(i - 1, 0), 0, 0)),
            scratch_shapes=[pltpu.VMEM((D_MODEL, D_EXPERT), F32),
                            pltpu.VMEM((D_MODEL, D_EXPERT), F32),
                            pltpu.VMEM((D_EXPERT, D_MODEL), F32),
                            pltpu.VMEM((D_MODEL, D_EXPERT), BF16),
                            pltpu.VMEM((D_MODEL, D_EXPERT), BF16),
                            pltpu.VMEM((D_EXPERT, D_MODEL), BF16),
                            pltpu.VMEM((ROW_TILES * ROW_STRIDE, LANES), F32),
                            pltpu.VMEM((ROW_TILES * ROW_STRIDE, LANES), F32),
                            pltpu.SemaphoreType.DMA((3,))]),
        out_shape=jax.ShapeDtypeStruct((n_blocks * MOE_ROWS, ROW_TILES, LANES), BF16),
        compiler_params=_cparams(("arbitrary",), 48),
        name="experts",
    )(block_expert, n_used, first, next_expert, xs, w_gate, w_up, w_down)


def _combine_kernel(src_ref, ntot_ref, lidx_ref, gate_ref, ys_hbm, out_ref,
                    stage, tile, sem, *, stage_rows):
    i = pl.program_id(0)
    n_tiles = pl.num_programs(0)

    def chunk_copy(src, slot, dst):
        return pltpu.make_async_copy(ys_hbm.at[pl.ds(src, COMB_CH)],
                                     stage.at[pl.ds(slot * stage_rows + dst, COMB_CH)],
                                     sem.at[slot])

    def issue(tile_idx, slot):
        def c_body(q, _):
            chunk_copy(src_ref[tile_idx, q], slot, q * COMB_CH).start()
            return 0

        lax.fori_loop(0, ntot_ref[tile_idx], c_body, 0)

    @pl.when(i == 0)
    def _():
        issue(0, 0)

    @pl.when(i + 1 < n_tiles)
    def _():
        issue(i + 1, (i + 1) % 2)

    slot = i % 2

    def w_body(c, _):
        chunk_copy(0, slot, 0).wait()
        return 0

    lax.fori_loop(0, ntot_ref[i], w_body, 0)

    stride = COMB_TOK + SUBLANES

    for tk in range(COMB_TOK):
        acc = None
        for k in range(TOP_K):
            w = stage[lidx_ref[tk * TOP_K + k]].astype(F32)
            term = gate_ref[tk * TOP_K + k] * w
            acc = term if acc is None else acc + term
        tile[pl.ds(tk, ROW_TILES, stride=stride), :] = acc
    for j in range(ROW_TILES):
        out_ref[:, j * LANES:(j + 1) * LANES] = tile[j * stride:j * stride + COMB_TOK, :]


def _combine(ys, src_flat, n_total, lidx, gate, t, stage_rows):
    n_tiles = t // COMB_TOK
    stride = COMB_TOK + SUBLANES
    return pl.pallas_call(
        functools.partial(_combine_kernel, stage_rows=stage_rows),
        grid_spec=pltpu.PrefetchScalarGridSpec(
            num_scalar_prefetch=2,
            grid=(n_tiles,),
            in_specs=[pl.BlockSpec((COMB_TOK * TOP_K,), lambda i, *_: (i,), memory_space=pltpu.SMEM),
                      pl.BlockSpec((COMB_TOK * TOP_K,), lambda i, *_: (i,), memory_space=pltpu.SMEM),
                      pl.BlockSpec(memory_space=pl.ANY)],
            out_specs=pl.BlockSpec((COMB_TOK, D_MODEL), lambda i, *_: (i, 0)),
            scratch_shapes=[pltpu.VMEM((2 * stage_rows, ROW_TILES, LANES), BF16),
                            pltpu.VMEM((ROW_TILES * stride, LANES), F32),
                            pltpu.SemaphoreType.DMA((2,))]),
        out_shape=jax.ShapeDtypeStruct((t, D_MODEL), F32),
        compiler_params=_cparams(("arbitrary",), 48),
        name="combine",
    )(src_flat, n_total, lidx, gate, ys)


def _final_kernel(h_ref, x1_ref, rt_ref, g2_ref, wsg_ref, wsu_ref, wsd_ref, gf_ref, y_ref):
    h = h_ref[...]
    a = _silu(jnp.dot(h, wsg_ref[...], preferred_element_type=F32))
    b = jnp.dot(h, wsu_ref[...], preferred_element_type=F32)
    shared = jnp.dot((a * b).astype(BF16), wsd_ref[...], preferred_element_type=F32)
    xo = x1_ref[...] + _mod(g2_ref) * (shared + rt_ref[...])
    y_ref[...] = xo * lax.rsqrt(jnp.mean(xo * xo, axis=-1, keepdims=True) + RMS_EPS) * gf_ref[...]


def _final(h2, x1, routed, routed_row0, mods, rows_per_batch, w_sg, w_su, w_sd, gf, tm):
    t = x1.shape[0]
    row = lambda m: (m, 0)
    assert routed_row0 % tm == 0
    block0 = routed_row0 // tm
    return pl.pallas_call(
        _final_kernel,
        grid=(t // tm,),
        in_specs=[pl.BlockSpec((tm, D_MODEL), row), pl.BlockSpec((tm, D_MODEL), row),
                  pl.BlockSpec((tm, D_MODEL), lambda m: (m + block0, 0)),
                  _mod_spec(mods, 5, rows_per_batch, tm),
                  _const_spec((D_MODEL, D_EXPERT)), _const_spec((D_MODEL, D_EXPERT)),
                  _const_spec((D_EXPERT, D_MODEL)), _const_spec((1, D_MODEL))],
        out_specs=pl.BlockSpec((tm, D_MODEL), row),
        out_shape=jax.ShapeDtypeStruct((t, D_MODEL), F32),
        compiler_params=_cparams(("arbitrary",), 48),
        name="final",
    )(h2, x1, routed, mods, w_sg, w_su, w_sd, gf)


def _moe_plan(top_e, t):
    ar = jnp.arange(N_EXPERTS, dtype=I32)
    sel = top_e[:, :, None] == ar
    onehot = jnp.any(sel, axis=1).astype(I32)
    csum = jnp.cumsum(onehot, axis=0)
    excl = csum - onehot
    counts = csum[-1]
    padded = (counts + MOE_ROWS - 1) // MOE_ROWS * MOE_ROWS
    pend = jnp.cumsum(padded)
    pstart = pend - padded
    pick = lambda per_expert: jnp.sum(jnp.where(sel, per_expert, 0), axis=-1)
    rank = pick(excl[:, None, :])
    n_blocks = -(-(t * TOP_K + N_EXPERTS * (MOE_ROWS - 1)) // MOE_ROWS)
    block_row = jnp.arange(n_blocks, dtype=I32) * MOE_ROWS
    block_expert = jnp.minimum(jnp.sum((pend[None, :] <= block_row[:, None]).astype(I32), axis=1),
                               N_EXPERTS - 1)
    n_used = (pend[-1] // MOE_ROWS).astype(I32).reshape(1)
    first = jnp.concatenate([jnp.ones((1,), I32), (block_expert[1:] != block_expert[:-1]).astype(I32)])
    later = jnp.logical_and(counts[None, :] > 0, ar[None, :] > ar[:, None])
    next_nonempty = jnp.min(jnp.where(later, ar[None, :], N_EXPERTS), axis=1)
    next_nonempty = jnp.where(next_nonempty < N_EXPERTS, next_nonempty, -1)
    next_expert = jnp.sum(jnp.where(block_expert[:, None] == ar[None, :], next_nonempty[None, :], 0), axis=1)
    assert t + MOE_ROWS <= 1 << TOKEN_BITS
    real = ((top_e << TOKEN_BITS) | jnp.arange(t, dtype=I32)[:, None]).reshape(-1)
    r = jnp.arange(MOE_ROWS - 1, dtype=I32)
    int_max = jnp.iinfo(jnp.int32).max
    pad_keys = jnp.where(r[None, :] < (padded - counts)[:, None],
                         (ar[:, None] << TOKEN_BITS) | (t + r[None, :]), int_max).reshape(-1)
    n_rows = n_blocks * MOE_ROWS
    fill = jnp.full((n_rows - real.shape[0] - pad_keys.shape[0],), int_max, I32)
    tokv = jnp.sort(jnp.concatenate([real, pad_keys, fill])) & ((1 << TOKEN_BITS) - 1)
    tok_buf = jnp.where(tokv < t, tokv, 0)
    lo = excl[::COMB_TOK]
    hi = jnp.concatenate([lo[1:], counts[None]], axis=0)
    n_chunks = (hi - lo + COMB_CH - 1) // COMB_CH
    cend = jnp.cumsum(n_chunks, axis=1)
    cstart = cend - n_chunks
    n_total = cend[:, -1].astype(I32)
    src_start = pstart[None, :] + lo
    q = jnp.arange(COMB_MAX_CHUNKS, dtype=I32)
    eq = jnp.minimum(jnp.sum((cend[:, None, :] <= q[None, :, None]).astype(I32), axis=-1), N_EXPERTS - 1)
    sel_q = eq[:, :, None] == ar
    pick_q = lambda v: jnp.sum(jnp.where(sel_q, v[:, None, :], 0), axis=-1)
    src_flat = pick_q(src_start) + (q[None, :] - pick_q(cstart)) * COMB_CH
    src_flat = jnp.where(q[None, :] < n_total[:, None], src_flat, 0)
    stage_rows = COMB_MAX_CHUNKS * COMB_CH
    n_tiles = lo.shape[0]
    t_pad = n_tiles * COMB_TOK
    slot_base = (jnp.arange(n_tiles, dtype=I32) % 2) * stage_rows
    local = jnp.repeat(cstart * COMB_CH - lo + slot_base[:, None], COMB_TOK, axis=0)[:t]
    lidx = pick(local[:, None, :]) + rank
    lidx = jnp.concatenate([lidx, jnp.broadcast_to(slot_base[-1], (t_pad - t, TOP_K))], axis=0)
    return dict(block_expert=block_expert.astype(I32), n_used=n_used, tok_buf=tok_buf.astype(I32),
                first=first, next_expert=next_expert.astype(I32),
                n_blocks=n_blocks, src_flat=src_flat.astype(I32), n_total=n_total,
                lidx=lidx.astype(I32), stage_rows=stage_rows, t_pad=t_pad)


def kernel(x_prompt, x_sample, state_ssm_re, state_ssm_im, state_conv, c_prompt, c_sample, w_ada, b_ada, norm1_g, norm2_g, w_in, ssm_a_re, ssm_a_im, ssm_log_dt, ssm_b_re, ssm_b_im, ssm_c_re, ssm_c_im, ssm_d, w_glu, b_glu, w_lift_a, conv_w, conv_b, w_lift_b, w_out, w_router, router_bias, w_exp_gate, w_exp_up, w_exp_down, w_sh_gate, w_sh_up, w_sh_down, final_norm_g):
    depth = w_ada.shape[0]
    assert depth == 1
    bp, seq, _ = x_prompt.shape
    bs = x_sample.shape[0]
    tp = bp * seq
    l = 0

    w_in_b = w_in[l].astype(BF16)
    wr_t = w_router[l].T
    wr_hi = wr_t.astype(BF16)
    wr_lo = (wr_t - wr_hi.astype(F32)).astype(BF16)
    mix_w = (ssm_d[l].reshape(1, -1), w_glu[l].astype(BF16), b_glu[l].reshape(1, -1),
             w_lift_a[l].astype(BF16), conv_w[l], conv_b[l].reshape(1, -1), w_lift_b[l].astype(BF16),
             w_out[l].astype(BF16), norm2_g[l].reshape(1, -1), wr_hi, wr_lo)
    g1 = norm1_g[l].reshape(1, -1)

    ada = _ada(jnp.concatenate([c_prompt, c_sample], axis=0), w_ada[l], b_ada[l])
    mods_p = ada[:bp].reshape(bp, 1, -1)
    mods_s = ada[bp:]

    apow, ssm_tm, ssm_bk, ssm_ck = _ssm_prep(ssm_a_re[l], ssm_a_im[l], ssm_log_dt[l], ssm_b_re[l],
                                             ssm_b_im[l], ssm_c_re[l], ssm_c_im[l])
    a1_re, a1_im, al_re, al_im = (_state_layout(apow[i]) for i in range(4))

    xp = x_prompt.reshape(tp, D_MODEL)
    u_p, rest_p = _in_proj(xp, mods_p, seq, g1, w_in_b, tm=256)
    n_chunks = seq // SSM_CHUNK
    s_loc = _ssm_state(u_p, ssm_bk, SSM_CHUNK, n_chunks)
    s_prev, s_fin_p = _ssm_scan(s_loc.reshape(bp, n_chunks, -1), al_re, al_im,
                                jnp.zeros((bp, OCTETS * OCT_STATE), F32))
    y_p = _ssm_out(u_p, s_prev.reshape(bp * n_chunks, -1), ssm_tm, ssm_ck, SSM_CHUNK, n_chunks)
    x1_p, h2_p, lg_p, conv_p = _mixer_post(xp, y_p, u_p, rest_p, mods_p, seq, mix_w, None, tm=256)

    xs = x_sample.reshape(bs, D_MODEL)
    u_s, rest_s = _in_proj(xs, mods_s, 1, g1, w_in_b, tm=bs)
    s0_s = _pack_state(state_ssm_re[l], state_ssm_im[l])
    s_new_s = _ssm_step(_ssm_state(u_s, ssm_bk, 1, bs), s0_s, a1_re, a1_im)
    y_s = _ssm_out(u_s, s0_s, ssm_tm, ssm_ck, 1, bs)
    x1_s, h2_s, lg_s, nc0, nc1 = _mixer_post(xs, y_s, u_s, rest_s, mods_s, 1, mix_w,
                                                    state_conv[l], tm=bs)

    t = tp + bs
    te_p, gate_p = _route(lg_p, router_bias[l])
    te_s, gate_s = _route(lg_s, router_bias[l])
    top_e = jnp.concatenate([te_p, te_s], axis=1).T
    gate = jnp.concatenate([gate_p, gate_s], axis=1).T
    plan = _moe_plan(top_e, t)
    h_tiles = jnp.concatenate([h2_p, h2_s], axis=0).reshape(t, ROW_TILES, LANES)
    xs_sorted = _dispatch(h_tiles, plan["tok_buf"], plan["n_used"], plan["n_blocks"])
    ys = _experts(xs_sorted, plan["block_expert"], plan["n_used"], plan["first"], plan["next_expert"],
                  w_exp_gate[l], w_exp_up[l], w_exp_down[l], plan["n_blocks"])
    gate_pad = jnp.concatenate([gate, jnp.zeros((plan["t_pad"] - t, TOP_K), F32)], axis=0)
    routed = _combine(ys, plan["src_flat"], plan["n_total"], plan["lidx"].reshape(-1),
                      gate_pad.reshape(-1), plan["t_pad"], plan["stage_rows"])

    w_sg, w_su, w_sd = w_sh_gate[l].astype(BF16), w_sh_up[l].astype(BF16), w_sh_down[l].astype(BF16)
    gf = final_norm_g.reshape(1, -1)
    y_prompt = _final(h2_p, x1_p, routed, 0, mods_p, seq, w_sg, w_su, w_sd, gf, tm=512)
    y_sample = _final(h2_s, x1_s, routed, tp, mods_s, 1, w_sg, w_su, w_sd, gf, tm=bs)

    re_p, im_p = _unpack_state(s_fin_p)
    re_s, im_s = _unpack_state(s_new_s)
    return (y_prompt.reshape(bp, seq, D_MODEL), y_sample.reshape(bs, 1, D_MODEL),
            re_p[None], im_p[None], conv_p[None],
            re_s[None], im_s[None], jnp.stack([nc0, nc1], axis=1)[None])
```

```python
import functools

import jax
import jax.numpy as jnp
from jax import lax
from jax.experimental import pallas as pl
from jax.experimental.pallas import tpu as pltpu

F32 = jnp.float32
BF16 = jnp.bfloat16
I32 = jnp.int32

D_MODEL = 2048
D_SSM = 1024
SSM_GROUP = 16
N_GROUPS = 64
SSM_STATE = 64
D_CONV = 1024
CONV_W = 3
N_EXPERTS = 64
TOP_K = 8
N_EXPERT_GROUPS = 8
EXPERTS_PER_GROUP = 8
TOPK_GROUPS = 4
D_EXPERT = 512
ROUTED_SCALE = 2.5
RMS_EPS = 1e-6

LANES = 128
SUBLANES = 8
OCTETS = N_GROUPS * SSM_GROUP // LANES
OCT_STATE = 2 * (N_GROUPS // OCTETS) * SSM_STATE
SSM_CHUNK = 8
MOE_ROWS = 256
COMB_TOK = 128
COMB_CH = 16
ROW_TILES = D_MODEL // LANES
ROW_STRIDE = MOE_ROWS + SUBLANES
DISP_ROWS = 2 * MOE_ROWS
DISP_STRIDE = DISP_ROWS + SUBLANES
COMB_MAX_CHUNKS = COMB_TOK * TOP_K // COMB_CH + N_EXPERTS
TOKEN_BITS = 14
MIX_ROWS = 128
NEG_INF = float("-inf")


def _cparams(sem, vmem_mb):
    return pltpu.CompilerParams(dimension_semantics=sem, vmem_limit_bytes=vmem_mb * 1024 * 1024)


def _const_spec(shape):
    nd = len(shape)
    return pl.BlockSpec(shape, lambda *_: (0,) * nd, pipeline_mode=pl.Buffered(1))


def _mod(ref):
    v = ref[...]
    return v[0] if v.ndim == 3 else v


def _silu(x):
    return x * jax.nn.sigmoid(x)


def _ada_kernel(c_ref, w_ref, b_ref, o_ref):
    a = _silu(c_ref[...]).astype(BF16)
    o_ref[...] = jnp.dot(a, w_ref[...].astype(BF16), preferred_element_type=F32) + b_ref[...]


def _ada(c_all, w_ada, b_ada):
    m = c_all.shape[0]
    n = w_ada.shape[1]
    tn = 1024
    return pl.pallas_call(
        _ada_kernel,
        grid=(n // tn,),
        in_specs=[pl.BlockSpec((m, D_MODEL), lambda j: (0, 0)),
                  pl.BlockSpec((D_MODEL, tn), lambda j: (0, j)),
                  pl.BlockSpec((1, tn), lambda j: (0, j))],
        out_specs=pl.BlockSpec((m, tn), lambda j: (0, j)),
        out_shape=jax.ShapeDtypeStruct((m, n), F32),
        compiler_params=_cparams(("arbitrary",), 40),
        name="ada",
    )(c_all, w_ada, b_ada.reshape(1, n))


def _mod_spec(mods, col, rows_per_batch, tm):
    if mods.ndim == 3:
        tiles_per_batch = rows_per_batch // tm
        return pl.BlockSpec((1, 1, D_MODEL), lambda m, *_: (m // tiles_per_batch, 0, col))
    return pl.BlockSpec((tm, D_MODEL), lambda m, *_: (m, col))


def _inproj_kernel(x_ref, g_ref, sh_ref, sc_ref, w_ref, u_ref, r_ref):
    x = x_ref[...]
    y = x * lax.rsqrt(jnp.mean(x * x, axis=-1, keepdims=True) + RMS_EPS) * g_ref[...]
    h = (y * (1.0 + _mod(sc_ref)) + _mod(sh_ref)).astype(BF16)
    tn = D_SSM
    u_ref[...] = jnp.dot(h, w_ref[:, 0:tn], preferred_element_type=F32)
    for n in range(1, w_ref.shape[1] // tn):
        r_ref[:, (n - 1) * tn:n * tn] = jnp.dot(h, w_ref[:, n * tn:(n + 1) * tn],
                                                preferred_element_type=F32).astype(BF16)


def _in_proj(x, mods, rows_per_batch, g1, w_in, tm):
    t = x.shape[0]
    d_in = w_in.shape[1]
    n_rest = d_in - D_SSM
    return pl.pallas_call(
        _inproj_kernel,
        grid=(t // tm,),
        in_specs=[pl.BlockSpec((tm, D_MODEL), lambda m: (m, 0)),
                  _const_spec((1, D_MODEL)),
                  _mod_spec(mods, 0, rows_per_batch, tm),
                  _mod_spec(mods, 1, rows_per_batch, tm),
                  _const_spec((D_MODEL, d_in))],
        out_specs=[pl.BlockSpec((tm, D_SSM), lambda m: (m, 0)),
                   pl.BlockSpec((tm, n_rest), lambda m: (m, 0))],
        out_shape=[jax.ShapeDtypeStruct((t, D_SSM), F32),
                   jax.ShapeDtypeStruct((t, n_rest), BF16)],
        compiler_params=_cparams(("arbitrary",), 56),
        name="in_proj",
    )(x, g1, mods, mods, w_in)


def _ssm_prep_kernel(are_ref, aim_ref, ldt_ref, btr_ref, bti_ref, cre_ref, cim_ref,
                     apow_ref, tm_ref, bk_ref, ck_ref):
    lc = SSM_CHUNK
    ng = N_GROUPS // OCTETS
    half = OCT_STATE // 2
    lr = are_ref[...]
    li = aim_ref[...]
    dt = jnp.exp(ldt_ref[...])
    mag = jnp.exp(lr * dt)
    ar = mag * jnp.cos(li * dt)
    ai = mag * jnp.sin(li * dt)
    den = lr * lr + li * li
    xr = ar - 1.0
    cr = (xr * lr + ai * li) / den
    ci = (ai * lr - xr * li) / den
    btr = btr_ref[...]
    bti = bti_ref[...]
    bbr = cr * btr - ci * bti
    bbi = cr * bti + ci * btr
    cre = cre_ref[...]
    cim = cim_ref[...]
    cre_w = jnp.concatenate([cre] * ng, axis=1)
    cim_w = jnp.concatenate([cim] * ng, axis=1)

    def iota(shape, dim):
        return lax.broadcasted_iota(I32, shape, dim)

    width = lc * LANES
    sel_b = ((iota((SSM_STATE, half), 1) & (SSM_STATE - 1)) == iota((SSM_STATE, half), 0)).astype(BF16)
    mask_b = ((iota((width, half), 0) & (LANES - 1)) >> 4) == (iota((width, half), 1) >> 6)
    sel_c = ((iota((half, SSM_STATE), 0) & (SSM_STATE - 1)) == iota((half, SSM_STATE), 1)).astype(BF16)
    mask_c = (iota((half, width), 0) >> 6) == ((iota((half, width), 1) & (LANES - 1)) >> 4)
    mask_t = (iota((LANES, LANES), 0) >> 4) == (iota((LANES, LANES), 1) >> 4)
    dn_batch = (((2,), (2,)), ((0,), (0,)))
    dn_nt = (((1,), (1,)), ((), ()))

    pr = jnp.ones_like(ar)
    pi = jnp.zeros_like(ar)
    w1r, w1i, w2r, w2i = [], [], [], []
    for l in range(lc):
        w1r.append(pr * bbr - pi * bbi)
        w1i.append(pr * bbi + pi * bbr)
        pr, pi = pr * ar - pi * ai, pr * ai + pi * ar
        w2r.append(cre * pr - cim * pi)
        w2i.append(-(cre * pi + cim * pr))
        if l == 0:
            apow_ref[0] = pr
            apow_ref[1] = pi
    apow_ref[2] = pr
    apow_ref[3] = pi

    def rows(ws, order):
        return jnp.concatenate([ws[l] for l in order], axis=0).reshape(width, SSM_STATE).astype(BF16)

    rev = range(lc - 1, -1, -1)
    for c, ws in ((0, w1r), (1, w1i)):
        rep = jnp.dot(rows(ws, rev), sel_b, preferred_element_type=F32)
        bk_ref[:, c * half:(c + 1) * half] = jnp.where(mask_b, rep, 0.0).astype(BF16)
    for c, ws in ((0, w2r), (1, w2i)):
        rep = lax.dot_general(sel_c, rows(ws, range(lc)), dn_nt, preferred_element_type=F32)
        ck_ref[c * half:(c + 1) * half, :] = jnp.where(mask_c, rep, 0.0).astype(BF16)
    w1r_all = jnp.concatenate(w1r, axis=1)
    w1i_all = jnp.concatenate(w1i, axis=1)
    kt = (lax.dot_general(w1r_all, cre_w, dn_batch, precision=lax.Precision.HIGHEST,
                          preferred_element_type=F32)
          - lax.dot_general(w1i_all, cim_w, dn_batch, precision=lax.Precision.HIGHEST,
                            preferred_element_type=F32))
    hh = SSM_GROUP
    t_blocks = [jnp.where(mask_t, kt[:, l * hh:(l + 1) * hh, :].reshape(LANES, LANES), 0.0).astype(BF16)
                for l in range(lc)]
    zero = jnp.zeros((LANES, LANES), BF16)
    for s in range(lc):
        for t in range(lc):
            tm_ref[s * LANES:(s + 1) * LANES, t * LANES:(t + 1) * LANES] = t_blocks[t - s] if t >= s else zero


def _ssm_prep(a_re, a_im, log_dt, b_re, b_im, c_re, c_im):
    g, p, h = N_GROUPS, SSM_STATE, SSM_GROUP
    ng = g // OCTETS
    width = SSM_CHUNK * LANES
    vec = pl.BlockSpec((ng, 1, p), lambda o: (o, 0, 0))
    mat = pl.BlockSpec((ng, h, p), lambda o: (o, 0, 0))
    big = lambda rows, cols: pl.BlockSpec((None, rows, cols), lambda o: (o, 0, 0))
    return pl.pallas_call(
        _ssm_prep_kernel,
        grid=(OCTETS,),
        in_specs=[vec, vec, pl.BlockSpec((ng, 1, 1), lambda o: (o, 0, 0)), mat, mat, mat, mat],
        out_specs=[pl.BlockSpec((4, ng, 1, p), lambda o: (0, o, 0, 0)),
                   big(width, width), big(width, OCT_STATE), big(OCT_STATE, width)],
        out_shape=[jax.ShapeDtypeStruct((4, g, 1, p), F32),
                   jax.ShapeDtypeStruct((OCTETS, width, width), BF16),
                   jax.ShapeDtypeStruct((OCTETS, width, OCT_STATE), BF16),
                   jax.ShapeDtypeStruct((OCTETS, OCT_STATE, width), BF16)],
        compiler_params=_cparams(("arbitrary",), 32),
        name="ssm_prep",
    )(a_re.reshape(g, 1, p), a_im.reshape(g, 1, p), log_dt.reshape(g, 1, 1),
      b_re.transpose(0, 2, 1), b_im.transpose(0, 2, 1), c_re, c_im)


def _state_layout(v):
    return v.reshape(OCTETS, 1, OCT_STATE // 2)


def _load_chunks(u_ref, lc, rb):
    parts = [u_ref[pl.ds(s, rb, stride=lc), :] if lc > 1 else u_ref[...] for s in range(lc)]
    u = parts[0] if lc == 1 else jnp.concatenate(parts, axis=-1)
    return u.astype(BF16)


def _ssm_state_kernel(u_ref, bk_ref, s_ref, *, lc, rb):
    s_ref[...] = jnp.dot(_load_chunks(u_ref, lc, rb), bk_ref[...], preferred_element_type=F32)


def _ssm_state(u, bk, lc, rb):
    t = u.shape[0]
    nb = t // (rb * lc)
    lag0_block = SSM_CHUNK // lc - 1
    return pl.pallas_call(
        functools.partial(_ssm_state_kernel, lc=lc, rb=rb),
        grid=(OCTETS, nb),
        in_specs=[pl.BlockSpec((rb * lc, LANES), lambda o, b: (b, o)),
                  pl.BlockSpec((None, lc * LANES, OCT_STATE), lambda o, b: (o, lag0_block, 0))],
        out_specs=pl.BlockSpec((rb, OCT_STATE), lambda o, b: (b, o)),
        out_shape=jax.ShapeDtypeStruct((t // lc, OCTETS * OCT_STATE), F32),
        compiler_params=_cparams(("arbitrary", "arbitrary"), 32),
        name="ssm_state",
    )(u, bk)


def _ssm_scan_kernel(sl_ref, ar_ref, ai_ref, s0_ref, sp_ref, sf_ref, *, nb, nc):
    half = OCT_STATE // 2
    ar = ar_ref[0]
    ai = ai_ref[0]
    init = tuple((s0_ref[b:b + 1, 0:half], s0_ref[b:b + 1, half:OCT_STATE]) for b in range(nb))

    def body(c, carry):
        new = []
        for b in range(nb):
            sr, si = carry[b]
            sp_ref[b, pl.ds(c, 1), 0:half] = sr
            sp_ref[b, pl.ds(c, 1), half:OCT_STATE] = si
            xr = sl_ref[b, pl.ds(c, 1), 0:half]
            xi = sl_ref[b, pl.ds(c, 1), half:OCT_STATE]
            new.append((ar * sr - ai * si + xr, ar * si + ai * sr + xi))
        return tuple(new)

    fin = lax.fori_loop(0, nc, body, init)
    for b in range(nb):
        sf_ref[b:b + 1, 0:half] = fin[b][0]
        sf_ref[b:b + 1, half:OCT_STATE] = fin[b][1]


def _ssm_scan(s_loc, al_re, al_im, s0):
    nb, nc, width = s_loc.shape
    return pl.pallas_call(
        functools.partial(_ssm_scan_kernel, nb=nb, nc=nc),
        grid=(OCTETS,),
        in_specs=[pl.BlockSpec((nb, nc, OCT_STATE), lambda o: (0, 0, o)),
                  pl.BlockSpec((1, 1, OCT_STATE // 2), lambda o: (o, 0, 0)),
                  pl.BlockSpec((1, 1, OCT_STATE // 2), lambda o: (o, 0, 0)),
                  pl.BlockSpec((nb, OCT_STATE), lambda o: (0, o))],
        out_specs=[pl.BlockSpec((nb, nc, OCT_STATE), lambda o: (0, 0, o)),
                   pl.BlockSpec((nb, OCT_STATE), lambda o: (0, o))],
        out_shape=[jax.ShapeDtypeStruct((nb, nc, width), F32),
                   jax.ShapeDtypeStruct((nb, width), F32)],
        compiler_params=_cparams(("arbitrary",), 32),
        name="ssm_scan",
    )(s_loc, al_re, al_im, s0)


def _ssm_out_kernel(u_ref, sp_ref, tm_ref, ck_ref, y_ref, *, lc, rb):
    y = (jnp.dot(_load_chunks(u_ref, lc, rb), tm_ref[...], preferred_element_type=F32)
         + jnp.dot(sp_ref[...].astype(BF16), ck_ref[...], preferred_element_type=F32))
    if lc == 1:
        y_ref[...] = y
    else:
        for t in range(lc):
            y_ref[pl.ds(t, rb, stride=lc), :] = y[:, t * LANES:(t + 1) * LANES]


def _ssm_out(u, s_prev, tm, ck, lc, rb):
    t = u.shape[0]
    nb = t // (rb * lc)
    return pl.pallas_call(
        functools.partial(_ssm_out_kernel, lc=lc, rb=rb),
        grid=(OCTETS, nb),
        in_specs=[pl.BlockSpec((rb * lc, LANES), lambda o, b: (b, o)),
                  pl.BlockSpec((rb, OCT_STATE), lambda o, b: (b, o)),
                  pl.BlockSpec((None, lc * LANES, lc * LANES), lambda o, b: (o, 0, 0)),
                  pl.BlockSpec((None, OCT_STATE, lc * LANES), lambda o, b: (o, 0, 0))],
        out_specs=pl.BlockSpec((rb * lc, LANES), lambda o, b: (b, o)),
        out_shape=jax.ShapeDtypeStruct((t, D_SSM), F32),
        compiler_params=_cparams(("arbitrary", "arbitrary"), 32),
        name="ssm_out",
    )(u, s_prev, tm, ck)


def _ssm_step_kernel(sl_ref, s0_ref, ar_ref, ai_ref, sn_ref):
    half = OCT_STATE // 2
    ar = ar_ref[0]
    ai = ai_ref[0]
    sr = s0_ref[:, 0:half]
    si = s0_ref[:, half:OCT_STATE]
    sn_ref[:, 0:half] = ar * sr - ai * si + sl_ref[:, 0:half]
    sn_ref[:, half:OCT_STATE] = ar * si + ai * sr + sl_ref[:, half:OCT_STATE]


def _ssm_step(s_loc, s0, a_re, a_im):
    rows, width = s_loc.shape
    return pl.pallas_call(
        _ssm_step_kernel,
        grid=(OCTETS,),
        in_specs=[pl.BlockSpec((rows, OCT_STATE), lambda o: (0, o)),
                  pl.BlockSpec((rows, OCT_STATE), lambda o: (0, o)),
                  pl.BlockSpec((1, 1, OCT_STATE // 2), lambda o: (o, 0, 0)),
                  pl.BlockSpec((1, 1, OCT_STATE // 2), lambda o: (o, 0, 0))],
        out_specs=pl.BlockSpec((rows, OCT_STATE), lambda o: (0, o)),
        out_shape=jax.ShapeDtypeStruct((rows, width), F32),
        compiler_params=_cparams(("arbitrary",), 32),
        name="ssm_step",
    )(s_loc, s0, a_re, a_im)


def _pack_state(s_re, s_im):
    b = s_re.shape[0]
    half = OCT_STATE // 2
    return jnp.stack([s_re.reshape(b, OCTETS, half), s_im.reshape(b, OCTETS, half)], axis=2).reshape(b, -1)


def _unpack_state(s):
    b = s.shape[0]
    s4 = s.reshape(b, OCTETS, 2, OCT_STATE // 2)
    return (s4[:, :, 0].reshape(b, N_GROUPS, SSM_STATE), s4[:, :, 1].reshape(b, N_GROUPS, SSM_STATE))


def _mixer_kernel(*refs, seq, tm, tiles_per_batch):
    (x_ref, ys_ref, u_ref, r_ref, g1_ref, sh2_ref, sc2_ref, d_ref, wglu_ref, bglu_ref, wla_ref,
     cw_ref, cb_ref, wlb_ref, wout_ref, n2g_ref, wrh_ref, wrl_ref) = refs[:18]
    if seq:
        x1_ref, h2_ref, lg_ref, nc_ref, carry = refs[18:]
    else:
        cb0_ref, cb1_ref, x1_ref, h2_ref, lg_ref, nc0_ref, nc1_ref = refs[18:]

    xb = r_ref[:, 0:D_CONV].astype(F32)
    gb = r_ref[:, D_CONV:2 * D_CONV].astype(F32)
    gc = r_ref[:, 2 * D_CONV:3 * D_CONV].astype(F32)
    v = gc * xb
    if seq:
        first = pl.program_id(0) % tiles_per_batch == 0

        @pl.when(first)
        def _():
            carry[...] = jnp.zeros_like(carry)

        c0 = carry[0:1, :]
        c1 = carry[1:2, :]
        row = lax.broadcasted_iota(I32, (tm, 1), 0)
        v1 = jnp.where(row == 0, c1, pltpu.roll(v, 1, 0))
        v2 = jnp.where(row == 0, c0, jnp.where(row == 1, c1, pltpu.roll(v, 2, 0)))
        tail = v[tm - 2:tm, :]
        carry[...] = tail
        nc_ref[0] = tail
    else:
        v2 = cb0_ref[...]
        v1 = cb1_ref[...]
        nc0_ref[...] = v1
        nc1_ref[...] = v
    conv = cb_ref[...] + cw_ref[0:1, :] * v2 + cw_ref[1:2, :] * v1 + cw_ref[2:3, :] * v
    gated_conv = (gb * conv).astype(BF16)

    def mod_rows(ref, rs):
        return ref[0] if len(ref.shape) == 3 else ref[rs, :]

    n_split = max(tm // MIX_ROWS, 1)
    rows = tm // n_split
    off = 3 * D_CONV
    dn = (((1,), (1,)), ((), ()))
    for part in range(n_split):
        rs = slice(part * rows, (part + 1) * rows)
        ya = jax.nn.gelu(ys_ref[rs, :] + d_ref[...] * u_ref[rs, :], approximate=True)
        z = jnp.dot(ya.astype(BF16), wglu_ref[...], preferred_element_type=F32) + bglu_ref[...]
        ya = ya * jax.nn.sigmoid(z)
        branch_a = jnp.dot(ya.astype(BF16), wla_ref[...], preferred_element_type=F32)
        branch_b = jnp.dot(gated_conv[rs], wlb_ref[...], preferred_element_type=F32)
        gla = r_ref[rs, off:off + D_MODEL].astype(F32)
        glb = r_ref[rs, off + D_MODEL:off + 2 * D_MODEL].astype(F32)
        merged = jax.nn.sigmoid(gla) * branch_a + jax.nn.sigmoid(glb) * branch_b
        x1 = x_ref[rs, :] + mod_rows(g1_ref, rs) * jnp.dot(merged.astype(BF16), wout_ref[...],
                                                           preferred_element_type=F32)
        x1_ref[rs, :] = x1

        y = x1 * lax.rsqrt(jnp.mean(x1 * x1, axis=-1, keepdims=True) + RMS_EPS) * n2g_ref[...]
        h2 = y * (1.0 + mod_rows(sc2_ref, rs)) + mod_rows(sh2_ref, rs)
        h2b = h2.astype(BF16)
        h2_ref[rs, :] = h2b

        h2l = (h2 - h2b.astype(F32)).astype(BF16)
        lg_ref[:, rs] = (lax.dot_general(wrh_ref[...], h2b, dn, preferred_element_type=F32)
                         + lax.dot_general(wrh_ref[...], h2l, dn, preferred_element_type=F32)
                         + lax.dot_general(wrl_ref[...], h2b, dn, preferred_element_type=F32))


def _mixer_post(x, y_ssm, u, rest, mods, rows_per_batch, weights, conv_state, tm):
    t = x.shape[0]
    seq = conv_state is None
    tiles_per_batch = rows_per_batch // tm
    (ssm_d, w_glu, b_glu, w_la, conv_w, conv_b, w_lb, w_out, n2g, wr_hi, wr_lo) = weights
    row = lambda m: (m, 0)
    in_specs = [pl.BlockSpec((tm, D_MODEL), row),
                pl.BlockSpec((tm, D_SSM), row),
                pl.BlockSpec((tm, D_SSM), row),
                pl.BlockSpec((tm, rest.shape[1]), row),
                _mod_spec(mods, 2, rows_per_batch, tm),
                _mod_spec(mods, 3, rows_per_batch, tm),
                _mod_spec(mods, 4, rows_per_batch, tm),
                _const_spec((1, D_SSM)), _const_spec((D_SSM, D_SSM)), _const_spec((1, D_SSM)),
                _const_spec((D_SSM, D_MODEL)), _const_spec((CONV_W, D_CONV)), _const_spec((1, D_CONV)),
                _const_spec((D_CONV, D_MODEL)), _const_spec((D_MODEL, D_MODEL)),
                _const_spec((1, D_MODEL)), _const_spec((N_EXPERTS, D_MODEL)),
                _const_spec((N_EXPERTS, D_MODEL))]
    args = [x, y_ssm, u, rest, mods, mods, mods, ssm_d, w_glu, b_glu, w_la, conv_w, conv_b, w_lb,
            w_out, n2g, wr_hi, wr_lo]
    out_specs = [pl.BlockSpec((tm, D_MODEL), row),
                 pl.BlockSpec((tm, D_MODEL), row),
                 pl.BlockSpec((N_EXPERTS, tm), lambda m: (0, m))]
    out_shape = [jax.ShapeDtypeStruct((t, D_MODEL), F32),
                 jax.ShapeDtypeStruct((t, D_MODEL), BF16),
                 jax.ShapeDtypeStruct((N_EXPERTS, t), F32)]
    scratch = []
    if seq:
        nb = t // rows_per_batch
        out_specs.append(pl.BlockSpec((1, CONV_W - 1, D_CONV), lambda m: (m // tiles_per_batch, 0, 0)))
        out_shape.append(jax.ShapeDtypeStruct((nb, CONV_W - 1, D_CONV), F32))
        scratch.append(pltpu.VMEM((CONV_W - 1, D_CONV), F32))
    else:
        in_specs += [pl.BlockSpec((tm, D_CONV), row), pl.BlockSpec((tm, D_CONV), row)]
        args += [conv_state[:, 0, :], conv_state[:, 1, :]]
        out_specs += [pl.BlockSpec((tm, D_CONV), row), pl.BlockSpec((tm, D_CONV), row)]
        out_shape += [jax.ShapeDtypeStruct((t, D_CONV), F32), jax.ShapeDtypeStruct((t, D_CONV), F32)]
    return pl.pallas_call(
        functools.partial(_mixer_kernel, seq=seq, tm=tm, tiles_per_batch=tiles_per_batch),
        grid=(t // tm,),
        in_specs=in_specs,
        out_specs=out_specs,
        out_shape=out_shape,
        scratch_shapes=scratch,
        compiler_params=_cparams(("arbitrary",), 56),
        name="mixer_post",
    )(*args)


def _route_kernel(lg_ref, bias_ref, te_ref, gate_ref):
    scores = [jax.nn.sigmoid(lg_ref[e]) for e in range(N_EXPERTS)]
    choice = [scores[e] + bias_ref[e] for e in range(N_EXPERTS)]
    shape = scores[0].shape

    def first_max(vals):
        m = functools.reduce(jnp.maximum, vals)
        found = jnp.zeros(shape, jnp.bool_)
        hot = []
        for v in vals:
            is_m = jnp.logical_and(v == m, jnp.logical_not(found))
            hot.append(is_m)
            found = jnp.logical_or(found, is_m)
        return m, hot

    gscore = []
    for g in range(N_EXPERT_GROUPS):
        vals = choice[g * EXPERTS_PER_GROUP:(g + 1) * EXPERTS_PER_GROUP]
        m1, hot = first_max(vals)
        rest = [jnp.where(h, NEG_INF, v) for h, v in zip(hot, vals)]
        gscore.append(m1 + functools.reduce(jnp.maximum, rest))

    gsel = [jnp.zeros(shape, jnp.bool_) for _ in range(N_EXPERT_GROUPS)]
    for _ in range(TOPK_GROUPS):
        _, hot = first_max(gscore)
        gsel = [jnp.logical_or(s, h) for s, h in zip(gsel, hot)]
        gscore = [jnp.where(h, NEG_INF, v) for h, v in zip(hot, gscore)]

    cand = [jnp.where(gsel[e // EXPERTS_PER_GROUP], choice[e], NEG_INF) for e in range(N_EXPERTS)]
    picked_e = []
    picked_s = []
    for _ in range(TOP_K):
        _, hot = first_max(cand)
        idx = jnp.zeros(shape, I32)
        sc = jnp.zeros(shape, F32)
        for e in range(N_EXPERTS):
            idx = jnp.where(hot[e], e, idx)
            sc = jnp.where(hot[e], scores[e], sc)
        cand = [jnp.where(h, NEG_INF, v) for h, v in zip(hot, cand)]
        picked_e.append(idx)
        picked_s.append(sc)
    total = functools.reduce(lambda a, b: a + b, picked_s)
    for k in range(TOP_K):
        te_ref[k] = picked_e[k]
        gate_ref[k] = picked_s[k] / total * ROUTED_SCALE


def _route(logits_t, bias):
    t = logits_t.shape[1]
    r = t // LANES
    rb = min(r, SUBLANES)
    lg = logits_t.reshape(N_EXPERTS, r, LANES)
    bias_b = jnp.broadcast_to(bias.astype(F32).reshape(N_EXPERTS, 1, 1), (N_EXPERTS, 1, LANES))
    te, gate = pl.pallas_call(
        _route_kernel,
        grid=(r // rb,),
        in_specs=[pl.BlockSpec((N_EXPERTS, rb, LANES), lambda i: (0, i, 0)),
                  pl.BlockSpec((N_EXPERTS, 1, LANES), lambda i: (0, 0, 0))],
        out_specs=[pl.BlockSpec((TOP_K, rb, LANES), lambda i: (0, i, 0)),
                   pl.BlockSpec((TOP_K, rb, LANES), lambda i: (0, i, 0))],
        out_shape=[jax.ShapeDtypeStruct((TOP_K, r, LANES), I32),
                   jax.ShapeDtypeStruct((TOP_K, r, LANES), F32)],
        compiler_params=_cparams(("arbitrary",), 32),
        name="route",
    )(lg, bias_b)
    return te.reshape(TOP_K, t), gate.reshape(TOP_K, t)


def _dispatch_kernel(nused_ref, tok_ref, h_ref, xs_ref, tile):
    used_rows = nused_ref[0] * MOE_ROWS
    live = pl.program_id(0) * DISP_ROWS < used_rows

    @pl.when(jnp.logical_not(live))
    def _():
        xs_ref[...] = jnp.zeros_like(xs_ref)

    @pl.when(live)
    def _():
        for mi in range(DISP_ROWS):
            tile[pl.ds(mi, ROW_TILES, stride=DISP_STRIDE), :] = h_ref[tok_ref[mi]].astype(F32)
        for j in range(ROW_TILES):
            xs_ref[:, j * LANES:(j + 1) * LANES] = (
                tile[j * DISP_STRIDE:j * DISP_STRIDE + DISP_ROWS, :].astype(BF16))


def _dispatch(h_tiles, tok_buf, n_used, n_blocks):
    n_rows = n_blocks * MOE_ROWS
    assert n_rows % DISP_ROWS == 0

    def last_live(i, nu):
        return jnp.minimum(i, (nu[0] * MOE_ROWS - 1) // DISP_ROWS)

    return pl.pallas_call(
        _dispatch_kernel,
        grid_spec=pltpu.PrefetchScalarGridSpec(
            num_scalar_prefetch=1,
            grid=(n_rows // DISP_ROWS,),
            in_specs=[pl.BlockSpec((DISP_ROWS,), lambda i, nu: (last_live(i, nu),),
                                   memory_space=pltpu.SMEM),
                      pl.BlockSpec(memory_space=pltpu.VMEM)],
            out_specs=pl.BlockSpec((DISP_ROWS, D_MODEL), lambda i, nu: (i, 0)),
            scratch_shapes=[pltpu.VMEM((ROW_TILES * DISP_STRIDE, LANES), F32)]),
        out_shape=jax.ShapeDtypeStruct((n_rows, D_MODEL), BF16),
        compiler_params=_cparams(("arbitrary",), 56),
        name="dispatch",
    )(n_used, tok_buf, h_tiles)


def _experts_kernel(be_ref, nused_ref, last_ref, n1_ref, n2_ref, slot_ref, xs_ref, wg_hbm, wu_hbm, wd_hbm,
                    ys_ref, wg_f, wu_f, wd_f, wg_s, wu_s, wd_s, tile_a, tile_b, sem):
    i = pl.program_id(0)
    n_used = nused_ref[0]

    def weight_copies(e):
        return (pltpu.make_async_copy(wg_hbm.at[e], wg_f, sem.at[0]),
                pltpu.make_async_copy(wu_hbm.at[e], wu_f, sem.at[1]),
                pltpu.make_async_copy(wd_hbm.at[e], wd_f, sem.at[2]))

    def cast_weights(s):
        wg_s[s] = wg_f[...].astype(BF16)
        wu_s[s] = wu_f[...].astype(BF16)
        wd_s[s] = wd_f[...].astype(BF16)

    @pl.when(i == 0)
    def _():
        for c in weight_copies(be_ref[0]):
            c.start()
        for c in weight_copies(be_ref[0]):
            c.wait()
        cast_weights(0)

        @pl.when(n1_ref[0] >= 0)
        def _():
            for c in weight_copies(n1_ref[0]):
                c.start()

    def ready_next_expert():
        for c in weight_copies(n1_ref[i]):
            c.wait()
        cast_weights(1 - slot_ref[i])

        @pl.when(n2_ref[i] >= 0)
        def _():
            for c in weight_copies(n2_ref[i]):
                c.start()

    def ffn_to_tile(tile):
        s = slot_ref[i]
        x = xs_ref[...]
        g = jnp.dot(x, wg_s[s], preferred_element_type=F32)
        u = jnp.dot(x, wu_s[s], preferred_element_type=F32)
        h = (_silu(g) * u).astype(BF16)
        y = jnp.dot(h, wd_s[s], preferred_element_type=F32)
        for j in range(ROW_TILES):
            tile[j * ROW_STRIDE:j * ROW_STRIDE + MOE_ROWS, :] = y[:, j * LANES:(j + 1) * LANES]

    def tile_to_rows(tile):
        for mi in range(MOE_ROWS):
            ys_ref[mi] = tile[pl.ds(mi, ROW_TILES, stride=ROW_STRIDE), :].astype(BF16)

    even = i % 2 == 0
    odd = jnp.logical_not(even)
    last = last_ref[jnp.minimum(i, pl.num_programs(0) - 2)] == 1
    mid = jnp.logical_not(last)
    both = jnp.logical_and(i >= 1, i < n_used)

    def region(cond, work):
        pl.when(cond)(work)

    def step(ffn_tile, rows_tile, with_cast):
        def work():
            ffn_to_tile(ffn_tile)
            if rows_tile is not None:
                tile_to_rows(rows_tile)
            if with_cast:
                ready_next_expert()
        return work

    region(jnp.logical_and(jnp.logical_and(both, even), mid), step(tile_a, tile_b, False))
    region(jnp.logical_and(jnp.logical_and(both, even), last), step(tile_a, tile_b, True))
    region(jnp.logical_and(jnp.logical_and(both, odd), mid), step(tile_b, tile_a, False))
    region(jnp.logical_and(jnp.logical_and(both, odd), last), step(tile_b, tile_a, True))
    region(jnp.logical_and(i == 0, mid), step(tile_a, None, False))
    region(jnp.logical_and(i == 0, last), step(tile_a, None, True))
    region(jnp.logical_and(i == n_used, even), lambda: tile_to_rows(tile_b))
    region(jnp.logical_and(i == n_used, odd), lambda: tile_to_rows(tile_a))

    @pl.when(i > n_used)
    def _():
        ys_ref[...] = jnp.zeros_like(ys_ref)


def _experts(xs, block_expert, n_used, last, next1, next2, slot, w_gate, w_up, w_down, n_blocks):
    def xmap(i, be, nu, *_):
        return (jnp.minimum(i, nu[0] - 1), 0)

    any_spec = pl.BlockSpec(memory_space=pl.ANY)
    tile = pltpu.VMEM((ROW_TILES * ROW_STRIDE, LANES), F32)
    return pl.pallas_call(
        _experts_kernel,
        grid_spec=pltpu.PrefetchScalarGridSpec(
            num_scalar_prefetch=6,
            grid=(n_blocks + 1,),
            in_specs=[pl.BlockSpec((MOE_ROWS, D_MODEL), xmap), any_spec, any_spec, any_spec],
            out_specs=pl.BlockSpec((MOE_ROWS, ROW_TILES, LANES),
                                   lambda i, *_: (jnp.maximum(i - 1, 0), 0, 0)),
            scratch_shapes=[pltpu.VMEM((D_MODEL, D_EXPERT), F32),
                            pltpu.VMEM((D_MODEL, D_EXPERT), F32),
                            pltpu.VMEM((D_EXPERT, D_MODEL), F32),
                            pltpu.VMEM((2, D_MODEL, D_EXPERT), BF16),
                            pltpu.VMEM((2, D_MODEL, D_EXPERT), BF16),
                            pltpu.VMEM((2, D_EXPERT, D_MODEL), BF16),
                            tile, tile,
                            pltpu.SemaphoreType.DMA((3,))]),
        out_shape=jax.ShapeDtypeStruct((n_blocks * MOE_ROWS, ROW_TILES, LANES), BF16),
        compiler_params=_cparams(("arbitrary",), 56),
        name="experts",
    )(block_expert, n_used, last, next1, next2, slot, xs, w_gate, w_up, w_down)


def _combine_kernel(src_ref, ntot_ref, lidx_ref, gate_ref, ys_hbm, out_ref,
                    stage, tile, sem, *, stage_rows):
    i = pl.program_id(0)
    n_tiles = pl.num_programs(0)

    def chunk_copy(src, slot, dst):
        return pltpu.make_async_copy(ys_hbm.at[pl.ds(src, COMB_CH)],
                                     stage.at[pl.ds(slot * stage_rows + dst, COMB_CH)],
                                     sem.at[slot])

    def issue(tile_idx, slot):
        def c_body(q, _):
            chunk_copy(src_ref[tile_idx, q], slot, q * COMB_CH).start()
            return 0

        lax.fori_loop(0, ntot_ref[tile_idx], c_body, 0)

    @pl.when(i == 0)
    def _():
        issue(0, 0)

    @pl.when(i + 1 < n_tiles)
    def _():
        issue(i + 1, (i + 1) % 2)

    slot = i % 2

    def w_body(c, _):
        chunk_copy(0, slot, 0).wait()
        return 0

    lax.fori_loop(0, ntot_ref[i], w_body, 0)

    stride = COMB_TOK + SUBLANES

    for tk in range(COMB_TOK):
        acc = None
        for k in range(TOP_K):
            w = stage[lidx_ref[tk * TOP_K + k]].astype(F32)
            term = gate_ref[tk * TOP_K + k] * w
            acc = term if acc is None else acc + term
        tile[pl.ds(tk, ROW_TILES, stride=stride), :] = acc
    for j in range(ROW_TILES):
        out_ref[:, j * LANES:(j + 1) * LANES] = tile[j * stride:j * stride + COMB_TOK, :]


def _combine(ys, src_flat, n_total, lidx, gate, t, stage_rows):
    n_tiles = t // COMB_TOK
    stride = COMB_TOK + SUBLANES
    return pl.pallas_call(
        functools.partial(_combine_kernel, stage_rows=stage_rows),
        grid_spec=pltpu.PrefetchScalarGridSpec(
            num_scalar_prefetch=2,
            grid=(n_tiles,),
            in_specs=[pl.BlockSpec((COMB_TOK * TOP_K,), lambda i, *_: (i,), memory_space=pltpu.SMEM),
                      pl.BlockSpec((COMB_TOK * TOP_K,), lambda i, *_: (i,), memory_space=pltpu.SMEM),
                      pl.BlockSpec(memory_space=pl.ANY)],
            out_specs=pl.BlockSpec((COMB_TOK, D_MODEL), lambda i, *_: (i, 0)),
            scratch_shapes=[pltpu.VMEM((2 * stage_rows, ROW_TILES, LANES), BF16),
                            pltpu.VMEM((ROW_TILES * stride, LANES), F32),
                            pltpu.SemaphoreType.DMA((2,))]),
        out_shape=jax.ShapeDtypeStruct((t, D_MODEL), F32),
        compiler_params=_cparams(("arbitrary",), 48),
        name="combine",
    )(src_flat, n_total, lidx, gate, ys)


def _final_kernel(h_ref, x1_ref, rt_ref, g2_ref, wsg_ref, wsu_ref, wsd_ref, gf_ref, y_ref):
    h = h_ref[...]
    a = _silu(jnp.dot(h, wsg_ref[...], preferred_element_type=F32))
    b = jnp.dot(h, wsu_ref[...], preferred_element_type=F32)
    shared = jnp.dot((a * b).astype(BF16), wsd_ref[...], preferred_element_type=F32)
    xo = x1_ref[...] + _mod(g2_ref) * (shared + rt_ref[...])
    y_ref[...] = xo * lax.rsqrt(jnp.mean(xo * xo, axis=-1, keepdims=True) + RMS_EPS) * gf_ref[...]


def _final(h2, x1, routed, routed_row0, mods, rows_per_batch, w_sg, w_su, w_sd, gf, tm):
    t = x1.shape[0]
    row = lambda m: (m, 0)
    assert routed_row0 % tm == 0
    block0 = routed_row0 // tm
    return pl.pallas_call(
        _final_kernel,
        grid=(t // tm,),
        in_specs=[pl.BlockSpec((tm, D_MODEL), row), pl.BlockSpec((tm, D_MODEL), row),
                  pl.BlockSpec((tm, D_MODEL), lambda m: (m + block0, 0)),
                  _mod_spec(mods, 5, rows_per_batch, tm),
                  _const_spec((D_MODEL, D_EXPERT)), _const_spec((D_MODEL, D_EXPERT)),
                  _const_spec((D_EXPERT, D_MODEL)), _const_spec((1, D_MODEL))],
        out_specs=pl.BlockSpec((tm, D_MODEL), row),
        out_shape=jax.ShapeDtypeStruct((t, D_MODEL), F32),
        compiler_params=_cparams(("arbitrary",), 48),
        name="final",
    )(h2, x1, routed, mods, w_sg, w_su, w_sd, gf)


def _moe_plan(top_e, t):
    ar = jnp.arange(N_EXPERTS, dtype=I32)
    sel = top_e[:, :, None] == ar
    onehot = jnp.any(sel, axis=1).astype(I32)
    csum = jnp.cumsum(onehot, axis=0)
    excl = csum - onehot
    counts = csum[-1]
    padded = (counts + MOE_ROWS - 1) // MOE_ROWS * MOE_ROWS
    pend = jnp.cumsum(padded)
    pstart = pend - padded
    pick = lambda per_expert: jnp.sum(jnp.where(sel, per_expert, 0), axis=-1)
    rank = pick(excl[:, None, :])
    n_blocks = -(-(t * TOP_K + N_EXPERTS * (MOE_ROWS - 1)) // MOE_ROWS)
    block_row = jnp.arange(n_blocks, dtype=I32) * MOE_ROWS
    block_expert = jnp.minimum(jnp.sum((pend[None, :] <= block_row[:, None]).astype(I32), axis=1),
                               N_EXPERTS - 1)
    n_used = (pend[-1] // MOE_ROWS).astype(I32).reshape(1)
    first = jnp.concatenate([jnp.ones((1,), I32), (block_expert[1:] != block_expert[:-1]).astype(I32)])
    later = jnp.logical_and(counts[None, :] > 0, ar[None, :] > ar[:, None])
    next_nonempty = jnp.min(jnp.where(later, ar[None, :], N_EXPERTS), axis=1)
    next_nonempty = jnp.where(next_nonempty < N_EXPERTS, next_nonempty, -1)
    per_block = lambda v: jnp.sum(jnp.where(block_expert[:, None] == ar[None, :], v[None, :], 0), axis=1)
    next_expert = per_block(next_nonempty)
    after = jnp.sum(jnp.where(next_nonempty[:, None] == ar[None, :], next_nonempty[None, :], 0), axis=1)
    next_next = per_block(jnp.where(next_nonempty >= 0, after, -1))
    blk = jnp.arange(n_blocks, dtype=I32)
    changes = jnp.concatenate([first[1:], jnp.zeros((1,), I32)])
    last = jnp.where(blk < n_used[0] - 1, changes, 0)
    slot = (jnp.cumsum(first) - 1) % 2
    assert t + MOE_ROWS <= 1 << TOKEN_BITS
    real = ((top_e << TOKEN_BITS) | jnp.arange(t, dtype=I32)[:, None]).reshape(-1)
    r = jnp.arange(MOE_ROWS - 1, dtype=I32)
    int_max = jnp.iinfo(jnp.int32).max
    pad_keys = jnp.where(r[None, :] < (padded - counts)[:, None],
                         (ar[:, None] << TOKEN_BITS) | (t + r[None, :]), int_max).reshape(-1)
    n_rows = n_blocks * MOE_ROWS
    fill = jnp.full((n_rows - real.shape[0] - pad_keys.shape[0],), int_max, I32)
    tokv = jnp.sort(jnp.concatenate([real, pad_keys, fill])) & ((1 << TOKEN_BITS) - 1)
    tok_buf = jnp.where(tokv < t, tokv, 0)
    lo = excl[::COMB_TOK]
    hi = jnp.concatenate([lo[1:], counts[None]], axis=0)
    n_chunks = (hi - lo + COMB_CH - 1) // COMB_CH
    cend = jnp.cumsum(n_chunks, axis=1)
    cstart = cend - n_chunks
    n_total = cend[:, -1].astype(I32)
    src_start = pstart[None, :] + lo
    q = jnp.arange(COMB_MAX_CHUNKS, dtype=I32)
    eq = jnp.minimum(jnp.sum((cend[:, None, :] <= q[None, :, None]).astype(I32), axis=-1), N_EXPERTS - 1)
    sel_q = eq[:, :, None] == ar
    pick_q = lambda v: jnp.sum(jnp.where(sel_q, v[:, None, :], 0), axis=-1)
    src_flat = pick_q(src_start) + (q[None, :] - pick_q(cstart)) * COMB_CH
    src_flat = jnp.where(q[None, :] < n_total[:, None], src_flat, 0)
    stage_rows = COMB_MAX_CHUNKS * COMB_CH
    n_tiles = lo.shape[0]
    t_pad = n_tiles * COMB_TOK
    slot_base = (jnp.arange(n_tiles, dtype=I32) % 2) * stage_rows
    local = jnp.repeat(cstart * COMB_CH - lo + slot_base[:, None], COMB_TOK, axis=0)[:t]
    lidx = pick(local[:, None, :]) + rank
    lidx = jnp.concatenate([lidx, jnp.broadcast_to(slot_base[-1], (t_pad - t, TOP_K))], axis=0)
    return dict(block_expert=block_expert.astype(I32), n_used=n_used, tok_buf=tok_buf.astype(I32),
                last=last.astype(I32), next_expert=next_expert.astype(I32),
                next_next=next_next.astype(I32), slot=slot.astype(I32),
                n_blocks=n_blocks, src_flat=src_flat.astype(I32), n_total=n_total,
                lidx=lidx.astype(I32), stage_rows=stage_rows, t_pad=t_pad)


def kernel(x_prompt, x_sample, state_ssm_re, state_ssm_im, state_conv, c_prompt, c_sample, w_ada, b_ada, norm1_g, norm2_g, w_in, ssm_a_re, ssm_a_im, ssm_log_dt, ssm_b_re, ssm_b_im, ssm_c_re, ssm_c_im, ssm_d, w_glu, b_glu, w_lift_a, conv_w, conv_b, w_lift_b, w_out, w_router, router_bias, w_exp_gate, w_exp_up, w_exp_down, w_sh_gate, w_sh_up, w_sh_down, final_norm_g):
    depth = w_ada.shape[0]
    assert depth == 1
    bp, seq, _ = x_prompt.shape
    bs = x_sample.shape[0]
    tp = bp * seq
    l = 0

    w_in_b = w_in[l].astype(BF16)
    wr_t = w_router[l].T
    wr_hi = wr_t.astype(BF16)
    wr_lo = (wr_t - wr_hi.astype(F32)).astype(BF16)
    mix_w = (ssm_d[l].reshape(1, -1), w_glu[l].astype(BF16), b_glu[l].reshape(1, -1),
             w_lift_a[l].astype(BF16), conv_w[l], conv_b[l].reshape(1, -1), w_lift_b[l].astype(BF16),
             w_out[l].astype(BF16), norm2_g[l].reshape(1, -1), wr_hi, wr_lo)
    g1 = norm1_g[l].reshape(1, -1)

    ada = _ada(jnp.concatenate([c_prompt, c_sample], axis=0), w_ada[l], b_ada[l])
    mods_p = ada[:bp].reshape(bp, 1, -1)
    mods_s = ada[bp:]

    apow, ssm_tm, ssm_bk, ssm_ck = _ssm_prep(ssm_a_re[l], ssm_a_im[l], ssm_log_dt[l], ssm_b_re[l],
                                             ssm_b_im[l], ssm_c_re[l], ssm_c_im[l])
    a1_re, a1_im, al_re, al_im = (_state_layout(apow[i]) for i in range(4))

    xp = x_prompt.reshape(tp, D_MODEL)
    u_p, rest_p = _in_proj(xp, mods_p, seq, g1, w_in_b, tm=256)
    n_chunks = seq // SSM_CHUNK
    s_loc = _ssm_state(u_p, ssm_bk, SSM_CHUNK, n_chunks)
    s_prev, s_fin_p = _ssm_scan(s_loc.reshape(bp, n_chunks, -1), al_re, al_im,
                                jnp.zeros((bp, OCTETS * OCT_STATE), F32))
    y_p = _ssm_out(u_p, s_prev.reshape(bp * n_chunks, -1), ssm_tm, ssm_ck, SSM_CHUNK, n_chunks)
    x1_p, h2_p, lg_p, conv_p = _mixer_post(xp, y_p, u_p, rest_p, mods_p, seq, mix_w, None, tm=256)

    xs = x_sample.reshape(bs, D_MODEL)
    u_s, rest_s = _in_proj(xs, mods_s, 1, g1, w_in_b, tm=bs)
    s0_s = _pack_state(state_ssm_re[l], state_ssm_im[l])
    s_new_s = _ssm_step(_ssm_state(u_s, ssm_bk, 1, bs), s0_s, a1_re, a1_im)
    y_s = _ssm_out(u_s, s0_s, ssm_tm, ssm_ck, 1, bs)
    x1_s, h2_s, lg_s, nc0, nc1 = _mixer_post(xs, y_s, u_s, rest_s, mods_s, 1, mix_w,
                                                    state_conv[l], tm=bs)

    t = tp + bs
    te_p, gate_p = _route(lg_p, router_bias[l])
    te_s, gate_s = _route(lg_s, router_bias[l])
    top_e = jnp.concatenate([te_p, te_s], axis=1).T
    gate = jnp.concatenate([gate_p, gate_s], axis=1).T
    plan = _moe_plan(top_e, t)
    h_tiles = jnp.concatenate([h2_p, h2_s], axis=0).reshape(t, ROW_TILES, LANES)
    xs_sorted = _dispatch(h_tiles, plan["tok_buf"], plan["n_used"], plan["n_blocks"])
    ys = _experts(xs_sorted, plan["block_expert"], plan["n_used"], plan["last"], plan["next_expert"],
                  plan["next_next"], plan["slot"], w_exp_gate[l], w_exp_up[l], w_exp_down[l],
                  plan["n_blocks"])
    gate_pad = jnp.concatenate([gate, jnp.zeros((plan["t_pad"] - t, TOP_K), F32)], axis=0)
    routed = _combine(ys, plan["src_flat"], plan["n_total"], plan["lidx"].reshape(-1),
                      gate_pad.reshape(-1), plan["t_pad"], plan["stage_rows"])

    w_sg, w_su, w_sd = w_sh_gate[l].astype(BF16), w_sh_up[l].astype(BF16), w_sh_down[l].astype(BF16)
    gf = final_norm_g.reshape(1, -1)
    y_prompt = _final(h2_p, x1_p, routed, 0, mods_p, seq, w_sg, w_su, w_sd, gf, tm=512)
    y_sample = _final(h2_s, x1_s, routed, tp, mods_s, 1, w_sg, w_su, w_sd, gf, tm=bs)

    re_p, im_p = _unpack_state(s_fin_p)
    re_s, im_s = _unpack_state(s_new_s)
    return (y_prompt.reshape(bp, seq, D_MODEL), y_sample.reshape(bs, 1, D_MODEL),
            re_p[None], im_p[None], conv_p[None],
            re_s[None], im_s[None], jnp.stack([nc0, nc1], axis=1)[None])
```

```python
import functools

import jax
import jax.numpy as jnp
from jax import lax
from jax.experimental import pallas as pl
from jax.experimental.pallas import tpu as pltpu

F32 = jnp.float32
BF16 = jnp.bfloat16
I32 = jnp.int32

D_MODEL = 2048
D_SSM = 1024
SSM_GROUP = 16
N_GROUPS = 64
SSM_STATE = 64
D_CONV = 1024
CONV_W = 3
N_EXPERTS = 64
TOP_K = 8
N_EXPERT_GROUPS = 8
EXPERTS_PER_GROUP = 8
TOPK_GROUPS = 4
D_EXPERT = 512
ROUTED_SCALE = 2.5
RMS_EPS = 1e-6

LANES = 128
SUBLANES = 8
OCTETS = N_GROUPS * SSM_GROUP // LANES
OCT_STATE = 2 * (N_GROUPS // OCTETS) * SSM_STATE
SSM_CHUNK = 8
MOE_ROWS = 256
COMB_TOK = 128
COMB_CH = 16
ROW_TILES = D_MODEL // LANES
ROW_STRIDE = MOE_ROWS + SUBLANES
DISP_ROWS = 4 * MOE_ROWS
DISP_STRIDE = DISP_ROWS + SUBLANES
COMB_MAX_CHUNKS = COMB_TOK * TOP_K // COMB_CH + N_EXPERTS
TOKEN_BITS = 14
MIX_ROWS = 128
NEG_INF = float("-inf")


def _cparams(sem, vmem_mb):
    return pltpu.CompilerParams(dimension_semantics=sem, vmem_limit_bytes=vmem_mb * 1024 * 1024)


def _const_spec(shape):
    nd = len(shape)
    return pl.BlockSpec(shape, lambda *_: (0,) * nd, pipeline_mode=pl.Buffered(1))


def _mod(ref):
    v = ref[...]
    return v[0] if v.ndim == 3 else v


def _silu(x):
    return x * jax.nn.sigmoid(x)


def _ada_kernel(c_ref, w_ref, b_ref, o_ref):
    a = _silu(c_ref[...]).astype(BF16)
    o_ref[...] = jnp.dot(a, w_ref[...].astype(BF16), preferred_element_type=F32) + b_ref[...]


def _ada(c_all, w_ada, b_ada):
    m = c_all.shape[0]
    n = w_ada.shape[1]
    tn = 1024
    return pl.pallas_call(
        _ada_kernel,
        grid=(n // tn,),
        in_specs=[pl.BlockSpec((m, D_MODEL), lambda j: (0, 0)),
                  pl.BlockSpec((D_MODEL, tn), lambda j: (0, j)),
                  pl.BlockSpec((1, tn), lambda j: (0, j))],
        out_specs=pl.BlockSpec((m, tn), lambda j: (0, j)),
        out_shape=jax.ShapeDtypeStruct((m, n), F32),
        compiler_params=_cparams(("arbitrary",), 40),
        name="ada",
    )(c_all, w_ada, b_ada.reshape(1, n))


def _mod_spec(mods, col, rows_per_batch, tm):
    if mods.ndim == 3:
        tiles_per_batch = rows_per_batch // tm
        return pl.BlockSpec((1, 1, D_MODEL), lambda m, *_: (m // tiles_per_batch, 0, col))
    return pl.BlockSpec((tm, D_MODEL), lambda m, *_: (m, col))


def _inproj_kernel(x_ref, g_ref, sh_ref, sc_ref, w_ref, u_ref, r_ref):
    x = x_ref[...]
    y = x * lax.rsqrt(jnp.mean(x * x, axis=-1, keepdims=True) + RMS_EPS) * g_ref[...]
    h = (y * (1.0 + _mod(sc_ref)) + _mod(sh_ref)).astype(BF16)
    tn = D_SSM
    u_ref[...] = jnp.dot(h, w_ref[:, 0:tn], preferred_element_type=F32)
    for n in range(1, w_ref.shape[1] // tn):
        r_ref[:, (n - 1) * tn:n * tn] = jnp.dot(h, w_ref[:, n * tn:(n + 1) * tn],
                                                preferred_element_type=F32).astype(BF16)


def _in_proj(x, mods, rows_per_batch, g1, w_in, tm):
    t = x.shape[0]
    d_in = w_in.shape[1]
    n_rest = d_in - D_SSM
    return pl.pallas_call(
        _inproj_kernel,
        grid=(t // tm,),
        in_specs=[pl.BlockSpec((tm, D_MODEL), lambda m: (m, 0)),
                  _const_spec((1, D_MODEL)),
                  _mod_spec(mods, 0, rows_per_batch, tm),
                  _mod_spec(mods, 1, rows_per_batch, tm),
                  _const_spec((D_MODEL, d_in))],
        out_specs=[pl.BlockSpec((tm, D_SSM), lambda m: (m, 0)),
                   pl.BlockSpec((tm, n_rest), lambda m: (m, 0))],
        out_shape=[jax.ShapeDtypeStruct((t, D_SSM), F32),
                   jax.ShapeDtypeStruct((t, n_rest), BF16)],
        compiler_params=_cparams(("arbitrary",), 56),
        name="in_proj",
    )(x, g1, mods, mods, w_in)


def _ssm_prep_kernel(are_ref, aim_ref, ldt_ref, btr_ref, bti_ref, cre_ref, cim_ref,
                     apow_ref, tm_ref, bk_ref, ck_ref):
    lc = SSM_CHUNK
    ng = N_GROUPS // OCTETS
    half = OCT_STATE // 2
    lr = are_ref[...]
    li = aim_ref[...]
    dt = jnp.exp(ldt_ref[...])
    mag = jnp.exp(lr * dt)
    ar = mag * jnp.cos(li * dt)
    ai = mag * jnp.sin(li * dt)
    den = lr * lr + li * li
    xr = ar - 1.0
    cr = (xr * lr + ai * li) / den
    ci = (ai * lr - xr * li) / den
    btr = btr_ref[...]
    bti = bti_ref[...]
    bbr = cr * btr - ci * bti
    bbi = cr * bti + ci * btr
    cre = cre_ref[...]
    cim = cim_ref[...]
    cre_w = jnp.concatenate([cre] * ng, axis=1)
    cim_w = jnp.concatenate([cim] * ng, axis=1)

    def iota(shape, dim):
        return lax.broadcasted_iota(I32, shape, dim)

    width = lc * LANES
    sel_b = ((iota((SSM_STATE, half), 1) & (SSM_STATE - 1)) == iota((SSM_STATE, half), 0)).astype(BF16)
    mask_b = ((iota((width, half), 0) & (LANES - 1)) >> 4) == (iota((width, half), 1) >> 6)
    sel_c = ((iota((half, SSM_STATE), 0) & (SSM_STATE - 1)) == iota((half, SSM_STATE), 1)).astype(BF16)
    mask_c = (iota((half, width), 0) >> 6) == ((iota((half, width), 1) & (LANES - 1)) >> 4)
    mask_t = (iota((LANES, LANES), 0) >> 4) == (iota((LANES, LANES), 1) >> 4)
    dn_batch = (((2,), (2,)), ((0,), (0,)))
    dn_nt = (((1,), (1,)), ((), ()))

    pr = jnp.ones_like(ar)
    pi = jnp.zeros_like(ar)
    w1r, w1i, w2r, w2i = [], [], [], []
    for l in range(lc):
        w1r.append(pr * bbr - pi * bbi)
        w1i.append(pr * bbi + pi * bbr)
        pr, pi = pr * ar - pi * ai, pr * ai + pi * ar
        w2r.append(cre * pr - cim * pi)
        w2i.append(-(cre * pi + cim * pr))
        if l == 0:
            apow_ref[0] = pr
            apow_ref[1] = pi
    apow_ref[2] = pr
    apow_ref[3] = pi

    def rows(ws, order):
        return jnp.concatenate([ws[l] for l in order], axis=0).reshape(width, SSM_STATE).astype(BF16)

    rev = range(lc - 1, -1, -1)
    for c, ws in ((0, w1r), (1, w1i)):
        rep = jnp.dot(rows(ws, rev), sel_b, preferred_element_type=F32)
        bk_ref[:, c * half:(c + 1) * half] = jnp.where(mask_b, rep, 0.0).astype(BF16)
    for c, ws in ((0, w2r), (1, w2i)):
        rep = lax.dot_general(sel_c, rows(ws, range(lc)), dn_nt, preferred_element_type=F32)
        ck_ref[c * half:(c + 1) * half, :] = jnp.where(mask_c, rep, 0.0).astype(BF16)
    w1r_all = jnp.concatenate(w1r, axis=1)
    w1i_all = jnp.concatenate(w1i, axis=1)
    kt = (lax.dot_general(w1r_all, cre_w, dn_batch, precision=lax.Precision.HIGHEST,
                          preferred_element_type=F32)
          - lax.dot_general(w1i_all, cim_w, dn_batch, precision=lax.Precision.HIGHEST,
                            preferred_element_type=F32))
    hh = SSM_GROUP
    t_blocks = [jnp.where(mask_t, kt[:, l * hh:(l + 1) * hh, :].reshape(LANES, LANES), 0.0).astype(BF16)
                for l in range(lc)]
    zero = jnp.zeros((LANES, LANES), BF16)
    for s in range(lc):
        for t in range(lc):
            tm_ref[s * LANES:(s + 1) * LANES, t * LANES:(t + 1) * LANES] = t_blocks[t - s] if t >= s else zero


def _ssm_prep(a_re, a_im, log_dt, b_re, b_im, c_re, c_im):
    g, p, h = N_GROUPS, SSM_STATE, SSM_GROUP
    ng = g // OCTETS
    width = SSM_CHUNK * LANES
    vec = pl.BlockSpec((ng, 1, p), lambda o: (o, 0, 0))
    mat = pl.BlockSpec((ng, h, p), lambda o: (o, 0, 0))
    big = lambda rows, cols: pl.BlockSpec((None, rows, cols), lambda o: (o, 0, 0))
    return pl.pallas_call(
        _ssm_prep_kernel,
        grid=(OCTETS,),
        in_specs=[vec, vec, pl.BlockSpec((ng, 1, 1), lambda o: (o, 0, 0)), mat, mat, mat, mat],
        out_specs=[pl.BlockSpec((4, ng, 1, p), lambda o: (0, o, 0, 0)),
                   big(width, width), big(width, OCT_STATE), big(OCT_STATE, width)],
        out_shape=[jax.ShapeDtypeStruct((4, g, 1, p), F32),
                   jax.ShapeDtypeStruct((OCTETS, width, width), BF16),
                   jax.ShapeDtypeStruct((OCTETS, width, OCT_STATE), BF16),
                   jax.ShapeDtypeStruct((OCTETS, OCT_STATE, width), BF16)],
        compiler_params=_cparams(("arbitrary",), 32),
        name="ssm_prep",
    )(a_re.reshape(g, 1, p), a_im.reshape(g, 1, p), log_dt.reshape(g, 1, 1),
      b_re.transpose(0, 2, 1), b_im.transpose(0, 2, 1), c_re, c_im)


def _state_layout(v):
    return v.reshape(OCTETS, 1, OCT_STATE // 2)


def _load_chunks(u_ref, lc, rb):
    parts = [u_ref[pl.ds(s, rb, stride=lc), :] if lc > 1 else u_ref[...] for s in range(lc)]
    u = parts[0] if lc == 1 else jnp.concatenate(parts, axis=-1)
    return u.astype(BF16)


def _ssm_state_kernel(u_ref, bk_ref, s_ref, *, lc, rb):
    s_ref[...] = jnp.dot(_load_chunks(u_ref, lc, rb), bk_ref[...], preferred_element_type=F32)


def _ssm_state(u, bk, lc, rb):
    t = u.shape[0]
    nb = t // (rb * lc)
    lag0_block = SSM_CHUNK // lc - 1
    return pl.pallas_call(
        functools.partial(_ssm_state_kernel, lc=lc, rb=rb),
        grid=(OCTETS, nb),
        in_specs=[pl.BlockSpec((rb * lc, LANES), lambda o, b: (b, o)),
                  pl.BlockSpec((None, lc * LANES, OCT_STATE), lambda o, b: (o, lag0_block, 0))],
        out_specs=pl.BlockSpec((rb, OCT_STATE), lambda o, b: (b, o)),
        out_shape=jax.ShapeDtypeStruct((t // lc, OCTETS * OCT_STATE), F32),
        compiler_params=_cparams(("arbitrary", "arbitrary"), 32),
        name="ssm_state",
    )(u, bk)


def _ssm_scan_kernel(sl_ref, ar_ref, ai_ref, s0_ref, sp_ref, sf_ref, *, nb, nc):
    half = OCT_STATE // 2
    ar = ar_ref[0]
    ai = ai_ref[0]
    init = tuple((s0_ref[b:b + 1, 0:half], s0_ref[b:b + 1, half:OCT_STATE]) for b in range(nb))

    def body(c, carry):
        new = []
        for b in range(nb):
            sr, si = carry[b]
            sp_ref[b, pl.ds(c, 1), 0:half] = sr
            sp_ref[b, pl.ds(c, 1), half:OCT_STATE] = si
            xr = sl_ref[b, pl.ds(c, 1), 0:half]
            xi = sl_ref[b, pl.ds(c, 1), half:OCT_STATE]
            new.append((ar * sr - ai * si + xr, ar * si + ai * sr + xi))
        return tuple(new)

    fin = lax.fori_loop(0, nc, body, init)
    for b in range(nb):
        sf_ref[b:b + 1, 0:half] = fin[b][0]
        sf_ref[b:b + 1, half:OCT_STATE] = fin[b][1]


def _ssm_scan(s_loc, al_re, al_im, s0):
    nb, nc, width = s_loc.shape
    return pl.pallas_call(
        functools.partial(_ssm_scan_kernel, nb=nb, nc=nc),
        grid=(OCTETS,),
        in_specs=[pl.BlockSpec((nb, nc, OCT_STATE), lambda o: (0, 0, o)),
                  pl.BlockSpec((1, 1, OCT_STATE // 2), lambda o: (o, 0, 0)),
                  pl.BlockSpec((1, 1, OCT_STATE // 2), lambda o: (o, 0, 0)),
                  pl.BlockSpec((nb, OCT_STATE), lambda o: (0, o))],
        out_specs=[pl.BlockSpec((nb, nc, OCT_STATE), lambda o: (0, 0, o)),
                   pl.BlockSpec((nb, OCT_STATE), lambda o: (0, o))],
        out_shape=[jax.ShapeDtypeStruct((nb, nc, width), F32),
                   jax.ShapeDtypeStruct((nb, width), F32)],
        compiler_params=_cparams(("arbitrary",), 32),
        name="ssm_scan",
    )(s_loc, al_re, al_im, s0)


def _ssm_out_kernel(u_ref, sp_ref, tm_ref, ck_ref, y_ref, *, lc, rb):
    y = (jnp.dot(_load_chunks(u_ref, lc, rb), tm_ref[...], preferred_element_type=F32)
         + jnp.dot(sp_ref[...].astype(BF16), ck_ref[...], preferred_element_type=F32))
    if lc == 1:
        y_ref[...] = y
    else:
        for t in range(lc):
            y_ref[pl.ds(t, rb, stride=lc), :] = y[:, t * LANES:(t + 1) * LANES]


def _ssm_out(u, s_prev, tm, ck, lc, rb):
    t = u.shape[0]
    nb = t // (rb * lc)
    return pl.pallas_call(
        functools.partial(_ssm_out_kernel, lc=lc, rb=rb),
        grid=(OCTETS, nb),
        in_specs=[pl.BlockSpec((rb * lc, LANES), lambda o, b: (b, o)),
                  pl.BlockSpec((rb, OCT_STATE), lambda o, b: (b, o)),
                  pl.BlockSpec((None, lc * LANES, lc * LANES), lambda o, b: (o, 0, 0)),
                  pl.BlockSpec((None, OCT_STATE, lc * LANES), lambda o, b: (o, 0, 0))],
        out_specs=pl.BlockSpec((rb * lc, LANES), lambda o, b: (b, o)),
        out_shape=jax.ShapeDtypeStruct((t, D_SSM), F32),
        compiler_params=_cparams(("arbitrary", "arbitrary"), 32),
        name="ssm_out",
    )(u, s_prev, tm, ck)


def _ssm_step_kernel(sl_ref, s0_ref, ar_ref, ai_ref, sn_ref):
    half = OCT_STATE // 2
    ar = ar_ref[0]
    ai = ai_ref[0]
    sr = s0_ref[:, 0:half]
    si = s0_ref[:, half:OCT_STATE]
    sn_ref[:, 0:half] = ar * sr - ai * si + sl_ref[:, 0:half]
    sn_ref[:, half:OCT_STATE] = ar * si + ai * sr + sl_ref[:, half:OCT_STATE]


def _ssm_step(s_loc, s0, a_re, a_im):
    rows, width = s_loc.shape
    return pl.pallas_call(
        _ssm_step_kernel,
        grid=(OCTETS,),
        in_specs=[pl.BlockSpec((rows, OCT_STATE), lambda o: (0, o)),
                  pl.BlockSpec((rows, OCT_STATE), lambda o: (0, o)),
                  pl.BlockSpec((1, 1, OCT_STATE // 2), lambda o: (o, 0, 0)),
                  pl.BlockSpec((1, 1, OCT_STATE // 2), lambda o: (o, 0, 0))],
        out_specs=pl.BlockSpec((rows, OCT_STATE), lambda o: (0, o)),
        out_shape=jax.ShapeDtypeStruct((rows, width), F32),
        compiler_params=_cparams(("arbitrary",), 32),
        name="ssm_step",
    )(s_loc, s0, a_re, a_im)


def _pack_state(s_re, s_im):
    b = s_re.shape[0]
    half = OCT_STATE // 2
    return jnp.stack([s_re.reshape(b, OCTETS, half), s_im.reshape(b, OCTETS, half)], axis=2).reshape(b, -1)


def _unpack_state(s):
    b = s.shape[0]
    s4 = s.reshape(b, OCTETS, 2, OCT_STATE // 2)
    return (s4[:, :, 0].reshape(b, N_GROUPS, SSM_STATE), s4[:, :, 1].reshape(b, N_GROUPS, SSM_STATE))


def _mixer_kernel(*refs, seq, tm, tiles_per_batch):
    (x_ref, ys_ref, u_ref, r_ref, g1_ref, sh2_ref, sc2_ref, d_ref, wglu_ref, bglu_ref, wla_ref,
     cw_ref, cb_ref, wlb_ref, wout_ref, n2g_ref, wrh_ref, wrl_ref) = refs[:18]
    if seq:
        x1_ref, h2_ref, lg_ref, nc_ref, carry = refs[18:]
    else:
        cb0_ref, cb1_ref, x1_ref, h2_ref, lg_ref, nc0_ref, nc1_ref = refs[18:]

    xb = r_ref[:, 0:D_CONV].astype(F32)
    gb = r_ref[:, D_CONV:2 * D_CONV].astype(F32)
    gc = r_ref[:, 2 * D_CONV:3 * D_CONV].astype(F32)
    v = gc * xb
    if seq:
        first = pl.program_id(0) % tiles_per_batch == 0

        @pl.when(first)
        def _():
            carry[...] = jnp.zeros_like(carry)

        c0 = carry[0:1, :]
        c1 = carry[1:2, :]
        row = lax.broadcasted_iota(I32, (tm, 1), 0)
        v1 = jnp.where(row == 0, c1, pltpu.roll(v, 1, 0))
        v2 = jnp.where(row == 0, c0, jnp.where(row == 1, c1, pltpu.roll(v, 2, 0)))
        tail = v[tm - 2:tm, :]
        carry[...] = tail
        nc_ref[0] = tail
    else:
        v2 = cb0_ref[...]
        v1 = cb1_ref[...]
        nc0_ref[...] = v1
        nc1_ref[...] = v
    conv = cb_ref[...] + cw_ref[0:1, :] * v2 + cw_ref[1:2, :] * v1 + cw_ref[2:3, :] * v
    gated_conv = (gb * conv).astype(BF16)

    def mod_rows(ref, rs):
        return ref[0] if len(ref.shape) == 3 else ref[rs, :]

    n_split = max(tm // MIX_ROWS, 1)
    rows = tm // n_split
    off = 3 * D_CONV
    dn = (((1,), (1,)), ((), ()))
    for part in range(n_split):
        rs = slice(part * rows, (part + 1) * rows)
        ya = jax.nn.gelu(ys_ref[rs, :] + d_ref[...] * u_ref[rs, :], approximate=True)
        z = jnp.dot(ya.astype(BF16), wglu_ref[...], preferred_element_type=F32) + bglu_ref[...]
        ya = ya * jax.nn.sigmoid(z)
        branch_a = jnp.dot(ya.astype(BF16), wla_ref[...], preferred_element_type=F32)
        branch_b = jnp.dot(gated_conv[rs], wlb_ref[...], preferred_element_type=F32)
        gla = r_ref[rs, off:off + D_MODEL].astype(F32)
        glb = r_ref[rs, off + D_MODEL:off + 2 * D_MODEL].astype(F32)
        merged = jax.nn.sigmoid(gla) * branch_a + jax.nn.sigmoid(glb) * branch_b
        x1 = x_ref[rs, :] + mod_rows(g1_ref, rs) * jnp.dot(merged.astype(BF16), wout_ref[...],
                                                           preferred_element_type=F32)
        x1_ref[rs, :] = x1

        y = x1 * lax.rsqrt(jnp.mean(x1 * x1, axis=-1, keepdims=True) + RMS_EPS) * n2g_ref[...]
        h2 = y * (1.0 + mod_rows(sc2_ref, rs)) + mod_rows(sh2_ref, rs)
        h2b = h2.astype(BF16)
        h2_ref[rs, :] = h2b

        h2l = (h2 - h2b.astype(F32)).astype(BF16)
        lg_ref[:, rs] = (lax.dot_general(wrh_ref[...], h2b, dn, preferred_element_type=F32)
                         + lax.dot_general(wrh_ref[...], h2l, dn, preferred_element_type=F32)
                         + lax.dot_general(wrl_ref[...], h2b, dn, preferred_element_type=F32))


def _mixer_post(x, y_ssm, u, rest, mods, rows_per_batch, weights, conv_state, tm):
    t = x.shape[0]
    seq = conv_state is None
    tiles_per_batch = rows_per_batch // tm
    (ssm_d, w_glu, b_glu, w_la, conv_w, conv_b, w_lb, w_out, n2g, wr_hi, wr_lo) = weights
    row = lambda m: (m, 0)
    in_specs = [pl.BlockSpec((tm, D_MODEL), row),
                pl.BlockSpec((tm, D_SSM), row),
                pl.BlockSpec((tm, D_SSM), row),
                pl.BlockSpec((tm, rest.shape[1]), row),
                _mod_spec(mods, 2, rows_per_batch, tm),
                _mod_spec(mods, 3, rows_per_batch, tm),
                _mod_spec(mods, 4, rows_per_batch, tm),
                _const_spec((1, D_SSM)), _const_spec((D_SSM, D_SSM)), _const_spec((1, D_SSM)),
                _const_spec((D_SSM, D_MODEL)), _const_spec((CONV_W, D_CONV)), _const_spec((1, D_CONV)),
                _const_spec((D_CONV, D_MODEL)), _const_spec((D_MODEL, D_MODEL)),
                _const_spec((1, D_MODEL)), _const_spec((N_EXPERTS, D_MODEL)),
                _const_spec((N_EXPERTS, D_MODEL))]
    args = [x, y_ssm, u, rest, mods, mods, mods, ssm_d, w_glu, b_glu, w_la, conv_w, conv_b, w_lb,
            w_out, n2g, wr_hi, wr_lo]
    out_specs = [pl.BlockSpec((tm, D_MODEL), row),
                 pl.BlockSpec((tm, D_MODEL), row),
                 pl.BlockSpec((N_EXPERTS, tm), lambda m: (0, m))]
    out_shape = [jax.ShapeDtypeStruct((t, D_MODEL), F32),
                 jax.ShapeDtypeStruct((t, D_MODEL), BF16),
                 jax.ShapeDtypeStruct((N_EXPERTS, t), F32)]
    scratch = []
    if seq:
        nb = t // rows_per_batch
        out_specs.append(pl.BlockSpec((1, CONV_W - 1, D_CONV), lambda m: (m // tiles_per_batch, 0, 0)))
        out_shape.append(jax.ShapeDtypeStruct((nb, CONV_W - 1, D_CONV), F32))
        scratch.append(pltpu.VMEM((CONV_W - 1, D_CONV), F32))
    else:
        in_specs += [pl.BlockSpec((tm, D_CONV), row), pl.BlockSpec((tm, D_CONV), row)]
        args += [conv_state[:, 0, :], conv_state[:, 1, :]]
        out_specs += [pl.BlockSpec((tm, D_CONV), row), pl.BlockSpec((tm, D_CONV), row)]
        out_shape += [jax.ShapeDtypeStruct((t, D_CONV), F32), jax.ShapeDtypeStruct((t, D_CONV), F32)]
    return pl.pallas_call(
        functools.partial(_mixer_kernel, seq=seq, tm=tm, tiles_per_batch=tiles_per_batch),
        grid=(t // tm,),
        in_specs=in_specs,
        out_specs=out_specs,
        out_shape=out_shape,
        scratch_shapes=scratch,
        compiler_params=_cparams(("arbitrary",), 56),
        name="mixer_post",
    )(*args)


def _route_kernel(lg_ref, bias_ref, te_ref, gate_ref):
    scores = [jax.nn.sigmoid(lg_ref[e]) for e in range(N_EXPERTS)]
    choice = [scores[e] + bias_ref[e] for e in range(N_EXPERTS)]
    shape = scores[0].shape

    def first_max(vals):
        m = functools.reduce(jnp.maximum, vals)
        found = jnp.zeros(shape, jnp.bool_)
        hot = []
        for v in vals:
            is_m = jnp.logical_and(v == m, jnp.logical_not(found))
            hot.append(is_m)
            found = jnp.logical_or(found, is_m)
        return m, hot

    gscore = []
    for g in range(N_EXPERT_GROUPS):
        vals = choice[g * EXPERTS_PER_GROUP:(g + 1) * EXPERTS_PER_GROUP]
        m1, hot = first_max(vals)
        rest = [jnp.where(h, NEG_INF, v) for h, v in zip(hot, vals)]
        gscore.append(m1 + functools.reduce(jnp.maximum, rest))

    gsel = [jnp.zeros(shape, jnp.bool_) for _ in range(N_EXPERT_GROUPS)]
    for _ in range(TOPK_GROUPS):
        _, hot = first_max(gscore)
        gsel = [jnp.logical_or(s, h) for s, h in zip(gsel, hot)]
        gscore = [jnp.where(h, NEG_INF, v) for h, v in zip(hot, gscore)]

    cand = [jnp.where(gsel[e // EXPERTS_PER_GROUP], choice[e], NEG_INF) for e in range(N_EXPERTS)]
    picked_e = []
    picked_s = []
    for _ in range(TOP_K):
        _, hot = first_max(cand)
        idx = jnp.zeros(shape, I32)
        sc = jnp.zeros(shape, F32)
        for e in range(N_EXPERTS):
            idx = jnp.where(hot[e], e, idx)
            sc = jnp.where(hot[e], scores[e], sc)
        cand = [jnp.where(h, NEG_INF, v) for h, v in zip(hot, cand)]
        picked_e.append(idx)
        picked_s.append(sc)
    total = functools.reduce(lambda a, b: a + b, picked_s)
    for k in range(TOP_K):
        te_ref[k] = picked_e[k]
        gate_ref[k] = picked_s[k] / total * ROUTED_SCALE


def _route(logits_t, bias):
    t = logits_t.shape[1]
    r = t // LANES
    rb = min(r, SUBLANES)
    lg = logits_t.reshape(N_EXPERTS, r, LANES)
    bias_b = jnp.broadcast_to(bias.astype(F32).reshape(N_EXPERTS, 1, 1), (N_EXPERTS, 1, LANES))
    te, gate = pl.pallas_call(
        _route_kernel,
        grid=(r // rb,),
        in_specs=[pl.BlockSpec((N_EXPERTS, rb, LANES), lambda i: (0, i, 0)),
                  pl.BlockSpec((N_EXPERTS, 1, LANES), lambda i: (0, 0, 0))],
        out_specs=[pl.BlockSpec((TOP_K, rb, LANES), lambda i: (0, i, 0)),
                   pl.BlockSpec((TOP_K, rb, LANES), lambda i: (0, i, 0))],
        out_shape=[jax.ShapeDtypeStruct((TOP_K, r, LANES), I32),
                   jax.ShapeDtypeStruct((TOP_K, r, LANES), F32)],
        compiler_params=_cparams(("arbitrary",), 32),
        name="route",
    )(lg, bias_b)
    return te.reshape(TOP_K, t), gate.reshape(TOP_K, t)


def _dispatch_kernel(nused_ref, tok_ref, h_ref, xs_ref, tile):
    used_rows = nused_ref[0] * MOE_ROWS
    live = pl.program_id(0) * DISP_ROWS < used_rows

    @pl.when(jnp.logical_not(live))
    def _():
        xs_ref[...] = jnp.zeros_like(xs_ref)

    @pl.when(live)
    def _():
        for mi in range(DISP_ROWS):
            tile[pl.ds(mi, ROW_TILES, stride=DISP_STRIDE), :] = h_ref[tok_ref[mi]].astype(F32)
        for j in range(ROW_TILES):
            xs_ref[:, j * LANES:(j + 1) * LANES] = (
                tile[j * DISP_STRIDE:j * DISP_STRIDE + DISP_ROWS, :].astype(BF16))


def _dispatch(h_tiles, tok_buf, n_used, n_blocks):
    n_rows = n_blocks * MOE_ROWS
    assert n_rows % DISP_ROWS == 0

    def last_live(i, nu):
        return jnp.minimum(i, (nu[0] * MOE_ROWS - 1) // DISP_ROWS)

    return pl.pallas_call(
        _dispatch_kernel,
        grid_spec=pltpu.PrefetchScalarGridSpec(
            num_scalar_prefetch=1,
            grid=(n_rows // DISP_ROWS,),
            in_specs=[pl.BlockSpec((DISP_ROWS,), lambda i, nu: (last_live(i, nu),),
                                   memory_space=pltpu.SMEM),
                      pl.BlockSpec(memory_space=pltpu.VMEM)],
            out_specs=pl.BlockSpec((DISP_ROWS, D_MODEL), lambda i, nu: (i, 0)),
            scratch_shapes=[pltpu.VMEM((ROW_TILES * DISP_STRIDE, LANES), F32)]),
        out_shape=jax.ShapeDtypeStruct((n_rows, D_MODEL), BF16),
        compiler_params=_cparams(("arbitrary",), 56),
        name="dispatch",
    )(n_used, tok_buf, h_tiles)


def _experts_kernel(be_ref, nused_ref, first_ref, next_ref, xs_ref, wg_hbm, wu_hbm, wd_hbm, ys_ref,
                    wg_f, wu_f, wd_f, wg_s, wu_s, wd_s, tile_a, tile_b, sem):
    i = pl.program_id(0)

    def weight_copies(e):
        return (pltpu.make_async_copy(wg_hbm.at[e], wg_f, sem.at[0]),
                pltpu.make_async_copy(wu_hbm.at[e], wu_f, sem.at[1]),
                pltpu.make_async_copy(wd_hbm.at[e], wd_f, sem.at[2]))

    n_used = nused_ref[0]

    @pl.when(i < n_used)
    def _():
        @pl.when(i == 0)
        def _():
            for c in weight_copies(be_ref[0]):
                c.start()

        @pl.when(first_ref[i] == 1)
        def _():
            for c in weight_copies(be_ref[i]):
                c.wait()
            wg_s[...] = wg_f[...].astype(BF16)
            wu_s[...] = wu_f[...].astype(BF16)
            wd_s[...] = wd_f[...].astype(BF16)

            @pl.when(next_ref[i] >= 0)
            def _():
                for c in weight_copies(next_ref[i]):
                    c.start()

    def ffn_to_tile(tile):
        x = xs_ref[...]
        g = jnp.dot(x, wg_s[...], preferred_element_type=F32)
        u = jnp.dot(x, wu_s[...], preferred_element_type=F32)
        h = (_silu(g) * u).astype(BF16)
        y = jnp.dot(h, wd_s[...], preferred_element_type=F32)
        for j in range(ROW_TILES):
            tile[j * ROW_STRIDE:j * ROW_STRIDE + MOE_ROWS, :] = y[:, j * LANES:(j + 1) * LANES]

    def tile_to_rows(tile):
        for mi in range(MOE_ROWS):
            ys_ref[mi] = tile[pl.ds(mi, ROW_TILES, stride=ROW_STRIDE), :].astype(BF16)

    even = i % 2 == 0
    both = jnp.logical_and(i >= 1, i < n_used)

    @pl.when(jnp.logical_and(both, even))
    def _():
        ffn_to_tile(tile_a)
        tile_to_rows(tile_b)

    @pl.when(jnp.logical_and(both, jnp.logical_not(even)))
    def _():
        ffn_to_tile(tile_b)
        tile_to_rows(tile_a)

    @pl.when(i == 0)
    def _():
        ffn_to_tile(tile_a)

    @pl.when(jnp.logical_and(i == n_used, even))
    def _():
        tile_to_rows(tile_b)

    @pl.when(jnp.logical_and(i == n_used, jnp.logical_not(even)))
    def _():
        tile_to_rows(tile_a)

    @pl.when(i > n_used)
    def _():
        ys_ref[...] = jnp.zeros_like(ys_ref)


def _experts(xs, block_expert, n_used, first, next_expert, w_gate, w_up, w_down, n_blocks):
    def xmap(i, be, nu, fi, nx):
        return (jnp.minimum(i, nu[0] - 1), 0)

    any_spec = pl.BlockSpec(memory_space=pl.ANY)
    return pl.pallas_call(
        _experts_kernel,
        grid_spec=pltpu.PrefetchScalarGridSpec(
            num_scalar_prefetch=4,
            grid=(n_blocks + 1,),
            in_specs=[pl.BlockSpec((MOE_ROWS, D_MODEL), xmap), any_spec, any_spec, any_spec],
            out_specs=pl.BlockSpec((MOE_ROWS, ROW_TILES, LANES),
                                   lambda i, *_: (jnp.maximum(i - 1, 0), 0, 0)),
            scratch_shapes=[pltpu.VMEM((D_MODEL, D_EXPERT), F32),
                            pltpu.VMEM((D_MODEL, D_EXPERT), F32),
                            pltpu.VMEM((D_EXPERT, D_MODEL), F32),
                            pltpu.VMEM((D_MODEL, D_EXPERT), BF16),
                            pltpu.VMEM((D_MODEL, D_EXPERT), BF16),
                            pltpu.VMEM((D_EXPERT, D_MODEL), BF16),
                            pltpu.VMEM((ROW_TILES * ROW_STRIDE, LANES), F32),
                            pltpu.VMEM((ROW_TILES * ROW_STRIDE, LANES), F32),
                            pltpu.SemaphoreType.DMA((3,))]),
        out_shape=jax.ShapeDtypeStruct((n_blocks * MOE_ROWS, ROW_TILES, LANES), BF16),
        compiler_params=_cparams(("arbitrary",), 48),
        name="experts",
    )(block_expert, n_used, first, next_expert, xs, w_gate, w_up, w_down)


def _combine_kernel(src_ref, ntot_ref, lidx_ref, gate_ref, ys_hbm, out_ref,
                    stage, tile, sem, *, stage_rows):
    i = pl.program_id(0)
    n_tiles = pl.num_programs(0)

    def chunk_copy(src, slot, dst):
        return pltpu.make_async_copy(ys_hbm.at[pl.ds(src, COMB_CH)],
                                     stage.at[pl.ds(slot * stage_rows + dst, COMB_CH)],
                                     sem.at[slot])

    def issue(tile_idx, slot):
        def c_body(q, _):
            chunk_copy(src_ref[tile_idx, q], slot, q * COMB_CH).start()
            return 0

        lax.fori_loop(0, ntot_ref[tile_idx], c_body, 0)

    @pl.when(i == 0)
    def _():
        issue(0, 0)

    @pl.when(i + 1 < n_tiles)
    def _():
        issue(i + 1, (i + 1) % 2)

    slot = i % 2

    def w_body(c, _):
        chunk_copy(0, slot, 0).wait()
        return 0

    lax.fori_loop(0, ntot_ref[i], w_body, 0)

    stride = COMB_TOK + SUBLANES

    for tk in range(COMB_TOK):
        acc = None
        for k in range(TOP_K):
            w = stage[lidx_ref[tk * TOP_K + k]].astype(F32)
            term = gate_ref[tk * TOP_K + k] * w
            acc = term if acc is None else acc + term
        tile[pl.ds(tk, ROW_TILES, stride=stride), :] = acc
    for j in range(ROW_TILES):
        out_ref[:, j * LANES:(j + 1) * LANES] = tile[j * stride:j * stride + COMB_TOK, :]


def _combine(ys, src_flat, n_total, lidx, gate, t, stage_rows):
    n_tiles = t // COMB_TOK
    stride = COMB_TOK + SUBLANES
    return pl.pallas_call(
        functools.partial(_combine_kernel, stage_rows=stage_rows),
        grid_spec=pltpu.PrefetchScalarGridSpec(
            num_scalar_prefetch=2,
            grid=(n_tiles,),
            in_specs=[pl.BlockSpec((COMB_TOK * TOP_K,), lambda i, *_: (i,), memory_space=pltpu.SMEM),
                      pl.BlockSpec((COMB_TOK * TOP_K,), lambda i, *_: (i,), memory_space=pltpu.SMEM),
                      pl.BlockSpec(memory_space=pl.ANY)],
            out_specs=pl.BlockSpec((COMB_TOK, D_MODEL), lambda i, *_: (i, 0)),
            scratch_shapes=[pltpu.VMEM((2 * stage_rows, ROW_TILES, LANES), BF16),
                            pltpu.VMEM((ROW_TILES * stride, LANES), F32),
                            pltpu.SemaphoreType.DMA((2,))]),
        out_shape=jax.ShapeDtypeStruct((t, D_MODEL), F32),
        compiler_params=_cparams(("arbitrary",), 48),
        name="combine",
    )(src_flat, n_total, lidx, gate, ys)


def _final_kernel(h_ref, x1_ref, rt_ref, g2_ref, wsg_ref, wsu_ref, wsd_ref, gf_ref, y_ref):
    h = h_ref[...]
    a = _silu(jnp.dot(h, wsg_ref[...], preferred_element_type=F32))
    b = jnp.dot(h, wsu_ref[...], preferred_element_type=F32)
    shared = jnp.dot((a * b).astype(BF16), wsd_ref[...], preferred_element_type=F32)
    xo = x1_ref[...] + _mod(g2_ref) * (shared + rt_ref[...])
    y_ref[...] = xo * lax.rsqrt(jnp.mean(xo * xo, axis=-1, keepdims=True) + RMS_EPS) * gf_ref[...]


def _final(h2, x1, routed, routed_row0, mods, rows_per_batch, w_sg, w_su, w_sd, gf, tm):
    t = x1.shape[0]
    row = lambda m: (m, 0)
    assert routed_row0 % tm == 0
    block0 = routed_row0 // tm
    return pl.pallas_call(
        _final_kernel,
        grid=(t // tm,),
        in_specs=[pl.BlockSpec((tm, D_MODEL), row), pl.BlockSpec((tm, D_MODEL), row),
                  pl.BlockSpec((tm, D_MODEL), lambda m: (m + block0, 0)),
                  _mod_spec(mods, 5, rows_per_batch, tm),
                  _const_spec((D_MODEL, D_EXPERT)), _const_spec((D_MODEL, D_EXPERT)),
                  _const_spec((D_EXPERT, D_MODEL)), _const_spec((1, D_MODEL))],
        out_specs=pl.BlockSpec((tm, D_MODEL), row),
        out_shape=jax.ShapeDtypeStruct((t, D_MODEL), F32),
        compiler_params=_cparams(("arbitrary",), 48),
        name="final",
    )(h2, x1, routed, mods, w_sg, w_su, w_sd, gf)


def _moe_plan(top_e, t):
    ar = jnp.arange(N_EXPERTS, dtype=I32)
    sel = top_e[:, :, None] == ar
    onehot = jnp.any(sel, axis=1).astype(I32)
    assert t % LANES == 0
    groups = onehot.astype(F32).reshape(t // LANES, LANES, N_EXPERTS)
    within = jnp.einsum("ab,nbe->nae", jnp.tril(jnp.ones((LANES, LANES), F32)), groups)
    totals = within[:, -1, :]
    csum = (within + (jnp.cumsum(totals, axis=0) - totals)[:, None, :]).reshape(t, N_EXPERTS).astype(I32)
    excl = csum - onehot
    counts = csum[-1]
    padded = (counts + MOE_ROWS - 1) // MOE_ROWS * MOE_ROWS
    pend = jnp.cumsum(padded)
    pstart = pend - padded
    pick = lambda per_expert: jnp.sum(jnp.where(sel, per_expert, 0), axis=-1)
    rank = pick(excl[:, None, :])
    n_blocks = -(-(t * TOP_K + N_EXPERTS * (MOE_ROWS - 1)) // MOE_ROWS)
    block_row = jnp.arange(n_blocks, dtype=I32) * MOE_ROWS
    block_expert = jnp.minimum(jnp.sum((pend[None, :] <= block_row[:, None]).astype(I32), axis=1),
                               N_EXPERTS - 1)
    n_used = (pend[-1] // MOE_ROWS).astype(I32).reshape(1)
    first = jnp.concatenate([jnp.ones((1,), I32), (block_expert[1:] != block_expert[:-1]).astype(I32)])
    later = jnp.logical_and(counts[None, :] > 0, ar[None, :] > ar[:, None])
    next_nonempty = jnp.min(jnp.where(later, ar[None, :], N_EXPERTS), axis=1)
    next_nonempty = jnp.where(next_nonempty < N_EXPERTS, next_nonempty, -1)
    next_expert = jnp.sum(jnp.where(block_expert[:, None] == ar[None, :], next_nonempty[None, :], 0), axis=1)
    assert t + MOE_ROWS <= 1 << TOKEN_BITS
    real = ((top_e << TOKEN_BITS) | jnp.arange(t, dtype=I32)[:, None]).reshape(-1)
    r = jnp.arange(MOE_ROWS - 1, dtype=I32)
    int_max = jnp.iinfo(jnp.int32).max
    pad_keys = jnp.where(r[None, :] < (padded - counts)[:, None],
                         (ar[:, None] << TOKEN_BITS) | (t + r[None, :]), int_max).reshape(-1)
    n_rows = n_blocks * MOE_ROWS
    fill = jnp.full((n_rows - real.shape[0] - pad_keys.shape[0],), int_max, I32)
    tokv = jnp.sort(jnp.concatenate([real, pad_keys, fill])) & ((1 << TOKEN_BITS) - 1)
    tok_buf = jnp.where(tokv < t, tokv, 0)
    lo = excl[::COMB_TOK]
    hi = jnp.concatenate([lo[1:], counts[None]], axis=0)
    n_chunks = (hi - lo + COMB_CH - 1) // COMB_CH
    cend = jnp.cumsum(n_chunks, axis=1)
    cstart = cend - n_chunks
    n_total = cend[:, -1].astype(I32)
    src_start = pstart[None, :] + lo
    q = jnp.arange(COMB_MAX_CHUNKS, dtype=I32)
    eq = jnp.minimum(jnp.sum((cend[:, None, :] <= q[None, :, None]).astype(I32), axis=-1), N_EXPERTS - 1)
    sel_q = eq[:, :, None] == ar
    pick_q = lambda v: jnp.sum(jnp.where(sel_q, v[:, None, :], 0), axis=-1)
    src_flat = pick_q(src_start) + (q[None, :] - pick_q(cstart)) * COMB_CH
    src_flat = jnp.where(q[None, :] < n_total[:, None], src_flat, 0)
    stage_rows = COMB_MAX_CHUNKS * COMB_CH
    n_tiles = lo.shape[0]
    t_pad = n_tiles * COMB_TOK
    slot_base = (jnp.arange(n_tiles, dtype=I32) % 2) * stage_rows
    local = jnp.repeat(cstart * COMB_CH - lo + slot_base[:, None], COMB_TOK, axis=0)[:t]
    lidx = pick(local[:, None, :]) + rank
    lidx = jnp.concatenate([lidx, jnp.broadcast_to(slot_base[-1], (t_pad - t, TOP_K))], axis=0)
    return dict(block_expert=block_expert.astype(I32), n_used=n_used, tok_buf=tok_buf.astype(I32),
                first=first, next_expert=next_expert.astype(I32),
                n_blocks=n_blocks, src_flat=src_flat.astype(I32), n_total=n_total,
                lidx=lidx.astype(I32), stage_rows=stage_rows, t_pad=t_pad)


def kernel(x_prompt, x_sample, state_ssm_re, state_ssm_im, state_conv, c_prompt, c_sample, w_ada, b_ada, norm1_g, norm2_g, w_in, ssm_a_re, ssm_a_im, ssm_log_dt, ssm_b_re, ssm_b_im, ssm_c_re, ssm_c_im, ssm_d, w_glu, b_glu, w_lift_a, conv_w, conv_b, w_lift_b, w_out, w_router, router_bias, w_exp_gate, w_exp_up, w_exp_down, w_sh_gate, w_sh_up, w_sh_down, final_norm_g):
    depth = w_ada.shape[0]
    assert depth == 1
    bp, seq, _ = x_prompt.shape
    bs = x_sample.shape[0]
    tp = bp * seq
    l = 0

    w_in_b = w_in[l].astype(BF16)
    wr_t = w_router[l].T
    wr_hi = wr_t.astype(BF16)
    wr_lo = (wr_t - wr_hi.astype(F32)).astype(BF16)
    mix_w = (ssm_d[l].reshape(1, -1), w_glu[l].astype(BF16), b_glu[l].reshape(1, -1),
             w_lift_a[l].astype(BF16), conv_w[l], conv_b[l].reshape(1, -1), w_lift_b[l].astype(BF16),
             w_out[l].astype(BF16), norm2_g[l].reshape(1, -1), wr_hi, wr_lo)
    g1 = norm1_g[l].reshape(1, -1)

    ada = _ada(jnp.concatenate([c_prompt, c_sample], axis=0), w_ada[l], b_ada[l])
    mods_p = ada[:bp].reshape(bp, 1, -1)
    mods_s = ada[bp:]

    apow, ssm_tm, ssm_bk, ssm_ck = _ssm_prep(ssm_a_re[l], ssm_a_im[l], ssm_log_dt[l], ssm_b_re[l],
                                             ssm_b_im[l], ssm_c_re[l], ssm_c_im[l])
    a1_re, a1_im, al_re, al_im = (_state_layout(apow[i]) for i in range(4))

    xp = x_prompt.reshape(tp, D_MODEL)
    u_p, rest_p = _in_proj(xp, mods_p, seq, g1, w_in_b, tm=256)
    n_chunks = seq // SSM_CHUNK
    s_loc = _ssm_state(u_p, ssm_bk, SSM_CHUNK, n_chunks)
    s_prev, s_fin_p = _ssm_scan(s_loc.reshape(bp, n_chunks, -1), al_re, al_im,
                                jnp.zeros((bp, OCTETS * OCT_STATE), F32))
    y_p = _ssm_out(u_p, s_prev.reshape(bp * n_chunks, -1), ssm_tm, ssm_ck, SSM_CHUNK, n_chunks)
    x1_p, h2_p, lg_p, conv_p = _mixer_post(xp, y_p, u_p, rest_p, mods_p, seq, mix_w, None, tm=256)

    xs = x_sample.reshape(bs, D_MODEL)
    u_s, rest_s = _in_proj(xs, mods_s, 1, g1, w_in_b, tm=bs)
    s0_s = _pack_state(state_ssm_re[l], state_ssm_im[l])
    s_new_s = _ssm_step(_ssm_state(u_s, ssm_bk, 1, bs), s0_s, a1_re, a1_im)
    y_s = _ssm_out(u_s, s0_s, ssm_tm, ssm_ck, 1, bs)
    x1_s, h2_s, lg_s, nc0, nc1 = _mixer_post(xs, y_s, u_s, rest_s, mods_s, 1, mix_w,
                                                    state_conv[l], tm=bs)

    t = tp + bs
    te_p, gate_p = _route(lg_p, router_bias[l])
    te_s, gate_s = _route(lg_s, router_bias[l])
    top_e = jnp.concatenate([te_p, te_s], axis=1).T
    gate = jnp.concatenate([gate_p, gate_s], axis=1).T
    plan = _moe_plan(top_e, t)
    h_tiles = jnp.concatenate([h2_p, h2_s], axis=0).reshape(t, ROW_TILES, LANES)
    xs_sorted = _dispatch(h_tiles, plan["tok_buf"], plan["n_used"], plan["n_blocks"])
    ys = _experts(xs_sorted, plan["block_expert"], plan["n_used"], plan["first"], plan["next_expert"],
                  w_exp_gate[l], w_exp_up[l], w_exp_down[l], plan["n_blocks"])
    gate_pad = jnp.concatenate([gate, jnp.zeros((plan["t_pad"] - t, TOP_K), F32)], axis=0)
    routed = _combine(ys, plan["src_flat"], plan["n_total"], plan["lidx"].reshape(-1),
                      gate_pad.reshape(-1), plan["t_pad"], plan["stage_rows"])

    w_sg, w_su, w_sd = w_sh_gate[l].astype(BF16), w_sh_up[l].astype(BF16), w_sh_down[l].astype(BF16)
    gf = final_norm_g.reshape(1, -1)
    y_prompt = _final(h2_p, x1_p, routed, 0, mods_p, seq, w_sg, w_su, w_sd, gf, tm=512)
    y_sample = _final(h2_s, x1_s, routed, tp, mods_s, 1, w_sg, w_su, w_sd, gf, tm=bs)

    re_p, im_p = _unpack_state(s_fin_p)
    re_s, im_s = _unpack_state(s_new_s)
    return (y_prompt.reshape(bp, seq, D_MODEL), y_sample.reshape(bs, 1, D_MODEL),
            re_p[None], im_p[None], conv_p[None],
            re_s[None], im_s[None], jnp.stack([nc0, nc1], axis=1)[None])
```

```python
import functools

import jax
import jax.numpy as jnp
from jax import lax
from jax.experimental import pallas as pl
from jax.experimental.pallas import tpu as pltpu

F32 = jnp.float32
BF16 = jnp.bfloat16
I32 = jnp.int32

D_MODEL = 2048
D_SSM = 1024
SSM_GROUP = 16
N_GROUPS = 64
SSM_STATE = 64
D_CONV = 1024
CONV_W = 3
N_EXPERTS = 64
TOP_K = 8
N_EXPERT_GROUPS = 8
EXPERTS_PER_GROUP = 8
TOPK_GROUPS = 4
D_EXPERT = 512
ROUTED_SCALE = 2.5
RMS_EPS = 1e-6

LANES = 128
SUBLANES = 8
OCTETS = N_GROUPS * SSM_GROUP // LANES
OCT_STATE = 2 * (N_GROUPS // OCTETS) * SSM_STATE
SSM_CHUNK = 8
MOE_ROWS = 256
COMB_TOK = 128
COMB_CH = 16
ROW_TILES = D_MODEL // LANES
ROW_STRIDE = MOE_ROWS + SUBLANES
DISP_ROWS = 4 * MOE_ROWS
DISP_STRIDE = DISP_ROWS + SUBLANES
COMB_MAX_CHUNKS = COMB_TOK * TOP_K // COMB_CH + N_EXPERTS
TOKEN_BITS = 14
MIX_ROWS = 128
NEG_INF = float("-inf")


def _cparams(sem, vmem_mb):
    return pltpu.CompilerParams(dimension_semantics=sem, vmem_limit_bytes=vmem_mb * 1024 * 1024)


def _const_spec(shape):
    nd = len(shape)
    return pl.BlockSpec(shape, lambda *_: (0,) * nd, pipeline_mode=pl.Buffered(1))


def _mod(ref):
    v = ref[...]
    return v[0] if v.ndim == 3 else v


def _silu(x):
    return x * jax.nn.sigmoid(x)


def _ada_kernel(c_ref, w_ref, b_ref, o_ref):
    a = _silu(c_ref[...]).astype(BF16)
    o_ref[...] = jnp.dot(a, w_ref[...].astype(BF16), preferred_element_type=F32) + b_ref[...]


def _ada(c_all, w_ada, b_ada):
    m = c_all.shape[0]
    n = w_ada.shape[1]
    tn = 1024
    return pl.pallas_call(
        _ada_kernel,
        grid=(n // tn,),
        in_specs=[pl.BlockSpec((m, D_MODEL), lambda j: (0, 0)),
                  pl.BlockSpec((D_MODEL, tn), lambda j: (0, j)),
                  pl.BlockSpec((1, tn), lambda j: (0, j))],
        out_specs=pl.BlockSpec((m, tn), lambda j: (0, j)),
        out_shape=jax.ShapeDtypeStruct((m, n), F32),
        compiler_params=_cparams(("arbitrary",), 40),
        name="ada",
    )(c_all, w_ada, b_ada.reshape(1, n))


def _mod_spec(mods, col, rows_per_batch, tm):
    if mods.ndim == 3:
        tiles_per_batch = rows_per_batch // tm
        return pl.BlockSpec((1, 1, D_MODEL), lambda m, *_: (m // tiles_per_batch, 0, col))
    return pl.BlockSpec((tm, D_MODEL), lambda m, *_: (m, col))


def _inproj_kernel(x_ref, g_ref, sh_ref, sc_ref, w_ref, u_ref, r_ref):
    x = x_ref[...]
    y = x * lax.rsqrt(jnp.mean(x * x, axis=-1, keepdims=True) + RMS_EPS) * g_ref[...]
    h = (y * (1.0 + _mod(sc_ref)) + _mod(sh_ref)).astype(BF16)
    tn = D_SSM
    u_ref[...] = jnp.dot(h, w_ref[:, 0:tn], preferred_element_type=F32)
    for n in range(1, w_ref.shape[1] // tn):
        r_ref[:, (n - 1) * tn:n * tn] = jnp.dot(h, w_ref[:, n * tn:(n + 1) * tn],
                                                preferred_element_type=F32).astype(BF16)


def _in_proj(x, mods, rows_per_batch, g1, w_in, tm):
    t = x.shape[0]
    d_in = w_in.shape[1]
    n_rest = d_in - D_SSM
    return pl.pallas_call(
        _inproj_kernel,
        grid=(t // tm,),
        in_specs=[pl.BlockSpec((tm, D_MODEL), lambda m: (m, 0)),
                  _const_spec((1, D_MODEL)),
                  _mod_spec(mods, 0, rows_per_batch, tm),
                  _mod_spec(mods, 1, rows_per_batch, tm),
                  _const_spec((D_MODEL, d_in))],
        out_specs=[pl.BlockSpec((tm, D_SSM), lambda m: (m, 0)),
                   pl.BlockSpec((tm, n_rest), lambda m: (m, 0))],
        out_shape=[jax.ShapeDtypeStruct((t, D_SSM), F32),
                   jax.ShapeDtypeStruct((t, n_rest), BF16)],
        compiler_params=_cparams(("arbitrary",), 56),
        name="in_proj",
    )(x, g1, mods, mods, w_in)


def _ssm_prep_kernel(are_ref, aim_ref, ldt_ref, btr_ref, bti_ref, cre_ref, cim_ref,
                     apow_ref, tm_ref, bk_ref, ck_ref):
    lc = SSM_CHUNK
    ng = N_GROUPS // OCTETS
    half = OCT_STATE // 2
    lr = are_ref[...]
    li = aim_ref[...]
    dt = jnp.exp(ldt_ref[...])
    mag = jnp.exp(lr * dt)
    ar = mag * jnp.cos(li * dt)
    ai = mag * jnp.sin(li * dt)
    den = lr * lr + li * li
    xr = ar - 1.0
    cr = (xr * lr + ai * li) / den
    ci = (ai * lr - xr * li) / den
    btr = btr_ref[...]
    bti = bti_ref[...]
    bbr = cr * btr - ci * bti
    bbi = cr * bti + ci * btr
    cre = cre_ref[...]
    cim = cim_ref[...]
    cre_w = jnp.concatenate([cre] * ng, axis=1)
    cim_w = jnp.concatenate([cim] * ng, axis=1)

    def iota(shape, dim):
        return lax.broadcasted_iota(I32, shape, dim)

    width = lc * LANES
    sel_b = ((iota((SSM_STATE, half), 1) & (SSM_STATE - 1)) == iota((SSM_STATE, half), 0)).astype(BF16)
    mask_b = ((iota((width, half), 0) & (LANES - 1)) >> 4) == (iota((width, half), 1) >> 6)
    sel_c = ((iota((half, SSM_STATE), 0) & (SSM_STATE - 1)) == iota((half, SSM_STATE), 1)).astype(BF16)
    mask_c = (iota((half, width), 0) >> 6) == ((iota((half, width), 1) & (LANES - 1)) >> 4)
    mask_t = (iota((LANES, LANES), 0) >> 4) == (iota((LANES, LANES), 1) >> 4)
    dn_batch = (((2,), (2,)), ((0,), (0,)))
    dn_nt = (((1,), (1,)), ((), ()))

    pr = jnp.ones_like(ar)
    pi = jnp.zeros_like(ar)
    w1r, w1i, w2r, w2i = [], [], [], []
    for l in range(lc):
        w1r.append(pr * bbr - pi * bbi)
        w1i.append(pr * bbi + pi * bbr)
        pr, pi = pr * ar - pi * ai, pr * ai + pi * ar
        w2r.append(cre * pr - cim * pi)
        w2i.append(-(cre * pi + cim * pr))
        if l == 0:
            apow_ref[0] = pr
            apow_ref[1] = pi
    apow_ref[2] = pr
    apow_ref[3] = pi

    def rows(ws, order):
        return jnp.concatenate([ws[l] for l in order], axis=0).reshape(width, SSM_STATE).astype(BF16)

    rev = range(lc - 1, -1, -1)
    for c, ws in ((0, w1r), (1, w1i)):
        rep = jnp.dot(rows(ws, rev), sel_b, preferred_element_type=F32)
        bk_ref[:, c * half:(c + 1) * half] = jnp.where(mask_b, rep, 0.0).astype(BF16)
    for c, ws in ((0, w2r), (1, w2i)):
        rep = lax.dot_general(sel_c, rows(ws, range(lc)), dn_nt, preferred_element_type=F32)
        ck_ref[c * half:(c + 1) * half, :] = jnp.where(mask_c, rep, 0.0).astype(BF16)
    w1r_all = jnp.concatenate(w1r, axis=1)
    w1i_all = jnp.concatenate(w1i, axis=1)
    kt = (lax.dot_general(w1r_all, cre_w, dn_batch, precision=lax.Precision.HIGHEST,
                          preferred_element_type=F32)
          - lax.dot_general(w1i_all, cim_w, dn_batch, precision=lax.Precision.HIGHEST,
                            preferred_element_type=F32))
    hh = SSM_GROUP
    t_blocks = [jnp.where(mask_t, kt[:, l * hh:(l + 1) * hh, :].reshape(LANES, LANES), 0.0).astype(BF16)
                for l in range(lc)]
    zero = jnp.zeros((LANES, LANES), BF16)
    for s in range(lc):
        for t in range(lc):
            tm_ref[s * LANES:(s + 1) * LANES, t * LANES:(t + 1) * LANES] = t_blocks[t - s] if t >= s else zero


def _ssm_prep(a_re, a_im, log_dt, b_re, b_im, c_re, c_im):
    g, p, h = N_GROUPS, SSM_STATE, SSM_GROUP
    ng = g // OCTETS
    width = SSM_CHUNK * LANES
    vec = pl.BlockSpec((ng, 1, p), lambda o: (o, 0, 0))
    mat = pl.BlockSpec((ng, h, p), lambda o: (o, 0, 0))
    big = lambda rows, cols: pl.BlockSpec((None, rows, cols), lambda o: (o, 0, 0))
    return pl.pallas_call(
        _ssm_prep_kernel,
        grid=(OCTETS,),
        in_specs=[vec, vec, pl.BlockSpec((ng, 1, 1), lambda o: (o, 0, 0)), mat, mat, mat, mat],
        out_specs=[pl.BlockSpec((4, ng, 1, p), lambda o: (0, o, 0, 0)),
                   big(width, width), big(width, OCT_STATE), big(OCT_STATE, width)],
        out_shape=[jax.ShapeDtypeStruct((4, g, 1, p), F32),
                   jax.ShapeDtypeStruct((OCTETS, width, width), BF16),
                   jax.ShapeDtypeStruct((OCTETS, width, OCT_STATE), BF16),
                   jax.ShapeDtypeStruct((OCTETS, OCT_STATE, width), BF16)],
        compiler_params=_cparams(("arbitrary",), 32),
        name="ssm_prep",
    )(a_re.reshape(g, 1, p), a_im.reshape(g, 1, p), log_dt.reshape(g, 1, 1),
      b_re.transpose(0, 2, 1), b_im.transpose(0, 2, 1), c_re, c_im)


def _state_layout(v):
    return v.reshape(OCTETS, 1, OCT_STATE // 2)


def _load_chunks(u_ref, lc, rb):
    parts = [u_ref[pl.ds(s, rb, stride=lc), :] if lc > 1 else u_ref[...] for s in range(lc)]
    u = parts[0] if lc == 1 else jnp.concatenate(parts, axis=-1)
    return u.astype(BF16)


def _ssm_state_kernel(u_ref, bk_ref, s_ref, *, lc, rb):
    s_ref[...] = jnp.dot(_load_chunks(u_ref, lc, rb), bk_ref[...], preferred_element_type=F32)


def _ssm_state(u, bk, lc, rb):
    t = u.shape[0]
    nb = t // (rb * lc)
    lag0_block = SSM_CHUNK // lc - 1
    return pl.pallas_call(
        functools.partial(_ssm_state_kernel, lc=lc, rb=rb),
        grid=(OCTETS, nb),
        in_specs=[pl.BlockSpec((rb * lc, LANES), lambda o, b: (b, o)),
                  pl.BlockSpec((None, lc * LANES, OCT_STATE), lambda o, b: (o, lag0_block, 0))],
        out_specs=pl.BlockSpec((rb, OCT_STATE), lambda o, b: (b, o)),
        out_shape=jax.ShapeDtypeStruct((t // lc, OCTETS * OCT_STATE), F32),
        compiler_params=_cparams(("arbitrary", "arbitrary"), 32),
        name="ssm_state",
    )(u, bk)


def _ssm_scan_kernel(sl_ref, ar_ref, ai_ref, s0_ref, sp_ref, sf_ref, *, nb, nc):
    half = OCT_STATE // 2
    ar = ar_ref[0]
    ai = ai_ref[0]
    init = tuple((s0_ref[b:b + 1, 0:half], s0_ref[b:b + 1, half:OCT_STATE]) for b in range(nb))

    def body(c, carry):
        new = []
        for b in range(nb):
            sr, si = carry[b]
            sp_ref[b, pl.ds(c, 1), 0:half] = sr
            sp_ref[b, pl.ds(c, 1), half:OCT_STATE] = si
            xr = sl_ref[b, pl.ds(c, 1), 0:half]
            xi = sl_ref[b, pl.ds(c, 1), half:OCT_STATE]
            new.append((ar * sr - ai * si + xr, ar * si + ai * sr + xi))
        return tuple(new)

    fin = lax.fori_loop(0, nc, body, init)
    for b in range(nb):
        sf_ref[b:b + 1, 0:half] = fin[b][0]
        sf_ref[b:b + 1, half:OCT_STATE] = fin[b][1]


def _ssm_scan(s_loc, al_re, al_im, s0):
    nb, nc, width = s_loc.shape
    return pl.pallas_call(
        functools.partial(_ssm_scan_kernel, nb=nb, nc=nc),
        grid=(OCTETS,),
        in_specs=[pl.BlockSpec((nb, nc, OCT_STATE), lambda o: (0, 0, o)),
                  pl.BlockSpec((1, 1, OCT_STATE // 2), lambda o: (o, 0, 0)),
                  pl.BlockSpec((1, 1, OCT_STATE // 2), lambda o: (o, 0, 0)),
                  pl.BlockSpec((nb, OCT_STATE), lambda o: (0, o))],
        out_specs=[pl.BlockSpec((nb, nc, OCT_STATE), lambda o: (0, 0, o)),
                   pl.BlockSpec((nb, OCT_STATE), lambda o: (0, o))],
        out_shape=[jax.ShapeDtypeStruct((nb, nc, width), F32),
                   jax.ShapeDtypeStruct((nb, width), F32)],
        compiler_params=_cparams(("arbitrary",), 32),
        name="ssm_scan",
    )(s_loc, al_re, al_im, s0)


def _ssm_out_kernel(u_ref, sp_ref, tm_ref, ck_ref, y_ref, *, lc, rb):
    y = (jnp.dot(_load_chunks(u_ref, lc, rb), tm_ref[...], preferred_element_type=F32)
         + jnp.dot(sp_ref[...].astype(BF16), ck_ref[...], preferred_element_type=F32))
    if lc == 1:
        y_ref[...] = y
    else:
        for t in range(lc):
            y_ref[pl.ds(t, rb, stride=lc), :] = y[:, t * LANES:(t + 1) * LANES]


def _ssm_out(u, s_prev, tm, ck, lc, rb):
    t = u.shape[0]
    nb = t // (rb * lc)
    return pl.pallas_call(
        functools.partial(_ssm_out_kernel, lc=lc, rb=rb),
        grid=(OCTETS, nb),
        in_specs=[pl.BlockSpec((rb * lc, LANES), lambda o, b: (b, o)),
                  pl.BlockSpec((rb, OCT_STATE), lambda o, b: (b, o)),
                  pl.BlockSpec((None, lc * LANES, lc * LANES), lambda o, b: (o, 0, 0)),
                  pl.BlockSpec((None, OCT_STATE, lc * LANES), lambda o, b: (o, 0, 0))],
        out_specs=pl.BlockSpec((rb * lc, LANES), lambda o, b: (b, o)),
        out_shape=jax.ShapeDtypeStruct((t, D_SSM), F32),
        compiler_params=_cparams(("arbitrary", "arbitrary"), 32),
        name="ssm_out",
    )(u, s_prev, tm, ck)


def _ssm_step_kernel(sl_ref, s0_ref, ar_ref, ai_ref, sn_ref):
    half = OCT_STATE // 2
    ar = ar_ref[0]
    ai = ai_ref[0]
    sr = s0_ref[:, 0:half]
    si = s0_ref[:, half:OCT_STATE]
    sn_ref[:, 0:half] = ar * sr - ai * si + sl_ref[:, 0:half]
    sn_ref[:, half:OCT_STATE] = ar * si + ai * sr + sl_ref[:, half:OCT_STATE]


def _ssm_step(s_loc, s0, a_re, a_im):
    rows, width = s_loc.shape
    return pl.pallas_call(
        _ssm_step_kernel,
        grid=(OCTETS,),
        in_specs=[pl.BlockSpec((rows, OCT_STATE), lambda o: (0, o)),
                  pl.BlockSpec((rows, OCT_STATE), lambda o: (0, o)),
                  pl.BlockSpec((1, 1, OCT_STATE // 2), lambda o: (o, 0, 0)),
                  pl.BlockSpec((1, 1, OCT_STATE // 2), lambda o: (o, 0, 0))],
        out_specs=pl.BlockSpec((rows, OCT_STATE), lambda o: (0, o)),
        out_shape=jax.ShapeDtypeStruct((rows, width), F32),
        compiler_params=_cparams(("arbitrary",), 32),
        name="ssm_step",
    )(s_loc, s0, a_re, a_im)


def _pack_state(s_re, s_im):
    b = s_re.shape[0]
    half = OCT_STATE // 2
    return jnp.stack([s_re.reshape(b, OCTETS, half), s_im.reshape(b, OCTETS, half)], axis=2).reshape(b, -1)


def _unpack_state(s):
    b = s.shape[0]
    s4 = s.reshape(b, OCTETS, 2, OCT_STATE // 2)
    return (s4[:, :, 0].reshape(b, N_GROUPS, SSM_STATE), s4[:, :, 1].reshape(b, N_GROUPS, SSM_STATE))


def _mixer_kernel(*refs, seq, tm, tiles_per_batch):
    (x_ref, ys_ref, u_ref, r_ref, g1_ref, sh2_ref, sc2_ref, d_ref, wglu_ref, bglu_ref, wla_ref,
     cw_ref, cb_ref, wlb_ref, wout_ref, n2g_ref, wrh_ref, wrl_ref) = refs[:18]
    if seq:
        x1_ref, h2_ref, lg_ref, nc_ref, carry = refs[18:]
    else:
        cb0_ref, cb1_ref, x1_ref, h2_ref, lg_ref, nc0_ref, nc1_ref = refs[18:]

    xb = r_ref[:, 0:D_CONV].astype(F32)
    gb = r_ref[:, D_CONV:2 * D_CONV].astype(F32)
    gc = r_ref[:, 2 * D_CONV:3 * D_CONV].astype(F32)
    v = gc * xb
    if seq:
        first = pl.program_id(0) % tiles_per_batch == 0

        @pl.when(first)
        def _():
            carry[...] = jnp.zeros_like(carry)

        c0 = carry[0:1, :]
        c1 = carry[1:2, :]
        row = lax.broadcasted_iota(I32, (tm, 1), 0)
        v1 = jnp.where(row == 0, c1, pltpu.roll(v, 1, 0))
        v2 = jnp.where(row == 0, c0, jnp.where(row == 1, c1, pltpu.roll(v, 2, 0)))
        tail = v[tm - 2:tm, :]
        carry[...] = tail
        nc_ref[0] = tail
    else:
        v2 = cb0_ref[...]
        v1 = cb1_ref[...]
        nc0_ref[...] = v1
        nc1_ref[...] = v
    conv = cb_ref[...] + cw_ref[0:1, :] * v2 + cw_ref[1:2, :] * v1 + cw_ref[2:3, :] * v
    gated_conv = (gb * conv).astype(BF16)

    def mod_rows(ref, rs):
        return ref[0] if len(ref.shape) == 3 else ref[rs, :]

    n_split = max(tm // MIX_ROWS, 1)
    rows = tm // n_split
    off = 3 * D_CONV
    dn = (((1,), (1,)), ((), ()))
    for part in range(n_split):
        rs = slice(part * rows, (part + 1) * rows)
        ya = jax.nn.gelu(ys_ref[rs, :] + d_ref[...] * u_ref[rs, :], approximate=True)
        z = jnp.dot(ya.astype(BF16), wglu_ref[...], preferred_element_type=F32) + bglu_ref[...]
        ya = ya * jax.nn.sigmoid(z)
        branch_a = jnp.dot(ya.astype(BF16), wla_ref[...], preferred_element_type=F32)
        branch_b = jnp.dot(gated_conv[rs], wlb_ref[...], preferred_element_type=F32)
        gla = r_ref[rs, off:off + D_MODEL].astype(F32)
        glb = r_ref[rs, off + D_MODEL:off + 2 * D_MODEL].astype(F32)
        merged = jax.nn.sigmoid(gla) * branch_a + jax.nn.sigmoid(glb) * branch_b
        x1 = x_ref[rs, :] + mod_rows(g1_ref, rs) * jnp.dot(merged.astype(BF16), wout_ref[...],
                                                           preferred_element_type=F32)
        x1_ref[rs, :] = x1

        y = x1 * lax.rsqrt(jnp.mean(x1 * x1, axis=-1, keepdims=True) + RMS_EPS) * n2g_ref[...]
        h2 = y * (1.0 + mod_rows(sc2_ref, rs)) + mod_rows(sh2_ref, rs)
        h2b = h2.astype(BF16)
        h2_ref[rs, :] = h2b

        h2l = (h2 - h2b.astype(F32)).astype(BF16)
        lg_ref[:, rs] = (lax.dot_general(wrh_ref[...], h2b, dn, preferred_element_type=F32)
                         + lax.dot_general(wrh_ref[...], h2l, dn, preferred_element_type=F32)
                         + lax.dot_general(wrl_ref[...], h2b, dn, preferred_element_type=F32))


def _mixer_post(x, y_ssm, u, rest, mods, rows_per_batch, weights, conv_state, tm):
    t = x.shape[0]
    seq = conv_state is None
    tiles_per_batch = rows_per_batch // tm
    (ssm_d, w_glu, b_glu, w_la, conv_w, conv_b, w_lb, w_out, n2g, wr_hi, wr_lo) = weights
    row = lambda m: (m, 0)
    in_specs = [pl.BlockSpec((tm, D_MODEL), row),
                pl.BlockSpec((tm, D_SSM), row),
                pl.BlockSpec((tm, D_SSM), row),
                pl.BlockSpec((tm, rest.shape[1]), row),
                _mod_spec(mods, 2, rows_per_batch, tm),
                _mod_spec(mods, 3, rows_per_batch, tm),
                _mod_spec(mods, 4, rows_per_batch, tm),
                _const_spec((1, D_SSM)), _const_spec((D_SSM, D_SSM)), _const_spec((1, D_SSM)),
                _const_spec((D_SSM, D_MODEL)), _const_spec((CONV_W, D_CONV)), _const_spec((1, D_CONV)),
                _const_spec((D_CONV, D_MODEL)), _const_spec((D_MODEL, D_MODEL)),
                _const_spec((1, D_MODEL)), _const_spec((N_EXPERTS, D_MODEL)),
                _const_spec((N_EXPERTS, D_MODEL))]
    args = [x, y_ssm, u, rest, mods, mods, mods, ssm_d, w_glu, b_glu, w_la, conv_w, conv_b, w_lb,
            w_out, n2g, wr_hi, wr_lo]
    out_specs = [pl.BlockSpec((tm, D_MODEL), row),
                 pl.BlockSpec((tm, D_MODEL), row),
                 pl.BlockSpec((N_EXPERTS, tm), lambda m: (0, m))]
    out_shape = [jax.ShapeDtypeStruct((t, D_MODEL), F32),
                 jax.ShapeDtypeStruct((t, D_MODEL), BF16),
                 jax.ShapeDtypeStruct((N_EXPERTS, t), F32)]
    scratch = []
    if seq:
        nb = t // rows_per_batch
        out_specs.append(pl.BlockSpec((1, CONV_W - 1, D_CONV), lambda m: (m // tiles_per_batch, 0, 0)))
        out_shape.append(jax.ShapeDtypeStruct((nb, CONV_W - 1, D_CONV), F32))
        scratch.append(pltpu.VMEM((CONV_W - 1, D_CONV), F32))
    else:
        in_specs += [pl.BlockSpec((tm, D_CONV), row), pl.BlockSpec((tm, D_CONV), row)]
        args += [conv_state[:, 0, :], conv_state[:, 1, :]]
        out_specs += [pl.BlockSpec((tm, D_CONV), row), pl.BlockSpec((tm, D_CONV), row)]
        out_shape += [jax.ShapeDtypeStruct((t, D_CONV), F32), jax.ShapeDtypeStruct((t, D_CONV), F32)]
    return pl.pallas_call(
        functools.partial(_mixer_kernel, seq=seq, tm=tm, tiles_per_batch=tiles_per_batch),
        grid=(t // tm,),
        in_specs=in_specs,
        out_specs=out_specs,
        out_shape=out_shape,
        scratch_shapes=scratch,
        compiler_params=_cparams(("arbitrary",), 56),
        name="mixer_post",
    )(*args)


def _route_kernel(lg_ref, bias_ref, te_ref, gate_ref):
    scores = [jax.nn.sigmoid(lg_ref[e]) for e in range(N_EXPERTS)]
    choice = [scores[e] + bias_ref[e] for e in range(N_EXPERTS)]
    shape = scores[0].shape

    def first_max(vals):
        m = functools.reduce(jnp.maximum, vals)
        found = jnp.zeros(shape, jnp.bool_)
        hot = []
        for v in vals:
            is_m = jnp.logical_and(v == m, jnp.logical_not(found))
            hot.append(is_m)
            found = jnp.logical_or(found, is_m)
        return m, hot

    gscore = []
    for g in range(N_EXPERT_GROUPS):
        vals = choice[g * EXPERTS_PER_GROUP:(g + 1) * EXPERTS_PER_GROUP]
        m1, hot = first_max(vals)
        rest = [jnp.where(h, NEG_INF, v) for h, v in zip(hot, vals)]
        gscore.append(m1 + functools.reduce(jnp.maximum, rest))

    gsel = [jnp.zeros(shape, jnp.bool_) for _ in range(N_EXPERT_GROUPS)]
    for _ in range(TOPK_GROUPS):
        _, hot = first_max(gscore)
        gsel = [jnp.logical_or(s, h) for s, h in zip(gsel, hot)]
        gscore = [jnp.where(h, NEG_INF, v) for h, v in zip(hot, gscore)]

    cand = [jnp.where(gsel[e // EXPERTS_PER_GROUP], choice[e], NEG_INF) for e in range(N_EXPERTS)]
    picked_e = []
    picked_s = []
    for _ in range(TOP_K):
        _, hot = first_max(cand)
        idx = jnp.zeros(shape, I32)
        sc = jnp.zeros(shape, F32)
        for e in range(N_EXPERTS):
            idx = jnp.where(hot[e], e, idx)
            sc = jnp.where(hot[e], scores[e], sc)
        cand = [jnp.where(h, NEG_INF, v) for h, v in zip(hot, cand)]
        picked_e.append(idx)
        picked_s.append(sc)
    total = functools.reduce(lambda a, b: a + b, picked_s)
    for k in range(TOP_K):
        te_ref[k] = picked_e[k]
        gate_ref[k] = picked_s[k] / total * ROUTED_SCALE


def _route(logits_t, bias):
    t = logits_t.shape[1]
    r = t // LANES
    rb = min(r, SUBLANES)
    lg = logits_t.reshape(N_EXPERTS, r, LANES)
    bias_b = jnp.broadcast_to(bias.astype(F32).reshape(N_EXPERTS, 1, 1), (N_EXPERTS, 1, LANES))
    te, gate = pl.pallas_call(
        _route_kernel,
        grid=(r // rb,),
        in_specs=[pl.BlockSpec((N_EXPERTS, rb, LANES), lambda i: (0, i, 0)),
                  pl.BlockSpec((N_EXPERTS, 1, LANES), lambda i: (0, 0, 0))],
        out_specs=[pl.BlockSpec((TOP_K, rb, LANES), lambda i: (0, i, 0)),
                   pl.BlockSpec((TOP_K, rb, LANES), lambda i: (0, i, 0))],
        out_shape=[jax.ShapeDtypeStruct((TOP_K, r, LANES), I32),
                   jax.ShapeDtypeStruct((TOP_K, r, LANES), F32)],
        compiler_params=_cparams(("arbitrary",), 32),
        name="route",
    )(lg, bias_b)
    return te.reshape(TOP_K, t), gate.reshape(TOP_K, t)


def _dispatch_kernel(nused_ref, tok_ref, h_ref, xs_ref, tile):
    used_rows = nused_ref[0] * MOE_ROWS
    live = pl.program_id(0) * DISP_ROWS < used_rows

    @pl.when(jnp.logical_not(live))
    def _():
        xs_ref[...] = jnp.zeros_like(xs_ref)

    @pl.when(live)
    def _():
        for mi in range(DISP_ROWS):
            tile[pl.ds(mi, ROW_TILES, stride=DISP_STRIDE), :] = h_ref[tok_ref[mi]].astype(F32)
        for j in range(ROW_TILES):
            xs_ref[:, j * LANES:(j + 1) * LANES] = (
                tile[j * DISP_STRIDE:j * DISP_STRIDE + DISP_ROWS, :].astype(BF16))


def _dispatch(h_tiles, tok_buf, n_used, n_blocks):
    n_rows = n_blocks * MOE_ROWS
    assert n_rows % DISP_ROWS == 0

    def last_live(i, nu):
        return jnp.minimum(i, (nu[0] * MOE_ROWS - 1) // DISP_ROWS)

    return pl.pallas_call(
        _dispatch_kernel,
        grid_spec=pltpu.PrefetchScalarGridSpec(
            num_scalar_prefetch=1,
            grid=(n_rows // DISP_ROWS,),
            in_specs=[pl.BlockSpec((DISP_ROWS,), lambda i, nu: (last_live(i, nu),),
                                   memory_space=pltpu.SMEM),
                      pl.BlockSpec(memory_space=pltpu.VMEM)],
            out_specs=pl.BlockSpec((DISP_ROWS, D_MODEL), lambda i, nu: (i, 0)),
            scratch_shapes=[pltpu.VMEM((ROW_TILES * DISP_STRIDE, LANES), F32)]),
        out_shape=jax.ShapeDtypeStruct((n_rows, D_MODEL), BF16),
        compiler_params=_cparams(("arbitrary",), 56),
        name="dispatch",
    )(n_used, tok_buf, h_tiles)


def _experts_kernel(be_ref, nused_ref, first_ref, next_ref, xs_ref, wg_hbm, wu_hbm, wd_hbm, ys_ref,
                    wg_f, wu_f, wd_f, wg_s, wu_s, wd_s, tile_a, tile_b, sem):
    i = pl.program_id(0)

    def weight_copies(e):
        return (pltpu.make_async_copy(wg_hbm.at[e], wg_f, sem.at[0]),
                pltpu.make_async_copy(wu_hbm.at[e], wu_f, sem.at[1]),
                pltpu.make_async_copy(wd_hbm.at[e], wd_f, sem.at[2]))

    n_used = nused_ref[0]

    @pl.when(i < n_used)
    def _():
        @pl.when(i == 0)
        def _():
            for c in weight_copies(be_ref[0]):
                c.start()

        @pl.when(first_ref[i] == 1)
        def _():
            for c in weight_copies(be_ref[i]):
                c.wait()
            wg_s[...] = wg_f[...].astype(BF16)
            wu_s[...] = wu_f[...].astype(BF16)
            wd_s[...] = wd_f[...].astype(BF16)

            @pl.when(next_ref[i] >= 0)
            def _():
                for c in weight_copies(next_ref[i]):
                    c.start()

    def ffn_to_tile(tile):
        x = xs_ref[...]
        g = jnp.dot(x, wg_s[...], preferred_element_type=F32)
        u = jnp.dot(x, wu_s[...], preferred_element_type=F32)
        h = (_silu(g) * u).astype(BF16)
        y = jnp.dot(h, wd_s[...], preferred_element_type=F32)
        for j in range(ROW_TILES):
            tile[j * ROW_STRIDE:j * ROW_STRIDE + MOE_ROWS, :] = y[:, j * LANES:(j + 1) * LANES]

    def tile_to_rows(tile):
        for mi in range(MOE_ROWS):
            ys_ref[mi] = tile[pl.ds(mi, ROW_TILES, stride=ROW_STRIDE), :].astype(BF16)

    even = i % 2 == 0
    both = jnp.logical_and(i >= 1, i < n_used)

    @pl.when(jnp.logical_and(both, even))
    def _():
        ffn_to_tile(tile_a)
        tile_to_rows(tile_b)

    @pl.when(jnp.logical_and(both, jnp.logical_not(even)))
    def _():
        ffn_to_tile(tile_b)
        tile_to_rows(tile_a)

    @pl.when(i == 0)
    def _():
        ffn_to_tile(tile_a)

    @pl.when(jnp.logical_and(i == n_used, even))
    def _():
        tile_to_rows(tile_b)

    @pl.when(jnp.logical_and(i == n_used, jnp.logical_not(even)))
    def _():
        tile_to_rows(tile_a)

    @pl.when(i > n_used)
    def _():
        ys_ref[...] = jnp.zeros_like(ys_ref)


def _experts(xs, block_expert, n_used, first, next_expert, w_gate, w_up, w_down, n_blocks):
    def xmap(i, be, nu, fi, nx):
        return (jnp.minimum(i, nu[0] - 1), 0)

    any_spec = pl.BlockSpec(memory_space=pl.ANY)
    return pl.pallas_call(
        _experts_kernel,
        grid_spec=pltpu.PrefetchScalarGridSpec(
            num_scalar_prefetch=4,
            grid=(n_blocks + 1,),
            in_specs=[pl.BlockSpec((MOE_ROWS, D_MODEL), xmap), any_spec, any_spec, any_spec],
            out_specs=pl.BlockSpec((MOE_ROWS, ROW_TILES, LANES),
                                   lambda i, *_: (jnp.maximum(i - 1, 0), 0, 0)),
            scratch_shapes=[pltpu.VMEM((D_MODEL, D_EXPERT), F32),
                            pltpu.VMEM((D_MODEL, D_EXPERT), F32),
                            pltpu.VMEM((D_EXPERT, D_MODEL), F32),
                            pltpu.VMEM((D_MODEL, D_EXPERT), BF16),
                            pltpu.VMEM((D_MODEL, D_EXPERT), BF16),
                            pltpu.VMEM((D_EXPERT, D_MODEL), BF16),
                            pltpu.VMEM((ROW_TILES * ROW_STRIDE, LANES), F32),
                            pltpu.VMEM((ROW_TILES * ROW_STRIDE, LANES), F32),
                            pltpu.SemaphoreType.DMA((3,))]),
        out_shape=jax.ShapeDtypeStruct((n_blocks * MOE_ROWS, ROW_TILES, LANES), BF16),
        compiler_params=_cparams(("arbitrary",), 48),
        name="experts",
    )(block_expert, n_used, first, next_expert, xs, w_gate, w_up, w_down)


def _combine_kernel(src_ref, ntot_ref, lidx_ref, gate_ref, ys_hbm, out_ref,
                    stage, tile, sem, *, stage_rows):
    i = pl.program_id(0)
    n_tiles = pl.num_programs(0)

    def chunk_copy(src, slot, dst):
        return pltpu.make_async_copy(ys_hbm.at[pl.ds(src, COMB_CH)],
                                     stage.at[pl.ds(slot * stage_rows + dst, COMB_CH)],
                                     sem.at[slot])

    def issue(tile_idx, slot):
        def c_body(q, _):
            chunk_copy(src_ref[tile_idx, q], slot, q * COMB_CH).start()
            return 0

        lax.fori_loop(0, ntot_ref[tile_idx], c_body, 0)

    @pl.when(i == 0)
    def _():
        issue(0, 0)

    @pl.when(i + 1 < n_tiles)
    def _():
        issue(i + 1, (i + 1) % 2)

    slot = i % 2

    def w_body(c, _):
        chunk_copy(0, slot, 0).wait()
        return 0

    lax.fori_loop(0, ntot_ref[i], w_body, 0)

    stride = COMB_TOK + SUBLANES

    for tk in range(COMB_TOK):
        acc = None
        for k in range(TOP_K):
            w = stage[lidx_ref[tk * TOP_K + k]].astype(F32)
            term = gate_ref[tk * TOP_K + k] * w
            acc = term if acc is None else acc + term
        tile[pl.ds(tk, ROW_TILES, stride=stride), :] = acc
    for j in range(ROW_TILES):
        out_ref[:, j * LANES:(j + 1) * LANES] = tile[j * stride:j * stride + COMB_TOK, :]


def _combine(ys, src_flat, n_total, lidx, gate, t, stage_rows):
    n_tiles = t // COMB_TOK
    stride = COMB_TOK + SUBLANES
    return pl.pallas_call(
        functools.partial(_combine_kernel, stage_rows=stage_rows),
        grid_spec=pltpu.PrefetchScalarGridSpec(
            num_scalar_prefetch=2,
            grid=(n_tiles,),
            in_specs=[pl.BlockSpec((COMB_TOK * TOP_K,), lambda i, *_: (i,), memory_space=pltpu.SMEM),
                      pl.BlockSpec((COMB_TOK * TOP_K,), lambda i, *_: (i,), memory_space=pltpu.SMEM),
                      pl.BlockSpec(memory_space=pl.ANY)],
            out_specs=pl.BlockSpec((COMB_TOK, D_MODEL), lambda i, *_: (i, 0)),
            scratch_shapes=[pltpu.VMEM((2 * stage_rows, ROW_TILES, LANES), BF16),
                            pltpu.VMEM((ROW_TILES * stride, LANES), F32),
                            pltpu.SemaphoreType.DMA((2,))]),
        out_shape=jax.ShapeDtypeStruct((t, D_MODEL), F32),
        compiler_params=_cparams(("arbitrary",), 48),
        name="combine",
    )(src_flat, n_total, lidx, gate, ys)


def _final_kernel(h_ref, x1_ref, rt_ref, g2_ref, wsg_ref, wsu_ref, wsd_ref, gf_ref, y_ref):
    tm = h_ref.shape[0]
    n_split = max(tm // (2 * MIX_ROWS), 1)
    rows = tm // n_split
    for part in range(n_split):
        rs = slice(part * rows, (part + 1) * rows)
        h = h_ref[rs, :]
        a = _silu(jnp.dot(h, wsg_ref[...], preferred_element_type=F32))
        b = jnp.dot(h, wsu_ref[...], preferred_element_type=F32)
        shared = jnp.dot((a * b).astype(BF16), wsd_ref[...], preferred_element_type=F32)
        g2 = g2_ref[0] if len(g2_ref.shape) == 3 else g2_ref[rs, :]
        xo = x1_ref[rs, :] + g2 * (shared + rt_ref[rs, :])
        y_ref[rs, :] = xo * lax.rsqrt(jnp.mean(xo * xo, axis=-1, keepdims=True) + RMS_EPS) * gf_ref[...]


def _final(h2, x1, routed, routed_row0, mods, rows_per_batch, w_sg, w_su, w_sd, gf, tm):
    t = x1.shape[0]
    row = lambda m: (m, 0)
    assert routed_row0 % tm == 0
    block0 = routed_row0 // tm
    return pl.pallas_call(
        _final_kernel,
        grid=(t // tm,),
        in_specs=[pl.BlockSpec((tm, D_MODEL), row), pl.BlockSpec((tm, D_MODEL), row),
                  pl.BlockSpec((tm, D_MODEL), lambda m: (m + block0, 0)),
                  _mod_spec(mods, 5, rows_per_batch, tm),
                  _const_spec((D_MODEL, D_EXPERT)), _const_spec((D_MODEL, D_EXPERT)),
                  _const_spec((D_EXPERT, D_MODEL)), _const_spec((1, D_MODEL))],
        out_specs=pl.BlockSpec((tm, D_MODEL), row),
        out_shape=jax.ShapeDtypeStruct((t, D_MODEL), F32),
        compiler_params=_cparams(("arbitrary",), 48),
        name="final",
    )(h2, x1, routed, mods, w_sg, w_su, w_sd, gf)


def _moe_plan(top_e, t):
    ar = jnp.arange(N_EXPERTS, dtype=I32)
    sel = top_e[:, :, None] == ar
    onehot = jnp.any(sel, axis=1).astype(I32)
    assert t % LANES == 0
    groups = onehot.astype(F32).reshape(t // LANES, LANES, N_EXPERTS)
    within = jnp.einsum("ab,nbe->nae", jnp.tril(jnp.ones((LANES, LANES), F32)), groups)
    totals = within[:, -1, :]
    csum = (within + (jnp.cumsum(totals, axis=0) - totals)[:, None, :]).reshape(t, N_EXPERTS).astype(I32)
    excl = csum - onehot
    counts = csum[-1]
    padded = (counts + MOE_ROWS - 1) // MOE_ROWS * MOE_ROWS
    pend = jnp.cumsum(padded)
    pstart = pend - padded
    pick = lambda per_expert: jnp.sum(jnp.where(sel, per_expert, 0), axis=-1)
    rank = pick(excl[:, None, :])
    n_blocks = -(-(t * TOP_K + N_EXPERTS * (MOE_ROWS - 1)) // MOE_ROWS)
    block_row = jnp.arange(n_blocks, dtype=I32) * MOE_ROWS
    block_expert = jnp.minimum(jnp.sum((pend[None, :] <= block_row[:, None]).astype(I32), axis=1),
                               N_EXPERTS - 1)
    n_used = (pend[-1] // MOE_ROWS).astype(I32).reshape(1)
    first = jnp.concatenate([jnp.ones((1,), I32), (block_expert[1:] != block_expert[:-1]).astype(I32)])
    later = jnp.logical_and(counts[None, :] > 0, ar[None, :] > ar[:, None])
    next_nonempty = jnp.min(jnp.where(later, ar[None, :], N_EXPERTS), axis=1)
    next_nonempty = jnp.where(next_nonempty < N_EXPERTS, next_nonempty, -1)
    next_expert = jnp.sum(jnp.where(block_expert[:, None] == ar[None, :], next_nonempty[None, :], 0), axis=1)
    assert t + MOE_ROWS <= 1 << TOKEN_BITS
    real = ((top_e << TOKEN_BITS) | jnp.arange(t, dtype=I32)[:, None]).reshape(-1)
    r = jnp.arange(MOE_ROWS - 1, dtype=I32)
    int_max = jnp.iinfo(jnp.int32).max
    pad_keys = jnp.where(r[None, :] < (padded - counts)[:, None],
                         (ar[:, None] << TOKEN_BITS) | (t + r[None, :]), int_max).reshape(-1)
    n_rows = n_blocks * MOE_ROWS
    fill = jnp.full((n_rows - real.shape[0] - pad_keys.shape[0],), int_max, I32)
    tokv = jnp.sort(jnp.concatenate([real, pad_keys, fill])) & ((1 << TOKEN_BITS) - 1)
    tok_buf = jnp.where(tokv < t, tokv, 0)
    lo = excl[::COMB_TOK]
    hi = jnp.concatenate([lo[1:], counts[None]], axis=0)
    n_chunks = (hi - lo + COMB_CH - 1) // COMB_CH
    cend = jnp.cumsum(n_chunks, axis=1)
    cstart = cend - n_chunks
    n_total = cend[:, -1].astype(I32)
    src_start = pstart[None, :] + lo
    q = jnp.arange(COMB_MAX_CHUNKS, dtype=I32)
    eq = jnp.minimum(jnp.sum((cend[:, None, :] <= q[None, :, None]).astype(I32), axis=-1), N_EXPERTS - 1)
    sel_q = eq[:, :, None] == ar
    pick_q = lambda v: jnp.sum(jnp.where(sel_q, v[:, None, :], 0), axis=-1)
    src_flat = pick_q(src_start) + (q[None, :] - pick_q(cstart)) * COMB_CH
    src_flat = jnp.where(q[None, :] < n_total[:, None], src_flat, 0)
    stage_rows = COMB_MAX_CHUNKS * COMB_CH
    n_tiles = lo.shape[0]
    t_pad = n_tiles * COMB_TOK
    slot_base = (jnp.arange(n_tiles, dtype=I32) % 2) * stage_rows
    local = jnp.repeat(cstart * COMB_CH - lo + slot_base[:, None], COMB_TOK, axis=0)[:t]
    lidx = pick(local[:, None, :]) + rank
    lidx = jnp.concatenate([lidx, jnp.broadcast_to(slot_base[-1], (t_pad - t, TOP_K))], axis=0)
    return dict(block_expert=block_expert.astype(I32), n_used=n_used, tok_buf=tok_buf.astype(I32),
                first=first, next_expert=next_expert.astype(I32),
                n_blocks=n_blocks, src_flat=src_flat.astype(I32), n_total=n_total,
                lidx=lidx.astype(I32), stage_rows=stage_rows, t_pad=t_pad)


def kernel(x_prompt, x_sample, state_ssm_re, state_ssm_im, state_conv, c_prompt, c_sample, w_ada, b_ada, norm1_g, norm2_g, w_in, ssm_a_re, ssm_a_im, ssm_log_dt, ssm_b_re, ssm_b_im, ssm_c_re, ssm_c_im, ssm_d, w_glu, b_glu, w_lift_a, conv_w, conv_b, w_lift_b, w_out, w_router, router_bias, w_exp_gate, w_exp_up, w_exp_down, w_sh_gate, w_sh_up, w_sh_down, final_norm_g):
    depth = w_ada.shape[0]
    assert depth == 1
    bp, seq, _ = x_prompt.shape
    bs = x_sample.shape[0]
    tp = bp * seq
    l = 0

    w_in_b = w_in[l].astype(BF16)
    wr_t = w_router[l].T
    wr_hi = wr_t.astype(BF16)
    wr_lo = (wr_t - wr_hi.astype(F32)).astype(BF16)
    mix_w = (ssm_d[l].reshape(1, -1), w_glu[l].astype(BF16), b_glu[l].reshape(1, -1),
             w_lift_a[l].astype(BF16), conv_w[l], conv_b[l].reshape(1, -1), w_lift_b[l].astype(BF16),
             w_out[l].astype(BF16), norm2_g[l].reshape(1, -1), wr_hi, wr_lo)
    g1 = norm1_g[l].reshape(1, -1)

    ada = _ada(jnp.concatenate([c_prompt, c_sample], axis=0), w_ada[l], b_ada[l])
    mods_p = ada[:bp].reshape(bp, 1, -1)
    mods_s = ada[bp:]

    apow, ssm_tm, ssm_bk, ssm_ck = _ssm_prep(ssm_a_re[l], ssm_a_im[l], ssm_log_dt[l], ssm_b_re[l],
                                             ssm_b_im[l], ssm_c_re[l], ssm_c_im[l])
    a1_re, a1_im, al_re, al_im = (_state_layout(apow[i]) for i in range(4))

    xp = x_prompt.reshape(tp, D_MODEL)
    u_p, rest_p = _in_proj(xp, mods_p, seq, g1, w_in_b, tm=256)
    n_chunks = seq // SSM_CHUNK
    ssm_rows = 2 * n_chunks
    s_loc = _ssm_state(u_p, ssm_bk, SSM_CHUNK, ssm_rows)
    s_prev, s_fin_p = _ssm_scan(s_loc.reshape(bp, n_chunks, -1), al_re, al_im,
                                jnp.zeros((bp, OCTETS * OCT_STATE), F32))
    y_p = _ssm_out(u_p, s_prev.reshape(bp * n_chunks, -1), ssm_tm, ssm_ck, SSM_CHUNK, ssm_rows)
    x1_p, h2_p, lg_p, conv_p = _mixer_post(xp, y_p, u_p, rest_p, mods_p, seq, mix_w, None, tm=256)

    xs = x_sample.reshape(bs, D_MODEL)
    u_s, rest_s = _in_proj(xs, mods_s, 1, g1, w_in_b, tm=bs)
    s0_s = _pack_state(state_ssm_re[l], state_ssm_im[l])
    s_new_s = _ssm_step(_ssm_state(u_s, ssm_bk, 1, bs), s0_s, a1_re, a1_im)
    y_s = _ssm_out(u_s, s0_s, ssm_tm, ssm_ck, 1, bs)
    x1_s, h2_s, lg_s, nc0, nc1 = _mixer_post(xs, y_s, u_s, rest_s, mods_s, 1, mix_w,
                                                    state_conv[l], tm=bs)

    t = tp + bs
    te_p, gate_p = _route(lg_p, router_bias[l])
    te_s, gate_s = _route(lg_s, router_bias[l])
    top_e = jnp.concatenate([te_p, te_s], axis=1).T
    gate = jnp.concatenate([gate_p, gate_s], axis=1).T
    plan = _moe_plan(top_e, t)
    h_tiles = jnp.concatenate([h2_p, h2_s], axis=0).reshape(t, ROW_TILES, LANES)
    xs_sorted = _dispatch(h_tiles, plan["tok_buf"], plan["n_used"], plan["n_blocks"])
    ys = _experts(xs_sorted, plan["block_expert"], plan["n_used"], plan["first"], plan["next_expert"],
                  w_exp_gate[l], w_exp_up[l], w_exp_down[l], plan["n_blocks"])
    gate_pad = jnp.concatenate([gate, jnp.zeros((plan["t_pad"] - t, TOP_K), F32)], axis=0)
    routed = _combine(ys, plan["src_flat"], plan["n_total"], plan["lidx"].reshape(-1),
                      gate_pad.reshape(-1), plan["t_pad"], plan["stage_rows"])

    w_sg, w_su, w_sd = w_sh_gate[l].astype(BF16), w_sh_up[l].astype(BF16), w_sh_down[l].astype(BF16)
    gf = final_norm_g.reshape(1, -1)
    y_prompt = _final(h2_p, x1_p, routed, 0, mods_p, seq, w_sg, w_su, w_sd, gf, tm=512)
    y_sample = _final(h2_s, x1_s, routed, tp, mods_s, 1, w_sg, w_su, w_sd, gf, tm=bs)

    re_p, im_p = _unpack_state(s_fin_p)
    re_s, im_s = _unpack_state(s_new_s)
    return (y_prompt.reshape(bp, seq, D_MODEL), y_sample.reshape(bs, 1, D_MODEL),
            re_p[None], im_p[None], conv_p[None],
            re_s[None], im_s[None], jnp.stack([nc0, nc1], axis=1)[None])
```

```python
import functools

import jax
import jax.numpy as jnp
from jax import lax
from jax.experimental import pallas as pl
from jax.experimental.pallas import tpu as pltpu

F32 = jnp.float32
BF16 = jnp.bfloat16
I32 = jnp.int32

D_MODEL = 2048
D_SSM = 1024
SSM_GROUP = 16
N_GROUPS = 64
SSM_STATE = 64
D_CONV = 1024
CONV_W = 3
N_EXPERTS = 64
TOP_K = 8
N_EXPERT_GROUPS = 8
EXPERTS_PER_GROUP = 8
TOPK_GROUPS = 4
D_EXPERT = 512
ROUTED_SCALE = 2.5
RMS_EPS = 1e-6

LANES = 128
SUBLANES = 8
OCTETS = N_GROUPS * SSM_GROUP // LANES
OCT_STATE = 2 * (N_GROUPS // OCTETS) * SSM_STATE
SSM_CHUNK = 8
MOE_ROWS = 256
COMB_TOK = 128
COMB_CH = 16
ROW_TILES = D_MODEL // LANES
ROW_STRIDE = MOE_ROWS + SUBLANES
DISP_ROWS = 4 * MOE_ROWS
DISP_STRIDE = DISP_ROWS + SUBLANES
COMB_MAX_CHUNKS = COMB_TOK * TOP_K // COMB_CH + N_EXPERTS
TOKEN_BITS = 14
MIX_ROWS = 128
NEG_INF = float("-inf")


def _cparams(sem, vmem_mb):
    return pltpu.CompilerParams(dimension_semantics=sem, vmem_limit_bytes=vmem_mb * 1024 * 1024)


def _const_spec(shape):
    nd = len(shape)
    return pl.BlockSpec(shape, lambda *_: (0,) * nd, pipeline_mode=pl.Buffered(1))


def _mod(ref):
    v = ref[...]
    return v[0] if v.ndim == 3 else v


def _silu(x):
    return x * jax.nn.sigmoid(x)


def _ada_kernel(c_ref, w_ref, b_ref, o_ref):
    a = _silu(c_ref[...]).astype(BF16)
    o_ref[...] = jnp.dot(a, w_ref[...].astype(BF16), preferred_element_type=F32) + b_ref[...]


def _ada(c_all, w_ada, b_ada):
    m = c_all.shape[0]
    n = w_ada.shape[1]
    tn = 1024
    return pl.pallas_call(
        _ada_kernel,
        grid=(n // tn,),
        in_specs=[pl.BlockSpec((m, D_MODEL), lambda j: (0, 0)),
                  pl.BlockSpec((D_MODEL, tn), lambda j: (0, j)),
                  pl.BlockSpec((1, tn), lambda j: (0, j))],
        out_specs=pl.BlockSpec((m, tn), lambda j: (0, j)),
        out_shape=jax.ShapeDtypeStruct((m, n), F32),
        compiler_params=_cparams(("arbitrary",), 40),
        name="ada",
    )(c_all, w_ada, b_ada.reshape(1, n))


def _mod_spec(mods, col, rows_per_batch, tm):
    if mods.ndim == 3:
        tiles_per_batch = rows_per_batch // tm
        return pl.BlockSpec((1, 1, D_MODEL), lambda m, *_: (m // tiles_per_batch, 0, col))
    return pl.BlockSpec((tm, D_MODEL), lambda m, *_: (m, col))


def _inproj_kernel(x_ref, g_ref, sh_ref, sc_ref, w_ref, u_ref, r_ref):
    x = x_ref[...]
    y = x * lax.rsqrt(jnp.mean(x * x, axis=-1, keepdims=True) + RMS_EPS) * g_ref[...]
    h = (y * (1.0 + _mod(sc_ref)) + _mod(sh_ref)).astype(BF16)
    tn = D_SSM
    u_ref[...] = jnp.dot(h, w_ref[:, 0:tn], preferred_element_type=F32)
    for n in range(1, w_ref.shape[1] // tn):
        r_ref[:, (n - 1) * tn:n * tn] = jnp.dot(h, w_ref[:, n * tn:(n + 1) * tn],
                                                preferred_element_type=F32).astype(BF16)


def _in_proj(x, mods, rows_per_batch, g1, w_in, tm):
    t = x.shape[0]
    d_in = w_in.shape[1]
    n_rest = d_in - D_SSM
    return pl.pallas_call(
        _inproj_kernel,
        grid=(t // tm,),
        in_specs=[pl.BlockSpec((tm, D_MODEL), lambda m: (m, 0)),
                  _const_spec((1, D_MODEL)),
                  _mod_spec(mods, 0, rows_per_batch, tm),
                  _mod_spec(mods, 1, rows_per_batch, tm),
                  _const_spec((D_MODEL, d_in))],
        out_specs=[pl.BlockSpec((tm, D_SSM), lambda m: (m, 0)),
                   pl.BlockSpec((tm, n_rest), lambda m: (m, 0))],
        out_shape=[jax.ShapeDtypeStruct((t, D_SSM), F32),
                   jax.ShapeDtypeStruct((t, n_rest), BF16)],
        compiler_params=_cparams(("arbitrary",), 56),
        name="in_proj",
    )(x, g1, mods, mods, w_in)


def _ssm_prep_kernel(are_ref, aim_ref, ldt_ref, btr_ref, bti_ref, cre_ref, cim_ref,
                     apow_ref, tm_ref, bk_ref, ck_ref):
    lc = SSM_CHUNK
    ng = N_GROUPS // OCTETS
    half = OCT_STATE // 2
    lr = are_ref[...]
    li = aim_ref[...]
    dt = jnp.exp(ldt_ref[...])
    mag = jnp.exp(lr * dt)
    ar = mag * jnp.cos(li * dt)
    ai = mag * jnp.sin(li * dt)
    den = lr * lr + li * li
    xr = ar - 1.0
    cr = (xr * lr + ai * li) / den
    ci = (ai * lr - xr * li) / den
    btr = btr_ref[...]
    bti = bti_ref[...]
    bbr = cr * btr - ci * bti
    bbi = cr * bti + ci * btr
    cre = cre_ref[...]
    cim = cim_ref[...]
    cre_w = jnp.concatenate([cre] * ng, axis=1)
    cim_w = jnp.concatenate([cim] * ng, axis=1)

    def iota(shape, dim):
        return lax.broadcasted_iota(I32, shape, dim)

    width = lc * LANES
    sel_b = ((iota((SSM_STATE, half), 1) & (SSM_STATE - 1)) == iota((SSM_STATE, half), 0)).astype(BF16)
    mask_b = ((iota((width, half), 0) & (LANES - 1)) >> 4) == (iota((width, half), 1) >> 6)
    sel_c = ((iota((half, SSM_STATE), 0) & (SSM_STATE - 1)) == iota((half, SSM_STATE), 1)).astype(BF16)
    mask_c = (iota((half, width), 0) >> 6) == ((iota((half, width), 1) & (LANES - 1)) >> 4)
    mask_t = (iota((LANES, LANES), 0) >> 4) == (iota((LANES, LANES), 1) >> 4)
    dn_batch = (((2,), (2,)), ((0,), (0,)))
    dn_nt = (((1,), (1,)), ((), ()))

    pr = jnp.ones_like(ar)
    pi = jnp.zeros_like(ar)
    w1r, w1i, w2r, w2i = [], [], [], []
    for l in range(lc):
        w1r.append(pr * bbr - pi * bbi)
        w1i.append(pr * bbi + pi * bbr)
        pr, pi = pr * ar - pi * ai, pr * ai + pi * ar
        w2r.append(cre * pr - cim * pi)
        w2i.append(-(cre * pi + cim * pr))
        if l == 0:
            apow_ref[0] = pr
            apow_ref[1] = pi
    apow_ref[2] = pr
    apow_ref[3] = pi

    def rows(ws, order):
        return jnp.concatenate([ws[l] for l in order], axis=0).reshape(width, SSM_STATE).astype(BF16)

    rev = range(lc - 1, -1, -1)
    for c, ws in ((0, w1r), (1, w1i)):
        rep = jnp.dot(rows(ws, rev), sel_b, preferred_element_type=F32)
        bk_ref[:, c * half:(c + 1) * half] = jnp.where(mask_b, rep, 0.0).astype(BF16)
    for c, ws in ((0, w2r), (1, w2i)):
        rep = lax.dot_general(sel_c, rows(ws, range(lc)), dn_nt, preferred_element_type=F32)
        ck_ref[c * half:(c + 1) * half, :] = jnp.where(mask_c, rep, 0.0).astype(BF16)
    w1r_all = jnp.concatenate(w1r, axis=1)
    w1i_all = jnp.concatenate(w1i, axis=1)
    kt = (lax.dot_general(w1r_all, cre_w, dn_batch, precision=lax.Precision.HIGHEST,
                          preferred_element_type=F32)
          - lax.dot_general(w1i_all, cim_w, dn_batch, precision=lax.Precision.HIGHEST,
                            preferred_element_type=F32))
    hh = SSM_GROUP
    t_blocks = [jnp.where(mask_t, kt[:, l * hh:(l + 1) * hh, :].reshape(LANES, LANES), 0.0).astype(BF16)
                for l in range(lc)]
    zero = jnp.zeros((LANES, LANES), BF16)
    for s in range(lc):
        for t in range(lc):
            tm_ref[s * LANES:(s + 1) * LANES, t * LANES:(t + 1) * LANES] = t_blocks[t - s] if t >= s else zero


def _ssm_prep(a_re, a_im, log_dt, b_re, b_im, c_re, c_im):
    g, p, h = N_GROUPS, SSM_STATE, SSM_GROUP
    ng = g // OCTETS
    width = SSM_CHUNK * LANES
    vec = pl.BlockSpec((ng, 1, p), lambda o: (o, 0, 0))
    mat = pl.BlockSpec((ng, h, p), lambda o: (o, 0, 0))
    big = lambda rows, cols: pl.BlockSpec((None, rows, cols), lambda o: (o, 0, 0))
    return pl.pallas_call(
        _ssm_prep_kernel,
        grid=(OCTETS,),
        in_specs=[vec, vec, pl.BlockSpec((ng, 1, 1), lambda o: (o, 0, 0)), mat, mat, mat, mat],
        out_specs=[pl.BlockSpec((4, ng, 1, p), lambda o: (0, o, 0, 0)),
                   big(width, width), big(width, OCT_STATE), big(OCT_STATE, width)],
        out_shape=[jax.ShapeDtypeStruct((4, g, 1, p), F32),
                   jax.ShapeDtypeStruct((OCTETS, width, width), BF16),
                   jax.ShapeDtypeStruct((OCTETS, width, OCT_STATE), BF16),
                   jax.ShapeDtypeStruct((OCTETS, OCT_STATE, width), BF16)],
        compiler_params=_cparams(("arbitrary",), 32),
        name="ssm_prep",
    )(a_re.reshape(g, 1, p), a_im.reshape(g, 1, p), log_dt.reshape(g, 1, 1),
      b_re.transpose(0, 2, 1), b_im.transpose(0, 2, 1), c_re, c_im)


def _state_layout(v):
    return v.reshape(OCTETS, 1, OCT_STATE // 2)


def _load_chunks(u_ref, lc, rb):
    parts = [u_ref[pl.ds(s, rb, stride=lc), :] if lc > 1 else u_ref[...] for s in range(lc)]
    u = parts[0] if lc == 1 else jnp.concatenate(parts, axis=-1)
    return u.astype(BF16)


def _ssm_state_kernel(u_ref, bk_ref, s_ref, *, lc, rb):
    s_ref[...] = jnp.dot(_load_chunks(u_ref, lc, rb), bk_ref[...], preferred_element_type=F32)


def _ssm_state(u, bk, lc, rb):
    t = u.shape[0]
    nb = t // (rb * lc)
    lag0_block = SSM_CHUNK // lc - 1
    return pl.pallas_call(
        functools.partial(_ssm_state_kernel, lc=lc, rb=rb),
        grid=(OCTETS, nb),
        in_specs=[pl.BlockSpec((rb * lc, LANES), lambda o, b: (b, o)),
                  pl.BlockSpec((None, lc * LANES, OCT_STATE), lambda o, b: (o, lag0_block, 0))],
        out_specs=pl.BlockSpec((rb, OCT_STATE), lambda o, b: (b, o)),
        out_shape=jax.ShapeDtypeStruct((t // lc, OCTETS * OCT_STATE), F32),
        compiler_params=_cparams(("arbitrary", "arbitrary"), 32),
        name="ssm_state",
    )(u, bk)


def _ssm_scan_kernel(sl_ref, ar_ref, ai_ref, s0_ref, sp_ref, sf_ref, *, nb, nc):
    half = OCT_STATE // 2
    ar = ar_ref[0]
    ai = ai_ref[0]
    init = tuple((s0_ref[b:b + 1, 0:half], s0_ref[b:b + 1, half:OCT_STATE]) for b in range(nb))

    def body(c, carry):
        new = []
        for b in range(nb):
            sr, si = carry[b]
            sp_ref[b, pl.ds(c, 1), 0:half] = sr
            sp_ref[b, pl.ds(c, 1), half:OCT_STATE] = si
            xr = sl_ref[b, pl.ds(c, 1), 0:half]
            xi = sl_ref[b, pl.ds(c, 1), half:OCT_STATE]
            new.append((ar * sr - ai * si + xr, ar * si + ai * sr + xi))
        return tuple(new)

    fin = lax.fori_loop(0, nc, body, init)
    for b in range(nb):
        sf_ref[b:b + 1, 0:half] = fin[b][0]
        sf_ref[b:b + 1, half:OCT_STATE] = fin[b][1]


def _ssm_scan(s_loc, al_re, al_im, s0):
    nb, nc, width = s_loc.shape
    return pl.pallas_call(
        functools.partial(_ssm_scan_kernel, nb=nb, nc=nc),
        grid=(OCTETS,),
        in_specs=[pl.BlockSpec((nb, nc, OCT_STATE), lambda o: (0, 0, o)),
                  pl.BlockSpec((1, 1, OCT_STATE // 2), lambda o: (o, 0, 0)),
                  pl.BlockSpec((1, 1, OCT_STATE // 2), lambda o: (o, 0, 0)),
                  pl.BlockSpec((nb, OCT_STATE), lambda o: (0, o))],
        out_specs=[pl.BlockSpec((nb, nc, OCT_STATE), lambda o: (0, 0, o)),
                   pl.BlockSpec((nb, OCT_STATE), lambda o: (0, o))],
        out_shape=[jax.ShapeDtypeStruct((nb, nc, width), F32),
                   jax.ShapeDtypeStruct((nb, width), F32)],
        compiler_params=_cparams(("arbitrary",), 32),
        name="ssm_scan",
    )(s_loc, al_re, al_im, s0)


def _ssm_out_kernel(u_ref, sp_ref, tm_ref, ck_ref, y_ref, *, lc, rb):
    y = (jnp.dot(_load_chunks(u_ref, lc, rb), tm_ref[...], preferred_element_type=F32)
         + jnp.dot(sp_ref[...].astype(BF16), ck_ref[...], preferred_element_type=F32))
    if lc == 1:
        y_ref[...] = y
    else:
        for t in range(lc):
            y_ref[pl.ds(t, rb, stride=lc), :] = y[:, t * LANES:(t + 1) * LANES]


def _ssm_out(u, s_prev, tm, ck, lc, rb):
    t = u.shape[0]
    nb = t // (rb * lc)
    return pl.pallas_call(
        functools.partial(_ssm_out_kernel, lc=lc, rb=rb),
        grid=(OCTETS, nb),
        in_specs=[pl.BlockSpec((rb * lc, LANES), lambda o, b: (b, o)),
                  pl.BlockSpec((rb, OCT_STATE), lambda o, b: (b, o)),
                  pl.BlockSpec((None, lc * LANES, lc * LANES), lambda o, b: (o, 0, 0)),
                  pl.BlockSpec((None, OCT_STATE, lc * LANES), lambda o, b: (o, 0, 0))],
        out_specs=pl.BlockSpec((rb * lc, LANES), lambda o, b: (b, o)),
        out_shape=jax.ShapeDtypeStruct((t, D_SSM), F32),
        compiler_params=_cparams(("arbitrary", "arbitrary"), 48),
        name="ssm_out",
    )(u, s_prev, tm, ck)


def _ssm_step_kernel(sl_ref, s0_ref, ar_ref, ai_ref, sn_ref):
    half = OCT_STATE // 2
    ar = ar_ref[0]
    ai = ai_ref[0]
    sr = s0_ref[:, 0:half]
    si = s0_ref[:, half:OCT_STATE]
    sn_ref[:, 0:half] = ar * sr - ai * si + sl_ref[:, 0:half]
    sn_ref[:, half:OCT_STATE] = ar * si + ai * sr + sl_ref[:, half:OCT_STATE]


def _ssm_step(s_loc, s0, a_re, a_im):
    rows, width = s_loc.shape
    return pl.pallas_call(
        _ssm_step_kernel,
        grid=(OCTETS,),
        in_specs=[pl.BlockSpec((rows, OCT_STATE), lambda o: (0, o)),
                  pl.BlockSpec((rows, OCT_STATE), lambda o: (0, o)),
                  pl.BlockSpec((1, 1, OCT_STATE // 2), lambda o: (o, 0, 0)),
                  pl.BlockSpec((1, 1, OCT_STATE // 2), lambda o: (o, 0, 0))],
        out_specs=pl.BlockSpec((rows, OCT_STATE), lambda o: (0, o)),
        out_shape=jax.ShapeDtypeStruct((rows, width), F32),
        compiler_params=_cparams(("arbitrary",), 32),
        name="ssm_step",
    )(s_loc, s0, a_re, a_im)


def _pack_state(s_re, s_im):
    b = s_re.shape[0]
    half = OCT_STATE // 2
    return jnp.stack([s_re.reshape(b, OCTETS, half), s_im.reshape(b, OCTETS, half)], axis=2).reshape(b, -1)


def _unpack_state(s):
    b = s.shape[0]
    s4 = s.reshape(b, OCTETS, 2, OCT_STATE // 2)
    return (s4[:, :, 0].reshape(b, N_GROUPS, SSM_STATE), s4[:, :, 1].reshape(b, N_GROUPS, SSM_STATE))


def _mixer_kernel(*refs, seq, tm, tiles_per_batch):
    (x_ref, ys_ref, u_ref, r_ref, g1_ref, sh2_ref, sc2_ref, d_ref, wglu_ref, bglu_ref, wla_ref,
     cw_ref, cb_ref, wlb_ref, wout_ref, n2g_ref, wrh_ref, wrl_ref) = refs[:18]
    if seq:
        x1_ref, h2_ref, lg_ref, nc_ref, carry = refs[18:]
    else:
        cb0_ref, cb1_ref, x1_ref, h2_ref, lg_ref, nc0_ref, nc1_ref = refs[18:]

    xb = r_ref[:, 0:D_CONV].astype(F32)
    gb = r_ref[:, D_CONV:2 * D_CONV].astype(F32)
    gc = r_ref[:, 2 * D_CONV:3 * D_CONV].astype(F32)
    v = gc * xb
    if seq:
        first = pl.program_id(0) % tiles_per_batch == 0

        @pl.when(first)
        def _():
            carry[...] = jnp.zeros_like(carry)

        c0 = carry[0:1, :]
        c1 = carry[1:2, :]
        row = lax.broadcasted_iota(I32, (tm, 1), 0)
        v1 = jnp.where(row == 0, c1, pltpu.roll(v, 1, 0))
        v2 = jnp.where(row == 0, c0, jnp.where(row == 1, c1, pltpu.roll(v, 2, 0)))
        tail = v[tm - 2:tm, :]
        carry[...] = tail
        nc_ref[0] = tail
    else:
        v2 = cb0_ref[...]
        v1 = cb1_ref[...]
        nc0_ref[...] = v1
        nc1_ref[...] = v
    conv = cb_ref[...] + cw_ref[0:1, :] * v2 + cw_ref[1:2, :] * v1 + cw_ref[2:3, :] * v
    gated_conv = (gb * conv).astype(BF16)

    def mod_rows(ref, rs):
        return ref[0] if len(ref.shape) == 3 else ref[rs, :]

    n_split = max(tm // MIX_ROWS, 1)
    rows = tm // n_split
    off = 3 * D_CONV
    dn = (((1,), (1,)), ((), ()))
    for part in range(n_split):
        rs = slice(part * rows, (part + 1) * rows)
        ya = jax.nn.gelu(ys_ref[rs, :] + d_ref[...] * u_ref[rs, :], approximate=True)
        z = jnp.dot(ya.astype(BF16), wglu_ref[...], preferred_element_type=F32) + bglu_ref[...]
        ya = ya * jax.nn.sigmoid(z)
        branch_a = jnp.dot(ya.astype(BF16), wla_ref[...], preferred_element_type=F32)
        branch_b = jnp.dot(gated_conv[rs], wlb_ref[...], preferred_element_type=F32)
        gla = r_ref[rs, off:off + D_MODEL].astype(F32)
        glb = r_ref[rs, off + D_MODEL:off + 2 * D_MODEL].astype(F32)
        merged = jax.nn.sigmoid(gla) * branch_a + jax.nn.sigmoid(glb) * branch_b
        x1 = x_ref[rs, :] + mod_rows(g1_ref, rs) * jnp.dot(merged.astype(BF16), wout_ref[...],
                                                           preferred_element_type=F32)
        x1_ref[rs, :] = x1

        y = x1 * lax.rsqrt(jnp.mean(x1 * x1, axis=-1, keepdims=True) + RMS_EPS) * n2g_ref[...]
        h2 = y * (1.0 + mod_rows(sc2_ref, rs)) + mod_rows(sh2_ref, rs)
        h2b = h2.astype(BF16)
        h2_ref[rs, :] = h2b

        h2l = (h2 - h2b.astype(F32)).astype(BF16)
        lg_ref[:, rs] = (lax.dot_general(wrh_ref[...], h2b, dn, preferred_element_type=F32)
                         + lax.dot_general(wrh_ref[...], h2l, dn, preferred_element_type=F32)
                         + lax.dot_general(wrl_ref[...], h2b, dn, preferred_element_type=F32))


def _mixer_post(x, y_ssm, u, rest, mods, rows_per_batch, weights, conv_state, tm):
    t = x.shape[0]
    seq = conv_state is None
    tiles_per_batch = rows_per_batch // tm
    (ssm_d, w_glu, b_glu, w_la, conv_w, conv_b, w_lb, w_out, n2g, wr_hi, wr_lo) = weights
    row = lambda m: (m, 0)
    in_specs = [pl.BlockSpec((tm, D_MODEL), row),
                pl.BlockSpec((tm, D_SSM), row),
                pl.BlockSpec((tm, D_SSM), row),
                pl.BlockSpec((tm, rest.shape[1]), row),
                _mod_spec(mods, 2, rows_per_batch, tm),
                _mod_spec(mods, 3, rows_per_batch, tm),
                _mod_spec(mods, 4, rows_per_batch, tm),
                _const_spec((1, D_SSM)), _const_spec((D_SSM, D_SSM)), _const_spec((1, D_SSM)),
                _const_spec((D_SSM, D_MODEL)), _const_spec((CONV_W, D_CONV)), _const_spec((1, D_CONV)),
                _const_spec((D_CONV, D_MODEL)), _const_spec((D_MODEL, D_MODEL)),
                _const_spec((1, D_MODEL)), _const_spec((N_EXPERTS, D_MODEL)),
                _const_spec((N_EXPERTS, D_MODEL))]
    args = [x, y_ssm, u, rest, mods, mods, mods, ssm_d, w_glu, b_glu, w_la, conv_w, conv_b, w_lb,
            w_out, n2g, wr_hi, wr_lo]
    out_specs = [pl.BlockSpec((tm, D_MODEL), row),
                 pl.BlockSpec((tm, D_MODEL), row),
                 pl.BlockSpec((N_EXPERTS, tm), lambda m: (0, m))]
    out_shape = [jax.ShapeDtypeStruct((t, D_MODEL), F32),
                 jax.ShapeDtypeStruct((t, D_MODEL), BF16),
                 jax.ShapeDtypeStruct((N_EXPERTS, t), F32)]
    scratch = []
    if seq:
        nb = t // rows_per_batch
        out_specs.append(pl.BlockSpec((1, CONV_W - 1, D_CONV), lambda m: (m // tiles_per_batch, 0, 0)))
        out_shape.append(jax.ShapeDtypeStruct((nb, CONV_W - 1, D_CONV), F32))
        scratch.append(pltpu.VMEM((CONV_W - 1, D_CONV), F32))
    else:
        in_specs += [pl.BlockSpec((tm, D_CONV), row), pl.BlockSpec((tm, D_CONV), row)]
        args += [conv_state[:, 0, :], conv_state[:, 1, :]]
        out_specs += [pl.BlockSpec((tm, D_CONV), row), pl.BlockSpec((tm, D_CONV), row)]
        out_shape += [jax.ShapeDtypeStruct((t, D_CONV), F32), jax.ShapeDtypeStruct((t, D_CONV), F32)]
    return pl.pallas_call(
        functools.partial(_mixer_kernel, seq=seq, tm=tm, tiles_per_batch=tiles_per_batch),
        grid=(t // tm,),
        in_specs=in_specs,
        out_specs=out_specs,
        out_shape=out_shape,
        scratch_shapes=scratch,
        compiler_params=_cparams(("arbitrary",), 56),
        name="mixer_post",
    )(*args)


def _route_kernel(lg_ref, bias_ref, te_ref, gate_ref):
    scores = [jax.nn.sigmoid(lg_ref[e]) for e in range(N_EXPERTS)]
    choice = [scores[e] + bias_ref[e] for e in range(N_EXPERTS)]
    shape = scores[0].shape

    def first_max(vals):
        m = functools.reduce(jnp.maximum, vals)
        found = jnp.zeros(shape, jnp.bool_)
        hot = []
        for v in vals:
            is_m = jnp.logical_and(v == m, jnp.logical_not(found))
            hot.append(is_m)
            found = jnp.logical_or(found, is_m)
        return m, hot

    gscore = []
    for g in range(N_EXPERT_GROUPS):
        vals = choice[g * EXPERTS_PER_GROUP:(g + 1) * EXPERTS_PER_GROUP]
        m1, hot = first_max(vals)
        rest = [jnp.where(h, NEG_INF, v) for h, v in zip(hot, vals)]
        gscore.append(m1 + functools.reduce(jnp.maximum, rest))

    gsel = [jnp.zeros(shape, jnp.bool_) for _ in range(N_EXPERT_GROUPS)]
    for _ in range(TOPK_GROUPS):
        _, hot = first_max(gscore)
        gsel = [jnp.logical_or(s, h) for s, h in zip(gsel, hot)]
        gscore = [jnp.where(h, NEG_INF, v) for h, v in zip(hot, gscore)]

    cand = [jnp.where(gsel[e // EXPERTS_PER_GROUP], choice[e], NEG_INF) for e in range(N_EXPERTS)]
    picked_e = []
    picked_s = []
    for _ in range(TOP_K):
        _, hot = first_max(cand)
        idx = jnp.zeros(shape, I32)
        sc = jnp.zeros(shape, F32)
        for e in range(N_EXPERTS):
            idx = jnp.where(hot[e], e, idx)
            sc = jnp.where(hot[e], scores[e], sc)
        cand = [jnp.where(h, NEG_INF, v) for h, v in zip(hot, cand)]
        picked_e.append(idx)
        picked_s.append(sc)
    total = functools.reduce(lambda a, b: a + b, picked_s)
    for k in range(TOP_K):
        te_ref[k] = picked_e[k]
        gate_ref[k] = picked_s[k] / total * ROUTED_SCALE


def _route(logits_t, bias):
    t = logits_t.shape[1]
    r = t // LANES
    rb = min(r, SUBLANES)
    lg = logits_t.reshape(N_EXPERTS, r, LANES)
    bias_b = jnp.broadcast_to(bias.astype(F32).reshape(N_EXPERTS, 1, 1), (N_EXPERTS, 1, LANES))
    te, gate = pl.pallas_call(
        _route_kernel,
        grid=(r // rb,),
        in_specs=[pl.BlockSpec((N_EXPERTS, rb, LANES), lambda i: (0, i, 0)),
                  pl.BlockSpec((N_EXPERTS, 1, LANES), lambda i: (0, 0, 0))],
        out_specs=[pl.BlockSpec((TOP_K, rb, LANES), lambda i: (0, i, 0)),
                   pl.BlockSpec((TOP_K, rb, LANES), lambda i: (0, i, 0))],
        out_shape=[jax.ShapeDtypeStruct((TOP_K, r, LANES), I32),
                   jax.ShapeDtypeStruct((TOP_K, r, LANES), F32)],
        compiler_params=_cparams(("arbitrary",), 32),
        name="route",
    )(lg, bias_b)
    return te.reshape(TOP_K, t), gate.reshape(TOP_K, t)


def _dispatch_kernel(nused_ref, tok_ref, h_ref, xs_ref, tile):
    used_rows = nused_ref[0] * MOE_ROWS
    live = pl.program_id(0) * DISP_ROWS < used_rows

    @pl.when(jnp.logical_not(live))
    def _():
        xs_ref[...] = jnp.zeros_like(xs_ref)

    @pl.when(live)
    def _():
        for mi in range(DISP_ROWS):
            tile[pl.ds(mi, ROW_TILES, stride=DISP_STRIDE), :] = h_ref[tok_ref[mi]].astype(F32)
        for j in range(ROW_TILES):
            xs_ref[:, j * LANES:(j + 1) * LANES] = (
                tile[j * DISP_STRIDE:j * DISP_STRIDE + DISP_ROWS, :].astype(BF16))


def _dispatch(h_tiles, tok_buf, n_used, n_blocks):
    n_rows = n_blocks * MOE_ROWS
    assert n_rows % DISP_ROWS == 0

    def last_live(i, nu):
        return jnp.minimum(i, (nu[0] * MOE_ROWS - 1) // DISP_ROWS)

    return pl.pallas_call(
        _dispatch_kernel,
        grid_spec=pltpu.PrefetchScalarGridSpec(
            num_scalar_prefetch=1,
            grid=(n_rows // DISP_ROWS,),
            in_specs=[pl.BlockSpec((DISP_ROWS,), lambda i, nu: (last_live(i, nu),),
                                   memory_space=pltpu.SMEM),
                      pl.BlockSpec(memory_space=pltpu.VMEM)],
            out_specs=pl.BlockSpec((DISP_ROWS, D_MODEL), lambda i, nu: (i, 0)),
            scratch_shapes=[pltpu.VMEM((ROW_TILES * DISP_STRIDE, LANES), F32)]),
        out_shape=jax.ShapeDtypeStruct((n_rows, D_MODEL), BF16),
        compiler_params=_cparams(("arbitrary",), 56),
        name="dispatch",
    )(n_used, tok_buf, h_tiles)


def _experts_kernel(be_ref, nused_ref, first_ref, next_ref, xs_ref, wg_hbm, wu_hbm, wd_hbm, ys_ref,
                    wg_f, wu_f, wd_f, wg_s, wu_s, wd_s, tile_a, tile_b, sem):
    i = pl.program_id(0)

    def weight_copies(e):
        return (pltpu.make_async_copy(wg_hbm.at[e], wg_f, sem.at[0]),
                pltpu.make_async_copy(wu_hbm.at[e], wu_f, sem.at[1]),
                pltpu.make_async_copy(wd_hbm.at[e], wd_f, sem.at[2]))

    n_used = nused_ref[0]

    @pl.when(i < n_used)
    def _():
        @pl.when(i == 0)
        def _():
            for c in weight_copies(be_ref[0]):
                c.start()

        @pl.when(first_ref[i] == 1)
        def _():
            for c in weight_copies(be_ref[i]):
                c.wait()
            wg_s[...] = wg_f[...].astype(BF16)
            wu_s[...] = wu_f[...].astype(BF16)
            wd_s[...] = wd_f[...].astype(BF16)

            @pl.when(next_ref[i] >= 0)
            def _():
                for c in weight_copies(next_ref[i]):
                    c.start()

    def ffn_to_tile(tile):
        x = xs_ref[...]
        g = jnp.dot(x, wg_s[...], preferred_element_type=F32)
        u = jnp.dot(x, wu_s[...], preferred_element_type=F32)
        h = (_silu(g) * u).astype(BF16)
        y = jnp.dot(h, wd_s[...], preferred_element_type=F32)
        for j in range(ROW_TILES):
            tile[j * ROW_STRIDE:j * ROW_STRIDE + MOE_ROWS, :] = y[:, j * LANES:(j + 1) * LANES]

    def tile_to_rows(tile):
        for mi in range(MOE_ROWS):
            ys_ref[mi] = tile[pl.ds(mi, ROW_TILES, stride=ROW_STRIDE), :].astype(BF16)

    even = i % 2 == 0
    both = jnp.logical_and(i >= 1, i < n_used)

    @pl.when(jnp.logical_and(both, even))
    def _():
        ffn_to_tile(tile_a)
        tile_to_rows(tile_b)

    @pl.when(jnp.logical_and(both, jnp.logical_not(even)))
    def _():
        ffn_to_tile(tile_b)
        tile_to_rows(tile_a)

    @pl.when(i == 0)
    def _():
        ffn_to_tile(tile_a)

    @pl.when(jnp.logical_and(i == n_used, even))
    def _():
        tile_to_rows(tile_b)

    @pl.when(jnp.logical_and(i == n_used, jnp.logical_not(even)))
    def _():
        tile_to_rows(tile_a)

    @pl.when(i > n_used)
    def _():
        ys_ref[...] = jnp.zeros_like(ys_ref)


def _experts(xs, block_expert, n_used, first, next_expert, w_gate, w_up, w_down, n_blocks):
    def xmap(i, be, nu, fi, nx):
        return (jnp.minimum(i, nu[0] - 1), 0)

    any_spec = pl.BlockSpec(memory_space=pl.ANY)
    return pl.pallas_call(
        _experts_kernel,
        grid_spec=pltpu.PrefetchScalarGridSpec(
            num_scalar_prefetch=4,
            grid=(n_blocks + 1,),
            in_specs=[pl.BlockSpec((MOE_ROWS, D_MODEL), xmap), any_spec, any_spec, any_spec],
            out_specs=pl.BlockSpec((MOE_ROWS, ROW_TILES, LANES),
                                   lambda i, *_: (jnp.maximum(i - 1, 0), 0, 0)),
            scratch_shapes=[pltpu.VMEM((D_MODEL, D_EXPERT), F32),
                            pltpu.VMEM((D_MODEL, D_EXPERT), F32),
                            pltpu.VMEM((D_EXPERT, D_MODEL), F32),
                            pltpu.VMEM((D_MODEL, D_EXPERT), BF16),
                            pltpu.VMEM((D_MODEL, D_EXPERT), BF16),
                            pltpu.VMEM((D_EXPERT, D_MODEL), BF16),
                            pltpu.VMEM((ROW_TILES * ROW_STRIDE, LANES), F32),
                            pltpu.VMEM((ROW_TILES * ROW_STRIDE, LANES), F32),
                            pltpu.SemaphoreType.DMA((3,))]),
        out_shape=jax.ShapeDtypeStruct((n_blocks * MOE_ROWS, ROW_TILES, LANES), BF16),
        compiler_params=_cparams(("arbitrary",), 48),
        name="experts",
    )(block_expert, n_used, first, next_expert, xs, w_gate, w_up, w_down)


def _combine_kernel(src_ref, ntot_ref, lidx_ref, gate_ref, ys_hbm, out_ref,
                    stage, tile, sem, *, stage_rows):
    i = pl.program_id(0)
    n_tiles = pl.num_programs(0)

    def chunk_copy(src, slot, dst):
        return pltpu.make_async_copy(ys_hbm.at[pl.ds(src, COMB_CH)],
                                     stage.at[pl.ds(slot * stage_rows + dst, COMB_CH)],
                                     sem.at[slot])

    def issue(tile_idx, slot):
        def c_body(q, _):
            chunk_copy(src_ref[tile_idx, q], slot, q * COMB_CH).start()
            return 0

        lax.fori_loop(0, ntot_ref[tile_idx], c_body, 0)

    @pl.when(i == 0)
    def _():
        issue(0, 0)

    @pl.when(i + 1 < n_tiles)
    def _():
        issue(i + 1, (i + 1) % 2)

    slot = i % 2

    def w_body(c, _):
        chunk_copy(0, slot, 0).wait()
        return 0

    lax.fori_loop(0, ntot_ref[i], w_body, 0)

    stride = COMB_TOK + SUBLANES

    for tk in range(COMB_TOK):
        acc = None
        for k in range(TOP_K):
            w = stage[lidx_ref[tk * TOP_K + k]].astype(F32)
            term = gate_ref[tk * TOP_K + k] * w
            acc = term if acc is None else acc + term
        tile[pl.ds(tk, ROW_TILES, stride=stride), :] = acc
    for j in range(ROW_TILES):
        out_ref[:, j * LANES:(j + 1) * LANES] = tile[j * stride:j * stride + COMB_TOK, :]


def _combine(ys, src_flat, n_total, lidx, gate, t, stage_rows):
    n_tiles = t // COMB_TOK
    stride = COMB_TOK + SUBLANES
    return pl.pallas_call(
        functools.partial(_combine_kernel, stage_rows=stage_rows),
        grid_spec=pltpu.PrefetchScalarGridSpec(
            num_scalar_prefetch=2,
            grid=(n_tiles,),
            in_specs=[pl.BlockSpec((COMB_TOK * TOP_K,), lambda i, *_: (i,), memory_space=pltpu.SMEM),
                      pl.BlockSpec((COMB_TOK * TOP_K,), lambda i, *_: (i,), memory_space=pltpu.SMEM),
                      pl.BlockSpec(memory_space=pl.ANY)],
            out_specs=pl.BlockSpec((COMB_TOK, D_MODEL), lambda i, *_: (i, 0)),
            scratch_shapes=[pltpu.VMEM((2 * stage_rows, ROW_TILES, LANES), BF16),
                            pltpu.VMEM((ROW_TILES * stride, LANES), F32),
                            pltpu.SemaphoreType.DMA((2,))]),
        out_shape=jax.ShapeDtypeStruct((t, D_MODEL), F32),
        compiler_params=_cparams(("arbitrary",), 48),
        name="combine",
    )(src_flat, n_total, lidx, gate, ys)


def _final_kernel(h_ref, x1_ref, rt_ref, g2_ref, wsg_ref, wsu_ref, wsd_ref, gf_ref, y_ref):
    h = h_ref[...]
    a = _silu(jnp.dot(h, wsg_ref[...], preferred_element_type=F32))
    b = jnp.dot(h, wsu_ref[...], preferred_element_type=F32)
    shared = jnp.dot((a * b).astype(BF16), wsd_ref[...], preferred_element_type=F32)
    xo = x1_ref[...] + _mod(g2_ref) * (shared + rt_ref[...])
    y_ref[...] = xo * lax.rsqrt(jnp.mean(xo * xo, axis=-1, keepdims=True) + RMS_EPS) * gf_ref[...]


def _final(h2, x1, routed, routed_row0, mods, rows_per_batch, w_sg, w_su, w_sd, gf, tm):
    t = x1.shape[0]
    row = lambda m: (m, 0)
    assert routed_row0 % tm == 0
    block0 = routed_row0 // tm
    return pl.pallas_call(
        _final_kernel,
        grid=(t // tm,),
        in_specs=[pl.BlockSpec((tm, D_MODEL), row), pl.BlockSpec((tm, D_MODEL), row),
                  pl.BlockSpec((tm, D_MODEL), lambda m: (m + block0, 0)),
                  _mod_spec(mods, 5, rows_per_batch, tm),
                  _const_spec((D_MODEL, D_EXPERT)), _const_spec((D_MODEL, D_EXPERT)),
                  _const_spec((D_EXPERT, D_MODEL)), _const_spec((1, D_MODEL))],
        out_specs=pl.BlockSpec((tm, D_MODEL), row),
        out_shape=jax.ShapeDtypeStruct((t, D_MODEL), F32),
        compiler_params=_cparams(("arbitrary",), 48),
        name="final",
    )(h2, x1, routed, mods, w_sg, w_su, w_sd, gf)


def _moe_plan(top_e, t):
    ar = jnp.arange(N_EXPERTS, dtype=I32)
    sel = top_e[:, :, None] == ar
    onehot = jnp.any(sel, axis=1).astype(I32)
    assert t % LANES == 0
    groups = onehot.astype(F32).reshape(t // LANES, LANES, N_EXPERTS)
    within = jnp.einsum("ab,nbe->nae", jnp.tril(jnp.ones((LANES, LANES), F32)), groups)
    totals = within[:, -1, :]
    csum = (within + (jnp.cumsum(totals, axis=0) - totals)[:, None, :]).reshape(t, N_EXPERTS).astype(I32)
    excl = csum - onehot
    counts = csum[-1]
    padded = (counts + MOE_ROWS - 1) // MOE_ROWS * MOE_ROWS
    pend = jnp.cumsum(padded)
    pstart = pend - padded
    pick = lambda per_expert: jnp.sum(jnp.where(sel, per_expert, 0), axis=-1)
    rank = pick(excl[:, None, :])
    n_blocks = -(-(t * TOP_K + N_EXPERTS * (MOE_ROWS - 1)) // MOE_ROWS)
    block_row = jnp.arange(n_blocks, dtype=I32) * MOE_ROWS
    block_expert = jnp.minimum(jnp.sum((pend[None, :] <= block_row[:, None]).astype(I32), axis=1),
                               N_EXPERTS - 1)
    n_used = (pend[-1] // MOE_ROWS).astype(I32).reshape(1)
    first = jnp.concatenate([jnp.ones((1,), I32), (block_expert[1:] != block_expert[:-1]).astype(I32)])
    later = jnp.logical_and(counts[None, :] > 0, ar[None, :] > ar[:, None])
    next_nonempty = jnp.min(jnp.where(later, ar[None, :], N_EXPERTS), axis=1)
    next_nonempty = jnp.where(next_nonempty < N_EXPERTS, next_nonempty, -1)
    next_expert = jnp.sum(jnp.where(block_expert[:, None] == ar[None, :], next_nonempty[None, :], 0), axis=1)
    assert t + MOE_ROWS <= 1 << TOKEN_BITS
    real = ((top_e << TOKEN_BITS) | jnp.arange(t, dtype=I32)[:, None]).reshape(-1)
    r = jnp.arange(MOE_ROWS - 1, dtype=I32)
    int_max = jnp.iinfo(jnp.int32).max
    pad_keys = jnp.where(r[None, :] < (padded - counts)[:, None],
                         (ar[:, None] << TOKEN_BITS) | (t + r[None, :]), int_max).reshape(-1)
    n_rows = n_blocks * MOE_ROWS
    fill = jnp.full((n_rows - real.shape[0] - pad_keys.shape[0],), int_max, I32)
    tokv = jnp.sort(jnp.concatenate([real, pad_keys, fill])) & ((1 << TOKEN_BITS) - 1)
    tok_buf = jnp.where(tokv < t, tokv, 0)
    lo = excl[::COMB_TOK]
    hi = jnp.concatenate([lo[1:], counts[None]], axis=0)
    n_chunks = (hi - lo + COMB_CH - 1) // COMB_CH
    cend = jnp.cumsum(n_chunks, axis=1)
    cstart = cend - n_chunks
    n_total = cend[:, -1].astype(I32)
    src_start = pstart[None, :] + lo
    q = jnp.arange(COMB_MAX_CHUNKS, dtype=I32)
    eq = jnp.minimum(jnp.sum((cend[:, None, :] <= q[None, :, None]).astype(I32), axis=-1), N_EXPERTS - 1)
    sel_q = eq[:, :, None] == ar
    pick_q = lambda v: jnp.sum(jnp.where(sel_q, v[:, None, :], 0), axis=-1)
    src_flat = pick_q(src_start) + (q[None, :] - pick_q(cstart)) * COMB_CH
    src_flat = jnp.where(q[None, :] < n_total[:, None], src_flat, 0)
    stage_rows = COMB_MAX_CHUNKS * COMB_CH
    n_tiles = lo.shape[0]
    t_pad = n_tiles * COMB_TOK
    slot_base = (jnp.arange(n_tiles, dtype=I32) % 2) * stage_rows
    local = jnp.repeat(cstart * COMB_CH - lo + slot_base[:, None], COMB_TOK, axis=0)[:t]
    lidx = pick(local[:, None, :]) + rank
    lidx = jnp.concatenate([lidx, jnp.broadcast_to(slot_base[-1], (t_pad - t, TOP_K))], axis=0)
    return dict(block_expert=block_expert.astype(I32), n_used=n_used, tok_buf=tok_buf.astype(I32),
                first=first, next_expert=next_expert.astype(I32),
                n_blocks=n_blocks, src_flat=src_flat.astype(I32), n_total=n_total,
                lidx=lidx.astype(I32), stage_rows=stage_rows, t_pad=t_pad)


def kernel(x_prompt, x_sample, state_ssm_re, state_ssm_im, state_conv, c_prompt, c_sample, w_ada, b_ada, norm1_g, norm2_g, w_in, ssm_a_re, ssm_a_im, ssm_log_dt, ssm_b_re, ssm_b_im, ssm_c_re, ssm_c_im, ssm_d, w_glu, b_glu, w_lift_a, conv_w, conv_b, w_lift_b, w_out, w_router, router_bias, w_exp_gate, w_exp_up, w_exp_down, w_sh_gate, w_sh_up, w_sh_down, final_norm_g):
    depth = w_ada.shape[0]
    assert depth == 1
    bp, seq, _ = x_prompt.shape
    bs = x_sample.shape[0]
    tp = bp * seq
    l = 0

    w_in_b = w_in[l].astype(BF16)
    wr_t = w_router[l].T
    wr_hi = wr_t.astype(BF16)
    wr_lo = (wr_t - wr_hi.astype(F32)).astype(BF16)
    mix_w = (ssm_d[l].reshape(1, -1), w_glu[l].astype(BF16), b_glu[l].reshape(1, -1),
             w_lift_a[l].astype(BF16), conv_w[l], conv_b[l].reshape(1, -1), w_lift_b[l].astype(BF16),
             w_out[l].astype(BF16), norm2_g[l].reshape(1, -1), wr_hi, wr_lo)
    g1 = norm1_g[l].reshape(1, -1)

    ada = _ada(jnp.concatenate([c_prompt, c_sample], axis=0), w_ada[l], b_ada[l])
    mods_p = ada[:bp].reshape(bp, 1, -1)
    mods_s = ada[bp:]

    apow, ssm_tm, ssm_bk, ssm_ck = _ssm_prep(ssm_a_re[l], ssm_a_im[l], ssm_log_dt[l], ssm_b_re[l],
                                             ssm_b_im[l], ssm_c_re[l], ssm_c_im[l])
    a1_re, a1_im, al_re, al_im = (_state_layout(apow[i]) for i in range(4))

    xp = x_prompt.reshape(tp, D_MODEL)
    u_p, rest_p = _in_proj(xp, mods_p, seq, g1, w_in_b, tm=256)
    n_chunks = seq // SSM_CHUNK
    ssm_rows = bp * n_chunks
    s_loc = _ssm_state(u_p, ssm_bk, SSM_CHUNK, ssm_rows)
    s_prev, s_fin_p = _ssm_scan(s_loc.reshape(bp, n_chunks, -1), al_re, al_im,
                                jnp.zeros((bp, OCTETS * OCT_STATE), F32))
    y_p = _ssm_out(u_p, s_prev.reshape(bp * n_chunks, -1), ssm_tm, ssm_ck, SSM_CHUNK, ssm_rows)
    x1_p, h2_p, lg_p, conv_p = _mixer_post(xp, y_p, u_p, rest_p, mods_p, seq, mix_w, None, tm=256)

    xs = x_sample.reshape(bs, D_MODEL)
    u_s, rest_s = _in_proj(xs, mods_s, 1, g1, w_in_b, tm=bs)
    s0_s = _pack_state(state_ssm_re[l], state_ssm_im[l])
    s_new_s = _ssm_step(_ssm_state(u_s, ssm_bk, 1, bs), s0_s, a1_re, a1_im)
    y_s = _ssm_out(u_s, s0_s, ssm_tm, ssm_ck, 1, bs)
    x1_s, h2_s, lg_s, nc0, nc1 = _mixer_post(xs, y_s, u_s, rest_s, mods_s, 1, mix_w,
                                                    state_conv[l], tm=bs)

    t = tp + bs
    te_p, gate_p = _route(lg_p, router_bias[l])
    te_s, gate_s = _route(lg_s, router_bias[l])
    top_e = jnp.concatenate([te_p, te_s], axis=1).T
    gate = jnp.concatenate([gate_p, gate_s], axis=1).T
    plan = _moe_plan(top_e, t)
    h_tiles = jnp.concatenate([h2_p, h2_s], axis=0).reshape(t, ROW_TILES, LANES)
    xs_sorted = _dispatch(h_tiles, plan["tok_buf"], plan["n_used"], plan["n_blocks"])
    ys = _experts(xs_sorted, plan["block_expert"], plan["n_used"], plan["first"], plan["next_expert"],
                  w_exp_gate[l], w_exp_up[l], w_exp_down[l], plan["n_blocks"])
    gate_pad = jnp.concatenate([gate, jnp.zeros((plan["t_pad"] - t, TOP_K), F32)], axis=0)
    routed = _combine(ys, plan["src_flat"], plan["n_total"], plan["lidx"].reshape(-1),
                      gate_pad.reshape(-1), plan["t_pad"], plan["stage_rows"])

    w_sg, w_su, w_sd = w_sh_gate[l].astype(BF16), w_sh_up[l].astype(BF16), w_sh_down[l].astype(BF16)
    gf = final_norm_g.reshape(1, -1)
    y_prompt = _final(h2_p, x1_p, routed, 0, mods_p, seq, w_sg, w_su, w_sd, gf, tm=512)
    y_sample = _final(h2_s, x1_s, routed, tp, mods_s, 1, w_sg, w_su, w_sd, gf, tm=bs)

    re_p, im_p = _unpack_state(s_fin_p)
    re_s, im_s = _unpack_state(s_new_s)
    return (y_prompt.reshape(bp, seq, D_MODEL), y_sample.reshape(bs, 1, D_MODEL),
            re_p[None], im_p[None], conv_p[None],
            re_s[None], im_s[None], jnp.stack([nc0, nc1], axis=1)[None])
```

```python
import functools

import jax
import jax.numpy as jnp
from jax import lax
from jax.experimental import pallas as pl
from jax.experimental.pallas import tpu as pltpu

F32 = jnp.float32
BF16 = jnp.bfloat16
I32 = jnp.int32

D_MODEL = 2048
D_SSM = 1024
SSM_GROUP = 16
N_GROUPS = 64
SSM_STATE = 64
D_CONV = 1024
CONV_W = 3
N_EXPERTS = 64
TOP_K = 8
N_EXPERT_GROUPS = 8
EXPERTS_PER_GROUP = 8
TOPK_GROUPS = 4
D_EXPERT = 512
ROUTED_SCALE = 2.5
RMS_EPS = 1e-6

LANES = 128
SUBLANES = 8
OCTETS = N_GROUPS * SSM_GROUP // LANES
OCT_STATE = 2 * (N_GROUPS // OCTETS) * SSM_STATE
SSM_CHUNK = 8
MOE_ROWS = 256
COMB_TOK = 128
COMB_CH = 16
ROW_TILES = D_MODEL // LANES
ROW_STRIDE = MOE_ROWS + SUBLANES
DISP_ROWS = 4 * MOE_ROWS
DISP_STRIDE = DISP_ROWS + SUBLANES
COMB_MAX_CHUNKS = COMB_TOK * TOP_K // COMB_CH + N_EXPERTS
TOKEN_BITS = 14
MIX_ROWS = 128
COMB_ISSUE = 4
assert COMB_MAX_CHUNKS % COMB_ISSUE == 0
NEG_INF = float("-inf")


def _cparams(sem, vmem_mb):
    return pltpu.CompilerParams(dimension_semantics=sem, vmem_limit_bytes=vmem_mb * 1024 * 1024)


def _const_spec(shape):
    nd = len(shape)
    return pl.BlockSpec(shape, lambda *_: (0,) * nd, pipeline_mode=pl.Buffered(1))


def _mod(ref):
    v = ref[...]
    return v[0] if v.ndim == 3 else v


def _silu(x):
    return x * jax.nn.sigmoid(x)


def _ada_kernel(c_ref, w_ref, b_ref, o_ref):
    a = _silu(c_ref[...]).astype(BF16)
    o_ref[...] = jnp.dot(a, w_ref[...].astype(BF16), preferred_element_type=F32) + b_ref[...]


def _ada(c_all, w_ada, b_ada):
    m = c_all.shape[0]
    n = w_ada.shape[1]
    tn = 1024
    return pl.pallas_call(
        _ada_kernel,
        grid=(n // tn,),
        in_specs=[pl.BlockSpec((m, D_MODEL), lambda j: (0, 0)),
                  pl.BlockSpec((D_MODEL, tn), lambda j: (0, j)),
                  pl.BlockSpec((1, tn), lambda j: (0, j))],
        out_specs=pl.BlockSpec((m, tn), lambda j: (0, j)),
        out_shape=jax.ShapeDtypeStruct((m, n), F32),
        compiler_params=_cparams(("arbitrary",), 40),
        name="ada",
    )(c_all, w_ada, b_ada.reshape(1, n))


def _mod_spec(mods, col, rows_per_batch, tm):
    if mods.ndim == 3:
        tiles_per_batch = rows_per_batch // tm
        return pl.BlockSpec((1, 1, D_MODEL), lambda m, *_: (m // tiles_per_batch, 0, col))
    return pl.BlockSpec((tm, D_MODEL), lambda m, *_: (m, col))


def _inproj_kernel(x_ref, g_ref, sh_ref, sc_ref, w_ref, u_ref, r_ref):
    x = x_ref[...]
    y = x * lax.rsqrt(jnp.mean(x * x, axis=-1, keepdims=True) + RMS_EPS) * g_ref[...]
    h = (y * (1.0 + _mod(sc_ref)) + _mod(sh_ref)).astype(BF16)
    tn = D_SSM
    u_ref[...] = jnp.dot(h, w_ref[:, 0:tn], preferred_element_type=F32)
    for n in range(1, w_ref.shape[1] // tn):
        r_ref[:, (n - 1) * tn:n * tn] = jnp.dot(h, w_ref[:, n * tn:(n + 1) * tn],
                                                preferred_element_type=F32).astype(BF16)


def _in_proj(x, mods, rows_per_batch, g1, w_in, tm):
    t = x.shape[0]
    d_in = w_in.shape[1]
    n_rest = d_in - D_SSM
    return pl.pallas_call(
        _inproj_kernel,
        grid=(t // tm,),
        in_specs=[pl.BlockSpec((tm, D_MODEL), lambda m: (m, 0)),
                  _const_spec((1, D_MODEL)),
                  _mod_spec(mods, 0, rows_per_batch, tm),
                  _mod_spec(mods, 1, rows_per_batch, tm),
                  _const_spec((D_MODEL, d_in))],
        out_specs=[pl.BlockSpec((tm, D_SSM), lambda m: (m, 0)),
                   pl.BlockSpec((tm, n_rest), lambda m: (m, 0))],
        out_shape=[jax.ShapeDtypeStruct((t, D_SSM), F32),
                   jax.ShapeDtypeStruct((t, n_rest), BF16)],
        compiler_params=_cparams(("arbitrary",), 56),
        name="in_proj",
    )(x, g1, mods, mods, w_in)


def _ssm_prep_kernel(are_ref, aim_ref, ldt_ref, btr_ref, bti_ref, cre_ref, cim_ref,
                     apow_ref, tm_ref, bk_ref, ck_ref):
    lc = SSM_CHUNK
    ng = N_GROUPS // OCTETS
    half = OCT_STATE // 2
    lr = are_ref[...]
    li = aim_ref[...]
    dt = jnp.exp(ldt_ref[...])
    mag = jnp.exp(lr * dt)
    ar = mag * jnp.cos(li * dt)
    ai = mag * jnp.sin(li * dt)
    den = lr * lr + li * li
    xr = ar - 1.0
    cr = (xr * lr + ai * li) / den
    ci = (ai * lr - xr * li) / den
    btr = btr_ref[...]
    bti = bti_ref[...]
    bbr = cr * btr - ci * bti
    bbi = cr * bti + ci * btr
    cre = cre_ref[...]
    cim = cim_ref[...]
    cre_w = jnp.concatenate([cre] * ng, axis=1)
    cim_w = jnp.concatenate([cim] * ng, axis=1)

    def iota(shape, dim):
        return lax.broadcasted_iota(I32, shape, dim)

    width = lc * LANES
    sel_b = ((iota((SSM_STATE, half), 1) & (SSM_STATE - 1)) == iota((SSM_STATE, half), 0)).astype(BF16)
    mask_b = ((iota((width, half), 0) & (LANES - 1)) >> 4) == (iota((width, half), 1) >> 6)
    sel_c = ((iota((half, SSM_STATE), 0) & (SSM_STATE - 1)) == iota((half, SSM_STATE), 1)).astype(BF16)
    mask_c = (iota((half, width), 0) >> 6) == ((iota((half, width), 1) & (LANES - 1)) >> 4)
    mask_t = (iota((LANES, LANES), 0) >> 4) == (iota((LANES, LANES), 1) >> 4)
    dn_batch = (((2,), (2,)), ((0,), (0,)))
    dn_nt = (((1,), (1,)), ((), ()))

    pr = jnp.ones_like(ar)
    pi = jnp.zeros_like(ar)
    w1r, w1i, w2r, w2i = [], [], [], []
    for l in range(lc):
        w1r.append(pr * bbr - pi * bbi)
        w1i.append(pr * bbi + pi * bbr)
        pr, pi = pr * ar - pi * ai, pr * ai + pi * ar
        w2r.append(cre * pr - cim * pi)
        w2i.append(-(cre * pi + cim * pr))
        if l == 0:
            apow_ref[0] = pr
            apow_ref[1] = pi
    apow_ref[2] = pr
    apow_ref[3] = pi

    def rows(ws, order):
        return jnp.concatenate([ws[l] for l in order], axis=0).reshape(width, SSM_STATE).astype(BF16)

    rev = range(lc - 1, -1, -1)
    for c, ws in ((0, w1r), (1, w1i)):
        rep = jnp.dot(rows(ws, rev), sel_b, preferred_element_type=F32)
        bk_ref[:, c * half:(c + 1) * half] = jnp.where(mask_b, rep, 0.0).astype(BF16)
    for c, ws in ((0, w2r), (1, w2i)):
        rep = lax.dot_general(sel_c, rows(ws, range(lc)), dn_nt, preferred_element_type=F32)
        ck_ref[c * half:(c + 1) * half, :] = jnp.where(mask_c, rep, 0.0).astype(BF16)
    w1r_all = jnp.concatenate(w1r, axis=1)
    w1i_all = jnp.concatenate(w1i, axis=1)
    kt = (lax.dot_general(w1r_all, cre_w, dn_batch, precision=lax.Precision.HIGHEST,
                          preferred_element_type=F32)
          - lax.dot_general(w1i_all, cim_w, dn_batch, precision=lax.Precision.HIGHEST,
                            preferred_element_type=F32))
    hh = SSM_GROUP
    t_blocks = [jnp.where(mask_t, kt[:, l * hh:(l + 1) * hh, :].reshape(LANES, LANES), 0.0).astype(BF16)
                for l in range(lc)]
    zero = jnp.zeros((LANES, LANES), BF16)
    for s in range(lc):
        for t in range(lc):
            tm_ref[s * LANES:(s + 1) * LANES, t * LANES:(t + 1) * LANES] = t_blocks[t - s] if t >= s else zero


def _ssm_prep(a_re, a_im, log_dt, b_re, b_im, c_re, c_im):
    g, p, h = N_GROUPS, SSM_STATE, SSM_GROUP
    ng = g // OCTETS
    width = SSM_CHUNK * LANES
    vec = pl.BlockSpec((ng, 1, p), lambda o: (o, 0, 0))
    mat = pl.BlockSpec((ng, h, p), lambda o: (o, 0, 0))
    big = lambda rows, cols: pl.BlockSpec((None, rows, cols), lambda o: (o, 0, 0))
    return pl.pallas_call(
        _ssm_prep_kernel,
        grid=(OCTETS,),
        in_specs=[vec, vec, pl.BlockSpec((ng, 1, 1), lambda o: (o, 0, 0)), mat, mat, mat, mat],
        out_specs=[pl.BlockSpec((4, ng, 1, p), lambda o: (0, o, 0, 0)),
                   big(width, width), big(width, OCT_STATE), big(OCT_STATE, width)],
        out_shape=[jax.ShapeDtypeStruct((4, g, 1, p), F32),
                   jax.ShapeDtypeStruct((OCTETS, width, width), BF16),
                   jax.ShapeDtypeStruct((OCTETS, width, OCT_STATE), BF16),
                   jax.ShapeDtypeStruct((OCTETS, OCT_STATE, width), BF16)],
        compiler_params=_cparams(("arbitrary",), 32),
        name="ssm_prep",
    )(a_re.reshape(g, 1, p), a_im.reshape(g, 1, p), log_dt.reshape(g, 1, 1),
      b_re.transpose(0, 2, 1), b_im.transpose(0, 2, 1), c_re, c_im)


def _state_layout(v):
    return v.reshape(OCTETS, 1, OCT_STATE // 2)


def _load_chunks(u_ref, lc, rb):
    parts = [u_ref[pl.ds(s, rb, stride=lc), :] if lc > 1 else u_ref[...] for s in range(lc)]
    u = parts[0] if lc == 1 else jnp.concatenate(parts, axis=-1)
    return u.astype(BF16)


def _ssm_state_kernel(u_ref, bk_ref, s_ref, *, lc, rb):
    s_ref[...] = jnp.dot(_load_chunks(u_ref, lc, rb), bk_ref[...], preferred_element_type=F32)


def _ssm_state(u, bk, lc, rb):
    t = u.shape[0]
    nb = t // (rb * lc)
    lag0_block = SSM_CHUNK // lc - 1
    return pl.pallas_call(
        functools.partial(_ssm_state_kernel, lc=lc, rb=rb),
        grid=(OCTETS, nb),
        in_specs=[pl.BlockSpec((rb * lc, LANES), lambda o, b: (b, o)),
                  pl.BlockSpec((None, lc * LANES, OCT_STATE), lambda o, b: (o, lag0_block, 0))],
        out_specs=pl.BlockSpec((rb, OCT_STATE), lambda o, b: (b, o)),
        out_shape=jax.ShapeDtypeStruct((t // lc, OCTETS * OCT_STATE), F32),
        compiler_params=_cparams(("arbitrary", "arbitrary"), 32),
        name="ssm_state",
    )(u, bk)


def _ssm_scan_kernel(sl_ref, ar_ref, ai_ref, s0_ref, sp_ref, sf_ref, *, nb, nc):
    half = OCT_STATE // 2
    ar = ar_ref[0]
    ai = ai_ref[0]
    init = tuple((s0_ref[b:b + 1, 0:half], s0_ref[b:b + 1, half:OCT_STATE]) for b in range(nb))

    def body(c, carry):
        new = []
        for b in range(nb):
            sr, si = carry[b]
            sp_ref[b, pl.ds(c, 1), 0:half] = sr
            sp_ref[b, pl.ds(c, 1), half:OCT_STATE] = si
            xr = sl_ref[b, pl.ds(c, 1), 0:half]
            xi = sl_ref[b, pl.ds(c, 1), half:OCT_STATE]
            new.append((ar * sr - ai * si + xr, ar * si + ai * sr + xi))
        return tuple(new)

    fin = lax.fori_loop(0, nc, body, init)
    for b in range(nb):
        sf_ref[b:b + 1, 0:half] = fin[b][0]
        sf_ref[b:b + 1, half:OCT_STATE] = fin[b][1]


def _ssm_scan(s_loc, al_re, al_im, s0):
    nb, nc, width = s_loc.shape
    return pl.pallas_call(
        functools.partial(_ssm_scan_kernel, nb=nb, nc=nc),
        grid=(OCTETS,),
        in_specs=[pl.BlockSpec((nb, nc, OCT_STATE), lambda o: (0, 0, o)),
                  pl.BlockSpec((1, 1, OCT_STATE // 2), lambda o: (o, 0, 0)),
                  pl.BlockSpec((1, 1, OCT_STATE // 2), lambda o: (o, 0, 0)),
                  pl.BlockSpec((nb, OCT_STATE), lambda o: (0, o))],
        out_specs=[pl.BlockSpec((nb, nc, OCT_STATE), lambda o: (0, 0, o)),
                   pl.BlockSpec((nb, OCT_STATE), lambda o: (0, o))],
        out_shape=[jax.ShapeDtypeStruct((nb, nc, width), F32),
                   jax.ShapeDtypeStruct((nb, width), F32)],
        compiler_params=_cparams(("arbitrary",), 32),
        name="ssm_scan",
    )(s_loc, al_re, al_im, s0)


def _ssm_out_kernel(u_ref, sp_ref, tm_ref, ck_ref, y_ref, *, lc, rb):
    y = (jnp.dot(_load_chunks(u_ref, lc, rb), tm_ref[...], preferred_element_type=F32)
         + jnp.dot(sp_ref[...].astype(BF16), ck_ref[...], preferred_element_type=F32))
    if lc == 1:
        y_ref[...] = y
    else:
        for t in range(lc):
            y_ref[pl.ds(t, rb, stride=lc), :] = y[:, t * LANES:(t + 1) * LANES]


def _ssm_out(u, s_prev, tm, ck, lc, rb):
    t = u.shape[0]
    nb = t // (rb * lc)
    return pl.pallas_call(
        functools.partial(_ssm_out_kernel, lc=lc, rb=rb),
        grid=(OCTETS, nb),
        in_specs=[pl.BlockSpec((rb * lc, LANES), lambda o, b: (b, o)),
                  pl.BlockSpec((rb, OCT_STATE), lambda o, b: (b, o)),
                  pl.BlockSpec((None, lc * LANES, lc * LANES), lambda o, b: (o, 0, 0)),
                  pl.BlockSpec((None, OCT_STATE, lc * LANES), lambda o, b: (o, 0, 0))],
        out_specs=pl.BlockSpec((rb * lc, LANES), lambda o, b: (b, o)),
        out_shape=jax.ShapeDtypeStruct((t, D_SSM), F32),
        compiler_params=_cparams(("arbitrary", "arbitrary"), 48),
        name="ssm_out",
    )(u, s_prev, tm, ck)


def _ssm_step_kernel(sl_ref, s0_ref, ar_ref, ai_ref, sn_ref):
    half = OCT_STATE // 2
    ar = ar_ref[0]
    ai = ai_ref[0]
    sr = s0_ref[:, 0:half]
    si = s0_ref[:, half:OCT_STATE]
    sn_ref[:, 0:half] = ar * sr - ai * si + sl_ref[:, 0:half]
    sn_ref[:, half:OCT_STATE] = ar * si + ai * sr + sl_ref[:, half:OCT_STATE]


def _ssm_step(s_loc, s0, a_re, a_im):
    rows, width = s_loc.shape
    return pl.pallas_call(
        _ssm_step_kernel,
        grid=(OCTETS,),
        in_specs=[pl.BlockSpec((rows, OCT_STATE), lambda o: (0, o)),
                  pl.BlockSpec((rows, OCT_STATE), lambda o: (0, o)),
                  pl.BlockSpec((1, 1, OCT_STATE // 2), lambda o: (o, 0, 0)),
                  pl.BlockSpec((1, 1, OCT_STATE // 2), lambda o: (o, 0, 0))],
        out_specs=pl.BlockSpec((rows, OCT_STATE), lambda o: (0, o)),
        out_shape=jax.ShapeDtypeStruct((rows, width), F32),
        compiler_params=_cparams(("arbitrary",), 32),
        name="ssm_step",
    )(s_loc, s0, a_re, a_im)


def _pack_state(s_re, s_im):
    b = s_re.shape[0]
    half = OCT_STATE // 2
    return jnp.stack([s_re.reshape(b, OCTETS, half), s_im.reshape(b, OCTETS, half)], axis=2).reshape(b, -1)


def _unpack_state(s):
    b = s.shape[0]
    s4 = s.reshape(b, OCTETS, 2, OCT_STATE // 2)
    return (s4[:, :, 0].reshape(b, N_GROUPS, SSM_STATE), s4[:, :, 1].reshape(b, N_GROUPS, SSM_STATE))


def _mixer_kernel(*refs, seq, tm, tiles_per_batch):
    (x_ref, ys_ref, u_ref, r_ref, g1_ref, sh2_ref, sc2_ref, d_ref, wglu_ref, bglu_ref, wla_ref,
     cw_ref, cb_ref, wlb_ref, wout_ref, n2g_ref, wrh_ref, wrl_ref) = refs[:18]
    if seq:
        x1_ref, h2_ref, lg_ref, nc_ref, carry = refs[18:]
    else:
        cb0_ref, cb1_ref, x1_ref, h2_ref, lg_ref, nc0_ref, nc1_ref = refs[18:]

    xb = r_ref[:, 0:D_CONV].astype(F32)
    gb = r_ref[:, D_CONV:2 * D_CONV].astype(F32)
    gc = r_ref[:, 2 * D_CONV:3 * D_CONV].astype(F32)
    v = gc * xb
    if seq:
        first = pl.program_id(0) % tiles_per_batch == 0

        @pl.when(first)
        def _():
            carry[...] = jnp.zeros_like(carry)

        c0 = carry[0:1, :]
        c1 = carry[1:2, :]
        row = lax.broadcasted_iota(I32, (tm, 1), 0)
        v1 = jnp.where(row == 0, c1, pltpu.roll(v, 1, 0))
        v2 = jnp.where(row == 0, c0, jnp.where(row == 1, c1, pltpu.roll(v, 2, 0)))
        tail = v[tm - 2:tm, :]
        carry[...] = tail
        nc_ref[0] = tail
    else:
        v2 = cb0_ref[...]
        v1 = cb1_ref[...]
        nc0_ref[...] = v1
        nc1_ref[...] = v
    conv = cb_ref[...] + cw_ref[0:1, :] * v2 + cw_ref[1:2, :] * v1 + cw_ref[2:3, :] * v
    gated_conv = (gb * conv).astype(BF16)

    def mod_rows(ref, rs):
        return ref[0] if len(ref.shape) == 3 else ref[rs, :]

    n_split = max(tm // MIX_ROWS, 1)
    rows = tm // n_split
    off = 3 * D_CONV
    dn = (((1,), (1,)), ((), ()))
    for part in range(n_split):
        rs = slice(part * rows, (part + 1) * rows)
        ya = jax.nn.gelu(ys_ref[rs, :] + d_ref[...] * u_ref[rs, :], approximate=True)
        z = jnp.dot(ya.astype(BF16), wglu_ref[...], preferred_element_type=F32) + bglu_ref[...]
        ya = ya * jax.nn.sigmoid(z)
        branch_a = jnp.dot(ya.astype(BF16), wla_ref[...], preferred_element_type=F32)
        branch_b = jnp.dot(gated_conv[rs], wlb_ref[...], preferred_element_type=F32)
        gla = r_ref[rs, off:off + D_MODEL].astype(F32)
        glb = r_ref[rs, off + D_MODEL:off + 2 * D_MODEL].astype(F32)
        merged = jax.nn.sigmoid(gla) * branch_a + jax.nn.sigmoid(glb) * branch_b
        x1 = x_ref[rs, :] + mod_rows(g1_ref, rs) * jnp.dot(merged.astype(BF16), wout_ref[...],
                                                           preferred_element_type=F32)
        x1_ref[rs, :] = x1

        y = x1 * lax.rsqrt(jnp.mean(x1 * x1, axis=-1, keepdims=True) + RMS_EPS) * n2g_ref[...]
        h2 = y * (1.0 + mod_rows(sc2_ref, rs)) + mod_rows(sh2_ref, rs)
        h2b = h2.astype(BF16)
        h2_ref[rs, :] = h2b

        h2l = (h2 - h2b.astype(F32)).astype(BF16)
        lg_ref[:, rs] = (lax.dot_general(wrh_ref[...], h2b, dn, preferred_element_type=F32)
                         + lax.dot_general(wrh_ref[...], h2l, dn, preferred_element_type=F32)
                         + lax.dot_general(wrl_ref[...], h2b, dn, preferred_element_type=F32))


def _mixer_post(x, y_ssm, u, rest, mods, rows_per_batch, weights, conv_state, tm):
    t = x.shape[0]
    seq = conv_state is None
    tiles_per_batch = rows_per_batch // tm
    (ssm_d, w_glu, b_glu, w_la, conv_w, conv_b, w_lb, w_out, n2g, wr_hi, wr_lo) = weights
    row = lambda m: (m, 0)
    in_specs = [pl.BlockSpec((tm, D_MODEL), row),
                pl.BlockSpec((tm, D_SSM), row),
                pl.BlockSpec((tm, D_SSM), row),
                pl.BlockSpec((tm, rest.shape[1]), row),
                _mod_spec(mods, 2, rows_per_batch, tm),
                _mod_spec(mods, 3, rows_per_batch, tm),
                _mod_spec(mods, 4, rows_per_batch, tm),
                _const_spec((1, D_SSM)), _const_spec((D_SSM, D_SSM)), _const_spec((1, D_SSM)),
                _const_spec((D_SSM, D_MODEL)), _const_spec((CONV_W, D_CONV)), _const_spec((1, D_CONV)),
                _const_spec((D_CONV, D_MODEL)), _const_spec((D_MODEL, D_MODEL)),
                _const_spec((1, D_MODEL)), _const_spec((N_EXPERTS, D_MODEL)),
                _const_spec((N_EXPERTS, D_MODEL))]
    args = [x, y_ssm, u, rest, mods, mods, mods, ssm_d, w_glu, b_glu, w_la, conv_w, conv_b, w_lb,
            w_out, n2g, wr_hi, wr_lo]
    out_specs = [pl.BlockSpec((tm, D_MODEL), row),
                 pl.BlockSpec((tm, D_MODEL), row),
                 pl.BlockSpec((N_EXPERTS, tm), lambda m: (0, m))]
    out_shape = [jax.ShapeDtypeStruct((t, D_MODEL), F32),
                 jax.ShapeDtypeStruct((t, D_MODEL), BF16),
                 jax.ShapeDtypeStruct((N_EXPERTS, t), F32)]
    scratch = []
    if seq:
        nb = t // rows_per_batch
        out_specs.append(pl.BlockSpec((1, CONV_W - 1, D_CONV), lambda m: (m // tiles_per_batch, 0, 0)))
        out_shape.append(jax.ShapeDtypeStruct((nb, CONV_W - 1, D_CONV), F32))
        scratch.append(pltpu.VMEM((CONV_W - 1, D_CONV), F32))
    else:
        in_specs += [pl.BlockSpec((tm, D_CONV), row), pl.BlockSpec((tm, D_CONV), row)]
        args += [conv_state[:, 0, :], conv_state[:, 1, :]]
        out_specs += [pl.BlockSpec((tm, D_CONV), row), pl.BlockSpec((tm, D_CONV), row)]
        out_shape += [jax.ShapeDtypeStruct((t, D_CONV), F32), jax.ShapeDtypeStruct((t, D_CONV), F32)]
    return pl.pallas_call(
        functools.partial(_mixer_kernel, seq=seq, tm=tm, tiles_per_batch=tiles_per_batch),
        grid=(t // tm,),
        in_specs=in_specs,
        out_specs=out_specs,
        out_shape=out_shape,
        scratch_shapes=scratch,
        compiler_params=_cparams(("arbitrary",), 56),
        name="mixer_post",
    )(*args)


def _route_kernel(lg_ref, bias_ref, te_ref, gate_ref):
    scores = [jax.nn.sigmoid(lg_ref[e]) for e in range(N_EXPERTS)]
    choice = [scores[e] + bias_ref[e] for e in range(N_EXPERTS)]
    shape = scores[0].shape

    def first_max(vals):
        m = functools.reduce(jnp.maximum, vals)
        found = jnp.zeros(shape, jnp.bool_)
        hot = []
        for v in vals:
            is_m = jnp.logical_and(v == m, jnp.logical_not(found))
            hot.append(is_m)
            found = jnp.logical_or(found, is_m)
        return m, hot

    gscore = []
    for g in range(N_EXPERT_GROUPS):
        vals = choice[g * EXPERTS_PER_GROUP:(g + 1) * EXPERTS_PER_GROUP]
        m1, hot = first_max(vals)
        rest = [jnp.where(h, NEG_INF, v) for h, v in zip(hot, vals)]
        gscore.append(m1 + functools.reduce(jnp.maximum, rest))

    gsel = [jnp.zeros(shape, jnp.bool_) for _ in range(N_EXPERT_GROUPS)]
    for _ in range(TOPK_GROUPS):
        _, hot = first_max(gscore)
        gsel = [jnp.logical_or(s, h) for s, h in zip(gsel, hot)]
        gscore = [jnp.where(h, NEG_INF, v) for h, v in zip(hot, gscore)]

    cand = [jnp.where(gsel[e // EXPERTS_PER_GROUP], choice[e], NEG_INF) for e in range(N_EXPERTS)]
    picked_e = []
    picked_s = []
    for _ in range(TOP_K):
        _, hot = first_max(cand)
        idx = jnp.zeros(shape, I32)
        sc = jnp.zeros(shape, F32)
        for e in range(N_EXPERTS):
            idx = jnp.where(hot[e], e, idx)
            sc = jnp.where(hot[e], scores[e], sc)
        cand = [jnp.where(h, NEG_INF, v) for h, v in zip(hot, cand)]
        picked_e.append(idx)
        picked_s.append(sc)
    total = functools.reduce(lambda a, b: a + b, picked_s)
    for k in range(TOP_K):
        te_ref[k] = picked_e[k]
        gate_ref[k] = picked_s[k] / total * ROUTED_SCALE


def _route(logits_t, bias):
    t = logits_t.shape[1]
    r = t // LANES
    rb = min(r, SUBLANES)
    lg = logits_t.reshape(N_EXPERTS, r, LANES)
    bias_b = jnp.broadcast_to(bias.astype(F32).reshape(N_EXPERTS, 1, 1), (N_EXPERTS, 1, LANES))
    te, gate = pl.pallas_call(
        _route_kernel,
        grid=(r // rb,),
        in_specs=[pl.BlockSpec((N_EXPERTS, rb, LANES), lambda i: (0, i, 0)),
                  pl.BlockSpec((N_EXPERTS, 1, LANES), lambda i: (0, 0, 0))],
        out_specs=[pl.BlockSpec((TOP_K, rb, LANES), lambda i: (0, i, 0)),
                   pl.BlockSpec((TOP_K, rb, LANES), lambda i: (0, i, 0))],
        out_shape=[jax.ShapeDtypeStruct((TOP_K, r, LANES), I32),
                   jax.ShapeDtypeStruct((TOP_K, r, LANES), F32)],
        compiler_params=_cparams(("arbitrary",), 32),
        name="route",
    )(lg, bias_b)
    return te.reshape(TOP_K, t), gate.reshape(TOP_K, t)


def _dispatch_kernel(nused_ref, tok_ref, h_ref, xs_ref, tile):
    used_rows = nused_ref[0] * MOE_ROWS
    live = pl.program_id(0) * DISP_ROWS < used_rows

    @pl.when(jnp.logical_not(live))
    def _():
        xs_ref[...] = jnp.zeros_like(xs_ref)

    @pl.when(live)
    def _():
        for mi in range(DISP_ROWS):
            tile[pl.ds(mi, ROW_TILES, stride=DISP_STRIDE), :] = h_ref[tok_ref[mi]].astype(F32)
        for j in range(ROW_TILES):
            xs_ref[:, j * LANES:(j + 1) * LANES] = (
                tile[j * DISP_STRIDE:j * DISP_STRIDE + DISP_ROWS, :].astype(BF16))


def _dispatch(h_tiles, tok_buf, n_used, n_blocks):
    n_rows = n_blocks * MOE_ROWS
    assert n_rows % DISP_ROWS == 0

    def last_live(i, nu):
        return jnp.minimum(i, (nu[0] * MOE_ROWS - 1) // DISP_ROWS)

    return pl.pallas_call(
        _dispatch_kernel,
        grid_spec=pltpu.PrefetchScalarGridSpec(
            num_scalar_prefetch=1,
            grid=(n_rows // DISP_ROWS,),
            in_specs=[pl.BlockSpec((DISP_ROWS,), lambda i, nu: (last_live(i, nu),),
                                   memory_space=pltpu.SMEM),
                      pl.BlockSpec(memory_space=pltpu.VMEM)],
            out_specs=pl.BlockSpec((DISP_ROWS, D_MODEL), lambda i, nu: (i, 0)),
            scratch_shapes=[pltpu.VMEM((ROW_TILES * DISP_STRIDE, LANES), F32)]),
        out_shape=jax.ShapeDtypeStruct((n_rows, D_MODEL), BF16),
        compiler_params=_cparams(("arbitrary",), 56),
        name="dispatch",
    )(n_used, tok_buf, h_tiles)


def _experts_kernel(be_ref, nused_ref, first_ref, next_ref, xs_ref, wg_hbm, wu_hbm, wd_hbm, ys_ref,
                    wg_f, wu_f, wd_f, wg_s, wu_s, wd_s, tile_a, tile_b, sem):
    i = pl.program_id(0)

    def weight_copies(e):
        return (pltpu.make_async_copy(wg_hbm.at[e], wg_f, sem.at[0]),
                pltpu.make_async_copy(wu_hbm.at[e], wu_f, sem.at[1]),
                pltpu.make_async_copy(wd_hbm.at[e], wd_f, sem.at[2]))

    n_used = nused_ref[0]

    @pl.when(i < n_used)
    def _():
        @pl.when(i == 0)
        def _():
            for c in weight_copies(be_ref[0]):
                c.start()

        @pl.when(first_ref[i] == 1)
        def _():
            for c in weight_copies(be_ref[i]):
                c.wait()
            wg_s[...] = wg_f[...].astype(BF16)
            wu_s[...] = wu_f[...].astype(BF16)
            wd_s[...] = wd_f[...].astype(BF16)

            @pl.when(next_ref[i] >= 0)
            def _():
                for c in weight_copies(next_ref[i]):
                    c.start()

    def ffn_to_tile(tile):
        x = xs_ref[...]
        g = jnp.dot(x, wg_s[...], preferred_element_type=F32)
        u = jnp.dot(x, wu_s[...], preferred_element_type=F32)
        h = (_silu(g) * u).astype(BF16)
        y = jnp.dot(h, wd_s[...], preferred_element_type=F32)
        for j in range(ROW_TILES):
            tile[j * ROW_STRIDE:j * ROW_STRIDE + MOE_ROWS, :] = y[:, j * LANES:(j + 1) * LANES]

    def tile_to_rows(tile):
        for mi in range(MOE_ROWS):
            ys_ref[mi] = tile[pl.ds(mi, ROW_TILES, stride=ROW_STRIDE), :].astype(BF16)

    even = i % 2 == 0
    both = jnp.logical_and(i >= 1, i < n_used)

    @pl.when(jnp.logical_and(both, even))
    def _():
        ffn_to_tile(tile_a)
        tile_to_rows(tile_b)

    @pl.when(jnp.logical_and(both, jnp.logical_not(even)))
    def _():
        ffn_to_tile(tile_b)
        tile_to_rows(tile_a)

    @pl.when(i == 0)
    def _():
        ffn_to_tile(tile_a)

    @pl.when(jnp.logical_and(i == n_used, even))
    def _():
        tile_to_rows(tile_b)

    @pl.when(jnp.logical_and(i == n_used, jnp.logical_not(even)))
    def _():
        tile_to_rows(tile_a)

    @pl.when(i > n_used)
    def _():
        ys_ref[...] = jnp.zeros_like(ys_ref)


def _experts(xs, block_expert, n_used, first, next_expert, w_gate, w_up, w_down, n_blocks):
    def xmap(i, be, nu, fi, nx):
        return (jnp.minimum(i, nu[0] - 1), 0)

    any_spec = pl.BlockSpec(memory_space=pl.ANY)
    return pl.pallas_call(
        _experts_kernel,
        grid_spec=pltpu.PrefetchScalarGridSpec(
            num_scalar_prefetch=4,
            grid=(n_blocks + 1,),
            in_specs=[pl.BlockSpec((MOE_ROWS, D_MODEL), xmap), any_spec, any_spec, any_spec],
            out_specs=pl.BlockSpec((MOE_ROWS, ROW_TILES, LANES),
                                   lambda i, *_: (jnp.maximum(i - 1, 0), 0, 0)),
            scratch_shapes=[pltpu.VMEM((D_MODEL, D_EXPERT), F32),
                            pltpu.VMEM((D_MODEL, D_EXPERT), F32),
                            pltpu.VMEM((D_EXPERT, D_MODEL), F32),
                            pltpu.VMEM((D_MODEL, D_EXPERT), BF16),
                            pltpu.VMEM((D_MODEL, D_EXPERT), BF16),
                            pltpu.VMEM((D_EXPERT, D_MODEL), BF16),
                            pltpu.VMEM((ROW_TILES * ROW_STRIDE, LANES), F32),
                            pltpu.VMEM((ROW_TILES * ROW_STRIDE, LANES), F32),
                            pltpu.SemaphoreType.DMA((3,))]),
        out_shape=jax.ShapeDtypeStruct((n_blocks * MOE_ROWS, ROW_TILES, LANES), BF16),
        compiler_params=_cparams(("arbitrary",), 48),
        name="experts",
    )(block_expert, n_used, first, next_expert, xs, w_gate, w_up, w_down)


def _combine_kernel(src_ref, ntot_ref, lidx_ref, gate_ref, ys_hbm, out_ref,
                    stage, tile, sem, *, stage_rows):
    i = pl.program_id(0)
    n_tiles = pl.num_programs(0)

    def chunk_copy(src, slot, dst):
        return pltpu.make_async_copy(ys_hbm.at[pl.ds(src, COMB_CH)],
                                     stage.at[pl.ds(slot * stage_rows + dst, COMB_CH)],
                                     sem.at[slot])

    def issue(tile_idx, slot):
        def c_body(g, _):
            for k in range(COMB_ISSUE):
                q = g * COMB_ISSUE + k
                chunk_copy(src_ref[tile_idx, q], slot, q * COMB_CH).start()
            return 0

        lax.fori_loop(0, ntot_ref[tile_idx] // COMB_ISSUE, c_body, 0)

    @pl.when(i == 0)
    def _():
        issue(0, 0)

    @pl.when(i + 1 < n_tiles)
    def _():
        issue(i + 1, (i + 1) % 2)

    slot = i % 2

    def w_body(c, _):
        for _k in range(COMB_ISSUE):
            chunk_copy(0, slot, 0).wait()
        return 0

    lax.fori_loop(0, ntot_ref[i] // COMB_ISSUE, w_body, 0)

    stride = COMB_TOK + SUBLANES

    for tk in range(COMB_TOK):
        acc = None
        for k in range(TOP_K):
            w = stage[lidx_ref[tk * TOP_K + k]].astype(F32)
            term = gate_ref[tk * TOP_K + k] * w
            acc = term if acc is None else acc + term
        tile[pl.ds(tk, ROW_TILES, stride=stride), :] = acc
    for j in range(ROW_TILES):
        out_ref[:, j * LANES:(j + 1) * LANES] = tile[j * stride:j * stride + COMB_TOK, :]


def _combine(ys, src_flat, n_total, lidx, gate, t, stage_rows):
    n_tiles = t // COMB_TOK
    stride = COMB_TOK + SUBLANES
    return pl.pallas_call(
        functools.partial(_combine_kernel, stage_rows=stage_rows),
        grid_spec=pltpu.PrefetchScalarGridSpec(
            num_scalar_prefetch=2,
            grid=(n_tiles,),
            in_specs=[pl.BlockSpec((COMB_TOK * TOP_K,), lambda i, *_: (i,), memory_space=pltpu.SMEM),
                      pl.BlockSpec((COMB_TOK * TOP_K,), lambda i, *_: (i,), memory_space=pltpu.SMEM),
                      pl.BlockSpec(memory_space=pl.ANY)],
            out_specs=pl.BlockSpec((COMB_TOK, D_MODEL), lambda i, *_: (i, 0)),
            scratch_shapes=[pltpu.VMEM((2 * stage_rows, ROW_TILES, LANES), BF16),
                            pltpu.VMEM((ROW_TILES * stride, LANES), F32),
                            pltpu.SemaphoreType.DMA((2,))]),
        out_shape=jax.ShapeDtypeStruct((t, D_MODEL), F32),
        compiler_params=_cparams(("arbitrary",), 48),
        name="combine",
    )(src_flat, n_total, lidx, gate, ys)


def _final_kernel(h_ref, x1_ref, rt_ref, g2_ref, wsg_ref, wsu_ref, wsd_ref, gf_ref, y_ref):
    h = h_ref[...]
    a = _silu(jnp.dot(h, wsg_ref[...], preferred_element_type=F32))
    b = jnp.dot(h, wsu_ref[...], preferred_element_type=F32)
    shared = jnp.dot((a * b).astype(BF16), wsd_ref[...], preferred_element_type=F32)
    xo = x1_ref[...] + _mod(g2_ref) * (shared + rt_ref[...])
    y_ref[...] = xo * lax.rsqrt(jnp.mean(xo * xo, axis=-1, keepdims=True) + RMS_EPS) * gf_ref[...]


def _final(h2, x1, routed, routed_row0, mods, rows_per_batch, w_sg, w_su, w_sd, gf, tm):
    t = x1.shape[0]
    row = lambda m: (m, 0)
    assert routed_row0 % tm == 0
    block0 = routed_row0 // tm
    return pl.pallas_call(
        _final_kernel,
        grid=(t // tm,),
        in_specs=[pl.BlockSpec((tm, D_MODEL), row), pl.BlockSpec((tm, D_MODEL), row),
                  pl.BlockSpec((tm, D_MODEL), lambda m: (m + block0, 0)),
                  _mod_spec(mods, 5, rows_per_batch, tm),
                  _const_spec((D_MODEL, D_EXPERT)), _const_spec((D_MODEL, D_EXPERT)),
                  _const_spec((D_EXPERT, D_MODEL)), _const_spec((1, D_MODEL))],
        out_specs=pl.BlockSpec((tm, D_MODEL), row),
        out_shape=jax.ShapeDtypeStruct((t, D_MODEL), F32),
        compiler_params=_cparams(("arbitrary",), 48),
        name="final",
    )(h2, x1, routed, mods, w_sg, w_su, w_sd, gf)


def _moe_plan(top_e, t):
    ar = jnp.arange(N_EXPERTS, dtype=I32)
    sel = top_e[:, :, None] == ar
    onehot = jnp.any(sel, axis=1).astype(I32)
    assert t % LANES == 0
    groups = onehot.astype(F32).reshape(t // LANES, LANES, N_EXPERTS)
    within = jnp.einsum("ab,nbe->nae", jnp.tril(jnp.ones((LANES, LANES), F32)), groups)
    totals = within[:, -1, :]
    csum = (within + (jnp.cumsum(totals, axis=0) - totals)[:, None, :]).reshape(t, N_EXPERTS).astype(I32)
    excl = csum - onehot
    counts = csum[-1]
    padded = (counts + MOE_ROWS - 1) // MOE_ROWS * MOE_ROWS
    pend = jnp.cumsum(padded)
    pstart = pend - padded
    pick = lambda per_expert: jnp.sum(jnp.where(sel, per_expert, 0), axis=-1)
    rank = pick(excl[:, None, :])
    n_blocks = -(-(t * TOP_K + N_EXPERTS * (MOE_ROWS - 1)) // MOE_ROWS)
    block_row = jnp.arange(n_blocks, dtype=I32) * MOE_ROWS
    block_expert = jnp.minimum(jnp.sum((pend[None, :] <= block_row[:, None]).astype(I32), axis=1),
                               N_EXPERTS - 1)
    n_used = (pend[-1] // MOE_ROWS).astype(I32).reshape(1)
    first = jnp.concatenate([jnp.ones((1,), I32), (block_expert[1:] != block_expert[:-1]).astype(I32)])
    later = jnp.logical_and(counts[None, :] > 0, ar[None, :] > ar[:, None])
    next_nonempty = jnp.min(jnp.where(later, ar[None, :], N_EXPERTS), axis=1)
    next_nonempty = jnp.where(next_nonempty < N_EXPERTS, next_nonempty, -1)
    next_expert = jnp.sum(jnp.where(block_expert[:, None] == ar[None, :], next_nonempty[None, :], 0), axis=1)
    assert t + MOE_ROWS <= 1 << TOKEN_BITS
    real = ((top_e << TOKEN_BITS) | jnp.arange(t, dtype=I32)[:, None]).reshape(-1)
    r = jnp.arange(MOE_ROWS - 1, dtype=I32)
    int_max = jnp.iinfo(jnp.int32).max
    pad_keys = jnp.where(r[None, :] < (padded - counts)[:, None],
                         (ar[:, None] << TOKEN_BITS) | (t + r[None, :]), int_max).reshape(-1)
    n_rows = n_blocks * MOE_ROWS
    fill = jnp.full((n_rows - real.shape[0] - pad_keys.shape[0],), int_max, I32)
    tokv = jnp.sort(jnp.concatenate([real, pad_keys, fill])) & ((1 << TOKEN_BITS) - 1)
    tok_buf = jnp.where(tokv < t, tokv, 0)
    lo = excl[::COMB_TOK]
    hi = jnp.concatenate([lo[1:], counts[None]], axis=0)
    n_chunks = (hi - lo + COMB_CH - 1) // COMB_CH
    cend = jnp.cumsum(n_chunks, axis=1)
    cstart = cend - n_chunks
    n_total = cend[:, -1].astype(I32)
    src_start = pstart[None, :] + lo
    q = jnp.arange(COMB_MAX_CHUNKS, dtype=I32)
    eq = jnp.minimum(jnp.sum((cend[:, None, :] <= q[None, :, None]).astype(I32), axis=-1), N_EXPERTS - 1)
    sel_q = eq[:, :, None] == ar
    pick_q = lambda v: jnp.sum(jnp.where(sel_q, v[:, None, :], 0), axis=-1)
    src_flat = pick_q(src_start) + (q[None, :] - pick_q(cstart)) * COMB_CH
    src_flat = jnp.where(q[None, :] < n_total[:, None], src_flat, 0)
    n_total = (n_total + COMB_ISSUE - 1) // COMB_ISSUE * COMB_ISSUE
    stage_rows = COMB_MAX_CHUNKS * COMB_CH
    n_tiles = lo.shape[0]
    t_pad = n_tiles * COMB_TOK
    slot_base = (jnp.arange(n_tiles, dtype=I32) % 2) * stage_rows
    local = jnp.repeat(cstart * COMB_CH - lo + slot_base[:, None], COMB_TOK, axis=0)[:t]
    lidx = pick(local[:, None, :]) + rank
    lidx = jnp.concatenate([lidx, jnp.broadcast_to(slot_base[-1], (t_pad - t, TOP_K))], axis=0)
    return dict(block_expert=block_expert.astype(I32), n_used=n_used, tok_buf=tok_buf.astype(I32),
                first=first, next_expert=next_expert.astype(I32),
                n_blocks=n_blocks, src_flat=src_flat.astype(I32), n_total=n_total,
                lidx=lidx.astype(I32), stage_rows=stage_rows, t_pad=t_pad)


def kernel(x_prompt, x_sample, state_ssm_re, state_ssm_im, state_conv, c_prompt, c_sample, w_ada, b_ada, norm1_g, norm2_g, w_in, ssm_a_re, ssm_a_im, ssm_log_dt, ssm_b_re, ssm_b_im, ssm_c_re, ssm_c_im, ssm_d, w_glu, b_glu, w_lift_a, conv_w, conv_b, w_lift_b, w_out, w_router, router_bias, w_exp_gate, w_exp_up, w_exp_down, w_sh_gate, w_sh_up, w_sh_down, final_norm_g):
    depth = w_ada.shape[0]
    assert depth == 1
    bp, seq, _ = x_prompt.shape
    bs = x_sample.shape[0]
    tp = bp * seq
    l = 0

    w_in_b = w_in[l].astype(BF16)
    wr_t = w_router[l].T
    wr_hi = wr_t.astype(BF16)
    wr_lo = (wr_t - wr_hi.astype(F32)).astype(BF16)
    mix_w = (ssm_d[l].reshape(1, -1), w_glu[l].astype(BF16), b_glu[l].reshape(1, -1),
             w_lift_a[l].astype(BF16), conv_w[l], conv_b[l].reshape(1, -1), w_lift_b[l].astype(BF16),
             w_out[l].astype(BF16), norm2_g[l].reshape(1, -1), wr_hi, wr_lo)
    g1 = norm1_g[l].reshape(1, -1)

    ada = _ada(jnp.concatenate([c_prompt, c_sample], axis=0), w_ada[l], b_ada[l])
    mods_p = ada[:bp].reshape(bp, 1, -1)
    mods_s = ada[bp:]

    apow, ssm_tm, ssm_bk, ssm_ck = _ssm_prep(ssm_a_re[l], ssm_a_im[l], ssm_log_dt[l], ssm_b_re[l],
                                             ssm_b_im[l], ssm_c_re[l], ssm_c_im[l])
    a1_re, a1_im, al_re, al_im = (_state_layout(apow[i]) for i in range(4))

    xp = x_prompt.reshape(tp, D_MODEL)
    u_p, rest_p = _in_proj(xp, mods_p, seq, g1, w_in_b, tm=256)
    n_chunks = seq // SSM_CHUNK
    ssm_rows = bp * n_chunks
    s_loc = _ssm_state(u_p, ssm_bk, SSM_CHUNK, ssm_rows)
    s_prev, s_fin_p = _ssm_scan(s_loc.reshape(bp, n_chunks, -1), al_re, al_im,
                                jnp.zeros((bp, OCTETS * OCT_STATE), F32))
    y_p = _ssm_out(u_p, s_prev.reshape(bp * n_chunks, -1), ssm_tm, ssm_ck, SSM_CHUNK, ssm_rows)
    x1_p, h2_p, lg_p, conv_p = _mixer_post(xp, y_p, u_p, rest_p, mods_p, seq, mix_w, None, tm=256)

    xs = x_sample.reshape(bs, D_MODEL)
    u_s, rest_s = _in_proj(xs, mods_s, 1, g1, w_in_b, tm=bs)
    s0_s = _pack_state(state_ssm_re[l], state_ssm_im[l])
    s_new_s = _ssm_step(_ssm_state(u_s, ssm_bk, 1, bs), s0_s, a1_re, a1_im)
    y_s = _ssm_out(u_s, s0_s, ssm_tm, ssm_ck, 1, bs)
    x1_s, h2_s, lg_s, nc0, nc1 = _mixer_post(xs, y_s, u_s, rest_s, mods_s, 1, mix_w,
                                                    state_conv[l], tm=bs)

    t = tp + bs
    te_p, gate_p = _route(lg_p, router_bias[l])
    te_s, gate_s = _route(lg_s, router_bias[l])
    top_e = jnp.concatenate([te_p, te_s], axis=1).T
    gate = jnp.concatenate([gate_p, gate_s], axis=1).T
    plan = _moe_plan(top_e, t)
    h_tiles = jnp.concatenate([h2_p, h2_s], axis=0).reshape(t, ROW_TILES, LANES)
    xs_sorted = _dispatch(h_tiles, plan["tok_buf"], plan["n_used"], plan["n_blocks"])
    ys = _experts(xs_sorted, plan["block_expert"], plan["n_used"], plan["first"], plan["next_expert"],
                  w_exp_gate[l], w_exp_up[l], w_exp_down[l], plan["n_blocks"])
    gate_pad = jnp.concatenate([gate, jnp.zeros((plan["t_pad"] - t, TOP_K), F32)], axis=0)
    routed = _combine(ys, plan["src_flat"], plan["n_total"], plan["lidx"].reshape(-1),
                      gate_pad.reshape(-1), plan["t_pad"], plan["stage_rows"])

    w_sg, w_su, w_sd = w_sh_gate[l].astype(BF16), w_sh_up[l].astype(BF16), w_sh_down[l].astype(BF16)
    gf = final_norm_g.reshape(1, -1)
    y_prompt = _final(h2_p, x1_p, routed, 0, mods_p, seq, w_sg, w_su, w_sd, gf, tm=512)
    y_sample = _final(h2_s, x1_s, routed, tp, mods_s, 1, w_sg, w_su, w_sd, gf, tm=bs)

    re_p, im_p = _unpack_state(s_fin_p)
    re_s, im_s = _unpack_state(s_new_s)
    return (y_prompt.reshape(bp, seq, D_MODEL), y_sample.reshape(bs, 1, D_MODEL),
            re_p[None], im_p[None], conv_p[None],
            re_s[None], im_s[None], jnp.stack([nc0, nc1], axis=1)[None])
```

```python
import functools

import jax
import jax.numpy as jnp
from jax import lax
from jax.experimental import pallas as pl
from jax.experimental.pallas import tpu as pltpu

F32 = jnp.float32
BF16 = jnp.bfloat16
I32 = jnp.int32

D_MODEL = 2048
D_SSM = 1024
SSM_GROUP = 16
N_GROUPS = 64
SSM_STATE = 64
D_CONV = 1024
CONV_W = 3
N_EXPERTS = 64
TOP_K = 8
N_EXPERT_GROUPS = 8
EXPERTS_PER_GROUP = 8
TOPK_GROUPS = 4
D_EXPERT = 512
ROUTED_SCALE = 2.5
RMS_EPS = 1e-6

LANES = 128
SUBLANES = 8
OCTETS = N_GROUPS * SSM_GROUP // LANES
OCT_STATE = 2 * (N_GROUPS // OCTETS) * SSM_STATE
SSM_CHUNK = 8
MOE_ROWS = 256
COMB_TOK = 128
COMB_CH = 16
ROW_TILES = D_MODEL // LANES
ROW_STRIDE = MOE_ROWS + SUBLANES
DISP_ROWS = 4 * MOE_ROWS
DISP_STRIDE = DISP_ROWS + SUBLANES
COMB_MAX_CHUNKS = COMB_TOK * TOP_K // COMB_CH + N_EXPERTS
TOKEN_BITS = 14
MIX_ROWS = 128
COMB_ISSUE = 4
assert COMB_MAX_CHUNKS % COMB_ISSUE == 0
NEG_INF = float("-inf")


def _cparams(sem, vmem_mb):
    return pltpu.CompilerParams(dimension_semantics=sem, vmem_limit_bytes=vmem_mb * 1024 * 1024)


def _const_spec(shape):
    nd = len(shape)
    return pl.BlockSpec(shape, lambda *_: (0,) * nd, pipeline_mode=pl.Buffered(1))


def _mod(ref):
    v = ref[...]
    return v[0] if v.ndim == 3 else v


def _silu(x):
    return x * jax.nn.sigmoid(x)


def _ada_kernel(c_ref, w_ref, b_ref, o_ref):
    a = _silu(c_ref[...]).astype(BF16)
    o_ref[...] = jnp.dot(a, w_ref[...].astype(BF16), preferred_element_type=F32) + b_ref[...]


def _ada(c_all, w_ada, b_ada):
    m = c_all.shape[0]
    n = w_ada.shape[1]
    tn = 1024
    return pl.pallas_call(
        _ada_kernel,
        grid=(n // tn,),
        in_specs=[pl.BlockSpec((m, D_MODEL), lambda j: (0, 0)),
                  pl.BlockSpec((D_MODEL, tn), lambda j: (0, j)),
                  pl.BlockSpec((1, tn), lambda j: (0, j))],
        out_specs=pl.BlockSpec((m, tn), lambda j: (0, j)),
        out_shape=jax.ShapeDtypeStruct((m, n), F32),
        compiler_params=_cparams(("arbitrary",), 40),
        name="ada",
    )(c_all, w_ada, b_ada.reshape(1, n))


def _mod_spec(mods, col, rows_per_batch, tm):
    if mods.ndim == 3:
        tiles_per_batch = rows_per_batch // tm
        return pl.BlockSpec((1, 1, D_MODEL), lambda m, *_: (m // tiles_per_batch, 0, col))
    return pl.BlockSpec((tm, D_MODEL), lambda m, *_: (m, col))


def _inproj_kernel(x_ref, g_ref, sh_ref, sc_ref, w_ref, u_ref, r_ref):
    x = x_ref[...]
    y = x * lax.rsqrt(jnp.mean(x * x, axis=-1, keepdims=True) + RMS_EPS) * g_ref[...]
    h = (y * (1.0 + _mod(sc_ref)) + _mod(sh_ref)).astype(BF16)
    tn = D_SSM
    u_ref[...] = jnp.dot(h, w_ref[:, 0:tn], preferred_element_type=F32)
    for n in range(1, w_ref.shape[1] // tn):
        r_ref[:, (n - 1) * tn:n * tn] = jnp.dot(h, w_ref[:, n * tn:(n + 1) * tn],
                                                preferred_element_type=F32).astype(BF16)


def _in_proj(x, mods, rows_per_batch, g1, w_in, tm):
    t = x.shape[0]
    d_in = w_in.shape[1]
    n_rest = d_in - D_SSM
    return pl.pallas_call(
        _inproj_kernel,
        grid=(t // tm,),
        in_specs=[pl.BlockSpec((tm, D_MODEL), lambda m: (m, 0)),
                  _const_spec((1, D_MODEL)),
                  _mod_spec(mods, 0, rows_per_batch, tm),
                  _mod_spec(mods, 1, rows_per_batch, tm),
                  _const_spec((D_MODEL, d_in))],
        out_specs=[pl.BlockSpec((tm, D_SSM), lambda m: (m, 0)),
                   pl.BlockSpec((tm, n_rest), lambda m: (m, 0))],
        out_shape=[jax.ShapeDtypeStruct((t, D_SSM), F32),
                   jax.ShapeDtypeStruct((t, n_rest), BF16)],
        compiler_params=_cparams(("arbitrary",), 56),
        name="in_proj",
    )(x, g1, mods, mods, w_in)


def _ssm_prep_kernel(are_ref, aim_ref, ldt_ref, btr_ref, bti_ref, cre_ref, cim_ref,
                     apow_ref, tm_ref, bk_ref, ck_ref):
    lc = SSM_CHUNK
    ng = N_GROUPS // OCTETS
    half = OCT_STATE // 2
    lr = are_ref[...]
    li = aim_ref[...]
    dt = jnp.exp(ldt_ref[...])
    mag = jnp.exp(lr * dt)
    ar = mag * jnp.cos(li * dt)
    ai = mag * jnp.sin(li * dt)
    den = lr * lr + li * li
    xr = ar - 1.0
    cr = (xr * lr + ai * li) / den
    ci = (ai * lr - xr * li) / den
    btr = btr_ref[...]
    bti = bti_ref[...]
    bbr = cr * btr - ci * bti
    bbi = cr * bti + ci * btr
    cre = cre_ref[...]
    cim = cim_ref[...]
    cre_w = jnp.concatenate([cre] * ng, axis=1)
    cim_w = jnp.concatenate([cim] * ng, axis=1)

    def iota(shape, dim):
        return lax.broadcasted_iota(I32, shape, dim)

    width = lc * LANES
    sel_b = ((iota((SSM_STATE, half), 1) & (SSM_STATE - 1)) == iota((SSM_STATE, half), 0)).astype(BF16)
    mask_b = ((iota((width, half), 0) & (LANES - 1)) >> 4) == (iota((width, half), 1) >> 6)
    sel_c = ((iota((half, SSM_STATE), 0) & (SSM_STATE - 1)) == iota((half, SSM_STATE), 1)).astype(BF16)
    mask_c = (iota((half, width), 0) >> 6) == ((iota((half, width), 1) & (LANES - 1)) >> 4)
    mask_t = (iota((LANES, LANES), 0) >> 4) == (iota((LANES, LANES), 1) >> 4)
    dn_batch = (((2,), (2,)), ((0,), (0,)))
    dn_nt = (((1,), (1,)), ((), ()))

    pr = jnp.ones_like(ar)
    pi = jnp.zeros_like(ar)
    w1r, w1i, w2r, w2i = [], [], [], []
    for l in range(lc):
        w1r.append(pr * bbr - pi * bbi)
        w1i.append(pr * bbi + pi * bbr)
        pr, pi = pr * ar - pi * ai, pr * ai + pi * ar
        w2r.append(cre * pr - cim * pi)
        w2i.append(-(cre * pi + cim * pr))
        if l == 0:
            apow_ref[0] = pr
            apow_ref[1] = pi
    apow_ref[2] = pr
    apow_ref[3] = pi

    def rows(ws, order):
        return jnp.concatenate([ws[l] for l in order], axis=0).reshape(width, SSM_STATE).astype(BF16)

    rev = range(lc - 1, -1, -1)
    for c, ws in ((0, w1r), (1, w1i)):
        rep = jnp.dot(rows(ws, rev), sel_b, preferred_element_type=F32)
        bk_ref[:, c * half:(c + 1) * half] = jnp.where(mask_b, rep, 0.0).astype(BF16)
    for c, ws in ((0, w2r), (1, w2i)):
        rep = lax.dot_general(sel_c, rows(ws, range(lc)), dn_nt, preferred_element_type=F32)
        ck_ref[c * half:(c + 1) * half, :] = jnp.where(mask_c, rep, 0.0).astype(BF16)
    w1r_all = jnp.concatenate(w1r, axis=1)
    w1i_all = jnp.concatenate(w1i, axis=1)
    kt = (lax.dot_general(w1r_all, cre_w, dn_batch, precision=lax.Precision.HIGHEST,
                          preferred_element_type=F32)
          - lax.dot_general(w1i_all, cim_w, dn_batch, precision=lax.Precision.HIGHEST,
                            preferred_element_type=F32))
    hh = SSM_GROUP
    t_blocks = [jnp.where(mask_t, kt[:, l * hh:(l + 1) * hh, :].reshape(LANES, LANES), 0.0).astype(BF16)
                for l in range(lc)]
    zero = jnp.zeros((LANES, LANES), BF16)
    for s in range(lc):
        for t in range(lc):
            tm_ref[s * LANES:(s + 1) * LANES, t * LANES:(t + 1) * LANES] = t_blocks[t - s] if t >= s else zero


def _ssm_prep(a_re, a_im, log_dt, b_re, b_im, c_re, c_im):
    g, p, h = N_GROUPS, SSM_STATE, SSM_GROUP
    ng = g // OCTETS
    width = SSM_CHUNK * LANES
    vec = pl.BlockSpec((ng, 1, p), lambda o: (o, 0, 0))
    mat = pl.BlockSpec((ng, h, p), lambda o: (o, 0, 0))
    big = lambda rows, cols: pl.BlockSpec((None, rows, cols), lambda o: (o, 0, 0))
    return pl.pallas_call(
        _ssm_prep_kernel,
        grid=(OCTETS,),
        in_specs=[vec, vec, pl.BlockSpec((ng, 1, 1), lambda o: (o, 0, 0)), mat, mat, mat, mat],
        out_specs=[pl.BlockSpec((4, ng, 1, p), lambda o: (0, o, 0, 0)),
                   big(width, width), big(width, OCT_STATE), big(OCT_STATE, width)],
        out_shape=[jax.ShapeDtypeStruct((4, g, 1, p), F32),
                   jax.ShapeDtypeStruct((OCTETS, width, width), BF16),
                   jax.ShapeDtypeStruct((OCTETS, width, OCT_STATE), BF16),
                   jax.ShapeDtypeStruct((OCTETS, OCT_STATE, width), BF16)],
        compiler_params=_cparams(("arbitrary",), 32),
        name="ssm_prep",
    )(a_re.reshape(g, 1, p), a_im.reshape(g, 1, p), log_dt.reshape(g, 1, 1),
      b_re.transpose(0, 2, 1), b_im.transpose(0, 2, 1), c_re, c_im)


def _state_layout(v):
    return v.reshape(OCTETS, 1, OCT_STATE // 2)


def _load_chunks(u_ref, lc, rb):
    parts = [u_ref[pl.ds(s, rb, stride=lc), :] if lc > 1 else u_ref[...] for s in range(lc)]
    u = parts[0] if lc == 1 else jnp.concatenate(parts, axis=-1)
    return u.astype(BF16)


def _ssm_state_kernel(u_ref, bk_ref, s_ref, *, lc, rb):
    s_ref[...] = jnp.dot(_load_chunks(u_ref, lc, rb), bk_ref[...], preferred_element_type=F32)


def _ssm_state(u, bk, lc, rb):
    t = u.shape[0]
    nb = t // (rb * lc)
    lag0_block = SSM_CHUNK // lc - 1
    return pl.pallas_call(
        functools.partial(_ssm_state_kernel, lc=lc, rb=rb),
        grid=(OCTETS, nb),
        in_specs=[pl.BlockSpec((rb * lc, LANES), lambda o, b: (b, o)),
                  pl.BlockSpec((None, lc * LANES, OCT_STATE), lambda o, b: (o, lag0_block, 0))],
        out_specs=pl.BlockSpec((rb, OCT_STATE), lambda o, b: (b, o)),
        out_shape=jax.ShapeDtypeStruct((t // lc, OCTETS * OCT_STATE), F32),
        compiler_params=_cparams(("arbitrary", "arbitrary"), 32),
        name="ssm_state",
    )(u, bk)


def _ssm_scan_kernel(sl_ref, ar_ref, ai_ref, s0_ref, sp_ref, sf_ref, *, nb, nc):
    half = OCT_STATE // 2
    ar = ar_ref[0]
    ai = ai_ref[0]
    init = tuple((s0_ref[b:b + 1, 0:half], s0_ref[b:b + 1, half:OCT_STATE]) for b in range(nb))

    def body(c, carry):
        new = []
        for b in range(nb):
            sr, si = carry[b]
            sp_ref[b, pl.ds(c, 1), 0:half] = sr
            sp_ref[b, pl.ds(c, 1), half:OCT_STATE] = si
            xr = sl_ref[b, pl.ds(c, 1), 0:half]
            xi = sl_ref[b, pl.ds(c, 1), half:OCT_STATE]
            new.append((ar * sr - ai * si + xr, ar * si + ai * sr + xi))
        return tuple(new)

    fin = lax.fori_loop(0, nc, body, init)
    for b in range(nb):
        sf_ref[b:b + 1, 0:half] = fin[b][0]
        sf_ref[b:b + 1, half:OCT_STATE] = fin[b][1]


def _ssm_scan(s_loc, al_re, al_im, s0):
    nb, nc, width = s_loc.shape
    return pl.pallas_call(
        functools.partial(_ssm_scan_kernel, nb=nb, nc=nc),
        grid=(OCTETS,),
        in_specs=[pl.BlockSpec((nb, nc, OCT_STATE), lambda o: (0, 0, o)),
                  pl.BlockSpec((1, 1, OCT_STATE // 2), lambda o: (o, 0, 0)),
                  pl.BlockSpec((1, 1, OCT_STATE // 2), lambda o: (o, 0, 0)),
                  pl.BlockSpec((nb, OCT_STATE), lambda o: (0, o))],
        out_specs=[pl.BlockSpec((nb, nc, OCT_STATE), lambda o: (0, 0, o)),
                   pl.BlockSpec((nb, OCT_STATE), lambda o: (0, o))],
        out_shape=[jax.ShapeDtypeStruct((nb, nc, width), F32),
                   jax.ShapeDtypeStruct((nb, width), F32)],
        compiler_params=_cparams(("arbitrary",), 32),
        name="ssm_scan",
    )(s_loc, al_re, al_im, s0)


def _ssm_out_kernel(u_ref, sp_ref, tm_ref, ck_ref, y_ref, *, lc, rb):
    y = (jnp.dot(_load_chunks(u_ref, lc, rb), tm_ref[...], preferred_element_type=F32)
         + jnp.dot(sp_ref[...].astype(BF16), ck_ref[...], preferred_element_type=F32))
    if lc == 1:
        y_ref[...] = y
    else:
        for t in range(lc):
            y_ref[pl.ds(t, rb, stride=lc), :] = y[:, t * LANES:(t + 1) * LANES]


def _ssm_out(u, s_prev, tm, ck, lc, rb):
    t = u.shape[0]
    nb = t // (rb * lc)
    return pl.pallas_call(
        functools.partial(_ssm_out_kernel, lc=lc, rb=rb),
        grid=(OCTETS, nb),
        in_specs=[pl.BlockSpec((rb * lc, LANES), lambda o, b: (b, o)),
                  pl.BlockSpec((rb, OCT_STATE), lambda o, b: (b, o)),
                  pl.BlockSpec((None, lc * LANES, lc * LANES), lambda o, b: (o, 0, 0)),
                  pl.BlockSpec((None, OCT_STATE, lc * LANES), lambda o, b: (o, 0, 0))],
        out_specs=pl.BlockSpec((rb * lc, LANES), lambda o, b: (b, o)),
        out_shape=jax.ShapeDtypeStruct((t, D_SSM), F32),
        compiler_params=_cparams(("arbitrary", "arbitrary"), 48),
        name="ssm_out",
    )(u, s_prev, tm, ck)


def _ssm_step_kernel(sl_ref, s0_ref, ar_ref, ai_ref, sn_ref):
    half = OCT_STATE // 2
    ar = ar_ref[0]
    ai = ai_ref[0]
    sr = s0_ref[:, 0:half]
    si = s0_ref[:, half:OCT_STATE]
    sn_ref[:, 0:half] = ar * sr - ai * si + sl_ref[:, 0:half]
    sn_ref[:, half:OCT_STATE] = ar * si + ai * sr + sl_ref[:, half:OCT_STATE]


def _ssm_step(s_loc, s0, a_re, a_im):
    rows, width = s_loc.shape
    return pl.pallas_call(
        _ssm_step_kernel,
        grid=(OCTETS,),
        in_specs=[pl.BlockSpec((rows, OCT_STATE), lambda o: (0, o)),
                  pl.BlockSpec((rows, OCT_STATE), lambda o: (0, o)),
                  pl.BlockSpec((1, 1, OCT_STATE // 2), lambda o: (o, 0, 0)),
                  pl.BlockSpec((1, 1, OCT_STATE // 2), lambda o: (o, 0, 0))],
        out_specs=pl.BlockSpec((rows, OCT_STATE), lambda o: (0, o)),
        out_shape=jax.ShapeDtypeStruct((rows, width), F32),
        compiler_params=_cparams(("arbitrary",), 32),
        name="ssm_step",
    )(s_loc, s0, a_re, a_im)


def _pack_state(s_re, s_im):
    b = s_re.shape[0]
    half = OCT_STATE // 2
    return jnp.stack([s_re.reshape(b, OCTETS, half), s_im.reshape(b, OCTETS, half)], axis=2).reshape(b, -1)


def _unpack_state(s):
    b = s.shape[0]
    s4 = s.reshape(b, OCTETS, 2, OCT_STATE // 2)
    return (s4[:, :, 0].reshape(b, N_GROUPS, SSM_STATE), s4[:, :, 1].reshape(b, N_GROUPS, SSM_STATE))


def _mixer_kernel(*refs, seq, tm, tiles_per_batch):
    (x_ref, ys_ref, u_ref, r_ref, g1_ref, sh2_ref, sc2_ref, d_ref, wglu_ref, bglu_ref, wla_ref,
     cw_ref, cb_ref, wlb_ref, wout_ref, n2g_ref, wrh_ref, wrl_ref) = refs[:18]
    if seq:
        x1_ref, h2_ref, lg_ref, nc_ref, carry = refs[18:]
    else:
        cb0_ref, cb1_ref, x1_ref, h2_ref, lg_ref, nc0_ref, nc1_ref = refs[18:]

    xb = r_ref[:, 0:D_CONV].astype(F32)
    gb = r_ref[:, D_CONV:2 * D_CONV].astype(F32)
    gc = r_ref[:, 2 * D_CONV:3 * D_CONV].astype(F32)
    v = gc * xb
    if seq:
        first = pl.program_id(0) % tiles_per_batch == 0

        @pl.when(first)
        def _():
            carry[...] = jnp.zeros_like(carry)

        c0 = carry[0:1, :]
        c1 = carry[1:2, :]
        row = lax.broadcasted_iota(I32, (tm, 1), 0)
        v1 = jnp.where(row == 0, c1, pltpu.roll(v, 1, 0))
        v2 = jnp.where(row == 0, c0, jnp.where(row == 1, c1, pltpu.roll(v, 2, 0)))
        tail = v[tm - 2:tm, :]
        carry[...] = tail
        nc_ref[0] = tail
    else:
        v2 = cb0_ref[...]
        v1 = cb1_ref[...]
        nc0_ref[...] = v1
        nc1_ref[...] = v
    conv = cb_ref[...] + cw_ref[0:1, :] * v2 + cw_ref[1:2, :] * v1 + cw_ref[2:3, :] * v
    gated_conv = (gb * conv).astype(BF16)

    def mod_rows(ref, rs):
        return ref[0] if len(ref.shape) == 3 else ref[rs, :]

    n_split = max(tm // MIX_ROWS, 1)
    rows = tm // n_split
    off = 3 * D_CONV
    dn = (((1,), (1,)), ((), ()))
    for part in range(n_split):
        rs = slice(part * rows, (part + 1) * rows)
        ya = jax.nn.gelu(ys_ref[rs, :] + d_ref[...] * u_ref[rs, :], approximate=True)
        z = jnp.dot(ya.astype(BF16), wglu_ref[...], preferred_element_type=F32) + bglu_ref[...]
        ya = ya * jax.nn.sigmoid(z)
        branch_a = jnp.dot(ya.astype(BF16), wla_ref[...], preferred_element_type=F32)
        branch_b = jnp.dot(gated_conv[rs], wlb_ref[...], preferred_element_type=F32)
        gla = r_ref[rs, off:off + D_MODEL].astype(F32)
        glb = r_ref[rs, off + D_MODEL:off + 2 * D_MODEL].astype(F32)
        merged = jax.nn.sigmoid(gla) * branch_a + jax.nn.sigmoid(glb) * branch_b
        x1 = x_ref[rs, :] + mod_rows(g1_ref, rs) * jnp.dot(merged.astype(BF16), wout_ref[...],
                                                           preferred_element_type=F32)
        x1_ref[rs, :] = x1

        y = x1 * lax.rsqrt(jnp.mean(x1 * x1, axis=-1, keepdims=True) + RMS_EPS) * n2g_ref[...]
        h2 = y * (1.0 + mod_rows(sc2_ref, rs)) + mod_rows(sh2_ref, rs)
        h2b = h2.astype(BF16)
        h2_ref[rs, :] = h2b

        h2l = (h2 - h2b.astype(F32)).astype(BF16)
        lg_ref[:, rs] = (lax.dot_general(wrh_ref[...], h2b, dn, preferred_element_type=F32)
                         + lax.dot_general(wrh_ref[...], h2l, dn, preferred_element_type=F32)
                         + lax.dot_general(wrl_ref[...], h2b, dn, preferred_element_type=F32))


def _mixer_post(x, y_ssm, u, rest, mods, rows_per_batch, weights, conv_state, tm):
    t = x.shape[0]
    seq = conv_state is None
    tiles_per_batch = rows_per_batch // tm
    (ssm_d, w_glu, b_glu, w_la, conv_w, conv_b, w_lb, w_out, n2g, wr_hi, wr_lo) = weights
    row = lambda m: (m, 0)
    in_specs = [pl.BlockSpec((tm, D_MODEL), row),
                pl.BlockSpec((tm, D_SSM), row),
                pl.BlockSpec((tm, D_SSM), row),
                pl.BlockSpec((tm, rest.shape[1]), row),
                _mod_spec(mods, 2, rows_per_batch, tm),
                _mod_spec(mods, 3, rows_per_batch, tm),
                _mod_spec(mods, 4, rows_per_batch, tm),
                _const_spec((1, D_SSM)), _const_spec((D_SSM, D_SSM)), _const_spec((1, D_SSM)),
                _const_spec((D_SSM, D_MODEL)), _const_spec((CONV_W, D_CONV)), _const_spec((1, D_CONV)),
                _const_spec((D_CONV, D_MODEL)), _const_spec((D_MODEL, D_MODEL)),
                _const_spec((1, D_MODEL)), _const_spec((N_EXPERTS, D_MODEL)),
                _const_spec((N_EXPERTS, D_MODEL))]
    args = [x, y_ssm, u, rest, mods, mods, mods, ssm_d, w_glu, b_glu, w_la, conv_w, conv_b, w_lb,
            w_out, n2g, wr_hi, wr_lo]
    out_specs = [pl.BlockSpec((tm, D_MODEL), row),
                 pl.BlockSpec((tm, D_MODEL), row),
                 pl.BlockSpec((N_EXPERTS, tm), lambda m: (0, m))]
    out_shape = [jax.ShapeDtypeStruct((t, D_MODEL), F32),
                 jax.ShapeDtypeStruct((t, D_MODEL), BF16),
                 jax.ShapeDtypeStruct((N_EXPERTS, t), F32)]
    scratch = []
    if seq:
        nb = t // rows_per_batch
        out_specs.append(pl.BlockSpec((1, CONV_W - 1, D_CONV), lambda m: (m // tiles_per_batch, 0, 0)))
        out_shape.append(jax.ShapeDtypeStruct((nb, CONV_W - 1, D_CONV), F32))
        scratch.append(pltpu.VMEM((CONV_W - 1, D_CONV), F32))
    else:
        in_specs += [pl.BlockSpec((tm, D_CONV), row), pl.BlockSpec((tm, D_CONV), row)]
        args += [conv_state[:, 0, :], conv_state[:, 1, :]]
        out_specs += [pl.BlockSpec((tm, D_CONV), row), pl.BlockSpec((tm, D_CONV), row)]
        out_shape += [jax.ShapeDtypeStruct((t, D_CONV), F32), jax.ShapeDtypeStruct((t, D_CONV), F32)]
    return pl.pallas_call(
        functools.partial(_mixer_kernel, seq=seq, tm=tm, tiles_per_batch=tiles_per_batch),
        grid=(t // tm,),
        in_specs=in_specs,
        out_specs=out_specs,
        out_shape=out_shape,
        scratch_shapes=scratch,
        compiler_params=_cparams(("arbitrary",), 56),
        name="mixer_post",
    )(*args)


def _route_kernel(lg_ref, bias_ref, te_ref, gate_ref):
    scores = [jax.nn.sigmoid(lg_ref[e]) for e in range(N_EXPERTS)]
    choice = [scores[e] + bias_ref[e] for e in range(N_EXPERTS)]
    shape = scores[0].shape

    def first_max(vals):
        m = functools.reduce(jnp.maximum, vals)
        found = jnp.zeros(shape, jnp.bool_)
        hot = []
        for v in vals:
            is_m = jnp.logical_and(v == m, jnp.logical_not(found))
            hot.append(is_m)
            found = jnp.logical_or(found, is_m)
        return m, hot

    gscore = []
    for g in range(N_EXPERT_GROUPS):
        vals = choice[g * EXPERTS_PER_GROUP:(g + 1) * EXPERTS_PER_GROUP]
        m1, hot = first_max(vals)
        rest = [jnp.where(h, NEG_INF, v) for h, v in zip(hot, vals)]
        gscore.append(m1 + functools.reduce(jnp.maximum, rest))

    gsel = [jnp.zeros(shape, jnp.bool_) for _ in range(N_EXPERT_GROUPS)]
    for _ in range(TOPK_GROUPS):
        _, hot = first_max(gscore)
        gsel = [jnp.logical_or(s, h) for s, h in zip(gsel, hot)]
        gscore = [jnp.where(h, NEG_INF, v) for h, v in zip(hot, gscore)]

    cand = [jnp.where(gsel[e // EXPERTS_PER_GROUP], choice[e], NEG_INF) for e in range(N_EXPERTS)]
    picked_e = []
    picked_s = []
    for _ in range(TOP_K):
        _, hot = first_max(cand)
        idx = jnp.zeros(shape, I32)
        sc = jnp.zeros(shape, F32)
        for e in range(N_EXPERTS):
            idx = jnp.where(hot[e], e, idx)
            sc = jnp.where(hot[e], scores[e], sc)
        cand = [jnp.where(h, NEG_INF, v) for h, v in zip(hot, cand)]
        picked_e.append(idx)
        picked_s.append(sc)
    total = functools.reduce(lambda a, b: a + b, picked_s)
    for k in range(TOP_K):
        te_ref[k] = picked_e[k]
        gate_ref[k] = picked_s[k] / total * ROUTED_SCALE


def _route(logits_t, bias):
    t = logits_t.shape[1]
    r = t // LANES
    rb = min(r, SUBLANES)
    lg = logits_t.reshape(N_EXPERTS, r, LANES)
    bias_b = jnp.broadcast_to(bias.astype(F32).reshape(N_EXPERTS, 1, 1), (N_EXPERTS, 1, LANES))
    te, gate = pl.pallas_call(
        _route_kernel,
        grid=(r // rb,),
        in_specs=[pl.BlockSpec((N_EXPERTS, rb, LANES), lambda i: (0, i, 0)),
                  pl.BlockSpec((N_EXPERTS, 1, LANES), lambda i: (0, 0, 0))],
        out_specs=[pl.BlockSpec((TOP_K, rb, LANES), lambda i: (0, i, 0)),
                   pl.BlockSpec((TOP_K, rb, LANES), lambda i: (0, i, 0))],
        out_shape=[jax.ShapeDtypeStruct((TOP_K, r, LANES), I32),
                   jax.ShapeDtypeStruct((TOP_K, r, LANES), F32)],
        compiler_params=_cparams(("arbitrary",), 32),
        name="route",
    )(lg, bias_b)
    return te.reshape(TOP_K, t), gate.reshape(TOP_K, t)


def _dispatch_kernel(nused_ref, tok_ref, h_ref, xs_ref, tile):
    used_rows = nused_ref[0] * MOE_ROWS
    live = pl.program_id(0) * DISP_ROWS < used_rows

    @pl.when(jnp.logical_not(live))
    def _():
        xs_ref[...] = jnp.zeros_like(xs_ref)

    @pl.when(live)
    def _():
        for mi in range(DISP_ROWS):
            tile[pl.ds(mi, ROW_TILES, stride=DISP_STRIDE), :] = h_ref[tok_ref[mi]].astype(F32)
        for j in range(ROW_TILES):
            xs_ref[:, j * LANES:(j + 1) * LANES] = (
                tile[j * DISP_STRIDE:j * DISP_STRIDE + DISP_ROWS, :].astype(BF16))


def _dispatch(h_tiles, tok_buf, n_used, n_blocks):
    n_rows = n_blocks * MOE_ROWS
    assert n_rows % DISP_ROWS == 0

    def last_live(i, nu):
        return jnp.minimum(i, (nu[0] * MOE_ROWS - 1) // DISP_ROWS)

    return pl.pallas_call(
        _dispatch_kernel,
        grid_spec=pltpu.PrefetchScalarGridSpec(
            num_scalar_prefetch=1,
            grid=(n_rows // DISP_ROWS,),
            in_specs=[pl.BlockSpec((DISP_ROWS,), lambda i, nu: (last_live(i, nu),),
                                   memory_space=pltpu.SMEM),
                      pl.BlockSpec(memory_space=pltpu.VMEM)],
            out_specs=pl.BlockSpec((DISP_ROWS, D_MODEL), lambda i, nu: (i, 0)),
            scratch_shapes=[pltpu.VMEM((ROW_TILES * DISP_STRIDE, LANES), F32)]),
        out_shape=jax.ShapeDtypeStruct((n_rows, D_MODEL), BF16),
        compiler_params=_cparams(("arbitrary",), 56),
        name="dispatch",
    )(n_used, tok_buf, h_tiles)


def _experts_kernel(be_ref, nused_ref, first_ref, next_ref, xs_ref, wg_hbm, wu_hbm, wd_hbm, ys_ref,
                    wg_f, wu_f, wd_f, wg_s, wu_s, wd_s, tile_a, tile_b, sem):
    i = pl.program_id(0)

    def weight_copies(e):
        return (pltpu.make_async_copy(wg_hbm.at[e], wg_f, sem.at[0]),
                pltpu.make_async_copy(wu_hbm.at[e], wu_f, sem.at[1]),
                pltpu.make_async_copy(wd_hbm.at[e], wd_f, sem.at[2]))

    n_used = nused_ref[0]

    @pl.when(i < n_used)
    def _():
        @pl.when(i == 0)
        def _():
            for c in weight_copies(be_ref[0]):
                c.start()

        @pl.when(first_ref[i] == 1)
        def _():
            for c in weight_copies(be_ref[i]):
                c.wait()
            wg_s[...] = wg_f[...].astype(BF16)
            wu_s[...] = wu_f[...].astype(BF16)
            wd_s[...] = wd_f[...].astype(BF16)

            @pl.when(next_ref[i] >= 0)
            def _():
                for c in weight_copies(next_ref[i]):
                    c.start()

    def ffn_to_tile(tile):
        x = xs_ref[...]
        g = jnp.dot(x, wg_s[...], preferred_element_type=F32)
        u = jnp.dot(x, wu_s[...], preferred_element_type=F32)
        h = (_silu(g) * u).astype(BF16)
        y = jnp.dot(h, wd_s[...], preferred_element_type=F32)
        for j in range(ROW_TILES):
            tile[j * ROW_STRIDE:j * ROW_STRIDE + MOE_ROWS, :] = y[:, j * LANES:(j + 1) * LANES]

    def tile_to_rows(tile):
        for mi in range(MOE_ROWS):
            ys_ref[mi] = tile[pl.ds(mi, ROW_TILES, stride=ROW_STRIDE), :].astype(BF16)

    even = i % 2 == 0
    both = jnp.logical_and(i >= 1, i < n_used)

    @pl.when(jnp.logical_and(both, even))
    def _():
        ffn_to_tile(tile_a)
        tile_to_rows(tile_b)

    @pl.when(jnp.logical_and(both, jnp.logical_not(even)))
    def _():
        ffn_to_tile(tile_b)
        tile_to_rows(tile_a)

    @pl.when(i == 0)
    def _():
        ffn_to_tile(tile_a)

    @pl.when(jnp.logical_and(i == n_used, even))
    def _():
        tile_to_rows(tile_b)

    @pl.when(jnp.logical_and(i == n_used, jnp.logical_not(even)))
    def _():
        tile_to_rows(tile_a)

    @pl.when(i > n_used)
    def _():
        ys_ref[...] = jnp.zeros_like(ys_ref)


def _experts(xs, block_expert, n_used, first, next_expert, w_gate, w_up, w_down, n_blocks):
    def xmap(i, be, nu, fi, nx):
        return (jnp.minimum(i, nu[0] - 1), 0)

    any_spec = pl.BlockSpec(memory_space=pl.ANY)
    return pl.pallas_call(
        _experts_kernel,
        grid_spec=pltpu.PrefetchScalarGridSpec(
            num_scalar_prefetch=4,
            grid=(n_blocks + 1,),
            in_specs=[pl.BlockSpec((MOE_ROWS, D_MODEL), xmap), any_spec, any_spec, any_spec],
            out_specs=pl.BlockSpec((MOE_ROWS, ROW_TILES, LANES),
                                   lambda i, *_: (jnp.maximum(i - 1, 0), 0, 0)),
            scratch_shapes=[pltpu.VMEM((D_MODEL, D_EXPERT), F32),
                            pltpu.VMEM((D_MODEL, D_EXPERT), F32),
                            pltpu.VMEM((D_EXPERT, D_MODEL), F32),
                            pltpu.VMEM((D_MODEL, D_EXPERT), BF16),
                            pltpu.VMEM((D_MODEL, D_EXPERT), BF16),
                            pltpu.VMEM((D_EXPERT, D_MODEL), BF16),
                            pltpu.VMEM((ROW_TILES * ROW_STRIDE, LANES), F32),
                            pltpu.VMEM((ROW_TILES * ROW_STRIDE, LANES), F32),
                            pltpu.SemaphoreType.DMA((3,))]),
        out_shape=jax.ShapeDtypeStruct((n_blocks * MOE_ROWS, ROW_TILES, LANES), BF16),
        compiler_params=_cparams(("arbitrary",), 48),
        name="experts",
    )(block_expert, n_used, first, next_expert, xs, w_gate, w_up, w_down)


def _combine_kernel(src_ref, ntot_ref, lidx_ref, gate_ref, ys_hbm, out_ref,
                    stage, tile, sem, *, stage_rows):
    i = pl.program_id(0)
    n_tiles = pl.num_programs(0)

    def chunk_copy(src, slot, dst):
        return pltpu.make_async_copy(ys_hbm.at[pl.ds(src, COMB_CH)],
                                     stage.at[pl.ds(slot * stage_rows + dst, COMB_CH)],
                                     sem.at[slot])

    def issue(tile_idx, slot):
        def c_body(g, _):
            for k in range(COMB_ISSUE):
                q = g * COMB_ISSUE + k
                chunk_copy(src_ref[tile_idx, q], slot, q * COMB_CH).start(priority=k % 2)
            return 0

        lax.fori_loop(0, ntot_ref[tile_idx] // COMB_ISSUE, c_body, 0)

    @pl.when(i == 0)
    def _():
        issue(0, 0)

    @pl.when(i + 1 < n_tiles)
    def _():
        issue(i + 1, (i + 1) % 2)

    slot = i % 2

    def w_body(c, _):
        for _k in range(COMB_ISSUE):
            chunk_copy(0, slot, 0).wait()
        return 0

    lax.fori_loop(0, ntot_ref[i] // COMB_ISSUE, w_body, 0)

    stride = COMB_TOK + SUBLANES

    for tk in range(COMB_TOK):
        acc = None
        for k in range(TOP_K):
            w = stage[lidx_ref[tk * TOP_K + k]].astype(F32)
            term = gate_ref[tk * TOP_K + k] * w
            acc = term if acc is None else acc + term
        tile[pl.ds(tk, ROW_TILES, stride=stride), :] = acc
    for j in range(ROW_TILES):
        out_ref[:, j * LANES:(j + 1) * LANES] = tile[j * stride:j * stride + COMB_TOK, :]


def _combine(ys, src_flat, n_total, lidx, gate, t, stage_rows):
    n_tiles = t // COMB_TOK
    stride = COMB_TOK + SUBLANES
    return pl.pallas_call(
        functools.partial(_combine_kernel, stage_rows=stage_rows),
        grid_spec=pltpu.PrefetchScalarGridSpec(
            num_scalar_prefetch=2,
            grid=(n_tiles,),
            in_specs=[pl.BlockSpec((COMB_TOK * TOP_K,), lambda i, *_: (i,), memory_space=pltpu.SMEM),
                      pl.BlockSpec((COMB_TOK * TOP_K,), lambda i, *_: (i,), memory_space=pltpu.SMEM),
                      pl.BlockSpec(memory_space=pl.ANY)],
            out_specs=pl.BlockSpec((COMB_TOK, D_MODEL), lambda i, *_: (i, 0)),
            scratch_shapes=[pltpu.VMEM((2 * stage_rows, ROW_TILES, LANES), BF16),
                            pltpu.VMEM((ROW_TILES * stride, LANES), F32),
                            pltpu.SemaphoreType.DMA((2,))]),
        out_shape=jax.ShapeDtypeStruct((t, D_MODEL), F32),
        compiler_params=_cparams(("arbitrary",), 48),
        name="combine",
    )(src_flat, n_total, lidx, gate, ys)


def _final_kernel(h_ref, x1_ref, rt_ref, g2_ref, wsg_ref, wsu_ref, wsd_ref, gf_ref, y_ref):
    h = h_ref[...]
    a = _silu(jnp.dot(h, wsg_ref[...], preferred_element_type=F32))
    b = jnp.dot(h, wsu_ref[...], preferred_element_type=F32)
    shared = jnp.dot((a * b).astype(BF16), wsd_ref[...], preferred_element_type=F32)
    xo = x1_ref[...] + _mod(g2_ref) * (shared + rt_ref[...])
    y_ref[...] = xo * lax.rsqrt(jnp.mean(xo * xo, axis=-1, keepdims=True) + RMS_EPS) * gf_ref[...]


def _final(h2, x1, routed, routed_row0, mods, rows_per_batch, w_sg, w_su, w_sd, gf, tm):
    t = x1.shape[0]
    row = lambda m: (m, 0)
    assert routed_row0 % tm == 0
    block0 = routed_row0 // tm
    return pl.pallas_call(
        _final_kernel,
        grid=(t // tm,),
        in_specs=[pl.BlockSpec((tm, D_MODEL), row), pl.BlockSpec((tm, D_MODEL), row),
                  pl.BlockSpec((tm, D_MODEL), lambda m: (m + block0, 0)),
                  _mod_spec(mods, 5, rows_per_batch, tm),
                  _const_spec((D_MODEL, D_EXPERT)), _const_spec((D_MODEL, D_EXPERT)),
                  _const_spec((D_EXPERT, D_MODEL)), _const_spec((1, D_MODEL))],
        out_specs=pl.BlockSpec((tm, D_MODEL), row),
        out_shape=jax.ShapeDtypeStruct((t, D_MODEL), F32),
        compiler_params=_cparams(("arbitrary",), 48),
        name="final",
    )(h2, x1, routed, mods, w_sg, w_su, w_sd, gf)


def _moe_plan(top_e, t):
    ar = jnp.arange(N_EXPERTS, dtype=I32)
    sel = top_e[:, :, None] == ar
    onehot = jnp.any(sel, axis=1).astype(I32)
    assert t % LANES == 0
    groups = onehot.astype(F32).reshape(t // LANES, LANES, N_EXPERTS)
    within = jnp.einsum("ab,nbe->nae", jnp.tril(jnp.ones((LANES, LANES), F32)), groups)
    totals = within[:, -1, :]
    csum = (within + (jnp.cumsum(totals, axis=0) - totals)[:, None, :]).reshape(t, N_EXPERTS).astype(I32)
    excl = csum - onehot
    counts = csum[-1]
    padded = (counts + MOE_ROWS - 1) // MOE_ROWS * MOE_ROWS
    pend = jnp.cumsum(padded)
    pstart = pend - padded
    pick = lambda per_expert: jnp.sum(jnp.where(sel, per_expert, 0), axis=-1)
    rank = pick(excl[:, None, :])
    n_blocks = -(-(t * TOP_K + N_EXPERTS * (MOE_ROWS - 1)) // MOE_ROWS)
    block_row = jnp.arange(n_blocks, dtype=I32) * MOE_ROWS
    block_expert = jnp.minimum(jnp.sum((pend[None, :] <= block_row[:, None]).astype(I32), axis=1),
                               N_EXPERTS - 1)
    n_used = (pend[-1] // MOE_ROWS).astype(I32).reshape(1)
    first = jnp.concatenate([jnp.ones((1,), I32), (block_expert[1:] != block_expert[:-1]).astype(I32)])
    later = jnp.logical_and(counts[None, :] > 0, ar[None, :] > ar[:, None])
    next_nonempty = jnp.min(jnp.where(later, ar[None, :], N_EXPERTS), axis=1)
    next_nonempty = jnp.where(next_nonempty < N_EXPERTS, next_nonempty, -1)
    next_expert = jnp.sum(jnp.where(block_expert[:, None] == ar[None, :], next_nonempty[None, :], 0), axis=1)
    assert t + MOE_ROWS <= 1 << TOKEN_BITS
    real = ((top_e << TOKEN_BITS) | jnp.arange(t, dtype=I32)[:, None]).reshape(-1)
    r = jnp.arange(MOE_ROWS - 1, dtype=I32)
    int_max = jnp.iinfo(jnp.int32).max
    pad_keys = jnp.where(r[None, :] < (padded - counts)[:, None],
                         (ar[:, None] << TOKEN_BITS) | (t + r[None, :]), int_max).reshape(-1)
    n_rows = n_blocks * MOE_ROWS
    fill = jnp.full((n_rows - real.shape[0] - pad_keys.shape[0],), int_max, I32)
    tokv = jnp.sort(jnp.concatenate([real, pad_keys, fill])) & ((1 << TOKEN_BITS) - 1)
    tok_buf = jnp.where(tokv < t, tokv, 0)
    lo = excl[::COMB_TOK]
    hi = jnp.concatenate([lo[1:], counts[None]], axis=0)
    n_chunks = (hi - lo + COMB_CH - 1) // COMB_CH
    cend = jnp.cumsum(n_chunks, axis=1)
    cstart = cend - n_chunks
    n_total = cend[:, -1].astype(I32)
    src_start = pstart[None, :] + lo
    q = jnp.arange(COMB_MAX_CHUNKS, dtype=I32)
    eq = jnp.minimum(jnp.sum((cend[:, None, :] <= q[None, :, None]).astype(I32), axis=-1), N_EXPERTS - 1)
    sel_q = eq[:, :, None] == ar
    pick_q = lambda v: jnp.sum(jnp.where(sel_q, v[:, None, :], 0), axis=-1)
    src_flat = pick_q(src_start) + (q[None, :] - pick_q(cstart)) * COMB_CH
    src_flat = jnp.where(q[None, :] < n_total[:, None], src_flat, 0)
    n_total = (n_total + COMB_ISSUE - 1) // COMB_ISSUE * COMB_ISSUE
    stage_rows = COMB_MAX_CHUNKS * COMB_CH
    n_tiles = lo.shape[0]
    t_pad = n_tiles * COMB_TOK
    slot_base = (jnp.arange(n_tiles, dtype=I32) % 2) * stage_rows
    local = jnp.repeat(cstart * COMB_CH - lo + slot_base[:, None], COMB_TOK, axis=0)[:t]
    lidx = pick(local[:, None, :]) + rank
    lidx = jnp.concatenate([lidx, jnp.broadcast_to(slot_base[-1], (t_pad - t, TOP_K))], axis=0)
    return dict(block_expert=block_expert.astype(I32), n_used=n_used, tok_buf=tok_buf.astype(I32),
                first=first, next_expert=next_expert.astype(I32),
                n_blocks=n_blocks, src_flat=src_flat.astype(I32), n_total=n_total,
                lidx=lidx.astype(I32), stage_rows=stage_rows, t_pad=t_pad)


def kernel(x_prompt, x_sample, state_ssm_re, state_ssm_im, state_conv, c_prompt, c_sample, w_ada, b_ada, norm1_g, norm2_g, w_in, ssm_a_re, ssm_a_im, ssm_log_dt, ssm_b_re, ssm_b_im, ssm_c_re, ssm_c_im, ssm_d, w_glu, b_glu, w_lift_a, conv_w, conv_b, w_lift_b, w_out, w_router, router_bias, w_exp_gate, w_exp_up, w_exp_down, w_sh_gate, w_sh_up, w_sh_down, final_norm_g):
    depth = w_ada.shape[0]
    assert depth == 1
    bp, seq, _ = x_prompt.shape
    bs = x_sample.shape[0]
    tp = bp * seq
    l = 0

    w_in_b = w_in[l].astype(BF16)
    wr_t = w_router[l].T
    wr_hi = wr_t.astype(BF16)
    wr_lo = (wr_t - wr_hi.astype(F32)).astype(BF16)
    mix_w = (ssm_d[l].reshape(1, -1), w_glu[l].astype(BF16), b_glu[l].reshape(1, -1),
             w_lift_a[l].astype(BF16), conv_w[l], conv_b[l].reshape(1, -1), w_lift_b[l].astype(BF16),
             w_out[l].astype(BF16), norm2_g[l].reshape(1, -1), wr_hi, wr_lo)
    g1 = norm1_g[l].reshape(1, -1)

    ada = _ada(jnp.concatenate([c_prompt, c_sample], axis=0), w_ada[l], b_ada[l])
    mods_p = ada[:bp].reshape(bp, 1, -1)
    mods_s = ada[bp:]

    apow, ssm_tm, ssm_bk, ssm_ck = _ssm_prep(ssm_a_re[l], ssm_a_im[l], ssm_log_dt[l], ssm_b_re[l],
                                             ssm_b_im[l], ssm_c_re[l], ssm_c_im[l])
    a1_re, a1_im, al_re, al_im = (_state_layout(apow[i]) for i in range(4))

    xp = x_prompt.reshape(tp, D_MODEL)
    u_p, rest_p = _in_proj(xp, mods_p, seq, g1, w_in_b, tm=256)
    n_chunks = seq // SSM_CHUNK
    ssm_rows = bp * n_chunks
    s_loc = _ssm_state(u_p, ssm_bk, SSM_CHUNK, ssm_rows)
    s_prev, s_fin_p = _ssm_scan(s_loc.reshape(bp, n_chunks, -1), al_re, al_im,
                                jnp.zeros((bp, OCTETS * OCT_STATE), F32))
    y_p = _ssm_out(u_p, s_prev.reshape(bp * n_chunks, -1), ssm_tm, ssm_ck, SSM_CHUNK, ssm_rows)
    x1_p, h2_p, lg_p, conv_p = _mixer_post(xp, y_p, u_p, rest_p, mods_p, seq, mix_w, None, tm=256)

    xs = x_sample.reshape(bs, D_MODEL)
    u_s, rest_s = _in_proj(xs, mods_s, 1, g1, w_in_b, tm=bs)
    s0_s = _pack_state(state_ssm_re[l], state_ssm_im[l])
    s_new_s = _ssm_step(_ssm_state(u_s, ssm_bk, 1, bs), s0_s, a1_re, a1_im)
    y_s = _ssm_out(u_s, s0_s, ssm_tm, ssm_ck, 1, bs)
    x1_s, h2_s, lg_s, nc0, nc1 = _mixer_post(xs, y_s, u_s, rest_s, mods_s, 1, mix_w,
                                                    state_conv[l], tm=bs)

    t = tp + bs
    te_p, gate_p = _route(lg_p, router_bias[l])
    te_s, gate_s = _route(lg_s, router_bias[l])
    top_e = jnp.concatenate([te_p, te_s], axis=1).T
    gate = jnp.concatenate([gate_p, gate_s], axis=1).T
    plan = _moe_plan(top_e, t)
    h_tiles = jnp.concatenate([h2_p, h2_s], axis=0).reshape(t, ROW_TILES, LANES)
    xs_sorted = _dispatch(h_tiles, plan["tok_buf"], plan["n_used"], plan["n_blocks"])
    ys = _experts(xs_sorted, plan["block_expert"], plan["n_used"], plan["first"], plan["next_expert"],
                  w_exp_gate[l], w_exp_up[l], w_exp_down[l], plan["n_blocks"])
    gate_pad = jnp.concatenate([gate, jnp.zeros((plan["t_pad"] - t, TOP_K), F32)], axis=0)
    routed = _combine(ys, plan["src_flat"], plan["n_total"], plan["lidx"].reshape(-1),
                      gate_pad.reshape(-1), plan["t_pad"], plan["stage_rows"])

    w_sg, w_su, w_sd = w_sh_gate[l].astype(BF16), w_sh_up[l].astype(BF16), w_sh_down[l].astype(BF16)
    gf = final_norm_g.reshape(1, -1)
    y_prompt = _final(h2_p, x1_p, routed, 0, mods_p, seq, w_sg, w_su, w_sd, gf, tm=512)
    y_sample = _final(h2_s, x1_s, routed, tp, mods_s, 1, w_sg, w_su, w_sd, gf, tm=bs)

    re_p, im_p = _unpack_state(s_fin_p)
    re_s, im_s = _unpack_state(s_new_s)
    return (y_prompt.reshape(bp, seq, D_MODEL), y_sample.reshape(bs, 1, D_MODEL),
            re_p[None], im_p[None], conv_p[None],
            re_s[None], im_s[None], jnp.stack([nc0, nc1], axis=1)[None])
```
